```python
import math
import jax, jax.numpy as jnp
from jax import lax
import numpy as np

D_MODEL = 1024
BATCH = 4
SEQ = 8192
DEPTH = 2

N_BRANCH = 3
BRANCH_DIM = D_MODEL
CONV_DIM = BRANCH_DIM
CONV_WIDTH = 3
MLSTM_HEADS = 8
MLSTM_DV = BRANCH_DIM // MLSTM_HEADS
MLSTM_DK = MLSTM_DV // 2
MLSTM_CHUNK = 128
GATE_CAP = 15.0
MLA_NOPE = 128
MLA_ROPE = 64
MLA_V = 128
MLA_HEADS = BRANCH_DIM // MLA_V
Q_LORA = D_MODEL // 4
KV_LORA = D_MODEL // 8
ROPE_BASE = 10000.0
ATTN_BLOCK = 128
D_FF = 4 * D_MODEL
NORM_EPS = 1e-6

IN_SPLITS = (
    CONV_DIM, CONV_DIM, CONV_DIM,
    MLSTM_HEADS * MLSTM_DK, MLSTM_HEADS * MLSTM_DK,
    MLSTM_HEADS * MLSTM_DV, MLSTM_HEADS * MLSTM_DV,
    MLSTM_HEADS, MLSTM_HEADS,
    Q_LORA, KV_LORA, MLA_ROPE,
    N_BRANCH * D_MODEL,
)
IN_DIM = (3 * CONV_DIM + 2 * MLSTM_HEADS * MLSTM_DK + 2 * MLSTM_HEADS * MLSTM_DV + 2 * MLSTM_HEADS
          + Q_LORA + KV_LORA + MLA_ROPE + N_BRANCH * D_MODEL)

kernel_name = 'hybrid_conv_mlstm_mla_gated_block'


def rms_norm(x, g):
    xf = x.astype(jnp.float32)
    y = xf * lax.rsqrt(jnp.mean(xf * xf, axis=-1, keepdims=True) + NORM_EPS)
    return (y * g.astype(jnp.float32)).astype(x.dtype)


def rope_tables(positions):
    inv_freq = ROPE_BASE ** (-jnp.arange(0, MLA_ROPE, 2, dtype=jnp.float32) / MLA_ROPE)
    ang = positions.astype(jnp.float32)[..., None] * inv_freq
    return jnp.cos(ang), jnp.sin(ang)


def apply_rope(x, cos, sin):
    half = MLA_ROPE // 2
    xf = x.astype(jnp.float32)
    x1, x2 = xf[..., :half], xf[..., half:]
    out = jnp.concatenate([x1 * cos - x2 * sin, x1 * sin + x2 * cos], axis=-1)
    return out.astype(x.dtype)


def split_columns(proj):
    idx = np.cumsum(np.array(IN_SPLITS))[:-1].tolist()
    return jnp.split(proj, idx, axis=-1)


def soft_cap(z):
    return GATE_CAP * jnp.tanh(z / GATE_CAP)


def short_conv_mixer(gate_b, gate_c, u, conv_w):
    z = gate_c * u
    zc = lax.conv_general_dilated(
        z, conv_w[:, None, :].astype(z.dtype), window_strides=(1,),
        padding=[(CONV_WIDTH - 1, 0)],
        dimension_numbers=('NWC', 'WIO', 'NWC'),
        feature_group_count=CONV_DIM)
    return gate_b * zc


def mlstm_mixer(q, k, v, o_pre, i_pre, f_pre, b_i, b_f, head_norm):
    bsz, seq = q.shape[:2]
    nc, L = seq // MLSTM_CHUNK, MLSTM_CHUNK
    f32 = jnp.float32

    def heads(t, d):
        return t.astype(f32).reshape(bsz, nc, L, MLSTM_HEADS, d).transpose(0, 3, 1, 2, 4)

    qh = heads(q, MLSTM_DK) * (MLSTM_DK ** -0.5)
    kh = heads(k, MLSTM_DK)
    vh = heads(v, MLSTM_DV)
    log_i = soft_cap(i_pre.astype(f32) + b_i.astype(f32))
    log_f = jax.nn.log_sigmoid(soft_cap(f_pre.astype(f32) + b_f.astype(f32)))
    log_i = log_i.reshape(bsz, nc, L, MLSTM_HEADS).transpose(0, 3, 1, 2)
    log_f = log_f.reshape(bsz, nc, L, MLSTM_HEADS).transpose(0, 3, 1, 2)

    b = jnp.cumsum(log_f, axis=-1)
    b_last = b[..., -1]
    a = b_last[..., None] - b + log_i
    m_loc = jnp.max(a, axis=-1)
    w = jnp.exp(a - m_loc[..., None])
    c_loc = jnp.einsum('bhcs,bhcsk,bhcsv->bhckv', w, kh, vh)
    n_loc = jnp.einsum('bhcs,bhcsk->bhck', w, kh)

    def step(carry, xs):
        c_st, n_st, m_st = carry
        bl, ml, cl, nl = xs
        m_new = jnp.maximum(bl + m_st, ml)
        s_old = jnp.exp(bl + m_st - m_new)
        s_loc = jnp.exp(ml - m_new)
        c_new = s_old[..., None, None] * c_st + s_loc[..., None, None] * cl
        n_new = s_old[..., None] * n_st + s_loc[..., None] * nl
        return (c_new, n_new, m_new), (c_st, n_st, m_st)

    init = (jnp.zeros((bsz, MLSTM_HEADS, MLSTM_DK, MLSTM_DV), f32),
            jnp.zeros((bsz, MLSTM_HEADS, MLSTM_DK), f32),
            jnp.zeros((bsz, MLSTM_HEADS), f32))
    xs = (jnp.moveaxis(b_last, 2, 0), jnp.moveaxis(m_loc, 2, 0),
          jnp.moveaxis(c_loc, 2, 0), jnp.moveaxis(n_loc, 2, 0))
    _, (c_prev, n_prev, m_prev) = lax.scan(step, init, xs)
    c_prev = jnp.moveaxis(c_prev, 0, 2)
    n_prev = jnp.moveaxis(n_prev, 0, 2)
    m_prev = jnp.moveaxis(m_prev, 0, 2)

    g = b + m_prev[..., None]
    d = b[..., :, None] - b[..., None, :] + log_i[..., None, :]
    causal = jnp.tril(jnp.ones((L, L), dtype=bool))
    d = jnp.where(causal, d, -jnp.inf)
    m_t = jnp.maximum(g, jnp.max(d, axis=-1))
    inter = jnp.exp(g - m_t)
    s = jnp.einsum('bhctk,bhcsk->bhcts', qh, kh) * jnp.exp(d - m_t[..., None])
    num = inter[..., None] * jnp.einsum('bhctk,bhckv->bhctv', qh, c_prev) + jnp.einsum('bhcts,bhcsv->bhctv', s, vh)
    den = inter * jnp.einsum('bhctk,bhck->bhct', qh, n_prev) + jnp.sum(s, axis=-1)
    h = num / jnp.maximum(jnp.abs(den), jnp.exp(-m_t))[..., None]
    h = h.transpose(0, 2, 3, 1, 4).reshape(bsz, seq, MLSTM_HEADS, MLSTM_DV)
    h = rms_norm(h, head_norm.reshape(MLSTM_HEADS, MLSTM_DV))
    h = h.reshape(bsz, seq, MLSTM_HEADS * MLSTM_DV) * jax.nn.sigmoid(o_pre.astype(f32))
    return h.astype(q.dtype)


def blocked_causal_attention(q, k, v):
    bsz, nh, seq, dh = q.shape
    nb = seq // ATTN_BLOCK
    scale = dh ** -0.5
    qb = q.reshape(bsz, nh, nb, ATTN_BLOCK, dh).transpose(2, 0, 1, 3, 4)
    key_idx = jnp.arange(seq)

    def one_block(args):
        qi, bi = args
        sc = jnp.einsum('bhqd,bhkd->bhqk', qi, k).astype(jnp.float32) * scale
        q_idx = bi * ATTN_BLOCK + jnp.arange(ATTN_BLOCK)
        sc = jnp.where(key_idx[None, :] <= q_idx[:, None], sc, -jnp.inf)
        p = jax.nn.softmax(sc, axis=-1)
        return jnp.einsum('bhqk,bhkd->bhqd', p.astype(v.dtype), v)

    out = lax.map(one_block, (qb, jnp.arange(nb)))
    return out.transpose(1, 2, 0, 3, 4).reshape(bsz, nh, seq, v.shape[-1])


def mla_mixer(c_q, c_kv, k_rope, q_a_norm, w_uq, kv_a_norm, w_ukv, q_norm, k_norm, cos, sin):
    bsz, seq = c_q.shape[:2]
    q = (rms_norm(c_q, q_a_norm) @ w_uq).reshape(bsz, seq, MLA_HEADS, MLA_NOPE + MLA_ROPE)
    kv = (rms_norm(c_kv, kv_a_norm) @ w_ukv).reshape(bsz, seq, MLA_HEADS, MLA_NOPE + MLA_V)
    q_nope = rms_norm(q[..., :MLA_NOPE], q_norm[:MLA_NOPE])
    q_pe = apply_rope(rms_norm(q[..., MLA_NOPE:], q_norm[MLA_NOPE:]), cos[:, :, None, :], sin[:, :, None, :])
    k_nope = rms_norm(kv[..., :MLA_NOPE], k_norm[:MLA_NOPE])
    v = kv[..., MLA_NOPE:]
    k_pe = apply_rope(rms_norm(k_rope, k_norm[MLA_NOPE:]), cos, sin)
    k_pe = jnp.broadcast_to(k_pe[:, :, None, :], (bsz, seq, MLA_HEADS, MLA_ROPE))
    qh = jnp.concatenate([q_nope, q_pe], axis=-1).transpose(0, 2, 1, 3)
    kh = jnp.concatenate([k_nope, k_pe], axis=-1).transpose(0, 2, 1, 3)
    vh = v.transpose(0, 2, 1, 3)
    o = blocked_causal_attention(qh, kh, vh)
    return o.transpose(0, 2, 1, 3).reshape(bsz, seq, MLA_HEADS * MLA_V)


def hybrid_layer(x, cos, sin, mix_norm, w_in, conv_w, b_i, b_f, head_norm,
                 q_a_norm, w_uq, kv_a_norm, w_ukv, q_norm, k_norm,
                 w_branch, w_out, mlp_norm, w_up, w_down):
    bsz, seq = x.shape[:2]
    xn = rms_norm(x, mix_norm)
    (cb, cc, cu, mq, mk, mv, mo, mi, mf, cq, ckv, krope, gate_pre) = split_columns(xn @ w_in)
    y_a = short_conv_mixer(cb, cc, cu, conv_w)
    y_b = mlstm_mixer(mq, mk, mv, mo, mi, mf, b_i, b_f, head_norm)
    y_c = mla_mixer(cq, ckv, krope, q_a_norm, w_uq, kv_a_norm, w_ukv, q_norm, k_norm, cos, sin)
    ys = jnp.stack([y_a, y_b, y_c], axis=0)
    branch = jnp.einsum('nbsc,ncd->nbsd', ys, w_branch)
    gates = jax.nn.sigmoid(gate_pre.reshape(bsz, seq, N_BRANCH, D_MODEL))
    merged = jnp.einsum('bsnd,nbsd->bsd', gates, branch)
    x = x + merged @ w_out
    h = rms_norm(x, mlp_norm)
    x = x + jnp.square(jax.nn.relu(h @ w_up)) @ w_down
    return x


def setup_inputs(seed: int = 0) -> dict:
    key = jax.random.key(seed)
    ks = jax.random.split(key, 24)
    f32 = jnp.float32

    def nrm(k, shape, scale):
        return jax.random.normal(k, shape, f32) * scale

    def gain(k, shape):
        return 1.0 + 0.1 * jax.random.normal(k, shape, f32)

    x = jax.random.normal(ks[0], (BATCH, SEQ, D_MODEL), f32)
    offsets = jax.random.randint(ks[1], (BATCH, 1), 0, 4096, dtype=jnp.int32)
    positions = (offsets + jnp.arange(SEQ, dtype=jnp.int32)[None, :]).astype(jnp.int32)
    f_bias = jnp.linspace(3.0, 6.0, MLSTM_HEADS, dtype=f32)[None, :] + 0.1 * jax.random.normal(ks[5], (DEPTH, MLSTM_HEADS), f32)
    return {
        'x': x,
        'positions': positions,
        'mix_norm': gain(ks[2], (DEPTH, D_MODEL)),
        'w_in': nrm(ks[3], (DEPTH, D_MODEL, IN_DIM), D_MODEL ** -0.5),
        'conv_w': nrm(ks[4], (DEPTH, CONV_WIDTH, CONV_DIM), CONV_WIDTH ** -0.5),
        'mlstm_igate_bias': nrm(ks[6], (DEPTH, MLSTM_HEADS), 0.1),
        'mlstm_fgate_bias': f_bias,
        'mlstm_head_norm': gain(ks[7], (DEPTH, MLSTM_HEADS * MLSTM_DV)),
        'mla_q_a_norm': gain(ks[8], (DEPTH, Q_LORA)),
        'mla_w_uq': nrm(ks[9], (DEPTH, Q_LORA, MLA_HEADS * (MLA_NOPE + MLA_ROPE)), Q_LORA ** -0.5),
        'mla_kv_a_norm': gain(ks[10], (DEPTH, KV_LORA)),
        'mla_w_ukv': nrm(ks[11], (DEPTH, KV_LORA, MLA_HEADS * (MLA_NOPE + MLA_V)), KV_LORA ** -0.5),
        'mla_q_norm': gain(ks[12], (DEPTH, MLA_NOPE + MLA_ROPE)),
        'mla_k_norm': gain(ks[13], (DEPTH, MLA_NOPE + MLA_ROPE)),
        'w_branch': nrm(ks[14], (DEPTH, N_BRANCH, BRANCH_DIM, D_MODEL), BRANCH_DIM ** -0.5),
        'w_out': nrm(ks[15], (DEPTH, D_MODEL, D_MODEL), D_MODEL ** -0.5),
        'mlp_norm': gain(ks[16], (DEPTH, D_MODEL)),
        'w_up': nrm(ks[17], (DEPTH, D_MODEL, D_FF), D_MODEL ** -0.5),
        'w_down': nrm(ks[18], (DEPTH, D_FF, D_MODEL), D_FF ** -0.5),
    }


def reference(x, positions, mix_norm, w_in, conv_w, mlstm_igate_bias, mlstm_fgate_bias,
              mlstm_head_norm, mla_q_a_norm, mla_w_uq, mla_kv_a_norm, mla_w_ukv,
              mla_q_norm, mla_k_norm, w_branch, w_out, mlp_norm, w_up, w_down):
    cos, sin = rope_tables(positions)
    for l in range(DEPTH):
        x = hybrid_layer(x, cos, sin, mix_norm[l], w_in[l], conv_w[l],
                         mlstm_igate_bias[l], mlstm_fgate_bias[l], mlstm_head_norm[l],
                         mla_q_a_norm[l], mla_w_uq[l], mla_kv_a_norm[l], mla_w_ukv[l],
                         mla_q_norm[l], mla_k_norm[l], w_branch[l], w_out[l],
                         mlp_norm[l], w_up[l], w_down[l])
    return x
```

```python
import functools

import jax
import jax.numpy as jnp
import numpy as np
from jax import lax
from jax.experimental import pallas as pl
from jax.experimental.pallas import tpu as pltpu

F32 = jnp.float32
BF16 = jnp.bfloat16

D_MODEL = 1024
N_BRANCH = 3
CONV_WIDTH = 3
HEADS = 8
MLSTM_DK = 64
MLSTM_DV = 128
CHUNK = 128
GATE_CAP = 15.0
MLA_NOPE = 128
MLA_ROPE = 64
MLA_V = 128
Q_LORA = 256
KV_LORA = 128
ROPE_BASE = 10000.0
D_FF = 4 * D_MODEL
NORM_EPS = 1e-6
QK_PAD = 256

MAIN_COLS = 9 * D_MODEL
TAIL_COLS = 512
GATE_COL0 = 6 * D_MODEL
TAIL_KROPE = Q_LORA + KV_LORA
IGATE_LANE = MLA_ROPE
FGATE_LANE = MLA_ROPE + HEADS

VMEM_LIMIT = 56 * 1024 * 1024


def _rms(x, g):
    return x * lax.rsqrt(jnp.mean(x * x, axis=-1, keepdims=True) + NORM_EPS) * g


def _dot(a, b):
    return jnp.dot(a, b, preferred_element_type=F32)


def _dot_nt(a, b):
    return lax.dot_general(a, b, (((1,), (1,)), ((), ())), preferred_element_type=F32)


def _dot_tn(a, b):
    return lax.dot_general(a, b, (((0,), (0,)), ((), ())), preferred_element_type=F32)


def _rope_table_kernel(pos_ref, freq_ref, cs_ref):
    ang = pos_ref[...].astype(F32) * freq_ref[...]
    c = jnp.cos(ang)
    s = jnp.sin(ang)
    cs_ref[...] = jnp.concatenate([c, c, -s, s], axis=-1)


def _rope_table(pos_col, tm):
    t = pos_col.shape[0]
    half = MLA_ROPE // 2
    inv_freq = (np.float32(ROPE_BASE) ** (-np.arange(0, MLA_ROPE, 2, dtype=np.float32) / np.float32(MLA_ROPE)))
    inv_freq = jnp.asarray(inv_freq.astype(np.float32)).reshape(1, half)
    return pl.pallas_call(
        _rope_table_kernel,
        grid=(t // tm,),
        in_specs=[pl.BlockSpec((tm, 1), lambda i: (i, 0)),
                  pl.BlockSpec((1, half), lambda i: (0, 0))],
        out_specs=pl.BlockSpec((tm, 2 * MLA_ROPE), lambda i: (i, 0)),
        out_shape=jax.ShapeDtypeStruct((t, 2 * MLA_ROPE), F32),
        name="rope_table",
    )(pos_col, inv_freq)


def _in_proj_kernel(x_ref, g_ref, w_ref, main_ref, tail_ref, xn_ref, *, n_main, n_gate0):
    j = pl.program_id(1)

    @pl.when(j == 0)
    def _():
        xn_ref[...] = _rms(x_ref[...], g_ref[...]).astype(BF16)

    @pl.when(j < n_gate0)
    def _():
        main_ref[...] = _dot(xn_ref[...], w_ref[...]).astype(BF16)

    @pl.when(jnp.logical_and(j >= n_gate0, j < n_main))
    def _():
        main_ref[...] = jax.nn.sigmoid(_dot(xn_ref[...], w_ref[...])).astype(BF16)

    @pl.when(j == n_main)
    def _():
        tail_ref[...] = _dot(xn_ref[...], w_ref[...])


def _in_proj(x2, g, w_all, tm, tn):
    t = x2.shape[0]
    n_main = MAIN_COLS // tn
    kern = functools.partial(_in_proj_kernel, n_main=n_main, n_gate0=GATE_COL0 // tn)
    return pl.pallas_call(
        kern,
        grid=(t // tm, n_main + 1),
        in_specs=[pl.BlockSpec((tm, D_MODEL), lambda i, j: (i, 0)),
                  pl.BlockSpec((1, D_MODEL), lambda i, j: (0, 0)),
                  pl.BlockSpec((D_MODEL, tn), lambda i, j: (0, j))],
        out_specs=[pl.BlockSpec((tm, tn), lambda i, j: (i, jnp.minimum(j, n_main - 1))),
                   pl.BlockSpec((tm, TAIL_COLS), lambda i, j: (i, 0))],
        out_shape=[jax.ShapeDtypeStruct((t, MAIN_COLS), BF16),
                   jax.ShapeDtypeStruct((t, TAIL_COLS), F32)],
        scratch_shapes=[pltpu.VMEM((tm, D_MODEL), BF16)],
        compiler_params=pltpu.CompilerParams(
            dimension_semantics=("parallel", "arbitrary"), vmem_limit_bytes=VMEM_LIMIT),
        name="in_proj",
    )(x2, g, w_all)


def _lane_scan(x, op, lane):
    s = 1
    while s < CHUNK:
        shifted = pltpu.roll(x, s, 1)
        x = jnp.where(lane >= s, op(x, shifted), x)
        s *= 2
    return x


def _mlstm_kernel(q_ref, k_ref, v_ref, o_ref, gate_ref, bias_ref, hn_ref, y_ref, c_ref, m_ref):
    L = CHUNK

    @pl.when(pl.program_id(1) == 0)
    def _():
        c_ref[...] = jnp.zeros_like(c_ref)
        m_ref[...] = jnp.zeros_like(m_ref)

    gt = (gate_ref[...] + bias_ref[...]).T
    log_i = GATE_CAP * jnp.tanh(gt[IGATE_LANE:IGATE_LANE + HEADS] * (1.0 / GATE_CAP))
    zf = GATE_CAP * jnp.tanh(gt[FGATE_LANE:FGATE_LANE + HEADS] * (1.0 / GATE_CAP))
    log_f = jnp.minimum(zf, 0.0) - jnp.log1p(jnp.exp(-jnp.abs(zf)))
    lane = lax.broadcasted_iota(jnp.int32, (HEADS, L), 1)
    b = _lane_scan(log_f, jnp.add, lane)
    m_prev = m_ref[...]
    b_last = jnp.broadcast_to(b[:, L - 1:L], (HEADS, L))
    rowd = log_i - b
    big_m = jnp.maximum(_lane_scan(rowd, jnp.maximum, lane), m_prev)
    inter = jnp.exp(m_prev - big_m)
    e_neg_m = jnp.exp(-(b + big_m))
    a = b_last + rowd
    m_loc = jnp.broadcast_to(jnp.max(a, axis=1, keepdims=True), (HEADS, L))
    w = jnp.exp(a - m_loc)
    m_new = jnp.maximum(b_last + m_prev, m_loc)
    s_old = jnp.exp(b_last + m_prev - m_new)
    s_loc = jnp.exp(m_loc - m_new)
    m_ref[...] = m_new

    stack = jnp.concatenate([big_m, inter, e_neg_m, w, jnp.zeros((L - 4 * HEADS, L), F32)], axis=0)
    cols = stack.T

    tri = (lax.broadcasted_iota(jnp.int32, (L, L), 0) >= lax.broadcasted_iota(jnp.int32, (L, L), 1))
    ones_col = (lax.broadcasted_iota(jnp.int32, (L, MLSTM_DV), 1) == 0).astype(BF16)
    q_all = q_ref[...] * (MLSTM_DK ** -0.5)

    for h in range(HEADS):
        qh = q_all[:, h * MLSTM_DK:(h + 1) * MLSTM_DK]
        kh = k_ref[:, h * MLSTM_DK:(h + 1) * MLSTM_DK]
        vh = v_ref[:, h * MLSTM_DV:(h + 1) * MLSTM_DV]
        v_aug = jnp.concatenate([vh, ones_col], axis=1)
        big_m_c = cols[:, h:h + 1]
        inter_c = cols[:, HEADS + h:HEADS + h + 1]
        e_neg_c = cols[:, 2 * HEADS + h:2 * HEADS + h + 1]
        w_c = cols[:, 3 * HEADS + h:3 * HEADS + h + 1]

        s = _dot_nt(qh, kh)
        dmat = jnp.where(tri, jnp.exp(rowd[h:h + 1, :] - big_m_c), 0.0)
        p = (s * dmat).astype(BF16)
        c_prev = c_ref[h]
        num_aug = _dot(p, v_aug) + inter_c * _dot(qh, c_prev.astype(BF16))
        num = num_aug[:, :MLSTM_DV]
        den = num_aug[:, MLSTM_DV:MLSTM_DV + 1]
        hh = num / jnp.maximum(jnp.abs(den), e_neg_c)
        hh = _rms(hh, hn_ref[:, h * MLSTM_DV:(h + 1) * MLSTM_DV])
        og = jax.nn.sigmoid(o_ref[:, h * MLSTM_DV:(h + 1) * MLSTM_DV].astype(F32))
        y_ref[:, h * MLSTM_DV:(h + 1) * MLSTM_DV] = (hh * og).astype(y_ref.dtype)

        wv = (w_c * v_aug.astype(F32)).astype(BF16)
        c_loc = _dot_tn(kh, wv)
        c_ref[h] = s_old[h:h + 1, :1] * c_prev + s_loc[h:h + 1, :1] * c_loc


def _mlstm(main, tail, gate_bias, head_norm, bsz, seq):
    t = main.shape[0]
    nc = seq // CHUNK
    row = lambda b, c: b * nc + c
    return pl.pallas_call(
        _mlstm_kernel,
        grid=(bsz, nc),
        in_specs=[pl.BlockSpec((CHUNK, 512), lambda b, c: (row(b, c), 6)),
                  pl.BlockSpec((CHUNK, 512), lambda b, c: (row(b, c), 7)),
                  pl.BlockSpec((CHUNK, D_MODEL), lambda b, c: (row(b, c), 4)),
                  pl.BlockSpec((CHUNK, D_MODEL), lambda b, c: (row(b, c), 5)),
                  pl.BlockSpec((CHUNK, 128), lambda b, c: (row(b, c), TAIL_KROPE // 128)),
                  pl.BlockSpec((1, 128), lambda b, c: (0, 0)),
                  pl.BlockSpec((1, D_MODEL), lambda b, c: (0, 0))],
        out_specs=pl.BlockSpec((CHUNK, D_MODEL), lambda b, c: (row(b, c), 0)),
        out_shape=jax.ShapeDtypeStruct((t, D_MODEL), BF16),
        scratch_shapes=[pltpu.VMEM((HEADS, MLSTM_DK, 2 * MLSTM_DV), F32),
                        pltpu.VMEM((HEADS, CHUNK), F32)],
        compiler_params=pltpu.CompilerParams(dimension_semantics=("parallel", "arbitrary")),
        name="mlstm",
    )(main, main, main, main, tail, gate_bias, head_norm)


def _rope(x, cs):
    half = MLA_ROPE // 2
    rot = jnp.concatenate([x[:, half:], x[:, :half]], axis=-1)
    return x * cs[:, :MLA_ROPE] + rot * cs[:, MLA_ROPE:]


def _mla_prep_kernel(tail_ref, cs_ref, qa_ref, wq_ref, kva_ref, wkv_ref, qn_ref, kn_ref,
                     q_out, k_out, v_out):
    tm = tail_ref.shape[0]
    cs = cs_ref[...]
    q_scale = (MLA_NOPE + MLA_ROPE) ** -0.5
    cq = _rms(tail_ref[:, :Q_LORA], qa_ref[...]).astype(BF16)
    ckv = _rms(tail_ref[:, Q_LORA:Q_LORA + KV_LORA], kva_ref[...]).astype(BF16)
    qf = _dot(cq, wq_ref[...])
    kvf = _dot(ckv, wkv_ref[...])
    qn_g, qp_g = qn_ref[:, :MLA_NOPE], qn_ref[:, MLA_NOPE:]
    kn_g, kp_g = kn_ref[:, :MLA_NOPE], kn_ref[:, MLA_NOPE:]
    k_pe = _rope(_rms(tail_ref[:, TAIL_KROPE:TAIL_KROPE + MLA_ROPE], kp_g), cs).astype(BF16)
    zpad = jnp.zeros((tm, QK_PAD - MLA_NOPE - MLA_ROPE), BF16)
    pe0 = HEADS * MLA_NOPE
    for h in range(HEADS):
        q_nope = _rms(qf[:, h * MLA_NOPE:(h + 1) * MLA_NOPE], qn_g) * q_scale
        q_pe = _rope(_rms(qf[:, pe0 + h * MLA_ROPE:pe0 + (h + 1) * MLA_ROPE], qp_g), cs) * q_scale
        q_out[0, h] = jnp.concatenate([q_nope.astype(BF16), q_pe.astype(BF16), zpad], axis=-1)
        k_nope = _rms(kvf[:, h * MLA_NOPE:(h + 1) * MLA_NOPE], kn_g)
        k_out[0, h] = jnp.concatenate([k_nope.astype(BF16), k_pe, zpad], axis=-1)
        v_out[0, h] = kvf[:, pe0 + h * MLA_V:pe0 + (h + 1) * MLA_V].astype(BF16)


def _mla_prep(tail, cs, qa, wq, kva, wkv, qn, kn, bsz, seq, tm):
    nt = seq // tm
    full = lambda shape: pl.BlockSpec(shape, lambda i: (0,) * len(shape))
    head_out = lambda w: pl.BlockSpec((1, HEADS, tm, w), lambda i: (i // nt, 0, i % nt, 0))
    return pl.pallas_call(
        _mla_prep_kernel,
        grid=(bsz * nt,),
        in_specs=[pl.BlockSpec((tm, TAIL_COLS), lambda i: (i, 0)),
                  pl.BlockSpec((tm, 2 * MLA_ROPE), lambda i: (i, 0)),
                  full((1, Q_LORA)), full(wq.shape), full((1, KV_LORA)), full(wkv.shape),
                  full((1, MLA_NOPE + MLA_ROPE)), full((1, MLA_NOPE + MLA_ROPE))],
        out_specs=[head_out(QK_PAD), head_out(QK_PAD), head_out(MLA_V)],
        out_shape=[jax.ShapeDtypeStruct((bsz, HEADS, seq, QK_PAD), BF16),
                   jax.ShapeDtypeStruct((bsz, HEADS, seq, QK_PAD), BF16),
                   jax.ShapeDtypeStruct((bsz, HEADS, seq, MLA_V), BF16)],
        compiler_params=pltpu.CompilerParams(dimension_semantics=("parallel",),
                                             vmem_limit_bytes=VMEM_LIMIT),
        name="mla_prep",
    )(tail, cs, qa, wq, kva, wkv, qn, kn)


def _attn_kernel(q_ref, k_ref, v_ref, o_ref, *, tq, tk):
    qi = pl.program_id(2)
    q = q_ref[0, 0]

    def step(j, carry, masked):
        m, l, acc = carry
        k0 = pl.multiple_of(j * tk, tk)
        s = _dot_nt(q, k_ref[0, 0, pl.ds(k0, tk), :])
        if masked:
            qpos = qi * tq + lax.broadcasted_iota(jnp.int32, (tq, tk), 0)
            kpos = k0 + lax.broadcasted_iota(jnp.int32, (tq, tk), 1)
            s = jnp.where(kpos <= qpos, s, -1e30)
        m_new = jnp.maximum(m, jnp.max(s, axis=-1, keepdims=True))
        p = jnp.exp(s - m_new)
        alpha = jnp.exp(m - m_new)
        l = alpha * l + jnp.sum(p, axis=-1, keepdims=True)
        acc = alpha * acc + _dot(p.astype(BF16), v_ref[0, 0, pl.ds(k0, tk), :])
        return m_new, l, acc

    init = (jnp.full((tq, 1), -1e30, F32), jnp.zeros((tq, 1), F32), jnp.zeros((tq, MLA_V), F32))
    n_full = (qi * tq) // tk
    carry = lax.fori_loop(0, n_full, functools.partial(step, masked=False), init)
    for d in range(tq // tk):
        carry = step(n_full + d, carry, True)
    _, l, acc = carry
    o_ref[0] = (acc / l).astype(o_ref.dtype)


def _attention(q, k, v, tq, tk):
    bsz, _, seq, _ = q.shape
    kern = functools.partial(_attn_kernel, tq=tq, tk=tk)
    return pl.pallas_call(
        kern,
        grid=(bsz, HEADS, seq // tq),
        in_specs=[pl.BlockSpec((1, 1, tq, QK_PAD), lambda b, h, i: (b, h, i, 0)),
                  pl.BlockSpec((1, 1, seq, QK_PAD), lambda b, h, i: (b, h, 0, 0)),
                  pl.BlockSpec((1, 1, seq, MLA_V), lambda b, h, i: (b, h, 0, 0))],
        out_specs=pl.BlockSpec((1, tq, MLA_V), lambda b, h, i: (b, i, h)),
        out_shape=jax.ShapeDtypeStruct((bsz, seq, HEADS * MLA_V), BF16),
        compiler_params=pltpu.CompilerParams(
            dimension_semantics=("parallel", "parallel", "arbitrary"), vmem_limit_bytes=VMEM_LIMIT),
        name="attn",
    )(q, k, v)


def _merge_kernel(cb_ref, cc_ref, cu_ref, gate_ref, yb_ref, yc_ref, x_ref, cw_ref, wb_ref, wo_ref,
                  out_ref, zprev_ref, *, tiles_per_seq):
    tm = x_ref.shape[0]

    @pl.when(pl.program_id(0) % tiles_per_seq == 0)
    def _():
        zprev_ref[...] = jnp.zeros_like(zprev_ref)

    z = cc_ref[...].astype(F32) * cu_ref[...].astype(F32)
    row = lax.broadcasted_iota(jnp.int32, (tm, D_MODEL), 0)
    p1 = zprev_ref[7:8, :]
    p2 = zprev_ref[6:7, :]
    z1 = jnp.where(row == 0, p1, pltpu.roll(z, 1, 0))
    z2 = jnp.where(row == 0, p2, jnp.where(row == 1, p1, pltpu.roll(z, 2, 0)))
    zprev_ref[...] = z[tm - 8:, :]
    cw = cw_ref[...]
    y_a = cb_ref[...].astype(F32) * (cw[0:1] * z2 + cw[1:2] * z1 + cw[2:3] * z)

    g = gate_ref[...]
    merged = g[:, :D_MODEL].astype(F32) * _dot(y_a.astype(BF16), wb_ref[0])
    merged += g[:, D_MODEL:2 * D_MODEL].astype(F32) * _dot(yb_ref[...], wb_ref[1])
    merged += g[:, 2 * D_MODEL:].astype(F32) * _dot(yc_ref[...], wb_ref[2])
    out_ref[...] = x_ref[...] + _dot(merged.astype(BF16), wo_ref[...])


def _merge(main, y_b, y_c, x2, conv_w, w_branch, w_out, seq, tm):
    t = x2.shape[0]
    kern = functools.partial(_merge_kernel, tiles_per_seq=seq // tm)
    col = lambda c: pl.BlockSpec((tm, D_MODEL), lambda i: (i, c))
    return pl.pallas_call(
        kern,
        grid=(t // tm,),
        in_specs=[col(0), col(1), col(2),
                  pl.BlockSpec((tm, N_BRANCH * D_MODEL), lambda i: (i, GATE_COL0 // (N_BRANCH * D_MODEL))),
                  col(0), col(0), col(0),
                  pl.BlockSpec((CONV_WIDTH, D_MODEL), lambda i: (0, 0)),
                  pl.BlockSpec((N_BRANCH, D_MODEL, D_MODEL), lambda i: (0, 0, 0)),
                  pl.BlockSpec((D_MODEL, D_MODEL), lambda i: (0, 0))],
        out_specs=col(0),
        out_shape=jax.ShapeDtypeStruct((t, D_MODEL), F32),
        scratch_shapes=[pltpu.VMEM((8, D_MODEL), F32)],
        compiler_params=pltpu.CompilerParams(dimension_semantics=("arbitrary",),
                                             vmem_limit_bytes=VMEM_LIMIT),
        name="merge",
    )(main, main, main, main, y_b, y_c, x2, conv_w, w_branch, w_out)


def _mlp_kernel(x_ref, g_ref, wu_ref, wd_ref, out_ref, *, n_chunk):
    x = x_ref[...]
    h = _rms(x, g_ref[...]).astype(BF16)
    ck = D_FF // n_chunk
    acc = x
    for c in range(n_chunk):
        u = jnp.maximum(_dot(h, wu_ref[:, c * ck:(c + 1) * ck]), 0.0)
        acc = acc + _dot((u * u).astype(BF16), wd_ref[c * ck:(c + 1) * ck, :])
    out_ref[...] = acc


def _mlp(x2, g, w_up, w_down, tm):
    t = x2.shape[0]
    kern = functools.partial(_mlp_kernel, n_chunk=4)
    return pl.pallas_call(
        kern,
        grid=(t // tm,),
        in_specs=[pl.BlockSpec((tm, D_MODEL), lambda i: (i, 0)),
                  pl.BlockSpec((1, D_MODEL), lambda i: (0, 0)),
                  pl.BlockSpec((D_MODEL, D_FF), lambda i: (0, 0)),
                  pl.BlockSpec((D_FF, D_MODEL), lambda i: (0, 0))],
        out_specs=pl.BlockSpec((tm, D_MODEL), lambda i: (i, 0)),
        out_shape=jax.ShapeDtypeStruct((t, D_MODEL), F32),
        compiler_params=pltpu.CompilerParams(dimension_semantics=("parallel",),
                                             vmem_limit_bytes=VMEM_LIMIT),
        name="mlp",
    )(x2, g, w_up, w_down)


def _regroup_w_in(w_in):
    c = D_MODEL
    q0 = 3 * c
    i0 = q0 + 2 * HEADS * MLSTM_DK + 2 * HEADS * MLSTM_DV
    cq0 = i0 + 2 * HEADS
    ckv0 = cq0 + Q_LORA
    kr0 = ckv0 + KV_LORA
    g0 = kr0 + MLA_ROPE
    pad = jnp.zeros((D_MODEL, TAIL_COLS - (Q_LORA + KV_LORA + MLA_ROPE + 2 * HEADS)), w_in.dtype)
    return jnp.concatenate(
        [w_in[:, :i0], w_in[:, g0:],
         w_in[:, cq0:g0], w_in[:, i0:cq0], pad], axis=1).astype(BF16)


def _heads_last_split(w, first, second):
    k = w.shape[0]
    w3 = w.reshape(k, HEADS, first + second)
    return jnp.concatenate([w3[:, :, :first].reshape(k, HEADS * first),
                            w3[:, :, first:].reshape(k, HEADS * second)], axis=1).astype(BF16)


def _pick(n, candidates):
    for c in candidates:
        if n % c == 0:
            return c
    raise ValueError(f"no tile size for {n}")


def kernel(x, positions, mix_norm, w_in, conv_w, mlstm_igate_bias, mlstm_fgate_bias, mlstm_head_norm,
           mla_q_a_norm, mla_w_uq, mla_kv_a_norm, mla_w_ukv, mla_q_norm, mla_k_norm, w_branch, w_out,
           mlp_norm, w_up, w_down):
    bsz, seq, _ = x.shape
    depth = w_in.shape[0]
    t = bsz * seq
    assert seq % CHUNK == 0
    tm_proj = _pick(t, (1024, 512, 256, 128))
    tm = _pick(seq, (512, 256, 128))
    tq = _pick(seq, (512, 256, 128))

    x2 = x.reshape(t, D_MODEL)
    cs = _rope_table(positions.reshape(t, 1), tm)
    row = lambda a: a.reshape(1, -1).astype(F32)

    for l in range(depth):
        gate_bias = jnp.zeros((1, 128), F32)
        gate_bias = gate_bias.at[0, IGATE_LANE:IGATE_LANE + HEADS].set(mlstm_igate_bias[l])
        gate_bias = gate_bias.at[0, FGATE_LANE:FGATE_LANE + HEADS].set(mlstm_fgate_bias[l])

        main, tail = _in_proj(x2, row(mix_norm[l]), _regroup_w_in(w_in[l]), tm_proj, 512)
        y_b = _mlstm(main, tail, gate_bias, row(mlstm_head_norm[l]), bsz, seq)
        q, k, v = _mla_prep(tail, cs, row(mla_q_a_norm[l]),
                            _heads_last_split(mla_w_uq[l], MLA_NOPE, MLA_ROPE),
                            row(mla_kv_a_norm[l]),
                            _heads_last_split(mla_w_ukv[l], MLA_NOPE, MLA_V),
                            row(mla_q_norm[l]), row(mla_k_norm[l]), bsz, seq, tm)
        y_c = _attention(q, k, v, tq, tq).reshape(t, D_MODEL)
        x2 = _merge(main, y_b, y_c, x2, conv_w[l].astype(F32), w_branch[l].astype(BF16),
                    w_out[l].astype(BF16), seq, tm)
        x2 = _mlp(x2, row(mlp_norm[l]), w_up[l].astype(BF16), w_down[l].astype(BF16), tm)
    return x2.reshape(bsz, seq, D_MODEL)
```

```python
import functools

import jax
import jax.numpy as jnp
import numpy as np
from jax import lax
from jax.experimental import pallas as pl
from jax.experimental.pallas import tpu as pltpu

F32 = jnp.float32
BF16 = jnp.bfloat16

D_MODEL = 1024
N_BRANCH = 3
CONV_WIDTH = 3
HEADS = 8
MLSTM_DK = 64
MLSTM_DV = 128
CHUNK = 128
GATE_CAP = 15.0
MLA_NOPE = 128
MLA_ROPE = 64
MLA_V = 128
Q_LORA = 256
KV_LORA = 128
ROPE_BASE = 10000.0
D_FF = 4 * D_MODEL
NORM_EPS = 1e-6
QK_PAD = 256

MAIN_COLS = 9 * D_MODEL
TAIL_COLS = 512
GATE_COL0 = 6 * D_MODEL
TAIL_KROPE = Q_LORA + KV_LORA
IGATE_LANE = MLA_ROPE
FGATE_LANE = MLA_ROPE + HEADS

ATT_T = 512
Q_SCALE = (MLA_NOPE + MLA_ROPE) ** -0.5 * 1.4426950408889634
MASK_VALUE = -1e30

VMEM_LIMIT = 56 * 1024 * 1024


def _rms(x, g):
    return x * lax.rsqrt(jnp.mean(x * x, axis=-1, keepdims=True) + NORM_EPS) * g


def _dot(a, b):
    return jnp.dot(a, b, preferred_element_type=F32)


def _dot_nt(a, b):
    return lax.dot_general(a, b, (((1,), (1,)), ((), ())), preferred_element_type=F32)


def _dot_tn(a, b):
    return lax.dot_general(a, b, (((0,), (0,)), ((), ())), preferred_element_type=F32)


def _rope_table_kernel(pos_col_ref, pos_row_ref, freq_row_ref, freq_col_ref, cs_ref, cst_ref):
    ang = pos_col_ref[...].astype(F32) * freq_row_ref[...]
    c, s = jnp.cos(ang), jnp.sin(ang)
    cs_ref[...] = jnp.concatenate([c, c, -s, s], axis=-1)
    ang_t = freq_col_ref[...] * pos_row_ref[...].astype(F32)
    cst_ref[...] = jnp.concatenate([jnp.cos(ang_t), jnp.sin(ang_t)], axis=0)


def _rope_table(positions, tm):
    t = positions.size
    half = MLA_ROPE // 2
    inv_freq = (np.float32(ROPE_BASE) ** (-np.arange(0, MLA_ROPE, 2, dtype=np.float32) / np.float32(MLA_ROPE)))
    inv_freq = jnp.asarray(inv_freq.astype(np.float32))
    return pl.pallas_call(
        _rope_table_kernel,
        grid=(t // tm,),
        in_specs=[pl.BlockSpec((tm, 1), lambda i: (i, 0)),
                  pl.BlockSpec((1, tm), lambda i: (0, i)),
                  pl.BlockSpec((1, half), lambda i: (0, 0)),
                  pl.BlockSpec((half, 1), lambda i: (0, 0))],
        out_specs=[pl.BlockSpec((tm, 2 * MLA_ROPE), lambda i: (i, 0)),
                   pl.BlockSpec((MLA_ROPE, tm), lambda i: (0, i))],
        out_shape=[jax.ShapeDtypeStruct((t, 2 * MLA_ROPE), F32),
                   jax.ShapeDtypeStruct((MLA_ROPE, t), F32)],
        name="rope_table",
    )(positions.reshape(t, 1), positions.reshape(1, t), inv_freq.reshape(1, half), inv_freq.reshape(half, 1))


def _in_proj_kernel(x_ref, g_ref, w_ref, main_ref, tail_ref, xn_ref, *, n_main, n_gate0):
    j = pl.program_id(1)

    @pl.when(j == 0)
    def _():
        xn_ref[...] = _rms(x_ref[...], g_ref[...]).astype(BF16)

    @pl.when(j < n_gate0)
    def _():
        main_ref[...] = _dot(xn_ref[...], w_ref[...]).astype(BF16)

    @pl.when(jnp.logical_and(j >= n_gate0, j < n_main))
    def _():
        main_ref[...] = jax.nn.sigmoid(_dot(xn_ref[...], w_ref[...])).astype(BF16)

    @pl.when(j == n_main)
    def _():
        tail_ref[...] = _dot(xn_ref[...], w_ref[...])


def _in_proj(x2, g, w_all, tm, tn):
    t = x2.shape[0]
    n_main = MAIN_COLS // tn
    kern = functools.partial(_in_proj_kernel, n_main=n_main, n_gate0=GATE_COL0 // tn)
    return pl.pallas_call(
        kern,
        grid=(t // tm, n_main + 1),
        in_specs=[pl.BlockSpec((tm, D_MODEL), lambda i, j: (i, 0)),
                  pl.BlockSpec((1, D_MODEL), lambda i, j: (0, 0)),
                  pl.BlockSpec((D_MODEL, tn), lambda i, j: (0, j))],
        out_specs=[pl.BlockSpec((tm, tn), lambda i, j: (i, jnp.minimum(j, n_main - 1))),
                   pl.BlockSpec((tm, TAIL_COLS), lambda i, j: (i, 0))],
        out_shape=[jax.ShapeDtypeStruct((t, MAIN_COLS), BF16),
                   jax.ShapeDtypeStruct((t, TAIL_COLS), F32)],
        scratch_shapes=[pltpu.VMEM((tm, D_MODEL), BF16)],
        compiler_params=pltpu.CompilerParams(
            dimension_semantics=("parallel", "arbitrary"), vmem_limit_bytes=VMEM_LIMIT),
        name="in_proj",
    )(x2, g, w_all)


def _lane_scan(x, op, lane):
    s = 1
    while s < CHUNK:
        shifted = pltpu.roll(x, s, 1)
        x = jnp.where(lane >= s, op(x, shifted), x)
        s *= 2
    return x


def _mlstm_kernel(q_ref, k_ref, v_ref, o_ref, gate_ref, bias_ref, hn_ref, y_ref, c_ref, m_ref):
    L = CHUNK

    @pl.when(pl.program_id(1) == 0)
    def _():
        c_ref[...] = jnp.zeros_like(c_ref)
        m_ref[...] = jnp.zeros_like(m_ref)

    gt = (gate_ref[...] + bias_ref[...]).T
    log_i = GATE_CAP * jnp.tanh(gt[IGATE_LANE:IGATE_LANE + HEADS] * (1.0 / GATE_CAP))
    zf = GATE_CAP * jnp.tanh(gt[FGATE_LANE:FGATE_LANE + HEADS] * (1.0 / GATE_CAP))
    log_f = jnp.minimum(zf, 0.0) - jnp.log1p(jnp.exp(-jnp.abs(zf)))
    lane = lax.broadcasted_iota(jnp.int32, (HEADS, L), 1)
    b = _lane_scan(log_f, jnp.add, lane)
    m_prev = m_ref[...]
    b_last = jnp.broadcast_to(b[:, L - 1:L], (HEADS, L))
    rowd = log_i - b
    big_m = jnp.maximum(_lane_scan(rowd, jnp.maximum, lane), m_prev)
    inter = jnp.exp(m_prev - big_m)
    e_neg_m = jnp.exp(-(b + big_m))
    a = b_last + rowd
    m_loc = jnp.broadcast_to(jnp.max(a, axis=1, keepdims=True), (HEADS, L))
    w = jnp.exp(a - m_loc)
    m_new = jnp.maximum(b_last + m_prev, m_loc)
    s_old = jnp.exp(b_last + m_prev - m_new)
    s_loc = jnp.exp(m_loc - m_new)
    m_ref[...] = m_new

    stack = jnp.concatenate([big_m, inter, e_neg_m, w, jnp.zeros((L - 4 * HEADS, L), F32)], axis=0)
    cols = stack.T

    tri = (lax.broadcasted_iota(jnp.int32, (L, L), 0) >= lax.broadcasted_iota(jnp.int32, (L, L), 1))
    ones_col = (lax.broadcasted_iota(jnp.int32, (L, MLSTM_DV), 1) == 0).astype(BF16)
    q_all = q_ref[...] * (MLSTM_DK ** -0.5)

    for h in range(HEADS):
        qh = q_all[:, h * MLSTM_DK:(h + 1) * MLSTM_DK]
        kh = k_ref[:, h * MLSTM_DK:(h + 1) * MLSTM_DK]
        vh = v_ref[:, h * MLSTM_DV:(h + 1) * MLSTM_DV]
        v_aug = jnp.concatenate([vh, ones_col], axis=1)
        big_m_c = cols[:, h:h + 1]
        inter_c = cols[:, HEADS + h:HEADS + h + 1]
        e_neg_c = cols[:, 2 * HEADS + h:2 * HEADS + h + 1]
        w_c = cols[:, 3 * HEADS + h:3 * HEADS + h + 1]

        s = _dot_nt(qh, kh)
        dmat = jnp.where(tri, jnp.exp(rowd[h:h + 1, :] - big_m_c), 0.0)
        p = (s * dmat).astype(BF16)
        c_prev = c_ref[h]
        num_aug = _dot(p, v_aug) + inter_c * _dot(qh, c_prev.astype(BF16))
        num = num_aug[:, :MLSTM_DV]
        den = num_aug[:, MLSTM_DV:MLSTM_DV + 1]
        hh = num / jnp.maximum(jnp.abs(den), e_neg_c)
        hh = _rms(hh, hn_ref[:, h * MLSTM_DV:(h + 1) * MLSTM_DV])
        og = jax.nn.sigmoid(o_ref[:, h * MLSTM_DV:(h + 1) * MLSTM_DV].astype(F32))
        y_ref[:, h * MLSTM_DV:(h + 1) * MLSTM_DV] = (hh * og).astype(y_ref.dtype)

        wv = (w_c * v_aug.astype(F32)).astype(BF16)
        c_loc = _dot_tn(kh, wv)
        c_ref[h] = s_old[h:h + 1, :1] * c_prev + s_loc[h:h + 1, :1] * c_loc


def _mlstm(main, tail, gate_bias, head_norm, bsz, seq):
    t = main.shape[0]
    nc = seq // CHUNK
    row = lambda b, c: b * nc + c
    return pl.pallas_call(
        _mlstm_kernel,
        grid=(bsz, nc),
        in_specs=[pl.BlockSpec((CHUNK, 512), lambda b, c: (row(b, c), 6)),
                  pl.BlockSpec((CHUNK, 512), lambda b, c: (row(b, c), 7)),
                  pl.BlockSpec((CHUNK, D_MODEL), lambda b, c: (row(b, c), 4)),
                  pl.BlockSpec((CHUNK, D_MODEL), lambda b, c: (row(b, c), 5)),
                  pl.BlockSpec((CHUNK, 128), lambda b, c: (row(b, c), TAIL_KROPE // 128)),
                  pl.BlockSpec((1, 128), lambda b, c: (0, 0)),
                  pl.BlockSpec((1, D_MODEL), lambda b, c: (0, 0))],
        out_specs=pl.BlockSpec((CHUNK, D_MODEL), lambda b, c: (row(b, c), 0)),
        out_shape=jax.ShapeDtypeStruct((t, D_MODEL), BF16),
        scratch_shapes=[pltpu.VMEM((HEADS, MLSTM_DK, 2 * MLSTM_DV), F32),
                        pltpu.VMEM((HEADS, CHUNK), F32)],
        compiler_params=pltpu.CompilerParams(dimension_semantics=("parallel", "arbitrary")),
        name="mlstm",
    )(main, main, main, main, tail, gate_bias, head_norm)


def _rms0(x, g_col):
    return x * lax.rsqrt(jnp.mean(x * x, axis=0, keepdims=True) + NORM_EPS) * g_col


def _rope(x, cs):
    half = MLA_ROPE // 2
    rot = jnp.concatenate([x[:, half:], x[:, :half]], axis=-1)
    return x * cs[:, :MLA_ROPE] + rot * cs[:, MLA_ROPE:]


def _mla_prep_kernel(tail_ref, cs_ref, cst_ref, qa_ref, wqt_ref, kva_ref, wk_ref, wvt_ref, qn_ref, kn_ref,
                     qt_out, k_out, vt_out):
    tm = tail_ref.shape[0]
    half = MLA_ROPE // 2
    cq = _rms(tail_ref[:, :Q_LORA], qa_ref[...]).astype(BF16)
    ckv = _rms(tail_ref[:, Q_LORA:Q_LORA + KV_LORA], kva_ref[...]).astype(BF16)
    qft = _dot_nt(wqt_ref[...], cq)
    kf = _dot(ckv, wk_ref[...])
    vft = _dot_nt(wvt_ref[...], ckv)
    cos_t, sin_t = cst_ref[:half, :], cst_ref[half:, :]
    qn_g, qp_g = qn_ref[:MLA_NOPE, :], qn_ref[MLA_NOPE:, :]
    kn_g, kp_g = kn_ref[:, :MLA_NOPE], kn_ref[:, MLA_NOPE:]
    k_pe = _rope(_rms(tail_ref[:, TAIL_KROPE:TAIL_KROPE + MLA_ROPE], kp_g), cs_ref[...]).astype(BF16)
    pe0 = HEADS * MLA_NOPE
    for h in range(HEADS):
        q_nope = _rms0(qft[h * MLA_NOPE:(h + 1) * MLA_NOPE], qn_g)
        q_pe = _rms0(qft[pe0 + h * MLA_ROPE:pe0 + (h + 1) * MLA_ROPE], qp_g)
        x1, x2 = q_pe[:half], q_pe[half:]
        q_rot = jnp.concatenate([x1 * cos_t - x2 * sin_t, x1 * sin_t + x2 * cos_t], axis=0)
        qt_out[0, h, 0] = (jnp.concatenate([q_nope, q_rot, jnp.zeros((QK_PAD - MLA_NOPE - MLA_ROPE, tm), F32)],
                                           axis=0) * Q_SCALE).astype(BF16)
        k_nope = _rms(kf[:, h * MLA_NOPE:(h + 1) * MLA_NOPE], kn_g)
        k_out[0, h] = jnp.concatenate(
            [k_nope.astype(BF16), k_pe, jnp.zeros((tm, QK_PAD - MLA_NOPE - MLA_ROPE), BF16)], axis=-1)
        vt_out[0, h, 0] = vft[h * MLA_V:(h + 1) * MLA_V].astype(BF16)


def _mla_prep(tail, cs, cst, qa, wqt, kva, wk, wvt, qn_col, kn, bsz, seq):
    tm = ATT_T
    nt = seq // tm
    full = lambda shape: pl.BlockSpec(shape, lambda i: (0,) * len(shape))
    blocked = lambda rows: pl.BlockSpec((1, HEADS, 1, rows, tm), lambda i: (i // nt, 0, i % nt, 0, 0))
    return pl.pallas_call(
        _mla_prep_kernel,
        grid=(bsz * nt,),
        in_specs=[pl.BlockSpec((tm, TAIL_COLS), lambda i: (i, 0)),
                  pl.BlockSpec((tm, 2 * MLA_ROPE), lambda i: (i, 0)),
                  pl.BlockSpec((MLA_ROPE, tm), lambda i: (0, i)),
                  full((1, Q_LORA)), full(wqt.shape), full((1, KV_LORA)), full(wk.shape), full(wvt.shape),
                  full((MLA_NOPE + MLA_ROPE, 1)), full((1, MLA_NOPE + MLA_ROPE))],
        out_specs=[blocked(QK_PAD),
                   pl.BlockSpec((1, HEADS, tm, QK_PAD), lambda i: (i // nt, 0, i % nt, 0)),
                   blocked(MLA_V)],
        out_shape=[jax.ShapeDtypeStruct((bsz, HEADS, nt, QK_PAD, tm), BF16),
                   jax.ShapeDtypeStruct((bsz, HEADS, seq, QK_PAD), BF16),
                   jax.ShapeDtypeStruct((bsz, HEADS, nt, MLA_V, tm), BF16)],
        compiler_params=pltpu.CompilerParams(dimension_semantics=("parallel",),
                                             vmem_limit_bytes=VMEM_LIMIT),
        name="mla_prep",
    )(tail, cs, cst, qa, wqt, kva, wk, wvt, qn_col, kn)


def _attn_kernel(qt_ref, k_ref, vt_ref, o_ref, s_scr, p_scr, m_all, l_all, acc_all, *, nq):
    t = ATT_T

    def qk(qi, j):
        k0 = pl.multiple_of(j * t, t)
        return _dot(k_ref[0, 0, pl.ds(k0, t), :], qt_ref[0, 0, qi])

    def stage(prev, cur, nxt, s_cur, p_prev, masked):
        (pq, pj), (qi, _), (nqi, nj) = prev, cur, nxt
        acc_all[pq] += _dot(vt_ref[0, 0, pj], p_prev)
        if masked:
            keep = lax.broadcasted_iota(jnp.int32, (t, t), 0) <= lax.broadcasted_iota(jnp.int32, (t, t), 1)
            s_cur = jnp.where(keep, s_cur, MASK_VALUE)
        m_old = m_all[qi]
        m_new = jnp.maximum(m_old, jnp.max(s_cur, axis=0, keepdims=True))
        alpha = jnp.exp2(m_old - m_new)
        p = jnp.exp2(s_cur - m_new)
        l_all[qi] = alpha * l_all[qi] + jnp.sum(p, axis=0, keepdims=True)
        m_all[qi] = m_new
        acc_all[qi] = alpha * acc_all[qi]
        s_next = qk(jnp.minimum(nqi, nq - 1), jnp.minimum(nj, nq - 1))
        return s_next, p.astype(BF16)

    def advance_lower(qi, j):
        wrap = j + 1 == qi
        return jnp.where(wrap, qi + 1, qi), jnp.where(wrap, 0, j + 1)

    def advance_diag(qi, j):
        return qi + 1, j + 1

    def run_phase(n_steps, first, advance, masked):
        if n_steps == 0:
            return
        s_scr[...] = qk(*first)
        p_scr[...] = jnp.zeros_like(p_scr)

        def body(_, carry):
            prev, cur = carry
            s_cur, p_prev = s_scr[...], p_scr[...]
            for _u in range(2):
                nxt = advance(*cur)
                s_cur, p_prev = stage(prev, cur, nxt, s_cur, p_prev, masked)
                prev, cur = cur, nxt
            s_scr[...] = s_cur
            p_scr[...] = p_prev
            return prev, cur

        first = (jnp.int32(first[0]), jnp.int32(first[1]))
        prev, cur = lax.fori_loop(0, n_steps // 2, body, (first, first))
        if n_steps % 2:
            _, p_last = stage(prev, cur, advance(*cur), s_scr[...], p_scr[...], masked)
            p_scr[...] = p_last
            prev = cur
        acc_all[prev[0]] += _dot(vt_ref[0, 0, prev[1]], p_scr[...])

    m_all[...] = jnp.full_like(m_all, MASK_VALUE)
    l_all[...] = jnp.zeros_like(l_all)
    acc_all[...] = jnp.zeros_like(acc_all)
    run_phase(nq * (nq - 1) // 2, (1, 0), advance_lower, False)
    run_phase(nq, (0, 0), advance_diag, True)
    for qi in range(nq):
        o_ref[0, qi * t:(qi + 1) * t, :] = (acc_all[qi] / l_all[qi]).T.astype(o_ref.dtype)


def _attention(qt, k, vt):
    bsz, _, seq, _ = k.shape
    nq = seq // ATT_T
    kern = functools.partial(_attn_kernel, nq=nq)
    return pl.pallas_call(
        kern,
        grid=(bsz, HEADS),
        in_specs=[pl.BlockSpec((1, 1, nq, QK_PAD, ATT_T), lambda b, h: (b, h, 0, 0, 0)),
                  pl.BlockSpec((1, 1, seq, QK_PAD), lambda b, h: (b, h, 0, 0)),
                  pl.BlockSpec((1, 1, nq, MLA_V, ATT_T), lambda b, h: (b, h, 0, 0, 0))],
        out_specs=pl.BlockSpec((1, seq, MLA_V), lambda b, h: (b, 0, h)),
        out_shape=jax.ShapeDtypeStruct((bsz, seq, HEADS * MLA_V), BF16),
        scratch_shapes=[pltpu.VMEM((ATT_T, ATT_T), F32), pltpu.VMEM((ATT_T, ATT_T), BF16),
                        pltpu.VMEM((nq, 1, ATT_T), F32), pltpu.VMEM((nq, 1, ATT_T), F32),
                        pltpu.VMEM((nq, MLA_V, ATT_T), F32)],
        compiler_params=pltpu.CompilerParams(dimension_semantics=("parallel", "parallel"),
                                             vmem_limit_bytes=VMEM_LIMIT),
        name="attn",
    )(qt, k, vt)


def _merge_kernel(cb_ref, cc_ref, cu_ref, gate_ref, yb_ref, yc_ref, x_ref, cw_ref, wb_ref, wo_ref,
                  out_ref, zprev_ref, *, tiles_per_seq):
    tm = x_ref.shape[0]

    @pl.when(pl.program_id(0) % tiles_per_seq == 0)
    def _():
        zprev_ref[...] = jnp.zeros_like(zprev_ref)

    z = cc_ref[...].astype(F32) * cu_ref[...].astype(F32)
    row = lax.broadcasted_iota(jnp.int32, (tm, D_MODEL), 0)
    p1 = zprev_ref[7:8, :]
    p2 = zprev_ref[6:7, :]
    z1 = jnp.where(row == 0, p1, pltpu.roll(z, 1, 0))
    z2 = jnp.where(row == 0, p2, jnp.where(row == 1, p1, pltpu.roll(z, 2, 0)))
    zprev_ref[...] = z[tm - 8:, :]
    cw = cw_ref[...]
    y_a = cb_ref[...].astype(F32) * (cw[0:1] * z2 + cw[1:2] * z1 + cw[2:3] * z)

    g = gate_ref[...]
    merged = g[:, :D_MODEL].astype(F32) * _dot(y_a.astype(BF16), wb_ref[0])
    merged += g[:, D_MODEL:2 * D_MODEL].astype(F32) * _dot(yb_ref[...], wb_ref[1])
    merged += g[:, 2 * D_MODEL:].astype(F32) * _dot(yc_ref[...], wb_ref[2])
    out_ref[...] = x_ref[...] + _dot(merged.astype(BF16), wo_ref[...])


def _merge(main, y_b, y_c, x2, conv_w, w_branch, w_out, seq, tm):
    t = x2.shape[0]
    kern = functools.partial(_merge_kernel, tiles_per_seq=seq // tm)
    col = lambda c: pl.BlockSpec((tm, D_MODEL), lambda i: (i, c))
    return pl.pallas_call(
        kern,
        grid=(t // tm,),
        in_specs=[col(0), col(1), col(2),
                  pl.BlockSpec((tm, N_BRANCH * D_MODEL), lambda i: (i, GATE_COL0 // (N_BRANCH * D_MODEL))),
                  col(0), col(0), col(0),
                  pl.BlockSpec((CONV_WIDTH, D_MODEL), lambda i: (0, 0)),
                  pl.BlockSpec((N_BRANCH, D_MODEL, D_MODEL), lambda i: (0, 0, 0)),
                  pl.BlockSpec((D_MODEL, D_MODEL), lambda i: (0, 0))],
        out_specs=col(0),
        out_shape=jax.ShapeDtypeStruct((t, D_MODEL), F32),
        scratch_shapes=[pltpu.VMEM((8, D_MODEL), F32)],
        compiler_params=pltpu.CompilerParams(dimension_semantics=("arbitrary",),
                                             vmem_limit_bytes=VMEM_LIMIT),
        name="merge",
    )(main, main, main, main, y_b, y_c, x2, conv_w, w_branch, w_out)


def _mlp_kernel(x_ref, g_ref, wu_ref, wd_ref, out_ref, *, n_chunk):
    x = x_ref[...]
    h = _rms(x, g_ref[...]).astype(BF16)
    ck = D_FF // n_chunk
    acc = x
    for c in range(n_chunk):
        u = jnp.maximum(_dot(h, wu_ref[:, c * ck:(c + 1) * ck]), 0.0)
        acc = acc + _dot((u * u).astype(BF16), wd_ref[c * ck:(c + 1) * ck, :])
    out_ref[...] = acc


def _mlp(x2, g, w_up, w_down, tm):
    t = x2.shape[0]
    kern = functools.partial(_mlp_kernel, n_chunk=4)
    return pl.pallas_call(
        kern,
        grid=(t // tm,),
        in_specs=[pl.BlockSpec((tm, D_MODEL), lambda i: (i, 0)),
                  pl.BlockSpec((1, D_MODEL), lambda i: (0, 0)),
                  pl.BlockSpec((D_MODEL, D_FF), lambda i: (0, 0)),
                  pl.BlockSpec((D_FF, D_MODEL), lambda i: (0, 0))],
        out_specs=pl.BlockSpec((tm, D_MODEL), lambda i: (i, 0)),
        out_shape=jax.ShapeDtypeStruct((t, D_MODEL), F32),
        compiler_params=pltpu.CompilerParams(dimension_semantics=("parallel",),
                                             vmem_limit_bytes=VMEM_LIMIT),
        name="mlp",
    )(x2, g, w_up, w_down)


def _regroup_w_in(w_in):
    c = D_MODEL
    q0 = 3 * c
    i0 = q0 + 2 * HEADS * MLSTM_DK + 2 * HEADS * MLSTM_DV
    cq0 = i0 + 2 * HEADS
    ckv0 = cq0 + Q_LORA
    kr0 = ckv0 + KV_LORA
    g0 = kr0 + MLA_ROPE
    pad = jnp.zeros((D_MODEL, TAIL_COLS - (Q_LORA + KV_LORA + MLA_ROPE + 2 * HEADS)), w_in.dtype)
    return jnp.concatenate(
        [w_in[:, :i0], w_in[:, g0:],
         w_in[:, cq0:g0], w_in[:, i0:cq0], pad], axis=1).astype(BF16)


def _heads_last_split(w, first, second):
    k = w.shape[0]
    w3 = w.reshape(k, HEADS, first + second)
    return jnp.concatenate([w3[:, :, :first].reshape(k, HEADS * first),
                            w3[:, :, first:].reshape(k, HEADS * second)], axis=1).astype(BF16)


def _pick(n, candidates):
    for c in candidates:
        if n % c == 0:
            return c
    raise ValueError(f"no tile size for {n}")


def kernel(x, positions, mix_norm, w_in, conv_w, mlstm_igate_bias, mlstm_fgate_bias, mlstm_head_norm,
           mla_q_a_norm, mla_w_uq, mla_kv_a_norm, mla_w_ukv, mla_q_norm, mla_k_norm, w_branch, w_out,
           mlp_norm, w_up, w_down):
    bsz, seq, _ = x.shape
    depth = w_in.shape[0]
    t = bsz * seq
    assert seq % ATT_T == 0 and seq % CHUNK == 0
    tm_proj = _pick(t, (1024, 512, 256, 128))
    tm = _pick(seq, (512, 256, 128))

    x2 = x.reshape(t, D_MODEL)
    cs, cst = _rope_table(positions, ATT_T)
    row = lambda a: a.reshape(1, -1).astype(F32)

    for l in range(depth):
        gate_bias = jnp.zeros((1, 128), F32)
        gate_bias = gate_bias.at[0, IGATE_LANE:IGATE_LANE + HEADS].set(mlstm_igate_bias[l])
        gate_bias = gate_bias.at[0, FGATE_LANE:FGATE_LANE + HEADS].set(mlstm_fgate_bias[l])
        w_q = _heads_last_split(mla_w_uq[l], MLA_NOPE, MLA_ROPE)
        w_kv = _heads_last_split(mla_w_ukv[l], MLA_NOPE, MLA_V)

        main, tail = _in_proj(x2, row(mix_norm[l]), _regroup_w_in(w_in[l]), tm_proj, 512)
        y_b = _mlstm(main, tail, gate_bias, row(mlstm_head_norm[l]), bsz, seq)
        qt, k, vt = _mla_prep(tail, cs, cst, row(mla_q_a_norm[l]), w_q.T, row(mla_kv_a_norm[l]),
                              w_kv[:, :HEADS * MLA_NOPE], w_kv[:, HEADS * MLA_NOPE:].T,
                              mla_q_norm[l].reshape(-1, 1).astype(F32), row(mla_k_norm[l]), bsz, seq)
        y_c = _attention(qt, k, vt).reshape(t, D_MODEL)
        x2 = _merge(main, y_b, y_c, x2, conv_w[l].astype(F32), w_branch[l].astype(BF16),
                    w_out[l].astype(BF16), seq, tm)
        x2 = _mlp(x2, row(mlp_norm[l]), w_up[l].astype(BF16), w_down[l].astype(BF16), tm)
    return x2.reshape(bsz, seq, D_MODEL)
```

```python
import functools

import jax
import jax.numpy as jnp
import numpy as np
from jax import lax
from jax.experimental import pallas as pl
from jax.experimental.pallas import tpu as pltpu

F32 = jnp.float32
BF16 = jnp.bfloat16

D_MODEL = 1024
N_BRANCH = 3
CONV_WIDTH = 3
HEADS = 8
MLSTM_DK = 64
MLSTM_DV = 128
CHUNK = 128
GATE_CAP = 15.0
MLA_NOPE = 128
MLA_ROPE = 64
MLA_V = 128
Q_LORA = 256
KV_LORA = 128
ROPE_BASE = 10000.0
D_FF = 4 * D_MODEL
NORM_EPS = 1e-6
QK_PAD = 256

MAIN_COLS = 9 * D_MODEL
TAIL_COLS = 512
GATE_COL0 = 6 * D_MODEL
TAIL_KROPE = Q_LORA + KV_LORA
IGATE_LANE = MLA_ROPE
FGATE_LANE = MLA_ROPE + HEADS

IN_PROJ_TN = 1536
MLSTM_GROUP = 1
ATT_T = 512
Q_SCALE = (MLA_NOPE + MLA_ROPE) ** -0.5 * 1.4426950408889634
MASK_VALUE = -1e30

VMEM_LIMIT = 56 * 1024 * 1024


def _rms(x, g):
    return x * lax.rsqrt(jnp.mean(x * x, axis=-1, keepdims=True) + NORM_EPS) * g


def _dot(a, b):
    return jnp.dot(a, b, preferred_element_type=F32)


def _dot_nt(a, b):
    return lax.dot_general(a, b, (((1,), (1,)), ((), ())), preferred_element_type=F32)


def _dot_tn(a, b):
    return lax.dot_general(a, b, (((0,), (0,)), ((), ())), preferred_element_type=F32)


def _rope_table_kernel(pos_col_ref, pos_row_ref, freq_row_ref, freq_col_ref, cs_ref, cst_ref):
    ang = pos_col_ref[...].astype(F32) * freq_row_ref[...]
    c, s = jnp.cos(ang), jnp.sin(ang)
    cs_ref[...] = jnp.concatenate([c, c, -s, s], axis=-1)
    ang_t = freq_col_ref[...] * pos_row_ref[...].astype(F32)
    cst_ref[...] = jnp.concatenate([jnp.cos(ang_t), jnp.sin(ang_t)], axis=0)


def _rope_table(positions, tm):
    t = positions.size
    half = MLA_ROPE // 2
    inv_freq = (np.float32(ROPE_BASE) ** (-np.arange(0, MLA_ROPE, 2, dtype=np.float32) / np.float32(MLA_ROPE)))
    inv_freq = jnp.asarray(inv_freq.astype(np.float32))
    return pl.pallas_call(
        _rope_table_kernel,
        grid=(t // tm,),
        in_specs=[pl.BlockSpec((tm, 1), lambda i: (i, 0)),
                  pl.BlockSpec((1, tm), lambda i: (0, i)),
                  pl.BlockSpec((1, half), lambda i: (0, 0)),
                  pl.BlockSpec((half, 1), lambda i: (0, 0))],
        out_specs=[pl.BlockSpec((tm, 2 * MLA_ROPE), lambda i: (i, 0)),
                   pl.BlockSpec((MLA_ROPE, tm), lambda i: (0, i))],
        out_shape=[jax.ShapeDtypeStruct((t, 2 * MLA_ROPE), F32),
                   jax.ShapeDtypeStruct((MLA_ROPE, t), F32)],
        name="rope_table",
    )(positions.reshape(t, 1), positions.reshape(1, t), inv_freq.reshape(1, half), inv_freq.reshape(half, 1))


def _in_proj_kernel(x_ref, g_ref, w_ref, wt_ref, main_ref, tail_ref, xn_ref, *, n_gate0):
    j = pl.program_id(1)

    @pl.when(j == 0)
    def _():
        xn = _rms(x_ref[...], g_ref[...]).astype(BF16)
        xn_ref[...] = xn
        tail_ref[...] = _dot(xn, wt_ref[...])

    @pl.when(j < n_gate0)
    def _():
        main_ref[...] = _dot(xn_ref[...], w_ref[...]).astype(BF16)

    @pl.when(j >= n_gate0)
    def _():
        main_ref[...] = jax.nn.sigmoid(_dot(xn_ref[...], w_ref[...])).astype(BF16)


def _in_proj(x2, g, w_main, w_tail, tm, tn):
    t = x2.shape[0]
    assert GATE_COL0 % tn == 0 and MAIN_COLS % tn == 0
    kern = functools.partial(_in_proj_kernel, n_gate0=GATE_COL0 // tn)
    return pl.pallas_call(
        kern,
        grid=(t // tm, MAIN_COLS // tn),
        in_specs=[pl.BlockSpec((tm, D_MODEL), lambda i, j: (i, 0)),
                  pl.BlockSpec((1, D_MODEL), lambda i, j: (0, 0)),
                  pl.BlockSpec((D_MODEL, tn), lambda i, j: (0, j)),
                  pl.BlockSpec((D_MODEL, TAIL_COLS), lambda i, j: (0, 0))],
        out_specs=[pl.BlockSpec((tm, tn), lambda i, j: (i, j)),
                   pl.BlockSpec((tm, TAIL_COLS), lambda i, j: (i, 0))],
        out_shape=[jax.ShapeDtypeStruct((t, MAIN_COLS), BF16),
                   jax.ShapeDtypeStruct((t, TAIL_COLS), F32)],
        scratch_shapes=[pltpu.VMEM((tm, D_MODEL), BF16)],
        compiler_params=pltpu.CompilerParams(
            dimension_semantics=("parallel", "arbitrary"), vmem_limit_bytes=VMEM_LIMIT),
        name="in_proj",
    )(x2, g, w_main, w_tail)


def _time_scan(x, op, row):
    s = 1
    while s < CHUNK:
        x = jnp.where(row >= s, op(x, pltpu.roll(x, s, 0)), x)
        s *= 2
    return x


def _mlstm_kernel(q_ref, k_ref, v_ref, o_ref, gate_ref, bias_ref, hn_ref, y_ref, c_ref, m_ref, *, n_chunk):
    L = CHUNK

    @pl.when(pl.program_id(1) == 0)
    def _():
        c_ref[...] = jnp.zeros_like(c_ref)
        m_ref[...] = jnp.zeros_like(m_ref)

    row = lax.broadcasted_iota(jnp.int32, (L, 128), 0)
    lane = lax.broadcasted_iota(jnp.int32, (L, 128), 1)
    gate_lanes = jnp.logical_and(lane >= IGATE_LANE, lane < FGATE_LANE + HEADS)
    tri = (lax.broadcasted_iota(jnp.int32, (L, L), 0) >= lax.broadcasted_iota(jnp.int32, (L, L), 1))
    ones_col = (lax.broadcasted_iota(jnp.int32, (L, MLSTM_DV), 1) == 0).astype(BF16)
    m_prev = m_ref[...]
    c_state = [c_ref[h] for h in range(HEADS)]

    for g in range(n_chunk):
        rows = slice(g * L, (g + 1) * L)
        pre = jnp.where(gate_lanes, gate_ref[rows, :] + bias_ref[...], 0.0)
        capped = GATE_CAP * jnp.tanh(pre * (1.0 / GATE_CAP))
        log_i = capped
        log_f = jnp.minimum(capped, 0.0) - jnp.log1p(jnp.exp(-jnp.abs(capped)))
        log_f = pltpu.roll(log_f, 128 - HEADS, 1)
        b = _time_scan(log_f, jnp.add, row)
        b_last = b[L - 1:L, :]
        rowd = log_i - b
        big_m = jnp.maximum(_time_scan(rowd, jnp.maximum, row), m_prev)
        e_neg_m = jnp.exp(-(b + big_m))
        a = b_last + rowd
        m_loc = jnp.max(a, axis=0, keepdims=True)
        w = jnp.exp(a - m_loc)
        m_new = jnp.maximum(b_last + m_prev, m_loc)
        s_old = jnp.exp(b_last + m_prev - m_new)
        s_loc = jnp.exp(m_loc - m_new)
        m_prev_g, m_prev = m_prev, m_new
        rowd_t = rowd.T
        w_t = w.T
        q_all = q_ref[rows, :] * (MLSTM_DK ** -0.5)
        k_t = k_ref[rows, :].astype(F32).T

        for h in range(HEADS):
            hk = slice(h * MLSTM_DK, (h + 1) * MLSTM_DK)
            hv = slice(h * MLSTM_DV, (h + 1) * MLSTM_DV)
            gl = slice(IGATE_LANE + h, IGATE_LANE + h + 1)
            qh = q_all[:, hk]
            kh_t = k_t[hk, :]
            v_aug = jnp.concatenate([v_ref[rows, hv], ones_col], axis=1)
            c_prev = c_state[h]

            big_m_b = jnp.broadcast_to(big_m[:, gl], (L, L))
            dmat = jnp.where(tri, jnp.exp(rowd_t[gl, :] - big_m_b), 0.0)
            p = (_dot(qh, kh_t.astype(BF16)) * dmat).astype(BF16)
            q_state = (qh.astype(F32) * jnp.exp(m_prev_g[:, gl] - big_m_b[:, :MLSTM_DK])).astype(BF16)
            lhs = jnp.concatenate([p, q_state, jnp.zeros((L, L - MLSTM_DK), BF16)], axis=1)
            rhs = jnp.concatenate([v_aug, c_prev.astype(BF16),
                                   jnp.zeros((L - MLSTM_DK, 2 * MLSTM_DV), BF16)], axis=0)
            num_aug = _dot(lhs, rhs)
            num = num_aug[:, :MLSTM_DV]
            den = num_aug[:, MLSTM_DV:MLSTM_DV + 1]
            d = jnp.maximum(jnp.abs(den), e_neg_m[:, gl])
            scale = lax.rsqrt(jnp.mean(num * num, axis=-1, keepdims=True) + NORM_EPS * d * d)
            og = jax.nn.sigmoid(o_ref[rows, hv].astype(F32))
            y_ref[rows, hv] = (num * scale * hn_ref[:, hv] * og).astype(y_ref.dtype)

            c_loc = _dot((kh_t * w_t[gl, :]).astype(BF16), v_aug)
            c_state[h] = s_old[:, gl] * c_prev + s_loc[:, gl] * c_loc

    m_ref[...] = m_prev
    for h in range(HEADS):
        c_ref[h] = c_state[h]


def _mlstm(main, tail, gate_bias, head_norm, bsz, seq, n_chunk):
    t = main.shape[0]
    rows = n_chunk * CHUNK
    nc = seq // rows
    row = lambda b, c: b * nc + c
    return pl.pallas_call(
        functools.partial(_mlstm_kernel, n_chunk=n_chunk),
        grid=(bsz, nc),
        in_specs=[pl.BlockSpec((rows, 512), lambda b, c: (row(b, c), 6)),
                  pl.BlockSpec((rows, 512), lambda b, c: (row(b, c), 7)),
                  pl.BlockSpec((rows, D_MODEL), lambda b, c: (row(b, c), 4)),
                  pl.BlockSpec((rows, D_MODEL), lambda b, c: (row(b, c), 5)),
                  pl.BlockSpec((rows, 128), lambda b, c: (row(b, c), TAIL_KROPE // 128)),
                  pl.BlockSpec((1, 128), lambda b, c: (0, 0)),
                  pl.BlockSpec((1, D_MODEL), lambda b, c: (0, 0))],
        out_specs=pl.BlockSpec((rows, D_MODEL), lambda b, c: (row(b, c), 0)),
        out_shape=jax.ShapeDtypeStruct((t, D_MODEL), BF16),
        scratch_shapes=[pltpu.VMEM((HEADS, MLSTM_DK, 2 * MLSTM_DV), F32),
                        pltpu.VMEM((1, 128), F32)],
        compiler_params=pltpu.CompilerParams(dimension_semantics=("parallel", "arbitrary")),
        name="mlstm",
    )(main, main, main, main, tail, gate_bias, head_norm)


def _rms0(x, g_col):
    return x * lax.rsqrt(jnp.mean(x * x, axis=0, keepdims=True) + NORM_EPS) * g_col


def _rope(x, cs):
    half = MLA_ROPE // 2
    rot = jnp.concatenate([x[:, half:], x[:, :half]], axis=-1)
    return x * cs[:, :MLA_ROPE] + rot * cs[:, MLA_ROPE:]


def _mla_prep_kernel(tail_ref, cs_ref, cst_ref, qa_ref, wqt_ref, kva_ref, wk_ref, wvt_ref, qn_ref, kn_ref,
                     qt_out, k_out, vt_out):
    tm = tail_ref.shape[0]
    half = MLA_ROPE // 2
    cq = _rms(tail_ref[:, :Q_LORA], qa_ref[...]).astype(BF16)
    ckv = _rms(tail_ref[:, Q_LORA:Q_LORA + KV_LORA], kva_ref[...]).astype(BF16)
    qft = _dot_nt(wqt_ref[...], cq)
    kf = _dot(ckv, wk_ref[...])
    vft = _dot_nt(wvt_ref[...], ckv)
    cos_t, sin_t = cst_ref[:half, :], cst_ref[half:, :]
    qn_g, qp_g = qn_ref[:MLA_NOPE, :], qn_ref[MLA_NOPE:, :]
    kn_g, kp_g = kn_ref[:, :MLA_NOPE], kn_ref[:, MLA_NOPE:]
    k_pe = _rope(_rms(tail_ref[:, TAIL_KROPE:TAIL_KROPE + MLA_ROPE], kp_g), cs_ref[...]).astype(BF16)
    pe0 = HEADS * MLA_NOPE
    for h in range(HEADS):
        q_nope = _rms0(qft[h * MLA_NOPE:(h + 1) * MLA_NOPE], qn_g)
        q_pe = _rms0(qft[pe0 + h * MLA_ROPE:pe0 + (h + 1) * MLA_ROPE], qp_g)
        x1, x2 = q_pe[:half], q_pe[half:]
        q_rot = jnp.concatenate([x1 * cos_t - x2 * sin_t, x1 * sin_t + x2 * cos_t], axis=0)
        qt_out[0, h, 0] = (jnp.concatenate([q_nope, q_rot, jnp.zeros((QK_PAD - MLA_NOPE - MLA_ROPE, tm), F32)],
                                           axis=0) * Q_SCALE).astype(BF16)
        k_nope = _rms(kf[:, h * MLA_NOPE:(h + 1) * MLA_NOPE], kn_g)
        k_out[0, h] = jnp.concatenate(
            [k_nope.astype(BF16), k_pe, jnp.zeros((tm, QK_PAD - MLA_NOPE - MLA_ROPE), BF16)], axis=-1)
        vt_out[0, h, 0] = vft[h * MLA_V:(h + 1) * MLA_V].astype(BF16)


def _mla_prep(tail, cs, cst, qa, wqt, kva, wk, wvt, qn_col, kn, bsz, seq):
    tm = ATT_T
    nt = seq // tm
    full = lambda shape: pl.BlockSpec(shape, lambda i: (0,) * len(shape))
    blocked = lambda rows: pl.BlockSpec((1, HEADS, 1, rows, tm), lambda i: (i // nt, 0, i % nt, 0, 0))
    return pl.pallas_call(
        _mla_prep_kernel,
        grid=(bsz * nt,),
        in_specs=[pl.BlockSpec((tm, TAIL_COLS), lambda i: (i, 0)),
                  pl.BlockSpec((tm, 2 * MLA_ROPE), lambda i: (i, 0)),
                  pl.BlockSpec((MLA_ROPE, tm), lambda i: (0, i)),
                  full((1, Q_LORA)), full(wqt.shape), full((1, KV_LORA)), full(wk.shape), full(wvt.shape),
                  full((MLA_NOPE + MLA_ROPE, 1)), full((1, MLA_NOPE + MLA_ROPE))],
        out_specs=[blocked(QK_PAD),
                   pl.BlockSpec((1, HEADS, tm, QK_PAD), lambda i: (i // nt, 0, i % nt, 0)),
                   blocked(MLA_V)],
        out_shape=[jax.ShapeDtypeStruct((bsz, HEADS, nt, QK_PAD, tm), BF16),
                   jax.ShapeDtypeStruct((bsz, HEADS, seq, QK_PAD), BF16),
                   jax.ShapeDtypeStruct((bsz, HEADS, nt, MLA_V, tm), BF16)],
        compiler_params=pltpu.CompilerParams(dimension_semantics=("parallel",),
                                             vmem_limit_bytes=VMEM_LIMIT),
        name="mla_prep",
    )(tail, cs, cst, qa, wqt, kva, wk, wvt, qn_col, kn)


def _attn_kernel(qt_ref, k_ref, vt_ref, o_ref, s_scr, p_scr, m_all, l_all, acc_all, *, nq):
    t = ATT_T

    def qk(qi, j):
        k0 = pl.multiple_of(j * t, t)
        return _dot(k_ref[0, 0, pl.ds(k0, t), :], qt_ref[0, 0, qi])

    def stage(prev, cur, nxt, s_cur, p_prev, masked):
        (pq, pj), (qi, _), (nqi, nj) = prev, cur, nxt
        acc_all[pq] += _dot(vt_ref[0, 0, pj], p_prev)
        if masked:
            keep = lax.broadcasted_iota(jnp.int32, (t, t), 0) <= lax.broadcasted_iota(jnp.int32, (t, t), 1)
            s_cur = jnp.where(keep, s_cur, MASK_VALUE)
        m_old = m_all[qi]
        m_new = jnp.maximum(m_old, jnp.max(s_cur, axis=0, keepdims=True))
        alpha = jnp.exp2(m_old - m_new)
        p = jnp.exp2(s_cur - m_new)
        l_all[qi] = alpha * l_all[qi] + jnp.sum(p, axis=0, keepdims=True)
        m_all[qi] = m_new
        acc_all[qi] = alpha * acc_all[qi]
        s_next = qk(jnp.minimum(nqi, nq - 1), jnp.minimum(nj, nq - 1))
        return s_next, p.astype(BF16)

    def advance_lower(qi, j):
        wrap = j + 1 == qi
        return jnp.where(wrap, qi + 1, qi), jnp.where(wrap, 0, j + 1)

    def advance_diag(qi, j):
        return qi + 1, j + 1

    def run_phase(n_steps, first, advance, masked):
        if n_steps == 0:
            return
        s_scr[...] = qk(*first)
        p_scr[...] = jnp.zeros_like(p_scr)

        def body(_, carry):
            prev, cur = carry
            s_cur, p_prev = s_scr[...], p_scr[...]
            for _u in range(2):
                nxt = advance(*cur)
                s_cur, p_prev = stage(prev, cur, nxt, s_cur, p_prev, masked)
                prev, cur = cur, nxt
            s_scr[...] = s_cur
            p_scr[...] = p_prev
            return prev, cur

        first = (jnp.int32(first[0]), jnp.int32(first[1]))
        prev, cur = lax.fori_loop(0, n_steps // 2, body, (first, first))
        if n_steps % 2:
            _, p_last = stage(prev, cur, advance(*cur), s_scr[...], p_scr[...], masked)
            p_scr[...] = p_last
            prev = cur
        acc_all[prev[0]] += _dot(vt_ref[0, 0, prev[1]], p_scr[...])

    m_all[...] = jnp.full_like(m_all, MASK_VALUE)
    l_all[...] = jnp.zeros_like(l_all)
    acc_all[...] = jnp.zeros_like(acc_all)
    run_phase(nq * (nq - 1) // 2, (1, 0), advance_lower, False)
    run_phase(nq, (0, 0), advance_diag, True)
    for qi in range(nq):
        o_ref[0, qi * t:(qi + 1) * t, :] = (acc_all[qi] / l_all[qi]).T.astype(o_ref.dtype)


def _attention(qt, k, vt):
    bsz, _, seq, _ = k.shape
    nq = seq // ATT_T
    kern = functools.partial(_attn_kernel, nq=nq)
    return pl.pallas_call(
        kern,
        grid=(bsz, HEADS),
        in_specs=[pl.BlockSpec((1, 1, nq, QK_PAD, ATT_T), lambda b, h: (b, h, 0, 0, 0)),
                  pl.BlockSpec((1, 1, seq, QK_PAD), lambda b, h: (b, h, 0, 0)),
                  pl.BlockSpec((1, 1, nq, MLA_V, ATT_T), lambda b, h: (b, h, 0, 0, 0))],
        out_specs=pl.BlockSpec((1, seq, MLA_V), lambda b, h: (b, 0, h)),
        out_shape=jax.ShapeDtypeStruct((bsz, seq, HEADS * MLA_V), BF16),
        scratch_shapes=[pltpu.VMEM((ATT_T, ATT_T), F32), pltpu.VMEM((ATT_T, ATT_T), BF16),
                        pltpu.VMEM((nq, 1, ATT_T), F32), pltpu.VMEM((nq, 1, ATT_T), F32),
                        pltpu.VMEM((nq, MLA_V, ATT_T), F32)],
        compiler_params=pltpu.CompilerParams(dimension_semantics=("parallel", "parallel"),
                                             vmem_limit_bytes=VMEM_LIMIT),
        name="attn",
    )(qt, k, vt)


def _merge_kernel(cb_ref, cc_ref, cu_ref, gate_ref, yb_ref, yc_ref, x_ref, cw_ref, wb_ref, wo_ref,
                  out_ref, zprev_ref, *, tiles_per_seq):
    tm = x_ref.shape[0]

    @pl.when(pl.program_id(0) % tiles_per_seq == 0)
    def _():
        zprev_ref[...] = jnp.zeros_like(zprev_ref)

    z = cc_ref[...].astype(F32) * cu_ref[...].astype(F32)
    row = lax.broadcasted_iota(jnp.int32, (tm, D_MODEL), 0)
    p1 = zprev_ref[7:8, :]
    p2 = zprev_ref[6:7, :]
    z1 = jnp.where(row == 0, p1, pltpu.roll(z, 1, 0))
    z2 = jnp.where(row == 0, p2, jnp.where(row == 1, p1, pltpu.roll(z, 2, 0)))
    zprev_ref[...] = z[tm - 8:, :]
    cw = cw_ref[...]
    y_a = cb_ref[...].astype(F32) * (cw[0:1] * z2 + cw[1:2] * z1 + cw[2:3] * z)

    g = gate_ref[...]
    merged = g[:, :D_MODEL].astype(F32) * _dot(y_a.astype(BF16), wb_ref[0])
    merged += g[:, D_MODEL:2 * D_MODEL].astype(F32) * _dot(yb_ref[...], wb_ref[1])
    merged += g[:, 2 * D_MODEL:].astype(F32) * _dot(yc_ref[...], wb_ref[2])
    out_ref[...] = x_ref[...] + _dot(merged.astype(BF16), wo_ref[...])


def _merge(main, y_b, y_c, x2, conv_w, w_branch, w_out, seq, tm):
    t = x2.shape[0]
    kern = functools.partial(_merge_kernel, tiles_per_seq=seq // tm)
    col = lambda c: pl.BlockSpec((tm, D_MODEL), lambda i: (i, c))
    return pl.pallas_call(
        kern,
        grid=(t // tm,),
        in_specs=[col(0), col(1), col(2),
                  pl.BlockSpec((tm, N_BRANCH * D_MODEL), lambda i: (i, GATE_COL0 // (N_BRANCH * D_MODEL))),
                  col(0), col(0), col(0),
                  pl.BlockSpec((CONV_WIDTH, D_MODEL), lambda i: (0, 0)),
                  pl.BlockSpec((N_BRANCH, D_MODEL, D_MODEL), lambda i: (0, 0, 0)),
                  pl.BlockSpec((D_MODEL, D_MODEL), lambda i: (0, 0))],
        out_specs=col(0),
        out_shape=jax.ShapeDtypeStruct((t, D_MODEL), F32),
        scratch_shapes=[pltpu.VMEM((8, D_MODEL), F32)],
        compiler_params=pltpu.CompilerParams(dimension_semantics=("arbitrary",),
                                             vmem_limit_bytes=VMEM_LIMIT),
        name="merge",
    )(main, main, main, main, y_b, y_c, x2, conv_w, w_branch, w_out)


def _mlp_kernel(x_ref, g_ref, wu_ref, wd_ref, out_ref, *, n_chunk):
    x = x_ref[...]
    h = _rms(x, g_ref[...]).astype(BF16)
    ck = D_FF // n_chunk
    acc = x
    for c in range(n_chunk):
        u = jnp.maximum(_dot(h, wu_ref[:, c * ck:(c + 1) * ck]), 0.0)
        acc = acc + _dot((u * u).astype(BF16), wd_ref[c * ck:(c + 1) * ck, :])
    out_ref[...] = acc


def _mlp(x2, g, w_up, w_down, tm):
    t = x2.shape[0]
    kern = functools.partial(_mlp_kernel, n_chunk=4)
    return pl.pallas_call(
        kern,
        grid=(t // tm,),
        in_specs=[pl.BlockSpec((tm, D_MODEL), lambda i: (i, 0)),
                  pl.BlockSpec((1, D_MODEL), lambda i: (0, 0)),
                  pl.BlockSpec((D_MODEL, D_FF), lambda i: (0, 0)),
                  pl.BlockSpec((D_FF, D_MODEL), lambda i: (0, 0))],
        out_specs=pl.BlockSpec((tm, D_MODEL), lambda i: (i, 0)),
        out_shape=jax.ShapeDtypeStruct((t, D_MODEL), F32),
        compiler_params=pltpu.CompilerParams(dimension_semantics=("parallel",),
                                             vmem_limit_bytes=VMEM_LIMIT),
        name="mlp",
    )(x2, g, w_up, w_down)


def _regroup_w_in(w_in):
    c = D_MODEL
    q0 = 3 * c
    i0 = q0 + 2 * HEADS * MLSTM_DK + 2 * HEADS * MLSTM_DV
    cq0 = i0 + 2 * HEADS
    ckv0 = cq0 + Q_LORA
    kr0 = ckv0 + KV_LORA
    g0 = kr0 + MLA_ROPE
    pad = jnp.zeros((D_MODEL, TAIL_COLS - (Q_LORA + KV_LORA + MLA_ROPE + 2 * HEADS)), w_in.dtype)
    w_main = jnp.concatenate([w_in[:, :i0], w_in[:, g0:]], axis=1).astype(BF16)
    w_tail = jnp.concatenate([w_in[:, cq0:g0], w_in[:, i0:cq0], pad], axis=1).astype(BF16)
    return w_main, w_tail


def _heads_last_split(w, first, second):
    k = w.shape[0]
    w3 = w.reshape(k, HEADS, first + second)
    return jnp.concatenate([w3[:, :, :first].reshape(k, HEADS * first),
                            w3[:, :, first:].reshape(k, HEADS * second)], axis=1).astype(BF16)


def _pick(n, candidates):
    for c in candidates:
        if n % c == 0:
            return c
    raise ValueError(f"no tile size for {n}")


def kernel(x, positions, mix_norm, w_in, conv_w, mlstm_igate_bias, mlstm_fgate_bias, mlstm_head_norm,
           mla_q_a_norm, mla_w_uq, mla_kv_a_norm, mla_w_ukv, mla_q_norm, mla_k_norm, w_branch, w_out,
           mlp_norm, w_up, w_down):
    bsz, seq, _ = x.shape
    depth = w_in.shape[0]
    t = bsz * seq
    assert seq % ATT_T == 0 and seq % (CHUNK * MLSTM_GROUP) == 0
    tm_proj = _pick(t, (1024, 512, 256, 128))
    tm = _pick(seq, (512, 256, 128))

    x2 = x.reshape(t, D_MODEL)
    cs, cst = _rope_table(positions, ATT_T)
    row = lambda a: a.reshape(1, -1).astype(F32)

    for l in range(depth):
        gate_bias = jnp.zeros((1, 128), F32)
        gate_bias = gate_bias.at[0, IGATE_LANE:IGATE_LANE + HEADS].set(mlstm_igate_bias[l])
        gate_bias = gate_bias.at[0, FGATE_LANE:FGATE_LANE + HEADS].set(mlstm_fgate_bias[l])
        w_q = _heads_last_split(mla_w_uq[l], MLA_NOPE, MLA_ROPE)
        w_kv = _heads_last_split(mla_w_ukv[l], MLA_NOPE, MLA_V)

        main, tail = _in_proj(x2, row(mix_norm[l]), *_regroup_w_in(w_in[l]), tm_proj, IN_PROJ_TN)
        y_b = _mlstm(main, tail, gate_bias, row(mlstm_head_norm[l]), bsz, seq, MLSTM_GROUP)
        qt, k, vt = _mla_prep(tail, cs, cst, row(mla_q_a_norm[l]), w_q.T, row(mla_kv_a_norm[l]),
                              w_kv[:, :HEADS * MLA_NOPE], w_kv[:, HEADS * MLA_NOPE:].T,
                              mla_q_norm[l].reshape(-1, 1).astype(F32), row(mla_k_norm[l]), bsz, seq)
        y_c = _attention(qt, k, vt).reshape(t, D_MODEL)
        x2 = _merge(main, y_b, y_c, x2, conv_w[l].astype(F32), w_branch[l].astype(BF16),
                    w_out[l].astype(BF16), seq, tm)
        x2 = _mlp(x2, row(mlp_norm[l]), w_up[l].astype(BF16), w_down[l].astype(BF16), tm)
    return x2.reshape(bsz, seq, D_MODEL)
```

```python
import functools

import jax
import jax.numpy as jnp
import numpy as np
from jax import lax
from jax.experimental import pallas as pl
from jax.experimental.pallas import tpu as pltpu

F32 = jnp.float32
BF16 = jnp.bfloat16

D_MODEL = 1024
N_BRANCH = 3
CONV_WIDTH = 3
HEADS = 8
MLSTM_DK = 64
MLSTM_DV = 128
CHUNK = 128
GATE_CAP = 15.0
MLA_NOPE = 128
MLA_ROPE = 64
MLA_V = 128
Q_LORA = 256
KV_LORA = 128
ROPE_BASE = 10000.0
D_FF = 4 * D_MODEL
NORM_EPS = 1e-6
QK_PAD = 256

MAIN_COLS = 9 * D_MODEL
TAIL_COLS = 512
GATE_COL0 = 6 * D_MODEL
TAIL_KROPE = Q_LORA + KV_LORA
IGATE_LANE = MLA_ROPE
FGATE_LANE = MLA_ROPE + HEADS

IN_PROJ_TN = 1536
MLSTM_GROUP = 1
ATT_T = 512
Q_SCALE = (MLA_NOPE + MLA_ROPE) ** -0.5 * 1.4426950408889634
MASK_VALUE = -1e30
ATT_UNROLL = 4
V_ROWS = MLA_V + 16

VMEM_LIMIT = 56 * 1024 * 1024


def _rms(x, g):
    return x * lax.rsqrt(jnp.mean(x * x, axis=-1, keepdims=True) + NORM_EPS) * g


def _dot(a, b):
    return jnp.dot(a, b, preferred_element_type=F32)


def _dot_nt(a, b):
    return lax.dot_general(a, b, (((1,), (1,)), ((), ())), preferred_element_type=F32)


def _dot_tn(a, b):
    return lax.dot_general(a, b, (((0,), (0,)), ((), ())), preferred_element_type=F32)


def _rope_table_kernel(pos_col_ref, pos_row_ref, freq_row_ref, freq_col_ref, cs_ref, cst_ref):
    ang = pos_col_ref[...].astype(F32) * freq_row_ref[...]
    c, s = jnp.cos(ang), jnp.sin(ang)
    cs_ref[...] = jnp.concatenate([c, c, -s, s], axis=-1)
    ang_t = freq_col_ref[...] * pos_row_ref[...].astype(F32)
    cst_ref[...] = jnp.concatenate([jnp.cos(ang_t), jnp.sin(ang_t)], axis=0)


def _rope_table(positions, tm):
    t = positions.size
    half = MLA_ROPE // 2
    inv_freq = (np.float32(ROPE_BASE) ** (-np.arange(0, MLA_ROPE, 2, dtype=np.float32) / np.float32(MLA_ROPE)))
    inv_freq = jnp.asarray(inv_freq.astype(np.float32))
    return pl.pallas_call(
        _rope_table_kernel,
        grid=(t // tm,),
        in_specs=[pl.BlockSpec((tm, 1), lambda i: (i, 0)),
                  pl.BlockSpec((1, tm), lambda i: (0, i)),
                  pl.BlockSpec((1, half), lambda i: (0, 0)),
                  pl.BlockSpec((half, 1), lambda i: (0, 0))],
        out_specs=[pl.BlockSpec((tm, 2 * MLA_ROPE), lambda i: (i, 0)),
                   pl.BlockSpec((MLA_ROPE, tm), lambda i: (0, i))],
        out_shape=[jax.ShapeDtypeStruct((t, 2 * MLA_ROPE), F32),
                   jax.ShapeDtypeStruct((MLA_ROPE, t), F32)],
        name="rope_table",
    )(positions.reshape(t, 1), positions.reshape(1, t), inv_freq.reshape(1, half), inv_freq.reshape(half, 1))


def _in_proj_kernel(x_ref, g_ref, w_ref, wt_ref, main_ref, tail_ref, xn_ref, *, n_gate0):
    j = pl.program_id(1)

    @pl.when(j == 0)
    def _():
        xn = _rms(x_ref[...], g_ref[...]).astype(BF16)
        xn_ref[...] = xn
        tail_ref[...] = _dot(xn, wt_ref[...])

    @pl.when(j < n_gate0)
    def _():
        main_ref[...] = _dot(xn_ref[...], w_ref[...]).astype(BF16)

    @pl.when(j >= n_gate0)
    def _():
        main_ref[...] = jax.nn.sigmoid(_dot(xn_ref[...], w_ref[...])).astype(BF16)


def _in_proj(x2, g, w_main, w_tail, tm, tn):
    t = x2.shape[0]
    assert GATE_COL0 % tn == 0 and MAIN_COLS % tn == 0
    kern = functools.partial(_in_proj_kernel, n_gate0=GATE_COL0 // tn)
    return pl.pallas_call(
        kern,
        grid=(t // tm, MAIN_COLS // tn),
        in_specs=[pl.BlockSpec((tm, D_MODEL), lambda i, j: (i, 0)),
                  pl.BlockSpec((1, D_MODEL), lambda i, j: (0, 0)),
                  pl.BlockSpec((D_MODEL, tn), lambda i, j: (0, j)),
                  pl.BlockSpec((D_MODEL, TAIL_COLS), lambda i, j: (0, 0))],
        out_specs=[pl.BlockSpec((tm, tn), lambda i, j: (i, j)),
                   pl.BlockSpec((tm, TAIL_COLS), lambda i, j: (i, 0))],
        out_shape=[jax.ShapeDtypeStruct((t, MAIN_COLS), BF16),
                   jax.ShapeDtypeStruct((t, TAIL_COLS), F32)],
        scratch_shapes=[pltpu.VMEM((tm, D_MODEL), BF16)],
        compiler_params=pltpu.CompilerParams(
            dimension_semantics=("parallel", "arbitrary"), vmem_limit_bytes=VMEM_LIMIT),
        name="in_proj",
    )(x2, g, w_main, w_tail)


def _time_scan(x, op, row):
    s = 1
    while s < CHUNK:
        x = jnp.where(row >= s, op(x, pltpu.roll(x, s, 0)), x)
        s *= 2
    return x


def _mlstm_kernel(q_ref, k_ref, v_ref, o_ref, gate_ref, bias_ref, hn_ref, y_ref, c_ref, m_ref, *, n_chunk):
    L = CHUNK

    @pl.when(pl.program_id(1) == 0)
    def _():
        c_ref[...] = jnp.zeros_like(c_ref)
        m_ref[...] = jnp.zeros_like(m_ref)

    row = lax.broadcasted_iota(jnp.int32, (L, 128), 0)
    lane = lax.broadcasted_iota(jnp.int32, (L, 128), 1)
    gate_lanes = jnp.logical_and(lane >= IGATE_LANE, lane < FGATE_LANE + HEADS)
    tri = (lax.broadcasted_iota(jnp.int32, (L, L), 0) >= lax.broadcasted_iota(jnp.int32, (L, L), 1))
    ones_col = (lax.broadcasted_iota(jnp.int32, (L, MLSTM_DV), 1) == 0).astype(BF16)
    m_prev = m_ref[...]
    c_state = [c_ref[h] for h in range(HEADS)]

    for g in range(n_chunk):
        rows = slice(g * L, (g + 1) * L)
        pre = jnp.where(gate_lanes, gate_ref[rows, :] + bias_ref[...], 0.0)
        capped = GATE_CAP * jnp.tanh(pre * (1.0 / GATE_CAP))
        log_i = capped
        log_f = jnp.minimum(capped, 0.0) - jnp.log1p(jnp.exp(-jnp.abs(capped)))
        log_f = pltpu.roll(log_f, 128 - HEADS, 1)
        b = _time_scan(log_f, jnp.add, row)
        b_last = b[L - 1:L, :]
        rowd = log_i - b
        big_m = jnp.maximum(_time_scan(rowd, jnp.maximum, row), m_prev)
        e_neg_m = jnp.exp(-(b + big_m))
        a = b_last + rowd
        m_loc = jnp.max(a, axis=0, keepdims=True)
        w = jnp.exp(a - m_loc)
        m_new = jnp.maximum(b_last + m_prev, m_loc)
        s_old = jnp.exp(b_last + m_prev - m_new)
        s_loc = jnp.exp(m_loc - m_new)
        m_prev_g, m_prev = m_prev, m_new
        rowd_t = rowd.T
        w_t = w.T
        q_all = q_ref[rows, :] * (MLSTM_DK ** -0.5)
        k_t = k_ref[rows, :].astype(F32).T

        for h in range(HEADS):
            hk = slice(h * MLSTM_DK, (h + 1) * MLSTM_DK)
            hv = slice(h * MLSTM_DV, (h + 1) * MLSTM_DV)
            gl = slice(IGATE_LANE + h, IGATE_LANE + h + 1)
            qh = q_all[:, hk]
            kh_t = k_t[hk, :]
            v_aug = jnp.concatenate([v_ref[rows, hv], ones_col], axis=1)
            c_prev = c_state[h]

            big_m_b = jnp.broadcast_to(big_m[:, gl], (L, L))
            dmat = jnp.where(tri, jnp.exp(rowd_t[gl, :] - big_m_b), 0.0)
            p = (_dot(qh, kh_t.astype(BF16)) * dmat).astype(BF16)
            q_state = (qh.astype(F32) * jnp.exp(m_prev_g[:, gl] - big_m_b[:, :MLSTM_DK])).astype(BF16)
            lhs = jnp.concatenate([p, q_state, jnp.zeros((L, L - MLSTM_DK), BF16)], axis=1)
            rhs = jnp.concatenate([v_aug, c_prev.astype(BF16),
                                   jnp.zeros((L - MLSTM_DK, 2 * MLSTM_DV), BF16)], axis=0)
            num_aug = _dot(lhs, rhs)
            num = num_aug[:, :MLSTM_DV]
            den = num_aug[:, MLSTM_DV:MLSTM_DV + 1]
            d = jnp.maximum(jnp.abs(den), e_neg_m[:, gl])
            scale = lax.rsqrt(jnp.mean(num * num, axis=-1, keepdims=True) + NORM_EPS * d * d)
            og = jax.nn.sigmoid(o_ref[rows, hv].astype(F32))
            y_ref[rows, hv] = (num * scale * hn_ref[:, hv] * og).astype(y_ref.dtype)

            c_loc = _dot((kh_t * w_t[gl, :]).astype(BF16), v_aug)
            c_state[h] = s_old[:, gl] * c_prev + s_loc[:, gl] * c_loc

    m_ref[...] = m_prev
    for h in range(HEADS):
        c_ref[h] = c_state[h]


def _mlstm(main, tail, gate_bias, head_norm, bsz, seq, n_chunk):
    t = main.shape[0]
    rows = n_chunk * CHUNK
    nc = seq // rows
    row = lambda b, c: b * nc + c
    return pl.pallas_call(
        functools.partial(_mlstm_kernel, n_chunk=n_chunk),
        grid=(bsz, nc),
        in_specs=[pl.BlockSpec((rows, 512), lambda b, c: (row(b, c), 6)),
                  pl.BlockSpec((rows, 512), lambda b, c: (row(b, c), 7)),
                  pl.BlockSpec((rows, D_MODEL), lambda b, c: (row(b, c), 4)),
                  pl.BlockSpec((rows, D_MODEL), lambda b, c: (row(b, c), 5)),
                  pl.BlockSpec((rows, 128), lambda b, c: (row(b, c), TAIL_KROPE // 128)),
                  pl.BlockSpec((1, 128), lambda b, c: (0, 0)),
                  pl.BlockSpec((1, D_MODEL), lambda b, c: (0, 0))],
        out_specs=pl.BlockSpec((rows, D_MODEL), lambda b, c: (row(b, c), 0)),
        out_shape=jax.ShapeDtypeStruct((t, D_MODEL), BF16),
        scratch_shapes=[pltpu.VMEM((HEADS, MLSTM_DK, 2 * MLSTM_DV), F32),
                        pltpu.VMEM((1, 128), F32)],
        compiler_params=pltpu.CompilerParams(dimension_semantics=("parallel", "arbitrary")),
        name="mlstm",
    )(main, main, main, main, tail, gate_bias, head_norm)


def _rms0(x, g_col):
    return x * lax.rsqrt(jnp.mean(x * x, axis=0, keepdims=True) + NORM_EPS) * g_col


def _rope(x, cs):
    half = MLA_ROPE // 2
    rot = jnp.concatenate([x[:, half:], x[:, :half]], axis=-1)
    return x * cs[:, :MLA_ROPE] + rot * cs[:, MLA_ROPE:]


def _mla_prep_kernel(tail_ref, cs_ref, cst_ref, qa_ref, wqt_ref, kva_ref, wk_ref, wvt_ref, qn_ref, kn_ref,
                     qt_out, k_out, vt_out):
    tm = tail_ref.shape[0]
    half = MLA_ROPE // 2
    cq = _rms(tail_ref[:, :Q_LORA], qa_ref[...]).astype(BF16)
    ckv = _rms(tail_ref[:, Q_LORA:Q_LORA + KV_LORA], kva_ref[...]).astype(BF16)
    qft = _dot_nt(wqt_ref[...], cq)
    kf = _dot(ckv, wk_ref[...])
    vft = _dot_nt(wvt_ref[...], ckv)
    cos_t, sin_t = cst_ref[:half, :], cst_ref[half:, :]
    qn_g, qp_g = qn_ref[:MLA_NOPE, :], qn_ref[MLA_NOPE:, :]
    kn_g, kp_g = kn_ref[:, :MLA_NOPE], kn_ref[:, MLA_NOPE:]
    k_pe = _rope(_rms(tail_ref[:, TAIL_KROPE:TAIL_KROPE + MLA_ROPE], kp_g), cs_ref[...]).astype(BF16)
    pe0 = HEADS * MLA_NOPE
    ones_rows = (lax.broadcasted_iota(jnp.int32, (V_ROWS - MLA_V, tm), 0) == 0).astype(BF16)
    for h in range(HEADS):
        q_nope = _rms0(qft[h * MLA_NOPE:(h + 1) * MLA_NOPE], qn_g)
        q_pe = _rms0(qft[pe0 + h * MLA_ROPE:pe0 + (h + 1) * MLA_ROPE], qp_g)
        x1, x2 = q_pe[:half], q_pe[half:]
        q_rot = jnp.concatenate([x1 * cos_t - x2 * sin_t, x1 * sin_t + x2 * cos_t], axis=0)
        qt_out[0, h, 0] = (jnp.concatenate([q_nope, q_rot, jnp.zeros((QK_PAD - MLA_NOPE - MLA_ROPE, tm), F32)],
                                           axis=0) * Q_SCALE).astype(BF16)
        k_nope = _rms(kf[:, h * MLA_NOPE:(h + 1) * MLA_NOPE], kn_g)
        k_out[0, h] = jnp.concatenate(
            [k_nope.astype(BF16), k_pe, jnp.zeros((tm, QK_PAD - MLA_NOPE - MLA_ROPE), BF16)], axis=-1)
        vt_out[0, h, 0] = jnp.concatenate([vft[h * MLA_V:(h + 1) * MLA_V].astype(BF16), ones_rows], axis=0)


def _mla_prep(tail, cs, cst, qa, wqt, kva, wk, wvt, qn_col, kn, bsz, seq):
    tm = ATT_T
    nt = seq // tm
    full = lambda shape: pl.BlockSpec(shape, lambda i: (0,) * len(shape))
    blocked = lambda rows: pl.BlockSpec((1, HEADS, 1, rows, tm), lambda i: (i // nt, 0, i % nt, 0, 0))
    return pl.pallas_call(
        _mla_prep_kernel,
        grid=(bsz * nt,),
        in_specs=[pl.BlockSpec((tm, TAIL_COLS), lambda i: (i, 0)),
                  pl.BlockSpec((tm, 2 * MLA_ROPE), lambda i: (i, 0)),
                  pl.BlockSpec((MLA_ROPE, tm), lambda i: (0, i)),
                  full((1, Q_LORA)), full(wqt.shape), full((1, KV_LORA)), full(wk.shape), full(wvt.shape),
                  full((MLA_NOPE + MLA_ROPE, 1)), full((1, MLA_NOPE + MLA_ROPE))],
        out_specs=[blocked(QK_PAD),
                   pl.BlockSpec((1, HEADS, tm, QK_PAD), lambda i: (i // nt, 0, i % nt, 0)),
                   blocked(V_ROWS)],
        out_shape=[jax.ShapeDtypeStruct((bsz, HEADS, nt, QK_PAD, tm), BF16),
                   jax.ShapeDtypeStruct((bsz, HEADS, seq, QK_PAD), BF16),
                   jax.ShapeDtypeStruct((bsz, HEADS, nt, V_ROWS, tm), BF16)],
        compiler_params=pltpu.CompilerParams(dimension_semantics=("parallel",),
                                             vmem_limit_bytes=VMEM_LIMIT),
        name="mla_prep",
    )(tail, cs, cst, qa, wqt, kva, wk, wvt, qn_col, kn)


def _attn_kernel(qt_ref, k_ref, vt_ref, o_ref, s_scr, p_scr, a_scr, m_all, acc_all, *, nq):
    t = ATT_T

    def qk(qi, j):
        k0 = pl.multiple_of(j * t, t)
        return _dot(k_ref[0, 0, pl.ds(k0, t), :], qt_ref[0, 0, qi])

    def accumulate(tile, alpha, p):
        q, j = tile
        acc_all[q] = alpha * acc_all[q] + _dot(vt_ref[0, 0, j], p)

    def stage(prev, cur, nxt, s_cur, p_prev, alpha_prev, masked):
        accumulate(prev, alpha_prev, p_prev)
        qi = cur[0]
        if masked:
            keep = lax.broadcasted_iota(jnp.int32, (t, t), 0) <= lax.broadcasted_iota(jnp.int32, (t, t), 1)
            s_cur = jnp.where(keep, s_cur, MASK_VALUE)
        m_old = m_all[qi]
        m_new = jnp.maximum(m_old, jnp.max(s_cur, axis=0, keepdims=True))
        m_all[qi] = m_new
        alpha = jnp.exp2(m_old - m_new)
        p = jnp.exp2((s_cur - m_new).astype(BF16))
        s_next = qk(jnp.minimum(nxt[0], nq - 1), jnp.minimum(nxt[1], nq - 1))
        return s_next, p, alpha

    def advance_lower(qi, j):
        wrap = j + 1 == qi
        return jnp.where(wrap, qi + 1, qi), jnp.where(wrap, 0, j + 1)

    def advance_diag(qi, j):
        return qi + 1, j + 1

    def run_phase(n_steps, first, advance, masked):
        if n_steps == 0:
            return
        s_scr[...] = qk(*first)
        p_scr[...] = jnp.zeros_like(p_scr)
        a_scr[...] = jnp.ones_like(a_scr)

        def steps(count, prev, cur):
            s_cur, p_prev, alpha_prev = s_scr[...], p_scr[...], a_scr[...]
            for _ in range(count):
                nxt = advance(*cur)
                s_cur, p_prev, alpha_prev = stage(prev, cur, nxt, s_cur, p_prev, alpha_prev, masked)
                prev, cur = cur, nxt
            s_scr[...] = s_cur
            p_scr[...] = p_prev
            a_scr[...] = alpha_prev
            return prev, cur

        first = (jnp.int32(first[0]), jnp.int32(first[1]))
        prev, cur = lax.fori_loop(0, n_steps // ATT_UNROLL, lambda _, c: steps(ATT_UNROLL, *c), (first, first))
        if n_steps % ATT_UNROLL:
            prev, cur = steps(n_steps % ATT_UNROLL, prev, cur)
        accumulate(prev, a_scr[...], p_scr[...])

    m_all[...] = jnp.full_like(m_all, MASK_VALUE)
    acc_all[...] = jnp.zeros_like(acc_all)
    run_phase(nq * (nq - 1) // 2, (1, 0), advance_lower, False)
    run_phase(nq, (0, 0), advance_diag, True)
    for qi in range(nq):
        acc = acc_all[qi]
        o_ref[0, qi * t:(qi + 1) * t, :] = (acc[:MLA_V] / acc[MLA_V:MLA_V + 1]).T.astype(o_ref.dtype)


def _attention(qt, k, vt):
    bsz, _, seq, _ = k.shape
    nq = seq // ATT_T
    kern = functools.partial(_attn_kernel, nq=nq)
    return pl.pallas_call(
        kern,
        grid=(bsz, HEADS),
        in_specs=[pl.BlockSpec((1, 1, nq, QK_PAD, ATT_T), lambda b, h: (b, h, 0, 0, 0)),
                  pl.BlockSpec((1, 1, seq, QK_PAD), lambda b, h: (b, h, 0, 0)),
                  pl.BlockSpec((1, 1, nq, V_ROWS, ATT_T), lambda b, h: (b, h, 0, 0, 0))],
        out_specs=pl.BlockSpec((1, seq, MLA_V), lambda b, h: (b, 0, h)),
        out_shape=jax.ShapeDtypeStruct((bsz, seq, HEADS * MLA_V), BF16),
        scratch_shapes=[pltpu.VMEM((ATT_T, ATT_T), F32), pltpu.VMEM((ATT_T, ATT_T), BF16),
                        pltpu.VMEM((1, ATT_T), F32), pltpu.VMEM((nq, 1, ATT_T), F32),
                        pltpu.VMEM((nq, V_ROWS, ATT_T), F32)],
        compiler_params=pltpu.CompilerParams(dimension_semantics=("parallel", "parallel"),
                                             vmem_limit_bytes=VMEM_LIMIT),
        name="attn",
    )(qt, k, vt)


def _merge_kernel(cb_ref, cc_ref, cu_ref, gate_ref, yb_ref, yc_ref, x_ref, cw_ref, wb_ref, wo_ref,
                  out_ref, zprev_ref, *, tiles_per_seq):
    tm = x_ref.shape[0]

    @pl.when(pl.program_id(0) % tiles_per_seq == 0)
    def _():
        zprev_ref[...] = jnp.zeros_like(zprev_ref)

    z = cc_ref[...].astype(F32) * cu_ref[...].astype(F32)
    row = lax.broadcasted_iota(jnp.int32, (tm, D_MODEL), 0)
    p1 = zprev_ref[7:8, :]
    p2 = zprev_ref[6:7, :]
    z1 = jnp.where(row == 0, p1, pltpu.roll(z, 1, 0))
    z2 = jnp.where(row == 0, p2, jnp.where(row == 1, p1, pltpu.roll(z, 2, 0)))
    zprev_ref[...] = z[tm - 8:, :]
    cw = cw_ref[...]
    y_a = cb_ref[...].astype(F32) * (cw[0:1] * z2 + cw[1:2] * z1 + cw[2:3] * z)

    g = gate_ref[...]
    merged = g[:, :D_MODEL].astype(F32) * _dot(y_a.astype(BF16), wb_ref[0])
    merged += g[:, D_MODEL:2 * D_MODEL].astype(F32) * _dot(yb_ref[...], wb_ref[1])
    merged += g[:, 2 * D_MODEL:].astype(F32) * _dot(yc_ref[...], wb_ref[2])
    out_ref[...] = x_ref[...] + _dot(merged.astype(BF16), wo_ref[...])


def _merge(main, y_b, y_c, x2, conv_w, w_branch, w_out, seq, tm):
    t = x2.shape[0]
    kern = functools.partial(_merge_kernel, tiles_per_seq=seq // tm)
    col = lambda c: pl.BlockSpec((tm, D_MODEL), lambda i: (i, c))
    return pl.pallas_call(
        kern,
        grid=(t // tm,),
        in_specs=[col(0), col(1), col(2),
                  pl.BlockSpec((tm, N_BRANCH * D_MODEL), lambda i: (i, GATE_COL0 // (N_BRANCH * D_MODEL))),
                  col(0), col(0), col(0),
                  pl.BlockSpec((CONV_WIDTH, D_MODEL), lambda i: (0, 0)),
                  pl.BlockSpec((N_BRANCH, D_MODEL, D_MODEL), lambda i: (0, 0, 0)),
                  pl.BlockSpec((D_MODEL, D_MODEL), lambda i: (0, 0))],
        out_specs=col(0),
        out_shape=jax.ShapeDtypeStruct((t, D_MODEL), F32),
        scratch_shapes=[pltpu.VMEM((8, D_MODEL), F32)],
        compiler_params=pltpu.CompilerParams(dimension_semantics=("arbitrary",),
                                             vmem_limit_bytes=VMEM_LIMIT),
        name="merge",
    )(main, main, main, main, y_b, y_c, x2, conv_w, w_branch, w_out)


def _mlp_kernel(x_ref, g_ref, wu_ref, wd_ref, out_ref, *, n_chunk):
    x = x_ref[...]
    h = _rms(x, g_ref[...]).astype(BF16)
    ck = D_FF // n_chunk
    acc = x
    for c in range(n_chunk):
        u = jnp.maximum(_dot(h, wu_ref[:, c * ck:(c + 1) * ck]), 0.0)
        acc = acc + _dot((u * u).astype(BF16), wd_ref[c * ck:(c + 1) * ck, :])
    out_ref[...] = acc


def _mlp(x2, g, w_up, w_down, tm):
    t = x2.shape[0]
    kern = functools.partial(_mlp_kernel, n_chunk=4)
    return pl.pallas_call(
        kern,
        grid=(t // tm,),
        in_specs=[pl.BlockSpec((tm, D_MODEL), lambda i: (i, 0)),
                  pl.BlockSpec((1, D_MODEL), lambda i: (0, 0)),
                  pl.BlockSpec((D_MODEL, D_FF), lambda i: (0, 0)),
                  pl.BlockSpec((D_FF, D_MODEL), lambda i: (0, 0))],
        out_specs=pl.BlockSpec((tm, D_MODEL), lambda i: (i, 0)),
        out_shape=jax.ShapeDtypeStruct((t, D_MODEL), F32),
        compiler_params=pltpu.CompilerParams(dimension_semantics=("parallel",),
                                             vmem_limit_bytes=VMEM_LIMIT),
        name="mlp",
    )(x2, g, w_up, w_down)


def _regroup_w_in(w_in):
    c = D_MODEL
    q0 = 3 * c
    i0 = q0 + 2 * HEADS * MLSTM_DK + 2 * HEADS * MLSTM_DV
    cq0 = i0 + 2 * HEADS
    ckv0 = cq0 + Q_LORA
    kr0 = ckv0 + KV_LORA
    g0 = kr0 + MLA_ROPE
    pad = jnp.zeros((D_MODEL, TAIL_COLS - (Q_LORA + KV_LORA + MLA_ROPE + 2 * HEADS)), w_in.dtype)
    w_main = jnp.concatenate([w_in[:, :i0], w_in[:, g0:]], axis=1).astype(BF16)
    w_tail = jnp.concatenate([w_in[:, cq0:g0], w_in[:, i0:cq0], pad], axis=1).astype(BF16)
    return w_main, w_tail


def _heads_last_split(w, first, second):
    k = w.shape[0]
    w3 = w.reshape(k, HEADS, first + second)
    return jnp.concatenate([w3[:, :, :first].reshape(k, HEADS * first),
                            w3[:, :, first:].reshape(k, HEADS * second)], axis=1).astype(BF16)


def _pick(n, candidates):
    for c in candidates:
        if n % c == 0:
            return c
    raise ValueError(f"no tile size for {n}")


def kernel(x, positions, mix_norm, w_in, conv_w, mlstm_igate_bias, mlstm_fgate_bias, mlstm_head_norm,
           mla_q_a_norm, mla_w_uq, mla_kv_a_norm, mla_w_ukv, mla_q_norm, mla_k_norm, w_branch, w_out,
           mlp_norm, w_up, w_down):
    bsz, seq, _ = x.shape
    depth = w_in.shape[0]
    t = bsz * seq
    assert seq % ATT_T == 0 and seq % (CHUNK * MLSTM_GROUP) == 0
    tm_proj = _pick(t, (1024, 512, 256, 128))
    tm = _pick(seq, (512, 256, 128))

    x2 = x.reshape(t, D_MODEL)
    cs, cst = _rope_table(positions, ATT_T)
    row = lambda a: a.reshape(1, -1).astype(F32)

    for l in range(depth):
        gate_bias = jnp.zeros((1, 128), F32)
        gate_bias = gate_bias.at[0, IGATE_LANE:IGATE_LANE + HEADS].set(mlstm_igate_bias[l])
        gate_bias = gate_bias.at[0, FGATE_LANE:FGATE_LANE + HEADS].set(mlstm_fgate_bias[l])
        w_q = _heads_last_split(mla_w_uq[l], MLA_NOPE, MLA_ROPE)
        w_kv = _heads_last_split(mla_w_ukv[l], MLA_NOPE, MLA_V)

        main, tail = _in_proj(x2, row(mix_norm[l]), *_regroup_w_in(w_in[l]), tm_proj, IN_PROJ_TN)
        y_b = _mlstm(main, tail, gate_bias, row(mlstm_head_norm[l]), bsz, seq, MLSTM_GROUP)
        qt, k, vt = _mla_prep(tail, cs, cst, row(mla_q_a_norm[l]), w_q.T, row(mla_kv_a_norm[l]),
                              w_kv[:, :HEADS * MLA_NOPE], w_kv[:, HEADS * MLA_NOPE:].T,
                              mla_q_norm[l].reshape(-1, 1).astype(F32), row(mla_k_norm[l]), bsz, seq)
        y_c = _attention(qt, k, vt).reshape(t, D_MODEL)
        x2 = _merge(main, y_b, y_c, x2, conv_w[l].astype(F32), w_branch[l].astype(BF16),
                    w_out[l].astype(BF16), seq, tm)
        x2 = _mlp(x2, row(mlp_norm[l]), w_up[l].astype(BF16), w_down[l].astype(BF16), tm)
    return x2.reshape(bsz, seq, D_MODEL)
```

```python
import functools

import jax
import jax.numpy as jnp
import numpy as np
from jax import lax
from jax.experimental import pallas as pl
from jax.experimental.pallas import tpu as pltpu

F32 = jnp.float32
BF16 = jnp.bfloat16

D_MODEL = 1024
N_BRANCH = 3
CONV_WIDTH = 3
HEADS = 8
MLSTM_DK = 64
MLSTM_DV = 128
CHUNK = 128
GATE_CAP = 15.0
MLA_NOPE = 128
MLA_ROPE = 64
MLA_V = 128
Q_LORA = 256
KV_LORA = 128
ROPE_BASE = 10000.0
D_FF = 4 * D_MODEL
NORM_EPS = 1e-6
QK_PAD = 256

MAIN_COLS = 9 * D_MODEL
TAIL_COLS = 512
GATE_COL0 = 6 * D_MODEL
TAIL_KROPE = Q_LORA + KV_LORA
IGATE_LANE = MLA_ROPE
FGATE_LANE = MLA_ROPE + HEADS

IN_PROJ_TN = 1536
MLSTM_GROUP = 2
ATT_T = 512
LOG2E = 1.4426950408889634
Q_SCALE = (MLA_NOPE + MLA_ROPE) ** -0.5 * LOG2E
MASK_VALUE = -1e30
ATT_UNROLL = 4
V_ROWS = MLA_V + 16

VMEM_LIMIT = 56 * 1024 * 1024


def _rms(x, g):
    return x * lax.rsqrt(jnp.mean(x * x, axis=-1, keepdims=True) + NORM_EPS) * g


def _dot(a, b):
    return jnp.dot(a, b, preferred_element_type=F32)


def _dot_nt(a, b):
    return lax.dot_general(a, b, (((1,), (1,)), ((), ())), preferred_element_type=F32)


def _rope_table_kernel(pos_col_ref, pos_row_ref, freq_row_ref, freq_col_ref, cs_ref, cst_ref):
    ang = pos_col_ref[...].astype(F32) * freq_row_ref[...]
    c, s = jnp.cos(ang), jnp.sin(ang)
    cs_ref[...] = jnp.concatenate([c, c, -s, s], axis=-1)
    ang_t = freq_col_ref[...] * pos_row_ref[...].astype(F32)
    cst_ref[...] = jnp.concatenate([jnp.cos(ang_t), jnp.sin(ang_t)], axis=0)


def _rope_table(positions, tm):
    t = positions.size
    half = MLA_ROPE // 2
    inv_freq = (np.float32(ROPE_BASE) ** (-np.arange(0, MLA_ROPE, 2, dtype=np.float32) / np.float32(MLA_ROPE)))
    inv_freq = jnp.asarray(inv_freq.astype(np.float32))
    return pl.pallas_call(
        _rope_table_kernel,
        grid=(t // tm,),
        in_specs=[pl.BlockSpec((tm, 1), lambda i: (i, 0)),
                  pl.BlockSpec((1, tm), lambda i: (0, i)),
                  pl.BlockSpec((1, half), lambda i: (0, 0)),
                  pl.BlockSpec((half, 1), lambda i: (0, 0))],
        out_specs=[pl.BlockSpec((tm, 2 * MLA_ROPE), lambda i: (i, 0)),
                   pl.BlockSpec((MLA_ROPE, tm), lambda i: (0, i))],
        out_shape=[jax.ShapeDtypeStruct((t, 2 * MLA_ROPE), F32),
                   jax.ShapeDtypeStruct((MLA_ROPE, t), F32)],
        name="rope_table",
    )(positions.reshape(t, 1), positions.reshape(1, t), inv_freq.reshape(1, half), inv_freq.reshape(half, 1))


def _in_proj_kernel(x_ref, g_ref, w_ref, wt_ref, main_ref, tail_ref, xn_ref, *, n_gate0):
    j = pl.program_id(1)

    @pl.when(j == 0)
    def _():
        xn = _rms(x_ref[...], g_ref[...]).astype(BF16)
        xn_ref[...] = xn
        tail_ref[...] = _dot(xn, wt_ref[...])

    @pl.when(j < n_gate0)
    def _():
        main_ref[...] = _dot(xn_ref[...], w_ref[...]).astype(BF16)

    @pl.when(j >= n_gate0)
    def _():
        main_ref[...] = jax.nn.sigmoid(_dot(xn_ref[...], w_ref[...])).astype(BF16)


def _in_proj(x2, g, w_main, w_tail, tm, tn):
    t = x2.shape[0]
    assert GATE_COL0 % tn == 0 and MAIN_COLS % tn == 0
    kern = functools.partial(_in_proj_kernel, n_gate0=GATE_COL0 // tn)
    return pl.pallas_call(
        kern,
        grid=(t // tm, MAIN_COLS // tn),
        in_specs=[pl.BlockSpec((tm, D_MODEL), lambda i, j: (i, 0)),
                  pl.BlockSpec((1, D_MODEL), lambda i, j: (0, 0)),
                  pl.BlockSpec((D_MODEL, tn), lambda i, j: (0, j)),
                  pl.BlockSpec((D_MODEL, TAIL_COLS), lambda i, j: (0, 0))],
        out_specs=[pl.BlockSpec((tm, tn), lambda i, j: (i, j)),
                   pl.BlockSpec((tm, TAIL_COLS), lambda i, j: (i, 0))],
        out_shape=[jax.ShapeDtypeStruct((t, MAIN_COLS), BF16),
                   jax.ShapeDtypeStruct((t, TAIL_COLS), F32)],
        scratch_shapes=[pltpu.VMEM((tm, D_MODEL), BF16)],
        compiler_params=pltpu.CompilerParams(
            dimension_semantics=("parallel", "arbitrary"), vmem_limit_bytes=VMEM_LIMIT),
        name="in_proj",
    )(x2, g, w_main, w_tail)


def _time_scan(x, op, row):
    s = 1
    while s < CHUNK:
        x = jnp.where(row >= s, op(x, pltpu.roll(x, s, 0)), x)
        s *= 2
    return x


def _mlstm_kernel(q_ref, k_ref, v_ref, o_ref, gate_ref, bias_ref, hn_ref, y_ref, c_ref, m_ref, *, n_chunk):
    L = CHUNK

    @pl.when(pl.program_id(1) == 0)
    def _():
        c_ref[...] = jnp.zeros_like(c_ref)
        m_ref[...] = jnp.zeros_like(m_ref)

    row = lax.broadcasted_iota(jnp.int32, (L, 128), 0)
    lane = lax.broadcasted_iota(jnp.int32, (L, 128), 1)
    gate_lanes = jnp.logical_and(lane >= IGATE_LANE, lane < FGATE_LANE + HEADS)
    tri = (lax.broadcasted_iota(jnp.int32, (L, L), 0) >= lax.broadcasted_iota(jnp.int32, (L, L), 1))
    ones_col = (lax.broadcasted_iota(jnp.int32, (L, MLSTM_DV), 1) == 0).astype(BF16)
    m_prev = m_ref[...]
    c_state = [c_ref[h] for h in range(HEADS)]

    for g in range(n_chunk):
        rows = slice(g * L, (g + 1) * L)
        pre = jnp.where(gate_lanes, gate_ref[rows, :] + bias_ref[...], 0.0)
        capped = GATE_CAP * jnp.tanh(pre * (1.0 / GATE_CAP))
        log_i = capped * LOG2E
        log_f = (jnp.minimum(capped, 0.0) - jnp.log1p(jnp.exp(-jnp.abs(capped)))) * LOG2E
        log_f = pltpu.roll(log_f, 128 - HEADS, 1)
        b = _time_scan(log_f, jnp.add, row)
        b_last = b[L - 1:L, :]
        rowd = log_i - b
        big_m = jnp.maximum(_time_scan(rowd, jnp.maximum, row), m_prev)
        e_neg_m = jnp.exp2(-(b + big_m))
        a = b_last + rowd
        m_loc = jnp.max(a, axis=0, keepdims=True)
        m_new = jnp.maximum(b_last + m_prev, m_loc)
        s_old = jnp.exp2(b_last + m_prev - m_new)
        w = jnp.exp2(a - m_new)
        m_prev_g, m_prev = m_prev, m_new
        rowd_t = rowd.T
        w_t = w.T
        q_all = q_ref[rows, :] * (MLSTM_DK ** -0.5)
        k_t = k_ref[rows, :].astype(F32).T

        hk = [slice(h * MLSTM_DK, (h + 1) * MLSTM_DK) for h in range(HEADS)]
        hv = [slice(h * MLSTM_DV, (h + 1) * MLSTM_DV) for h in range(HEADS)]
        gl = [slice(IGATE_LANE + h, IGATE_LANE + h + 1) for h in range(HEADS)]
        v_aug = [jnp.concatenate([v_ref[rows, hv[h]], ones_col], axis=1) for h in range(HEADS)]
        scores = [_dot(q_all[:, hk[h]], k_t[hk[h], :].astype(BF16)) for h in range(HEADS)]
        c_loc = [_dot((k_t[hk[h], :] * w_t[gl[h], :]).astype(BF16), v_aug[h]) for h in range(HEADS)]
        num_aug = []
        for h in range(HEADS):
            big_m_b = jnp.broadcast_to(big_m[:, gl[h]], (L, L))
            dmat = jnp.where(tri, jnp.exp2(rowd_t[gl[h], :] - big_m_b), 0.0)
            p = (scores[h] * dmat).astype(BF16)
            q_state = (q_all[:, hk[h]].astype(F32)
                       * jnp.exp2(m_prev_g[:, gl[h]] - big_m_b[:, :MLSTM_DK])).astype(BF16)
            lhs = jnp.concatenate([p, q_state, jnp.zeros((L, L - MLSTM_DK), BF16)], axis=1)
            rhs = jnp.concatenate([v_aug[h], c_state[h].astype(BF16),
                                   jnp.zeros((L - MLSTM_DK, 2 * MLSTM_DV), BF16)], axis=0)
            num_aug.append(_dot(lhs, rhs))
        for h in range(HEADS):
            num = num_aug[h][:, :MLSTM_DV]
            d = jnp.maximum(jnp.abs(num_aug[h][:, MLSTM_DV:MLSTM_DV + 1]), e_neg_m[:, gl[h]])
            scale = lax.rsqrt(jnp.mean(num * num, axis=-1, keepdims=True) + NORM_EPS * d * d)
            og = jax.nn.sigmoid(o_ref[rows, hv[h]].astype(F32))
            y_ref[rows, hv[h]] = (num * scale * hn_ref[:, hv[h]] * og).astype(y_ref.dtype)
            c_state[h] = s_old[:, gl[h]] * c_state[h] + c_loc[h]

    m_ref[...] = m_prev
    for h in range(HEADS):
        c_ref[h] = c_state[h]


def _mlstm(main, tail, gate_bias, head_norm, bsz, seq, n_chunk):
    t = main.shape[0]
    rows = n_chunk * CHUNK
    nc = seq // rows
    row = lambda b, c: b * nc + c
    return pl.pallas_call(
        functools.partial(_mlstm_kernel, n_chunk=n_chunk),
        grid=(bsz, nc),
        in_specs=[pl.BlockSpec((rows, 512), lambda b, c: (row(b, c), 6)),
                  pl.BlockSpec((rows, 512), lambda b, c: (row(b, c), 7)),
                  pl.BlockSpec((rows, D_MODEL), lambda b, c: (row(b, c), 4)),
                  pl.BlockSpec((rows, D_MODEL), lambda b, c: (row(b, c), 5)),
                  pl.BlockSpec((rows, 128), lambda b, c: (row(b, c), TAIL_KROPE // 128)),
                  pl.BlockSpec((1, 128), lambda b, c: (0, 0)),
                  pl.BlockSpec((1, D_MODEL), lambda b, c: (0, 0))],
        out_specs=pl.BlockSpec((rows, D_MODEL), lambda b, c: (row(b, c), 0)),
        out_shape=jax.ShapeDtypeStruct((t, D_MODEL), BF16),
        scratch_shapes=[pltpu.VMEM((HEADS, MLSTM_DK, 2 * MLSTM_DV), F32),
                        pltpu.VMEM((1, 128), F32)],
        compiler_params=pltpu.CompilerParams(dimension_semantics=("parallel", "arbitrary")),
        name="mlstm",
    )(main, main, main, main, tail, gate_bias, head_norm)


def _rms0(x, g_col):
    return x * lax.rsqrt(jnp.mean(x * x, axis=0, keepdims=True) + NORM_EPS) * g_col


def _rope(x, cs):
    half = MLA_ROPE // 2
    rot = jnp.concatenate([x[:, half:], x[:, :half]], axis=-1)
    return x * cs[:, :MLA_ROPE] + rot * cs[:, MLA_ROPE:]


def _mla_prep_kernel(tail_ref, cs_ref, cst_ref, qa_ref, wqt_ref, kva_ref, wk_ref, wvt_ref, qn_ref, kn_ref,
                     qt_out, k_out, vt_out):
    tm = tail_ref.shape[0]
    half = MLA_ROPE // 2
    cq = _rms(tail_ref[:, :Q_LORA], qa_ref[...]).astype(BF16)
    ckv = _rms(tail_ref[:, Q_LORA:Q_LORA + KV_LORA], kva_ref[...]).astype(BF16)
    qft = _dot_nt(wqt_ref[...], cq)
    kf = _dot(ckv, wk_ref[...])
    vft = _dot_nt(wvt_ref[...], ckv)
    cos_t, sin_t = cst_ref[:half, :], cst_ref[half:, :]
    qn_g, qp_g = qn_ref[:MLA_NOPE, :], qn_ref[MLA_NOPE:, :]
    kn_g, kp_g = kn_ref[:, :MLA_NOPE], kn_ref[:, MLA_NOPE:]
    k_pe = _rope(_rms(tail_ref[:, TAIL_KROPE:TAIL_KROPE + MLA_ROPE], kp_g), cs_ref[...]).astype(BF16)
    pe0 = HEADS * MLA_NOPE
    ones_rows = (lax.broadcasted_iota(jnp.int32, (V_ROWS - MLA_V, tm), 0) == 0).astype(BF16)
    for h in range(HEADS):
        q_nope = _rms0(qft[h * MLA_NOPE:(h + 1) * MLA_NOPE], qn_g)
        q_pe = _rms0(qft[pe0 + h * MLA_ROPE:pe0 + (h + 1) * MLA_ROPE], qp_g)
        x1, x2 = q_pe[:half], q_pe[half:]
        q_rot = jnp.concatenate([x1 * cos_t - x2 * sin_t, x1 * sin_t + x2 * cos_t], axis=0)
        qt_out[0, h, 0] = (jnp.concatenate([q_nope, q_rot, jnp.zeros((QK_PAD - MLA_NOPE - MLA_ROPE, tm), F32)],
                                           axis=0) * Q_SCALE).astype(BF16)
        k_nope = _rms(kf[:, h * MLA_NOPE:(h + 1) * MLA_NOPE], kn_g)
        k_out[0, h] = jnp.concatenate(
            [k_nope.astype(BF16), k_pe, jnp.zeros((tm, QK_PAD - MLA_NOPE - MLA_ROPE), BF16)], axis=-1)
        vt_out[0, h, 0] = jnp.concatenate([vft[h * MLA_V:(h + 1) * MLA_V].astype(BF16), ones_rows], axis=0)


def _mla_prep(tail, cs, cst, qa, wqt, kva, wk, wvt, qn_col, kn, bsz, seq):
    tm = ATT_T
    nt = seq // tm
    full = lambda shape: pl.BlockSpec(shape, lambda i: (0,) * len(shape))
    blocked = lambda rows: pl.BlockSpec((1, HEADS, 1, rows, tm), lambda i: (i // nt, 0, i % nt, 0, 0))
    return pl.pallas_call(
        _mla_prep_kernel,
        grid=(bsz * nt,),
        in_specs=[pl.BlockSpec((tm, TAIL_COLS), lambda i: (i, 0)),
                  pl.BlockSpec((tm, 2 * MLA_ROPE), lambda i: (i, 0)),
                  pl.BlockSpec((MLA_ROPE, tm), lambda i: (0, i)),
                  full((1, Q_LORA)), full(wqt.shape), full((1, KV_LORA)), full(wk.shape), full(wvt.shape),
                  full((MLA_NOPE + MLA_ROPE, 1)), full((1, MLA_NOPE + MLA_ROPE))],
        out_specs=[blocked(QK_PAD),
                   pl.BlockSpec((1, HEADS, tm, QK_PAD), lambda i: (i // nt, 0, i % nt, 0)),
                   blocked(V_ROWS)],
        out_shape=[jax.ShapeDtypeStruct((bsz, HEADS, nt, QK_PAD, tm), BF16),
                   jax.ShapeDtypeStruct((bsz, HEADS, seq, QK_PAD), BF16),
                   jax.ShapeDtypeStruct((bsz, HEADS, nt, V_ROWS, tm), BF16)],
        compiler_params=pltpu.CompilerParams(dimension_semantics=("parallel",),
                                             vmem_limit_bytes=VMEM_LIMIT),
        name="mla_prep",
    )(tail, cs, cst, qa, wqt, kva, wk, wvt, qn_col, kn)


def _attn_kernel(qt_ref, k_ref, vt_ref, o_ref, s_scr, p_scr, a_scr, m_all, acc_all, *, nq):
    t = ATT_T

    def qk(qi, j):
        k0 = pl.multiple_of(j * t, t)
        return _dot(k_ref[0, 0, pl.ds(k0, t), :], qt_ref[0, 0, qi])

    def accumulate(tile, alpha, p):
        q, j = tile
        acc_all[q] = alpha * acc_all[q] + _dot(vt_ref[0, 0, j], p)

    def stage(prev, cur, nxt, s_cur, p_prev, alpha_prev, masked):
        accumulate(prev, alpha_prev, p_prev)
        qi = cur[0]
        if masked:
            keep = lax.broadcasted_iota(jnp.int32, (t, t), 0) <= lax.broadcasted_iota(jnp.int32, (t, t), 1)
            s_cur = jnp.where(keep, s_cur, MASK_VALUE)
        m_old = m_all[qi]
        m_new = jnp.maximum(m_old, jnp.max(s_cur, axis=0, keepdims=True))
        m_all[qi] = m_new
        alpha = jnp.exp2(m_old - m_new)
        p = jnp.exp2((s_cur - m_new).astype(BF16))
        s_next = qk(jnp.minimum(nxt[0], nq - 1), jnp.minimum(nxt[1], nq - 1))
        return s_next, p, alpha

    def advance_lower(qi, j):
        wrap = j + 1 == qi
        return jnp.where(wrap, qi + 1, qi), jnp.where(wrap, 0, j + 1)

    def advance_diag(qi, j):
        return qi + 1, j + 1

    def run_phase(n_steps, first, advance, masked):
        if n_steps == 0:
            return
        s_scr[...] = qk(*first)
        p_scr[...] = jnp.zeros_like(p_scr)
        a_scr[...] = jnp.ones_like(a_scr)

        def steps(count, prev, cur):
            s_cur, p_prev, alpha_prev = s_scr[...], p_scr[...], a_scr[...]
            for _ in range(count):
                nxt = advance(*cur)
                s_cur, p_prev, alpha_prev = stage(prev, cur, nxt, s_cur, p_prev, alpha_prev, masked)
                prev, cur = cur, nxt
            s_scr[...] = s_cur
            p_scr[...] = p_prev
            a_scr[...] = alpha_prev
            return prev, cur

        first = (jnp.int32(first[0]), jnp.int32(first[1]))
        prev, cur = lax.fori_loop(0, n_steps // ATT_UNROLL, lambda _, c: steps(ATT_UNROLL, *c), (first, first))
        if n_steps % ATT_UNROLL:
            prev, cur = steps(n_steps % ATT_UNROLL, prev, cur)
        accumulate(prev, a_scr[...], p_scr[...])

    m_all[...] = jnp.full_like(m_all, MASK_VALUE)
    acc_all[...] = jnp.zeros_like(acc_all)
    run_phase(nq * (nq - 1) // 2, (1, 0), advance_lower, False)
    run_phase(nq, (0, 0), advance_diag, True)
    for qi in range(nq):
        acc = acc_all[qi]
        o_ref[0, qi * t:(qi + 1) * t, :] = (acc[:MLA_V] / acc[MLA_V:MLA_V + 1]).T.astype(o_ref.dtype)


def _attention(qt, k, vt):
    bsz, _, seq, _ = k.shape
    nq = seq // ATT_T
    kern = functools.partial(_attn_kernel, nq=nq)
    return pl.pallas_call(
        kern,
        grid=(bsz, HEADS),
        in_specs=[pl.BlockSpec((1, 1, nq, QK_PAD, ATT_T), lambda b, h: (b, h, 0, 0, 0)),
                  pl.BlockSpec((1, 1, seq, QK_PAD), lambda b, h: (b, h, 0, 0)),
                  pl.BlockSpec((1, 1, nq, V_ROWS, ATT_T), lambda b, h: (b, h, 0, 0, 0))],
        out_specs=pl.BlockSpec((1, seq, MLA_V), lambda b, h: (b, 0, h)),
        out_shape=jax.ShapeDtypeStruct((bsz, seq, HEADS * MLA_V), BF16),
        scratch_shapes=[pltpu.VMEM((ATT_T, ATT_T), F32), pltpu.VMEM((ATT_T, ATT_T), BF16),
                        pltpu.VMEM((1, ATT_T), F32), pltpu.VMEM((nq, 1, ATT_T), F32),
                        pltpu.VMEM((nq, V_ROWS, ATT_T), F32)],
        compiler_params=pltpu.CompilerParams(dimension_semantics=("parallel", "parallel"),
                                             vmem_limit_bytes=VMEM_LIMIT),
        name="attn",
    )(qt, k, vt)


def _merge_kernel(cb_ref, cc_ref, cu_ref, gate_ref, yb_ref, yc_ref, x_ref, cw_ref, wb_ref, wo_ref,
                  out_ref, zprev_ref, *, tiles_per_seq):
    tm = x_ref.shape[0]

    @pl.when(pl.program_id(0) % tiles_per_seq == 0)
    def _():
        zprev_ref[...] = jnp.zeros_like(zprev_ref)

    z = cc_ref[...].astype(F32) * cu_ref[...].astype(F32)
    row = lax.broadcasted_iota(jnp.int32, (tm, D_MODEL), 0)
    p1 = zprev_ref[7:8, :]
    p2 = zprev_ref[6:7, :]
    z1 = jnp.where(row == 0, p1, pltpu.roll(z, 1, 0))
    z2 = jnp.where(row == 0, p2, jnp.where(row == 1, p1, pltpu.roll(z, 2, 0)))
    zprev_ref[...] = z[tm - 8:, :]
    cw = cw_ref[...]
    y_a = cb_ref[...].astype(F32) * (cw[0:1] * z2 + cw[1:2] * z1 + cw[2:3] * z)

    g = gate_ref[...]
    merged = g[:, :D_MODEL].astype(F32) * _dot(y_a.astype(BF16), wb_ref[0])
    merged += g[:, D_MODEL:2 * D_MODEL].astype(F32) * _dot(yb_ref[...], wb_ref[1])
    merged += g[:, 2 * D_MODEL:].astype(F32) * _dot(yc_ref[...], wb_ref[2])
    out_ref[...] = x_ref[...] + _dot(merged.astype(BF16), wo_ref[...])


def _merge(main, y_b, y_c, x2, conv_w, w_branch, w_out, seq, tm):
    t = x2.shape[0]
    kern = functools.partial(_merge_kernel, tiles_per_seq=seq // tm)
    col = lambda c: pl.BlockSpec((tm, D_MODEL), lambda i: (i, c))
    return pl.pallas_call(
        kern,
        grid=(t // tm,),
        in_specs=[col(0), col(1), col(2),
                  pl.BlockSpec((tm, N_BRANCH * D_MODEL), lambda i: (i, GATE_COL0 // (N_BRANCH * D_MODEL))),
                  col(0), col(0), col(0),
                  pl.BlockSpec((CONV_WIDTH, D_MODEL), lambda i: (0, 0)),
                  pl.BlockSpec((N_BRANCH, D_MODEL, D_MODEL), lambda i: (0, 0, 0)),
                  pl.BlockSpec((D_MODEL, D_MODEL), lambda i: (0, 0))],
        out_specs=col(0),
        out_shape=jax.ShapeDtypeStruct((t, D_MODEL), F32),
        scratch_shapes=[pltpu.VMEM((8, D_MODEL), F32)],
        compiler_params=pltpu.CompilerParams(dimension_semantics=("arbitrary",),
                                             vmem_limit_bytes=VMEM_LIMIT),
        name="merge",
    )(main, main, main, main, y_b, y_c, x2, conv_w, w_branch, w_out)


def _mlp_kernel(x_ref, g_ref, wu_ref, wd_ref, out_ref, *, n_chunk):
    x = x_ref[...]
    h = _rms(x, g_ref[...]).astype(BF16)
    ck = D_FF // n_chunk
    acc = x
    for c in range(n_chunk):
        u = jnp.maximum(_dot(h, wu_ref[:, c * ck:(c + 1) * ck]), 0.0)
        acc = acc + _dot((u * u).astype(BF16), wd_ref[c * ck:(c + 1) * ck, :])
    out_ref[...] = acc


def _mlp(x2, g, w_up, w_down, tm):
    t = x2.shape[0]
    kern = functools.partial(_mlp_kernel, n_chunk=4)
    return pl.pallas_call(
        kern,
        grid=(t // tm,),
        in_specs=[pl.BlockSpec((tm, D_MODEL), lambda i: (i, 0)),
                  pl.BlockSpec((1, D_MODEL), lambda i: (0, 0)),
                  pl.BlockSpec((D_MODEL, D_FF), lambda i: (0, 0)),
                  pl.BlockSpec((D_FF, D_MODEL), lambda i: (0, 0))],
        out_specs=pl.BlockSpec((tm, D_MODEL), lambda i: (i, 0)),
        out_shape=jax.ShapeDtypeStruct((t, D_MODEL), F32),
        compiler_params=pltpu.CompilerParams(dimension_semantics=("parallel",),
                                             vmem_limit_bytes=VMEM_LIMIT),
        name="mlp",
    )(x2, g, w_up, w_down)


def _regroup_w_in(w_in):
    c = D_MODEL
    q0 = 3 * c
    i0 = q0 + 2 * HEADS * MLSTM_DK + 2 * HEADS * MLSTM_DV
    cq0 = i0 + 2 * HEADS
    ckv0 = cq0 + Q_LORA
    kr0 = ckv0 + KV_LORA
    g0 = kr0 + MLA_ROPE
    pad = jnp.zeros((D_MODEL, TAIL_COLS - (Q_LORA + KV_LORA + MLA_ROPE + 2 * HEADS)), w_in.dtype)
    w_main = jnp.concatenate([w_in[:, :i0], w_in[:, g0:]], axis=1).astype(BF16)
    w_tail = jnp.concatenate([w_in[:, cq0:g0], w_in[:, i0:cq0], pad], axis=1).astype(BF16)
    return w_main, w_tail


def _heads_last_split(w, first, second):
    k = w.shape[0]
    w3 = w.reshape(k, HEADS, first + second)
    return jnp.concatenate([w3[:, :, :first].reshape(k, HEADS * first),
                            w3[:, :, first:].reshape(k, HEADS * second)], axis=1).astype(BF16)


def _pick(n, candidates):
    for c in candidates:
        if n % c == 0:
            return c
    raise ValueError(f"no tile size for {n}")


def kernel(x, positions, mix_norm, w_in, conv_w, mlstm_igate_bias, mlstm_fgate_bias, mlstm_head_norm,
           mla_q_a_norm, mla_w_uq, mla_kv_a_norm, mla_w_ukv, mla_q_norm, mla_k_norm, w_branch, w_out,
           mlp_norm, w_up, w_down):
    bsz, seq, _ = x.shape
    depth = w_in.shape[0]
    t = bsz * seq
    assert seq % ATT_T == 0 and seq % (CHUNK * MLSTM_GROUP) == 0
    tm_proj = _pick(t, (1024, 512, 256, 128))
    tm = _pick(seq, (512, 256, 128))

    x2 = x.reshape(t, D_MODEL)
    cs, cst = _rope_table(positions, ATT_T)
    row = lambda a: a.reshape(1, -1).astype(F32)

    for l in range(depth):
        gate_bias = jnp.zeros((1, 128), F32)
        gate_bias = gate_bias.at[0, IGATE_LANE:IGATE_LANE + HEADS].set(mlstm_igate_bias[l])
        gate_bias = gate_bias.at[0, FGATE_LANE:FGATE_LANE + HEADS].set(mlstm_fgate_bias[l])
        w_q = _heads_last_split(mla_w_uq[l], MLA_NOPE, MLA_ROPE)
        w_kv = _heads_last_split(mla_w_ukv[l], MLA_NOPE, MLA_V)

        main, tail = _in_proj(x2, row(mix_norm[l]), *_regroup_w_in(w_in[l]), tm_proj, IN_PROJ_TN)
        y_b = _mlstm(main, tail, gate_bias, row(mlstm_head_norm[l]), bsz, seq, MLSTM_GROUP)
        qt, k, vt = _mla_prep(tail, cs, cst, row(mla_q_a_norm[l]), w_q.T, row(mla_kv_a_norm[l]),
                              w_kv[:, :HEADS * MLA_NOPE], w_kv[:, HEADS * MLA_NOPE:].T,
                              mla_q_norm[l].reshape(-1, 1).astype(F32), row(mla_k_norm[l]), bsz, seq)
        y_c = _attention(qt, k, vt).reshape(t, D_MODEL)
        x2 = _merge(main, y_b, y_c, x2, conv_w[l].astype(F32), w_branch[l].astype(BF16),
                    w_out[l].astype(BF16), seq, tm)
        x2 = _mlp(x2, row(mlp_norm[l]), w_up[l].astype(BF16), w_down[l].astype(BF16), tm)
    return x2.reshape(bsz, seq, D_MODEL)
```

```python
import functools

import jax
import jax.numpy as jnp
import numpy as np
from jax import lax
from jax.experimental import pallas as pl
from jax.experimental.pallas import tpu as pltpu

F32 = jnp.float32
BF16 = jnp.bfloat16

D_MODEL = 1024
N_BRANCH = 3
CONV_WIDTH = 3
HEADS = 8
MLSTM_DK = 64
MLSTM_DV = 128
CHUNK = 128
GATE_CAP = 15.0
MLA_NOPE = 128
MLA_ROPE = 64
MLA_V = 128
Q_LORA = 256
KV_LORA = 128
ROPE_BASE = 10000.0
D_FF = 4 * D_MODEL
NORM_EPS = 1e-6
QK_PAD = 256

MAIN_COLS = 9 * D_MODEL
TAIL_COLS = 512
GATE_COL0 = 6 * D_MODEL
TAIL_KROPE = Q_LORA + KV_LORA
IGATE_LANE = MLA_ROPE
FGATE_LANE = MLA_ROPE + HEADS

IN_PROJ_TN = 1536
MLSTM_GROUP = 2
ATT_T = 512
LOG2E = 1.4426950408889634
Q_SCALE = (MLA_NOPE + MLA_ROPE) ** -0.5 * LOG2E
MASK_VALUE = -1e30
SCORE_BOUND = 80.0
BOUND_SLACK = 1.02
ATT_UNROLL = 4
V_ROWS = MLA_V + 16

VMEM_LIMIT = 56 * 1024 * 1024


def _rms(x, g):
    return x * lax.rsqrt(jnp.mean(x * x, axis=-1, keepdims=True) + NORM_EPS) * g


def _dot(a, b):
    return jnp.dot(a, b, preferred_element_type=F32)


def _dot_nt(a, b):
    return lax.dot_general(a, b, (((1,), (1,)), ((), ())), preferred_element_type=F32)


def _rope_table_kernel(pos_col_ref, pos_row_ref, freq_row_ref, freq_col_ref, cs_ref, cst_ref):
    ang = pos_col_ref[...].astype(F32) * freq_row_ref[...]
    c, s = jnp.cos(ang), jnp.sin(ang)
    cs_ref[...] = jnp.concatenate([c, c, -s, s], axis=-1)
    ang_t = freq_col_ref[...] * pos_row_ref[...].astype(F32)
    cst_ref[...] = jnp.concatenate([jnp.cos(ang_t), jnp.sin(ang_t)], axis=0)


def _rope_table(positions, tm):
    t = positions.size
    half = MLA_ROPE // 2
    inv_freq = (np.float32(ROPE_BASE) ** (-np.arange(0, MLA_ROPE, 2, dtype=np.float32) / np.float32(MLA_ROPE)))
    inv_freq = jnp.asarray(inv_freq.astype(np.float32))
    return pl.pallas_call(
        _rope_table_kernel,
        grid=(t // tm,),
        in_specs=[pl.BlockSpec((tm, 1), lambda i: (i, 0)),
                  pl.BlockSpec((1, tm), lambda i: (0, i)),
                  pl.BlockSpec((1, half), lambda i: (0, 0)),
                  pl.BlockSpec((half, 1), lambda i: (0, 0))],
        out_specs=[pl.BlockSpec((tm, 2 * MLA_ROPE), lambda i: (i, 0)),
                   pl.BlockSpec((MLA_ROPE, tm), lambda i: (0, i))],
        out_shape=[jax.ShapeDtypeStruct((t, 2 * MLA_ROPE), F32),
                   jax.ShapeDtypeStruct((MLA_ROPE, t), F32)],
        name="rope_table",
    )(positions.reshape(t, 1), positions.reshape(1, t), inv_freq.reshape(1, half), inv_freq.reshape(half, 1))


def _in_proj_kernel(x_ref, g_ref, w_ref, wt_ref, main_ref, tail_ref, xn_ref, *, n_gate0):
    j = pl.program_id(1)

    @pl.when(j == 0)
    def _():
        xn = _rms(x_ref[...], g_ref[...]).astype(BF16)
        xn_ref[...] = xn
        tail_ref[...] = _dot(xn, wt_ref[...])

    @pl.when(j < n_gate0)
    def _():
        main_ref[...] = _dot(xn_ref[...], w_ref[...]).astype(BF16)

    @pl.when(j >= n_gate0)
    def _():
        main_ref[...] = jax.nn.sigmoid(_dot(xn_ref[...], w_ref[...])).astype(BF16)


def _in_proj(x2, g, w_main, w_tail, tm, tn):
    t = x2.shape[0]
    assert GATE_COL0 % tn == 0 and MAIN_COLS % tn == 0
    kern = functools.partial(_in_proj_kernel, n_gate0=GATE_COL0 // tn)
    return pl.pallas_call(
        kern,
        grid=(t // tm, MAIN_COLS // tn),
        in_specs=[pl.BlockSpec((tm, D_MODEL), lambda i, j: (i, 0)),
                  pl.BlockSpec((1, D_MODEL), lambda i, j: (0, 0)),
                  pl.BlockSpec((D_MODEL, tn), lambda i, j: (0, j)),
                  pl.BlockSpec((D_MODEL, TAIL_COLS), lambda i, j: (0, 0))],
        out_specs=[pl.BlockSpec((tm, tn), lambda i, j: (i, j)),
                   pl.BlockSpec((tm, TAIL_COLS), lambda i, j: (i, 0))],
        out_shape=[jax.ShapeDtypeStruct((t, MAIN_COLS), BF16),
                   jax.ShapeDtypeStruct((t, TAIL_COLS), F32)],
        scratch_shapes=[pltpu.VMEM((tm, D_MODEL), BF16)],
        compiler_params=pltpu.CompilerParams(
            dimension_semantics=("parallel", "arbitrary"), vmem_limit_bytes=VMEM_LIMIT),
        name="in_proj",
    )(x2, g, w_main, w_tail)


def _time_scan(x, op, row):
    s = 1
    while s < CHUNK:
        x = jnp.where(row >= s, op(x, pltpu.roll(x, s, 0)), x)
        s *= 2
    return x


def _mlstm_kernel(q_ref, k_ref, v_ref, o_ref, gate_ref, bias_ref, hn_ref, y_ref, c_ref, m_ref, *, n_chunk):
    L = CHUNK

    @pl.when(pl.program_id(1) == 0)
    def _():
        c_ref[...] = jnp.zeros_like(c_ref)
        m_ref[...] = jnp.zeros_like(m_ref)

    row = lax.broadcasted_iota(jnp.int32, (L, 128), 0)
    lane = lax.broadcasted_iota(jnp.int32, (L, 128), 1)
    gate_lanes = jnp.logical_and(lane >= IGATE_LANE, lane < FGATE_LANE + HEADS)
    tri = (lax.broadcasted_iota(jnp.int32, (L, L), 0) >= lax.broadcasted_iota(jnp.int32, (L, L), 1))
    ones_col = (lax.broadcasted_iota(jnp.int32, (L, MLSTM_DV), 1) == 0).astype(BF16)
    m_prev = m_ref[...]
    c_state = [c_ref[h] for h in range(HEADS)]

    for g in range(n_chunk):
        rows = slice(g * L, (g + 1) * L)
        pre = jnp.where(gate_lanes, gate_ref[rows, :] + bias_ref[...], 0.0)
        capped = GATE_CAP * jnp.tanh(pre * (1.0 / GATE_CAP))
        log_i = capped * LOG2E
        log_f = (jnp.minimum(capped, 0.0) - jnp.log1p(jnp.exp(-jnp.abs(capped)))) * LOG2E
        log_f = pltpu.roll(log_f, 128 - HEADS, 1)
        b = _time_scan(log_f, jnp.add, row)
        b_last = b[L - 1:L, :]
        rowd = log_i - b
        big_m = jnp.maximum(_time_scan(rowd, jnp.maximum, row), m_prev)
        e_neg_m = jnp.exp2(-(b + big_m))
        a = b_last + rowd
        m_loc = jnp.max(a, axis=0, keepdims=True)
        m_new = jnp.maximum(b_last + m_prev, m_loc)
        s_old = jnp.exp2(b_last + m_prev - m_new)
        w = jnp.exp2(a - m_new)
        m_prev_g, m_prev = m_prev, m_new
        rowd_t = rowd.T
        w_t = w.T
        q_all = q_ref[rows, :] * (MLSTM_DK ** -0.5)
        k_t = k_ref[rows, :].astype(F32).T

        hk = [slice(h * MLSTM_DK, (h + 1) * MLSTM_DK) for h in range(HEADS)]
        hv = [slice(h * MLSTM_DV, (h + 1) * MLSTM_DV) for h in range(HEADS)]
        gl = [slice(IGATE_LANE + h, IGATE_LANE + h + 1) for h in range(HEADS)]
        v_aug = [jnp.concatenate([v_ref[rows, hv[h]], ones_col], axis=1) for h in range(HEADS)]
        scores = [_dot(q_all[:, hk[h]], k_t[hk[h], :].astype(BF16)) for h in range(HEADS)]
        c_loc = [_dot((k_t[hk[h], :] * w_t[gl[h], :]).astype(BF16), v_aug[h]) for h in range(HEADS)]
        num_aug = []
        for h in range(HEADS):
            big_m_b = jnp.broadcast_to(big_m[:, gl[h]], (L, L))
            dmat = jnp.where(tri, jnp.exp2(rowd_t[gl[h], :] - big_m_b), 0.0)
            p = (scores[h] * dmat).astype(BF16)
            q_state = (q_all[:, hk[h]].astype(F32)
                       * jnp.exp2(m_prev_g[:, gl[h]] - big_m_b[:, :MLSTM_DK])).astype(BF16)
            lhs = jnp.concatenate([p, q_state, jnp.zeros((L, L - MLSTM_DK), BF16)], axis=1)
            rhs = jnp.concatenate([v_aug[h], c_state[h].astype(BF16),
                                   jnp.zeros((L - MLSTM_DK, 2 * MLSTM_DV), BF16)], axis=0)
            num_aug.append(_dot(lhs, rhs))
        for h in range(HEADS):
            num = num_aug[h][:, :MLSTM_DV]
            d = jnp.maximum(jnp.abs(num_aug[h][:, MLSTM_DV:MLSTM_DV + 1]), e_neg_m[:, gl[h]])
            scale = lax.rsqrt(jnp.mean(num * num, axis=-1, keepdims=True) + NORM_EPS * d * d)
            og = jax.nn.sigmoid(o_ref[rows, hv[h]].astype(F32))
            y_ref[rows, hv[h]] = (num * scale * hn_ref[:, hv[h]] * og).astype(y_ref.dtype)
            c_state[h] = s_old[:, gl[h]] * c_state[h] + c_loc[h]

    m_ref[...] = m_prev
    for h in range(HEADS):
        c_ref[h] = c_state[h]


def _mlstm(main, tail, gate_bias, head_norm, bsz, seq, n_chunk):
    t = main.shape[0]
    rows = n_chunk * CHUNK
    nc = seq // rows
    row = lambda b, c: b * nc + c
    return pl.pallas_call(
        functools.partial(_mlstm_kernel, n_chunk=n_chunk),
        grid=(bsz, nc),
        in_specs=[pl.BlockSpec((rows, 512), lambda b, c: (row(b, c), 6)),
                  pl.BlockSpec((rows, 512), lambda b, c: (row(b, c), 7)),
                  pl.BlockSpec((rows, D_MODEL), lambda b, c: (row(b, c), 4)),
                  pl.BlockSpec((rows, D_MODEL), lambda b, c: (row(b, c), 5)),
                  pl.BlockSpec((rows, 128), lambda b, c: (row(b, c), TAIL_KROPE // 128)),
                  pl.BlockSpec((1, 128), lambda b, c: (0, 0)),
                  pl.BlockSpec((1, D_MODEL), lambda b, c: (0, 0))],
        out_specs=pl.BlockSpec((rows, D_MODEL), lambda b, c: (row(b, c), 0)),
        out_shape=jax.ShapeDtypeStruct((t, D_MODEL), BF16),
        scratch_shapes=[pltpu.VMEM((HEADS, MLSTM_DK, 2 * MLSTM_DV), F32),
                        pltpu.VMEM((1, 128), F32)],
        compiler_params=pltpu.CompilerParams(dimension_semantics=("parallel", "arbitrary")),
        name="mlstm",
    )(main, main, main, main, tail, gate_bias, head_norm)


def _rms0(x, g_col):
    return x * lax.rsqrt(jnp.mean(x * x, axis=0, keepdims=True) + NORM_EPS) * g_col


def _rope(x, cs):
    half = MLA_ROPE // 2
    rot = jnp.concatenate([x[:, half:], x[:, :half]], axis=-1)
    return x * cs[:, :MLA_ROPE] + rot * cs[:, MLA_ROPE:]


def _mla_prep_kernel(tail_ref, cs_ref, cst_ref, qa_ref, wqt_ref, kva_ref, wk_ref, wvt_ref, qn_ref, kn_ref,
                     qt_out, k_out, vt_out):
    tm = tail_ref.shape[0]
    half = MLA_ROPE // 2
    cq = _rms(tail_ref[:, :Q_LORA], qa_ref[...]).astype(BF16)
    ckv = _rms(tail_ref[:, Q_LORA:Q_LORA + KV_LORA], kva_ref[...]).astype(BF16)
    qft = _dot_nt(wqt_ref[...], cq)
    kf = _dot(ckv, wk_ref[...])
    vft = _dot_nt(wvt_ref[...], ckv)
    cos_t, sin_t = cst_ref[:half, :], cst_ref[half:, :]
    qn_g, qp_g = qn_ref[:MLA_NOPE, :], qn_ref[MLA_NOPE:, :]
    kn_g, kp_g = kn_ref[:, :MLA_NOPE], kn_ref[:, MLA_NOPE:]
    k_pe = _rope(_rms(tail_ref[:, TAIL_KROPE:TAIL_KROPE + MLA_ROPE], kp_g), cs_ref[...]).astype(BF16)
    pe0 = HEADS * MLA_NOPE
    ones_rows = (lax.broadcasted_iota(jnp.int32, (V_ROWS - MLA_V, tm), 0) == 0).astype(BF16)
    for h in range(HEADS):
        q_nope = _rms0(qft[h * MLA_NOPE:(h + 1) * MLA_NOPE], qn_g)
        q_pe = _rms0(qft[pe0 + h * MLA_ROPE:pe0 + (h + 1) * MLA_ROPE], qp_g)
        x1, x2 = q_pe[:half], q_pe[half:]
        q_rot = jnp.concatenate([x1 * cos_t - x2 * sin_t, x1 * sin_t + x2 * cos_t], axis=0)
        qt_out[0, h, 0] = (jnp.concatenate([q_nope, q_rot, jnp.zeros((QK_PAD - MLA_NOPE - MLA_ROPE, tm), F32)],
                                           axis=0) * Q_SCALE).astype(BF16)
        k_nope = _rms(kf[:, h * MLA_NOPE:(h + 1) * MLA_NOPE], kn_g)
        k_out[0, h] = jnp.concatenate(
            [k_nope.astype(BF16), k_pe, jnp.zeros((tm, QK_PAD - MLA_NOPE - MLA_ROPE), BF16)], axis=-1)
        vt_out[0, h, 0] = jnp.concatenate([vft[h * MLA_V:(h + 1) * MLA_V].astype(BF16), ones_rows], axis=0)


def _mla_prep(tail, cs, cst, qa, wqt, kva, wk, wvt, qn_col, kn, bsz, seq):
    tm = ATT_T
    nt = seq // tm
    full = lambda shape: pl.BlockSpec(shape, lambda i: (0,) * len(shape))
    blocked = lambda rows: pl.BlockSpec((1, HEADS, 1, rows, tm), lambda i: (i // nt, 0, i % nt, 0, 0))
    return pl.pallas_call(
        _mla_prep_kernel,
        grid=(bsz * nt,),
        in_specs=[pl.BlockSpec((tm, TAIL_COLS), lambda i: (i, 0)),
                  pl.BlockSpec((tm, 2 * MLA_ROPE), lambda i: (i, 0)),
                  pl.BlockSpec((MLA_ROPE, tm), lambda i: (0, i)),
                  full((1, Q_LORA)), full(wqt.shape), full((1, KV_LORA)), full(wk.shape), full(wvt.shape),
                  full((MLA_NOPE + MLA_ROPE, 1)), full((1, MLA_NOPE + MLA_ROPE))],
        out_specs=[blocked(QK_PAD),
                   pl.BlockSpec((1, HEADS, tm, QK_PAD), lambda i: (i // nt, 0, i % nt, 0)),
                   blocked(V_ROWS)],
        out_shape=[jax.ShapeDtypeStruct((bsz, HEADS, nt, QK_PAD, tm), BF16),
                   jax.ShapeDtypeStruct((bsz, HEADS, seq, QK_PAD), BF16),
                   jax.ShapeDtypeStruct((bsz, HEADS, nt, V_ROWS, tm), BF16)],
        compiler_params=pltpu.CompilerParams(dimension_semantics=("parallel",),
                                             vmem_limit_bytes=VMEM_LIMIT),
        name="mla_prep",
    )(tail, cs, cst, qa, wqt, kva, wk, wvt, qn_col, kn)


def _attn_kernel(qt_ref, k_ref, vt_ref, o_ref, s_scr, p_scr, a_scr, m_all, acc_all, *, nq):
    t = ATT_T

    def qk(qi, j):
        k0 = pl.multiple_of(j * t, t)
        return _dot(k_ref[0, 0, pl.ds(k0, t), :], qt_ref[0, 0, qi])

    def accumulate(tile, alpha, p):
        q, j = tile
        acc_all[q] = alpha * acc_all[q] + _dot(vt_ref[0, 0, j], p)

    def stage(prev, cur, nxt, s_cur, p_prev, alpha_prev, masked):
        accumulate(prev, alpha_prev, p_prev)
        qi = cur[0]
        if masked:
            keep = lax.broadcasted_iota(jnp.int32, (t, t), 0) <= lax.broadcasted_iota(jnp.int32, (t, t), 1)
            s_cur = jnp.where(keep, s_cur, MASK_VALUE)
        m_old = m_all[qi]
        m_new = jnp.maximum(m_old, jnp.max(s_cur, axis=0, keepdims=True))
        m_all[qi] = m_new
        alpha = jnp.exp2(m_old - m_new)
        p = jnp.exp2((s_cur - m_new).astype(BF16))
        s_next = qk(jnp.minimum(nxt[0], nq - 1), jnp.minimum(nxt[1], nq - 1))
        return s_next, p, alpha

    def advance_lower(qi, j):
        wrap = j + 1 == qi
        return jnp.where(wrap, qi + 1, qi), jnp.where(wrap, 0, j + 1)

    def advance_diag(qi, j):
        return qi + 1, j + 1

    def run_phase(n_steps, first, advance, masked):
        if n_steps == 0:
            return
        s_scr[...] = qk(*first)
        p_scr[...] = jnp.zeros_like(p_scr)
        a_scr[...] = jnp.ones_like(a_scr)

        def steps(count, prev, cur):
            s_cur, p_prev, alpha_prev = s_scr[...], p_scr[...], a_scr[...]
            for _ in range(count):
                nxt = advance(*cur)
                s_cur, p_prev, alpha_prev = stage(prev, cur, nxt, s_cur, p_prev, alpha_prev, masked)
                prev, cur = cur, nxt
            s_scr[...] = s_cur
            p_scr[...] = p_prev
            a_scr[...] = alpha_prev
            return prev, cur

        first = (jnp.int32(first[0]), jnp.int32(first[1]))
        prev, cur = lax.fori_loop(0, n_steps // ATT_UNROLL, lambda _, c: steps(ATT_UNROLL, *c), (first, first))
        if n_steps % ATT_UNROLL:
            prev, cur = steps(n_steps % ATT_UNROLL, prev, cur)
        accumulate(prev, a_scr[...], p_scr[...])

    m_all[...] = jnp.full_like(m_all, MASK_VALUE)
    acc_all[...] = jnp.zeros_like(acc_all)
    run_phase(nq * (nq - 1) // 2, (1, 0), advance_lower, False)
    run_phase(nq, (0, 0), advance_diag, True)
    for qi in range(nq):
        acc = acc_all[qi]
        o_ref[0, qi * t:(qi + 1) * t, :] = (acc[:MLA_V] / acc[MLA_V:MLA_V + 1]).T.astype(o_ref.dtype)


def _attn_bounded_kernel(qt_ref, k_ref, vt_ref, o_ref, s_scr, p_scr, acc_all, *, nq):
    t = ATT_T

    def qk(qi, j):
        k0 = pl.multiple_of(j * t, t)
        return _dot(k_ref[0, 0, pl.ds(k0, t), :], qt_ref[0, 0, qi])

    def accumulate(tile, p):
        q, j = tile
        acc_all[q] += _dot(vt_ref[0, 0, j], p)

    def stage(prev, cur, nxt, s_cur, p_prev, masked):
        accumulate(prev, p_prev)
        p = jnp.exp2(s_cur)
        if masked:
            keep = lax.broadcasted_iota(jnp.int32, (t, t), 0) <= lax.broadcasted_iota(jnp.int32, (t, t), 1)
            p = jnp.where(keep, p, 0.0)
        s_next = qk(jnp.minimum(nxt[0], nq - 1), jnp.minimum(nxt[1], nq - 1))
        return s_next, p.astype(BF16)

    def advance_lower(qi, j):
        wrap = j + 1 == qi
        return jnp.where(wrap, qi + 1, qi), jnp.where(wrap, 0, j + 1)

    def advance_diag(qi, j):
        return qi + 1, j + 1

    def run_phase(n_steps, first, advance, masked):
        if n_steps == 0:
            return
        s_scr[...] = qk(*first)
        p_scr[...] = jnp.zeros_like(p_scr)

        def steps(count, prev, cur):
            s_cur, p_prev = s_scr[...], p_scr[...]
            for _ in range(count):
                nxt = advance(*cur)
                s_cur, p_prev = stage(prev, cur, nxt, s_cur, p_prev, masked)
                prev, cur = cur, nxt
            s_scr[...] = s_cur
            p_scr[...] = p_prev
            return prev, cur

        first = (jnp.int32(first[0]), jnp.int32(first[1]))
        prev, cur = lax.fori_loop(0, n_steps // ATT_UNROLL, lambda _, c: steps(ATT_UNROLL, *c), (first, first))
        if n_steps % ATT_UNROLL:
            prev, cur = steps(n_steps % ATT_UNROLL, prev, cur)
        accumulate(prev, p_scr[...])

    acc_all[...] = jnp.zeros_like(acc_all)
    run_phase(nq * (nq - 1) // 2, (1, 0), advance_lower, False)
    run_phase(nq, (0, 0), advance_diag, True)
    for qi in range(nq):
        acc = acc_all[qi]
        o_ref[0, qi * t:(qi + 1) * t, :] = (acc[:MLA_V] / acc[MLA_V:MLA_V + 1]).T.astype(o_ref.dtype)


def _attention_bounded(qt, k, vt):
    bsz, _, seq, _ = k.shape
    nq = seq // ATT_T
    kern = functools.partial(_attn_bounded_kernel, nq=nq)
    return pl.pallas_call(
        kern,
        grid=(bsz, HEADS),
        in_specs=[pl.BlockSpec((1, 1, nq, QK_PAD, ATT_T), lambda b, h: (b, h, 0, 0, 0)),
                  pl.BlockSpec((1, 1, seq, QK_PAD), lambda b, h: (b, h, 0, 0)),
                  pl.BlockSpec((1, 1, nq, V_ROWS, ATT_T), lambda b, h: (b, h, 0, 0, 0))],
        out_specs=pl.BlockSpec((1, seq, MLA_V), lambda b, h: (b, 0, h)),
        out_shape=jax.ShapeDtypeStruct((bsz, seq, HEADS * MLA_V), BF16),
        scratch_shapes=[pltpu.VMEM((ATT_T, ATT_T), F32), pltpu.VMEM((ATT_T, ATT_T), BF16),
                        pltpu.VMEM((nq, V_ROWS, ATT_T), F32)],
        compiler_params=pltpu.CompilerParams(dimension_semantics=("parallel", "parallel"),
                                             vmem_limit_bytes=VMEM_LIMIT),
        name="attn_bounded",
    )(qt, k, vt)


def _attention(qt, k, vt):
    bsz, _, seq, _ = k.shape
    nq = seq // ATT_T
    kern = functools.partial(_attn_kernel, nq=nq)
    return pl.pallas_call(
        kern,
        grid=(bsz, HEADS),
        in_specs=[pl.BlockSpec((1, 1, nq, QK_PAD, ATT_T), lambda b, h: (b, h, 0, 0, 0)),
                  pl.BlockSpec((1, 1, seq, QK_PAD), lambda b, h: (b, h, 0, 0)),
                  pl.BlockSpec((1, 1, nq, V_ROWS, ATT_T), lambda b, h: (b, h, 0, 0, 0))],
        out_specs=pl.BlockSpec((1, seq, MLA_V), lambda b, h: (b, 0, h)),
        out_shape=jax.ShapeDtypeStruct((bsz, seq, HEADS * MLA_V), BF16),
        scratch_shapes=[pltpu.VMEM((ATT_T, ATT_T), F32), pltpu.VMEM((ATT_T, ATT_T), BF16),
                        pltpu.VMEM((1, ATT_T), F32), pltpu.VMEM((nq, 1, ATT_T), F32),
                        pltpu.VMEM((nq, V_ROWS, ATT_T), F32)],
        compiler_params=pltpu.CompilerParams(dimension_semantics=("parallel", "parallel"),
                                             vmem_limit_bytes=VMEM_LIMIT),
        name="attn",
    )(qt, k, vt)


def _merge_kernel(cb_ref, cc_ref, cu_ref, gate_ref, yb_ref, yc_ref, x_ref, cw_ref, wb_ref, wo_ref,
                  out_ref, zprev_ref, *, tiles_per_seq):
    tm = x_ref.shape[0]

    @pl.when(pl.program_id(0) % tiles_per_seq == 0)
    def _():
        zprev_ref[...] = jnp.zeros_like(zprev_ref)

    z = cc_ref[...].astype(F32) * cu_ref[...].astype(F32)
    row = lax.broadcasted_iota(jnp.int32, (tm, D_MODEL), 0)
    p1 = zprev_ref[7:8, :]
    p2 = zprev_ref[6:7, :]
    z1 = jnp.where(row == 0, p1, pltpu.roll(z, 1, 0))
    z2 = jnp.where(row == 0, p2, jnp.where(row == 1, p1, pltpu.roll(z, 2, 0)))
    zprev_ref[...] = z[tm - 8:, :]
    cw = cw_ref[...]
    y_a = cb_ref[...].astype(F32) * (cw[0:1] * z2 + cw[1:2] * z1 + cw[2:3] * z)

    g = gate_ref[...]
    merged = g[:, :D_MODEL].astype(F32) * _dot(y_a.astype(BF16), wb_ref[0])
    merged += g[:, D_MODEL:2 * D_MODEL].astype(F32) * _dot(yb_ref[...], wb_ref[1])
    merged += g[:, 2 * D_MODEL:].astype(F32) * _dot(yc_ref[...], wb_ref[2])
    out_ref[...] = x_ref[...] + _dot(merged.astype(BF16), wo_ref[...])


def _merge(main, y_b, y_c, x2, conv_w, w_branch, w_out, seq, tm):
    t = x2.shape[0]
    kern = functools.partial(_merge_kernel, tiles_per_seq=seq // tm)
    col = lambda c: pl.BlockSpec((tm, D_MODEL), lambda i: (i, c))
    return pl.pallas_call(
        kern,
        grid=(t // tm,),
        in_specs=[col(0), col(1), col(2),
                  pl.BlockSpec((tm, N_BRANCH * D_MODEL), lambda i: (i, GATE_COL0 // (N_BRANCH * D_MODEL))),
                  col(0), col(0), col(0),
                  pl.BlockSpec((CONV_WIDTH, D_MODEL), lambda i: (0, 0)),
                  pl.BlockSpec((N_BRANCH, D_MODEL, D_MODEL), lambda i: (0, 0, 0)),
                  pl.BlockSpec((D_MODEL, D_MODEL), lambda i: (0, 0))],
        out_specs=col(0),
        out_shape=jax.ShapeDtypeStruct((t, D_MODEL), F32),
        scratch_shapes=[pltpu.VMEM((8, D_MODEL), F32)],
        compiler_params=pltpu.CompilerParams(dimension_semantics=("arbitrary",),
                                             vmem_limit_bytes=VMEM_LIMIT),
        name="merge",
    )(main, main, main, main, y_b, y_c, x2, conv_w, w_branch, w_out)


def _mlp_kernel(x_ref, g_ref, wu_ref, wd_ref, out_ref, *, n_chunk):
    x = x_ref[...]
    h = _rms(x, g_ref[...]).astype(BF16)
    ck = D_FF // n_chunk
    acc = x
    for c in range(n_chunk):
        u = jnp.maximum(_dot(h, wu_ref[:, c * ck:(c + 1) * ck]), 0.0)
        acc = acc + _dot((u * u).astype(BF16), wd_ref[c * ck:(c + 1) * ck, :])
    out_ref[...] = acc


def _mlp(x2, g, w_up, w_down, tm):
    t = x2.shape[0]
    kern = functools.partial(_mlp_kernel, n_chunk=4)
    return pl.pallas_call(
        kern,
        grid=(t // tm,),
        in_specs=[pl.BlockSpec((tm, D_MODEL), lambda i: (i, 0)),
                  pl.BlockSpec((1, D_MODEL), lambda i: (0, 0)),
                  pl.BlockSpec((D_MODEL, D_FF), lambda i: (0, 0)),
                  pl.BlockSpec((D_FF, D_MODEL), lambda i: (0, 0))],
        out_specs=pl.BlockSpec((tm, D_MODEL), lambda i: (i, 0)),
        out_shape=jax.ShapeDtypeStruct((t, D_MODEL), F32),
        compiler_params=pltpu.CompilerParams(dimension_semantics=("parallel",),
                                             vmem_limit_bytes=VMEM_LIMIT),
        name="mlp",
    )(x2, g, w_up, w_down)


def _regroup_w_in(w_in):
    c = D_MODEL
    q0 = 3 * c
    i0 = q0 + 2 * HEADS * MLSTM_DK + 2 * HEADS * MLSTM_DV
    cq0 = i0 + 2 * HEADS
    ckv0 = cq0 + Q_LORA
    kr0 = ckv0 + KV_LORA
    g0 = kr0 + MLA_ROPE
    pad = jnp.zeros((D_MODEL, TAIL_COLS - (Q_LORA + KV_LORA + MLA_ROPE + 2 * HEADS)), w_in.dtype)
    w_main = jnp.concatenate([w_in[:, :i0], w_in[:, g0:]], axis=1).astype(BF16)
    w_tail = jnp.concatenate([w_in[:, cq0:g0], w_in[:, i0:cq0], pad], axis=1).astype(BF16)
    return w_main, w_tail


def _heads_last_split(w, first, second):
    k = w.shape[0]
    w3 = w.reshape(k, HEADS, first + second)
    return jnp.concatenate([w3[:, :, :first].reshape(k, HEADS * first),
                            w3[:, :, first:].reshape(k, HEADS * second)], axis=1).astype(BF16)


def _score_bound(q_gain, k_gain):
    def sq_norm(g):
        g = g.astype(F32)
        return MLA_NOPE * jnp.max(g[:MLA_NOPE] ** 2) + MLA_ROPE * jnp.max(g[MLA_NOPE:] ** 2)
    return jnp.sqrt(sq_norm(q_gain) * sq_norm(k_gain)) * (Q_SCALE * BOUND_SLACK)


def _pick(n, candidates):
    for c in candidates:
        if n % c == 0:
            return c
    raise ValueError(f"no tile size for {n}")


def kernel(x, positions, mix_norm, w_in, conv_w, mlstm_igate_bias, mlstm_fgate_bias, mlstm_head_norm,
           mla_q_a_norm, mla_w_uq, mla_kv_a_norm, mla_w_ukv, mla_q_norm, mla_k_norm, w_branch, w_out,
           mlp_norm, w_up, w_down):
    bsz, seq, _ = x.shape
    depth = w_in.shape[0]
    t = bsz * seq
    assert seq % ATT_T == 0 and seq % (CHUNK * MLSTM_GROUP) == 0
    tm_proj = _pick(t, (1024, 512, 256, 128))
    tm = _pick(seq, (512, 256, 128))

    x2 = x.reshape(t, D_MODEL)
    cs, cst = _rope_table(positions, ATT_T)
    row = lambda a: a.reshape(1, -1).astype(F32)

    for l in range(depth):
        gate_bias = jnp.zeros((1, 128), F32)
        gate_bias = gate_bias.at[0, IGATE_LANE:IGATE_LANE + HEADS].set(mlstm_igate_bias[l])
        gate_bias = gate_bias.at[0, FGATE_LANE:FGATE_LANE + HEADS].set(mlstm_fgate_bias[l])
        w_q = _heads_last_split(mla_w_uq[l], MLA_NOPE, MLA_ROPE)
        w_kv = _heads_last_split(mla_w_ukv[l], MLA_NOPE, MLA_V)

        main, tail = _in_proj(x2, row(mix_norm[l]), *_regroup_w_in(w_in[l]), tm_proj, IN_PROJ_TN)
        y_b = _mlstm(main, tail, gate_bias, row(mlstm_head_norm[l]), bsz, seq, MLSTM_GROUP)
        qt, k, vt = _mla_prep(tail, cs, cst, row(mla_q_a_norm[l]), w_q.T, row(mla_kv_a_norm[l]),
                              w_kv[:, :HEADS * MLA_NOPE], w_kv[:, HEADS * MLA_NOPE:].T,
                              mla_q_norm[l].reshape(-1, 1).astype(F32), row(mla_k_norm[l]), bsz, seq)
        y_c = lax.cond(_score_bound(mla_q_norm[l], mla_k_norm[l]) <= SCORE_BOUND,
                       _attention_bounded, _attention, qt, k, vt).reshape(t, D_MODEL)
        x2 = _merge(main, y_b, y_c, x2, conv_w[l].astype(F32), w_branch[l].astype(BF16),
                    w_out[l].astype(BF16), seq, tm)
        x2 = _mlp(x2, row(mlp_norm[l]), w_up[l].astype(BF16), w_down[l].astype(BF16), tm)
    return x2.reshape(bsz, seq, D_MODEL)
```

```python
import functools

import jax
import jax.numpy as jnp
import numpy as np
from jax import lax
from jax.experimental import pallas as pl
from jax.experimental.pallas import tpu as pltpu

F32 = jnp.float32
BF16 = jnp.bfloat16

D_MODEL = 1024
N_BRANCH = 3
CONV_WIDTH = 3
HEADS = 8
MLSTM_DK = 64
MLSTM_DV = 128
CHUNK = 128
GATE_CAP = 15.0
MLA_NOPE = 128
MLA_ROPE = 64
MLA_V = 128
Q_LORA = 256
KV_LORA = 128
ROPE_BASE = 10000.0
D_FF = 4 * D_MODEL
NORM_EPS = 1e-6
QK_PAD = 256

MAIN_COLS = 9 * D_MODEL
TAIL_COLS = 512
GATE_COL0 = 6 * D_MODEL
TAIL_KROPE = Q_LORA + KV_LORA
IGATE_LANE = MLA_ROPE
FGATE_LANE = MLA_ROPE + HEADS

IN_PROJ_TN = 3072
IN_PROJ_CHUNK = 1536
MLSTM_GROUP = 4
ATT_T = 512
LOG2E = 1.4426950408889634
Q_SCALE = (MLA_NOPE + MLA_ROPE) ** -0.5 * LOG2E
MASK_VALUE = -1e30
SCORE_BOUND = 80.0
BOUND_SLACK = 1.02
ATT_UNROLL = 8
V_ROWS = MLA_V + 16

VMEM_LIMIT = 56 * 1024 * 1024


def _rms(x, g):
    return x * lax.rsqrt(jnp.mean(x * x, axis=-1, keepdims=True) + NORM_EPS) * g


def _dot(a, b):
    return jnp.dot(a, b, preferred_element_type=F32)


def _dot_nt(a, b):
    return lax.dot_general(a, b, (((1,), (1,)), ((), ())), preferred_element_type=F32)


def _rope_table_kernel(pos_ref, freq_ref, cs_ref, cst_ref):
    ang = freq_ref[...] * pos_ref[...].astype(F32)
    c, s = jnp.cos(ang), jnp.sin(ang)
    cst_ref[...] = jnp.concatenate([c, s], axis=0)
    cs_ref[...] = jnp.concatenate([c, c, -s, s], axis=0).T


def _rope_table(positions, tm):
    t = positions.size
    half = MLA_ROPE // 2
    inv_freq = (np.float32(ROPE_BASE) ** (-np.arange(0, MLA_ROPE, 2, dtype=np.float32) / np.float32(MLA_ROPE)))
    inv_freq = jnp.asarray(inv_freq.astype(np.float32))
    return pl.pallas_call(
        _rope_table_kernel,
        grid=(t // tm,),
        in_specs=[pl.BlockSpec((1, tm), lambda i: (0, i)),
                  pl.BlockSpec((half, 1), lambda i: (0, 0))],
        out_specs=[pl.BlockSpec((tm, 2 * MLA_ROPE), lambda i: (i, 0)),
                   pl.BlockSpec((MLA_ROPE, tm), lambda i: (0, i))],
        out_shape=[jax.ShapeDtypeStruct((t, 2 * MLA_ROPE), F32),
                   jax.ShapeDtypeStruct((MLA_ROPE, t), F32)],
        name="rope_table",
    )(positions.reshape(1, t), inv_freq.reshape(half, 1))


def _in_proj_kernel(x_ref, g_ref, w_ref, wt_ref, main_ref, tail_ref, xn_ref, *, n_gate0):
    j = pl.program_id(1)

    @pl.when(j == 0)
    def _():
        xn = _rms(x_ref[...], g_ref[...]).astype(BF16)
        xn_ref[...] = xn
        tail_ref[...] = _dot(xn, wt_ref[...])

    tn = w_ref.shape[1]
    cols = [slice(c, c + IN_PROJ_CHUNK) for c in range(0, tn, IN_PROJ_CHUNK)]

    @pl.when(j < n_gate0)
    def _():
        for c in cols:
            main_ref[:, c] = _dot(xn_ref[...], w_ref[:, c]).astype(BF16)

    @pl.when(j >= n_gate0)
    def _():
        for c in cols:
            main_ref[:, c] = jax.nn.sigmoid(_dot(xn_ref[...], w_ref[:, c])).astype(BF16)


def _in_proj(x2, g, w_main, w_tail, tm, tn):
    t = x2.shape[0]
    assert GATE_COL0 % tn == 0 and MAIN_COLS % tn == 0
    kern = functools.partial(_in_proj_kernel, n_gate0=GATE_COL0 // tn)
    return pl.pallas_call(
        kern,
        grid=(t // tm, MAIN_COLS // tn),
        in_specs=[pl.BlockSpec((tm, D_MODEL), lambda i, j: (i, 0)),
                  pl.BlockSpec((1, D_MODEL), lambda i, j: (0, 0)),
                  pl.BlockSpec((D_MODEL, tn), lambda i, j: (0, j)),
                  pl.BlockSpec((D_MODEL, TAIL_COLS), lambda i, j: (0, 0))],
        out_specs=[pl.BlockSpec((tm, tn), lambda i, j: (i, j)),
                   pl.BlockSpec((tm, TAIL_COLS), lambda i, j: (i, 0))],
        out_shape=[jax.ShapeDtypeStruct((t, MAIN_COLS), BF16),
                   jax.ShapeDtypeStruct((t, TAIL_COLS), F32)],
        scratch_shapes=[pltpu.VMEM((tm, D_MODEL), BF16)],
        compiler_params=pltpu.CompilerParams(
            dimension_semantics=("parallel", "arbitrary"), vmem_limit_bytes=VMEM_LIMIT),
        name="in_proj",
    )(x2, g, w_main, w_tail)


def _time_scan(x, op, row):
    s = 1
    while s < CHUNK:
        x = jnp.where(row >= s, op(x, pltpu.roll(x, s, 0)), x)
        s *= 2
    return x


def _mlstm_kernel(q_ref, k_ref, v_ref, o_ref, gate_ref, bias_ref, hn_ref, y_ref, c_ref, m_ref, *, n_chunk):
    L = CHUNK

    @pl.when(pl.program_id(1) == 0)
    def _():
        c_ref[...] = jnp.zeros_like(c_ref)
        m_ref[...] = jnp.zeros_like(m_ref)

    row = lax.broadcasted_iota(jnp.int32, (L, 128), 0)
    lane = lax.broadcasted_iota(jnp.int32, (L, 128), 1)
    gate_lanes = jnp.logical_and(lane >= IGATE_LANE, lane < FGATE_LANE + HEADS)
    tri = (lax.broadcasted_iota(jnp.int32, (L, L), 0) >= lax.broadcasted_iota(jnp.int32, (L, L), 1))
    ones_col = (lax.broadcasted_iota(jnp.int32, (L, MLSTM_DV), 1) == 0).astype(BF16)
    m_prev = m_ref[...]
    c_state = [c_ref[h] for h in range(HEADS)]

    for g in range(n_chunk):
        rows = slice(g * L, (g + 1) * L)
        pre = jnp.where(gate_lanes, gate_ref[rows, :] + bias_ref[...], 0.0)
        capped = GATE_CAP * jnp.tanh(pre * (1.0 / GATE_CAP))
        log_i = capped * LOG2E
        log_f = (jnp.minimum(capped, 0.0) - jnp.log1p(jnp.exp(-jnp.abs(capped)))) * LOG2E
        log_f = pltpu.roll(log_f, 128 - HEADS, 1)
        b = _time_scan(log_f, jnp.add, row)
        b_last = b[L - 1:L, :]
        rowd = log_i - b
        big_m = jnp.maximum(_time_scan(rowd, jnp.maximum, row), m_prev)
        e_neg_m = jnp.exp2(-(b + big_m))
        a = b_last + rowd
        m_loc = jnp.max(a, axis=0, keepdims=True)
        m_new = jnp.maximum(b_last + m_prev, m_loc)
        s_old = jnp.exp2(b_last + m_prev - m_new)
        w = jnp.exp2(a - m_new)
        m_prev_g, m_prev = m_prev, m_new
        rowd_t = rowd.T
        w_t = w.T
        q_all = q_ref[rows, :] * (MLSTM_DK ** -0.5)
        k_t = k_ref[rows, :].astype(F32).T

        hk = [slice(h * MLSTM_DK, (h + 1) * MLSTM_DK) for h in range(HEADS)]
        hv = [slice(h * MLSTM_DV, (h + 1) * MLSTM_DV) for h in range(HEADS)]
        gl = [slice(IGATE_LANE + h, IGATE_LANE + h + 1) for h in range(HEADS)]
        v_aug = [jnp.concatenate([v_ref[rows, hv[h]], ones_col], axis=1) for h in range(HEADS)]
        scores = [_dot(q_all[:, hk[h]], k_t[hk[h], :].astype(BF16)) for h in range(HEADS)]
        c_loc = [_dot((k_t[hk[h], :] * w_t[gl[h], :]).astype(BF16), v_aug[h]) for h in range(HEADS)]
        out_gain = (0.5 * jnp.tanh(0.5 * o_ref[rows, :].astype(F32)) + 0.5) * hn_ref[...]
        big_m_b = [jnp.broadcast_to(big_m[:, gl[h]], (L, L)) for h in range(HEADS)]
        dmat = [jnp.where(tri, jnp.exp2(rowd_t[gl[h], :] - big_m_b[h]), 0.0) for h in range(HEADS)]
        q_state = [(q_all[:, hk[h]].astype(F32)
                    * jnp.exp2(m_prev_g[:, gl[h]] - big_m_b[h][:, :MLSTM_DK])).astype(BF16) for h in range(HEADS)]
        num_aug = []
        for h in range(HEADS):
            lhs = jnp.concatenate([(scores[h] * dmat[h]).astype(BF16), q_state[h],
                                   jnp.zeros((L, L - MLSTM_DK), BF16)], axis=1)
            rhs = jnp.concatenate([v_aug[h], c_state[h].astype(BF16),
                                   jnp.zeros((L - MLSTM_DK, 2 * MLSTM_DV), BF16)], axis=0)
            num_aug.append(_dot(lhs, rhs))
        for h in range(HEADS):
            num = num_aug[h][:, :MLSTM_DV]
            d = jnp.maximum(jnp.abs(num_aug[h][:, MLSTM_DV:MLSTM_DV + 1]), e_neg_m[:, gl[h]])
            scale = lax.rsqrt(jnp.mean(num * num, axis=-1, keepdims=True) + NORM_EPS * d * d)
            y_ref[rows, hv[h]] = (num * scale * out_gain[:, hv[h]]).astype(y_ref.dtype)
            c_state[h] = s_old[:, gl[h]] * c_state[h] + c_loc[h]

    m_ref[...] = m_prev
    for h in range(HEADS):
        c_ref[h] = c_state[h]


def _mlstm(main, tail, gate_bias, head_norm, bsz, seq, n_chunk):
    t = main.shape[0]
    rows = n_chunk * CHUNK
    nc = seq // rows
    row = lambda b, c: b * nc + c
    return pl.pallas_call(
        functools.partial(_mlstm_kernel, n_chunk=n_chunk),
        grid=(bsz, nc),
        in_specs=[pl.BlockSpec((rows, 512), lambda b, c: (row(b, c), 6)),
                  pl.BlockSpec((rows, 512), lambda b, c: (row(b, c), 7)),
                  pl.BlockSpec((rows, D_MODEL), lambda b, c: (row(b, c), 4)),
                  pl.BlockSpec((rows, D_MODEL), lambda b, c: (row(b, c), 5)),
                  pl.BlockSpec((rows, 128), lambda b, c: (row(b, c), TAIL_KROPE // 128)),
                  pl.BlockSpec((1, 128), lambda b, c: (0, 0)),
                  pl.BlockSpec((1, D_MODEL), lambda b, c: (0, 0))],
        out_specs=pl.BlockSpec((rows, D_MODEL), lambda b, c: (row(b, c), 0)),
        out_shape=jax.ShapeDtypeStruct((t, D_MODEL), BF16),
        scratch_shapes=[pltpu.VMEM((HEADS, MLSTM_DK, 2 * MLSTM_DV), F32),
                        pltpu.VMEM((1, 128), F32)],
        compiler_params=pltpu.CompilerParams(dimension_semantics=("parallel", "arbitrary")),
        name="mlstm",
    )(main, main, main, main, tail, gate_bias, head_norm)


def _rms0(x, g_col):
    return x * lax.rsqrt(jnp.mean(x * x, axis=0, keepdims=True) + NORM_EPS) * g_col


def _rope(x, cs):
    half = MLA_ROPE // 2
    rot = jnp.concatenate([x[:, half:], x[:, :half]], axis=-1)
    return x * cs[:, :MLA_ROPE] + rot * cs[:, MLA_ROPE:]


def _mla_prep_kernel(tail_ref, cs_ref, cst_ref, qa_ref, wqt_ref, kva_ref, wk_ref, wvt_ref, qn_ref, kn_ref,
                     qt_out, k_out, vt_out):
    tm = tail_ref.shape[0]
    half = MLA_ROPE // 2
    cq = _rms(tail_ref[:, :Q_LORA], qa_ref[...]).astype(BF16)
    ckv = _rms(tail_ref[:, Q_LORA:Q_LORA + KV_LORA], kva_ref[...]).astype(BF16)
    qft = _dot_nt(wqt_ref[...], cq)
    kf = _dot(ckv, wk_ref[...])
    vft = _dot_nt(wvt_ref[...], ckv)
    cos_t, sin_t = cst_ref[:half, :], cst_ref[half:, :]
    qn_g, qp_g = qn_ref[:MLA_NOPE, :], qn_ref[MLA_NOPE:, :]
    kn_g, kp_g = kn_ref[:, :MLA_NOPE], kn_ref[:, MLA_NOPE:]
    k_pe = _rope(_rms(tail_ref[:, TAIL_KROPE:TAIL_KROPE + MLA_ROPE], kp_g), cs_ref[...]).astype(BF16)
    pe0 = HEADS * MLA_NOPE
    ones_rows = (lax.broadcasted_iota(jnp.int32, (V_ROWS - MLA_V, tm), 0) == 0).astype(BF16)
    for h in range(HEADS):
        q_nope = _rms0(qft[h * MLA_NOPE:(h + 1) * MLA_NOPE], qn_g)
        q_pe = _rms0(qft[pe0 + h * MLA_ROPE:pe0 + (h + 1) * MLA_ROPE], qp_g)
        x1, x2 = q_pe[:half], q_pe[half:]
        q_rot = jnp.concatenate([x1 * cos_t - x2 * sin_t, x1 * sin_t + x2 * cos_t], axis=0)
        qt_out[0, h, 0] = (jnp.concatenate([q_nope, q_rot, jnp.zeros((QK_PAD - MLA_NOPE - MLA_ROPE, tm), F32)],
                                           axis=0) * Q_SCALE).astype(BF16)
        k_nope = _rms(kf[:, h * MLA_NOPE:(h + 1) * MLA_NOPE], kn_g)
        k_out[0, h] = jnp.concatenate(
            [k_nope.astype(BF16), k_pe, jnp.zeros((tm, QK_PAD - MLA_NOPE - MLA_ROPE), BF16)], axis=-1)
        vt_out[0, h, 0] = jnp.concatenate([vft[h * MLA_V:(h + 1) * MLA_V].astype(BF16), ones_rows], axis=0)


def _mla_prep(tail, cs, cst, qa, wqt, kva, wk, wvt, qn_col, kn, bsz, seq):
    tm = ATT_T
    nt = seq // tm
    full = lambda shape: pl.BlockSpec(shape, lambda i: (0,) * len(shape))
    blocked = lambda rows: pl.BlockSpec((1, HEADS, 1, rows, tm), lambda i: (i // nt, 0, i % nt, 0, 0))
    return pl.pallas_call(
        _mla_prep_kernel,
        grid=(bsz * nt,),
        in_specs=[pl.BlockSpec((tm, TAIL_COLS), lambda i: (i, 0)),
                  pl.BlockSpec((tm, 2 * MLA_ROPE), lambda i: (i, 0)),
                  pl.BlockSpec((MLA_ROPE, tm), lambda i: (0, i)),
                  full((1, Q_LORA)), full(wqt.shape), full((1, KV_LORA)), full(wk.shape), full(wvt.shape),
                  full((MLA_NOPE + MLA_ROPE, 1)), full((1, MLA_NOPE + MLA_ROPE))],
        out_specs=[blocked(QK_PAD),
                   pl.BlockSpec((1, HEADS, tm, QK_PAD), lambda i: (i // nt, 0, i % nt, 0)),
                   blocked(V_ROWS)],
        out_shape=[jax.ShapeDtypeStruct((bsz, HEADS, nt, QK_PAD, tm), BF16),
                   jax.ShapeDtypeStruct((bsz, HEADS, seq, QK_PAD), BF16),
                   jax.ShapeDtypeStruct((bsz, HEADS, nt, V_ROWS, tm), BF16)],
        compiler_params=pltpu.CompilerParams(dimension_semantics=("parallel",),
                                             vmem_limit_bytes=VMEM_LIMIT),
        name="mla_prep",
    )(tail, cs, cst, qa, wqt, kva, wk, wvt, qn_col, kn)


def _attn_kernel(qt_ref, k_ref, vt_ref, o_ref, s_scr, p_scr, a_scr, m_all, acc_all, *, nq):
    t = ATT_T

    def qk(qi, j):
        k0 = pl.multiple_of(j * t, t)
        return _dot(k_ref[0, 0, pl.ds(k0, t), :], qt_ref[0, 0, qi])

    def accumulate(tile, alpha, p):
        q, j = tile
        acc_all[q] = alpha * acc_all[q] + _dot(vt_ref[0, 0, j], p)

    def stage(prev, cur, nxt, s_cur, p_prev, alpha_prev, masked):
        accumulate(prev, alpha_prev, p_prev)
        qi = cur[0]
        if masked:
            keep = lax.broadcasted_iota(jnp.int32, (t, t), 0) <= lax.broadcasted_iota(jnp.int32, (t, t), 1)
            s_cur = jnp.where(keep, s_cur, MASK_VALUE)
        m_old = m_all[qi]
        m_new = jnp.maximum(m_old, jnp.max(s_cur, axis=0, keepdims=True))
        m_all[qi] = m_new
        alpha = jnp.exp2(m_old - m_new)
        p = jnp.exp2((s_cur - m_new).astype(BF16))
        s_next = qk(jnp.minimum(nxt[0], nq - 1), jnp.minimum(nxt[1], nq - 1))
        return s_next, p, alpha

    def advance_lower(qi, j):
        wrap = j + 1 == qi
        return jnp.where(wrap, qi + 1, qi), jnp.where(wrap, 0, j + 1)

    def advance_diag(qi, j):
        return qi + 1, j + 1

    def run_phase(n_steps, first, advance, masked):
        if n_steps == 0:
            return
        s_scr[...] = qk(*first)
        p_scr[...] = jnp.zeros_like(p_scr)
        a_scr[...] = jnp.ones_like(a_scr)

        def steps(count, prev, cur):
            s_cur, p_prev, alpha_prev = s_scr[...], p_scr[...], a_scr[...]
            for _ in range(count):
                nxt = advance(*cur)
                s_cur, p_prev, alpha_prev = stage(prev, cur, nxt, s_cur, p_prev, alpha_prev, masked)
                prev, cur = cur, nxt
            s_scr[...] = s_cur
            p_scr[...] = p_prev
            a_scr[...] = alpha_prev
            return prev, cur

        first = (jnp.int32(first[0]), jnp.int32(first[1]))
        prev, cur = lax.fori_loop(0, n_steps // ATT_UNROLL, lambda _, c: steps(ATT_UNROLL, *c), (first, first))
        if n_steps % ATT_UNROLL:
            prev, cur = steps(n_steps % ATT_UNROLL, prev, cur)
        accumulate(prev, a_scr[...], p_scr[...])

    m_all[...] = jnp.full_like(m_all, MASK_VALUE)
    acc_all[...] = jnp.zeros_like(acc_all)
    run_phase(nq * (nq - 1) // 2, (1, 0), advance_lower, False)
    run_phase(nq, (0, 0), advance_diag, True)
    for qi in range(nq):
        acc = acc_all[qi]
        o_ref[0, qi * t:(qi + 1) * t, :] = (acc[:MLA_V] / acc[MLA_V:MLA_V + 1]).T.astype(o_ref.dtype)


def _attn_bounded_kernel(qt_ref, k_ref, vt_ref, o_ref, s_scr, p_scr, acc_all, *, nq):
    t = ATT_T

    def qk(qi, j):
        k0 = pl.multiple_of(j * t, t)
        return _dot(k_ref[0, 0, pl.ds(k0, t), :], qt_ref[0, 0, qi])

    def accumulate(tile, p):
        q, j = tile
        acc_all[q] += _dot(vt_ref[0, 0, j], p)

    def stage(prev, cur, nxt, s_cur, p_prev, masked):
        accumulate(prev, p_prev)
        p = jnp.exp2(s_cur)
        if masked:
            keep = lax.broadcasted_iota(jnp.int32, (t, t), 0) <= lax.broadcasted_iota(jnp.int32, (t, t), 1)
            p = jnp.where(keep, p, 0.0)
        s_next = qk(jnp.minimum(nxt[0], nq - 1), jnp.minimum(nxt[1], nq - 1))
        return s_next, p.astype(BF16)

    def advance_lower(qi, j):
        wrap = j + 1 == qi
        return jnp.where(wrap, qi + 1, qi), jnp.where(wrap, 0, j + 1)

    def advance_diag(qi, j):
        return qi + 1, j + 1

    def run_phase(n_steps, first, advance, masked):
        if n_steps == 0:
            return
        s_scr[...] = qk(*first)
        p_scr[...] = jnp.zeros_like(p_scr)

        def steps(count, prev, cur):
            s_cur, p_prev = s_scr[...], p_scr[...]
            for _ in range(count):
                nxt = advance(*cur)
                s_cur, p_prev = stage(prev, cur, nxt, s_cur, p_prev, masked)
                prev, cur = cur, nxt
            s_scr[...] = s_cur
            p_scr[...] = p_prev
            return prev, cur

        first = (jnp.int32(first[0]), jnp.int32(first[1]))
        prev, cur = lax.fori_loop(0, n_steps // ATT_UNROLL, lambda _, c: steps(ATT_UNROLL, *c), (first, first))
        if n_steps % ATT_UNROLL:
            prev, cur = steps(n_steps % ATT_UNROLL, prev, cur)
        accumulate(prev, p_scr[...])

    acc_all[...] = jnp.zeros_like(acc_all)
    run_phase(nq * (nq - 1) // 2, (1, 0), advance_lower, False)
    run_phase(nq, (0, 0), advance_diag, True)
    for qi in range(nq):
        acc = acc_all[qi]
        o_ref[0, qi * t:(qi + 1) * t, :] = (acc[:MLA_V] / acc[MLA_V:MLA_V + 1]).T.astype(o_ref.dtype)


def _attention_bounded(qt, k, vt):
    bsz, _, seq, _ = k.shape
    nq = seq // ATT_T
    kern = functools.partial(_attn_bounded_kernel, nq=nq)
    return pl.pallas_call(
        kern,
        grid=(bsz, HEADS),
        in_specs=[pl.BlockSpec((1, 1, nq, QK_PAD, ATT_T), lambda b, h: (b, h, 0, 0, 0)),
                  pl.BlockSpec((1, 1, seq, QK_PAD), lambda b, h: (b, h, 0, 0)),
                  pl.BlockSpec((1, 1, nq, V_ROWS, ATT_T), lambda b, h: (b, h, 0, 0, 0))],
        out_specs=pl.BlockSpec((1, seq, MLA_V), lambda b, h: (b, 0, h)),
        out_shape=jax.ShapeDtypeStruct((bsz, seq, HEADS * MLA_V), BF16),
        scratch_shapes=[pltpu.VMEM((ATT_T, ATT_T), F32), pltpu.VMEM((ATT_T, ATT_T), BF16),
                        pltpu.VMEM((nq, V_ROWS, ATT_T), F32)],
        compiler_params=pltpu.CompilerParams(dimension_semantics=("parallel", "parallel"),
                                             vmem_limit_bytes=VMEM_LIMIT),
        name="attn_bounded",
    )(qt, k, vt)


def _attention(qt, k, vt):
    bsz, _, seq, _ = k.shape
    nq = seq // ATT_T
    kern = functools.partial(_attn_kernel, nq=nq)
    return pl.pallas_call(
        kern,
        grid=(bsz, HEADS),
        in_specs=[pl.BlockSpec((1, 1, nq, QK_PAD, ATT_T), lambda b, h: (b, h, 0, 0, 0)),
                  pl.BlockSpec((1, 1, seq, QK_PAD), lambda b, h: (b, h, 0, 0)),
                  pl.BlockSpec((1, 1, nq, V_ROWS, ATT_T), lambda b, h: (b, h, 0, 0, 0))],
        out_specs=pl.BlockSpec((1, seq, MLA_V), lambda b, h: (b, 0, h)),
        out_shape=jax.ShapeDtypeStruct((bsz, seq, HEADS * MLA_V), BF16),
        scratch_shapes=[pltpu.VMEM((ATT_T, ATT_T), F32), pltpu.VMEM((ATT_T, ATT_T), BF16),
                        pltpu.VMEM((1, ATT_T), F32), pltpu.VMEM((nq, 1, ATT_T), F32),
                        pltpu.VMEM((nq, V_ROWS, ATT_T), F32)],
        compiler_params=pltpu.CompilerParams(dimension_semantics=("parallel", "parallel"),
                                             vmem_limit_bytes=VMEM_LIMIT),
        name="attn",
    )(qt, k, vt)


def _merge_kernel(cb_ref, cc_ref, cu_ref, gate_ref, yb_ref, yc_ref, x_ref, cw_ref, wb_ref, wo_ref,
                  out_ref, zprev_ref, *, tiles_per_seq):
    tm = x_ref.shape[0]

    @pl.when(pl.program_id(0) % tiles_per_seq == 0)
    def _():
        zprev_ref[...] = jnp.zeros_like(zprev_ref)

    z = cc_ref[...].astype(F32) * cu_ref[...].astype(F32)
    row = lax.broadcasted_iota(jnp.int32, (tm, D_MODEL), 0)
    p1 = zprev_ref[7:8, :]
    p2 = zprev_ref[6:7, :]
    z1 = jnp.where(row == 0, p1, pltpu.roll(z, 1, 0))
    z2 = jnp.where(row == 0, p2, jnp.where(row == 1, p1, pltpu.roll(z, 2, 0)))
    zprev_ref[...] = z[tm - 8:, :]
    cw = cw_ref[...]
    y_a = cb_ref[...].astype(F32) * (cw[0:1] * z2 + cw[1:2] * z1 + cw[2:3] * z)

    g = gate_ref[...]
    merged = g[:, :D_MODEL].astype(F32) * _dot(y_a.astype(BF16), wb_ref[0])
    merged += g[:, D_MODEL:2 * D_MODEL].astype(F32) * _dot(yb_ref[...], wb_ref[1])
    merged += g[:, 2 * D_MODEL:].astype(F32) * _dot(yc_ref[...], wb_ref[2])
    out_ref[...] = x_ref[...] + _dot(merged.astype(BF16), wo_ref[...])


def _merge(main, y_b, y_c, x2, conv_w, w_branch, w_out, seq, tm):
    t = x2.shape[0]
    kern = functools.partial(_merge_kernel, tiles_per_seq=seq // tm)
    col = lambda c: pl.BlockSpec((tm, D_MODEL), lambda i: (i, c))
    return pl.pallas_call(
        kern,
        grid=(t // tm,),
        in_specs=[col(0), col(1), col(2),
                  pl.BlockSpec((tm, N_BRANCH * D_MODEL), lambda i: (i, GATE_COL0 // (N_BRANCH * D_MODEL))),
                  col(0), col(0), col(0),
                  pl.BlockSpec((CONV_WIDTH, D_MODEL), lambda i: (0, 0)),
                  pl.BlockSpec((N_BRANCH, D_MODEL, D_MODEL), lambda i: (0, 0, 0)),
                  pl.BlockSpec((D_MODEL, D_MODEL), lambda i: (0, 0))],
        out_specs=col(0),
        out_shape=jax.ShapeDtypeStruct((t, D_MODEL), F32),
        scratch_shapes=[pltpu.VMEM((8, D_MODEL), F32)],
        compiler_params=pltpu.CompilerParams(dimension_semantics=("arbitrary",),
                                             vmem_limit_bytes=VMEM_LIMIT),
        name="merge",
    )(main, main, main, main, y_b, y_c, x2, conv_w, w_branch, w_out)


def _mlp_kernel(x_ref, g_ref, wu_ref, wd_ref, out_ref, *, n_chunk):
    x = x_ref[...]
    h = _rms(x, g_ref[...]).astype(BF16)
    ck = D_FF // n_chunk
    acc = x
    for c in range(n_chunk):
        u = jnp.maximum(_dot(h, wu_ref[:, c * ck:(c + 1) * ck]), 0.0)
        acc = acc + _dot((u * u).astype(BF16), wd_ref[c * ck:(c + 1) * ck, :])
    out_ref[...] = acc


def _mlp(x2, g, w_up, w_down, tm):
    t = x2.shape[0]
    kern = functools.partial(_mlp_kernel, n_chunk=4)
    return pl.pallas_call(
        kern,
        grid=(t // tm,),
        in_specs=[pl.BlockSpec((tm, D_MODEL), lambda i: (i, 0)),
                  pl.BlockSpec((1, D_MODEL), lambda i: (0, 0)),
                  pl.BlockSpec((D_MODEL, D_FF), lambda i: (0, 0)),
                  pl.BlockSpec((D_FF, D_MODEL), lambda i: (0, 0))],
        out_specs=pl.BlockSpec((tm, D_MODEL), lambda i: (i, 0)),
        out_shape=jax.ShapeDtypeStruct((t, D_MODEL), F32),
        compiler_params=pltpu.CompilerParams(dimension_semantics=("parallel",),
                                             vmem_limit_bytes=VMEM_LIMIT),
        name="mlp",
    )(x2, g, w_up, w_down)


def _regroup_w_in(w_in):
    c = D_MODEL
    q0 = 3 * c
    i0 = q0 + 2 * HEADS * MLSTM_DK + 2 * HEADS * MLSTM_DV
    cq0 = i0 + 2 * HEADS
    ckv0 = cq0 + Q_LORA
    kr0 = ckv0 + KV_LORA
    g0 = kr0 + MLA_ROPE
    pad = jnp.zeros((D_MODEL, TAIL_COLS - (Q_LORA + KV_LORA + MLA_ROPE + 2 * HEADS)), w_in.dtype)
    w_main = jnp.concatenate([w_in[:, :i0], w_in[:, g0:]], axis=1).astype(BF16)
    w_tail = jnp.concatenate([w_in[:, cq0:g0], w_in[:, i0:cq0], pad], axis=1).astype(BF16)
    return w_main, w_tail


def _heads_last_split(w, first, second):
    k = w.shape[0]
    w3 = w.reshape(k, HEADS, first + second)
    return jnp.concatenate([w3[:, :, :first].reshape(k, HEADS * first),
                            w3[:, :, first:].reshape(k, HEADS * second)], axis=1).astype(BF16)


def _score_bound(q_gain, k_gain):
    def sq_norm(g):
        g = g.astype(F32)
        return MLA_NOPE * jnp.max(g[:MLA_NOPE] ** 2) + MLA_ROPE * jnp.max(g[MLA_NOPE:] ** 2)
    return jnp.sqrt(sq_norm(q_gain) * sq_norm(k_gain)) * (Q_SCALE * BOUND_SLACK)


def _pick(n, candidates):
    for c in candidates:
        if n % c == 0:
            return c
    raise ValueError(f"no tile size for {n}")


def kernel(x, positions, mix_norm, w_in, conv_w, mlstm_igate_bias, mlstm_fgate_bias, mlstm_head_norm,
           mla_q_a_norm, mla_w_uq, mla_kv_a_norm, mla_w_ukv, mla_q_norm, mla_k_norm, w_branch, w_out,
           mlp_norm, w_up, w_down):
    bsz, seq, _ = x.shape
    depth = w_in.shape[0]
    t = bsz * seq
    assert seq % ATT_T == 0 and seq % (CHUNK * MLSTM_GROUP) == 0
    tm_proj = _pick(t, (1024, 512, 256, 128))
    tm = _pick(seq, (512, 256, 128))

    x2 = x.reshape(t, D_MODEL)
    cs, cst = _rope_table(positions, ATT_T)
    row = lambda a: a.reshape(1, -1).astype(F32)

    for l in range(depth):
        gate_bias = jnp.zeros((1, 128), F32)
        gate_bias = gate_bias.at[0, IGATE_LANE:IGATE_LANE + HEADS].set(mlstm_igate_bias[l])
        gate_bias = gate_bias.at[0, FGATE_LANE:FGATE_LANE + HEADS].set(mlstm_fgate_bias[l])
        w_q = _heads_last_split(mla_w_uq[l], MLA_NOPE, MLA_ROPE)
        w_kv = _heads_last_split(mla_w_ukv[l], MLA_NOPE, MLA_V)

        main, tail = _in_proj(x2, row(mix_norm[l]), *_regroup_w_in(w_in[l]), tm_proj, IN_PROJ_TN)
        y_b = _mlstm(main, tail, gate_bias, row(mlstm_head_norm[l]), bsz, seq, MLSTM_GROUP)
        qt, k, vt = _mla_prep(tail, cs, cst, row(mla_q_a_norm[l]), w_q.T, row(mla_kv_a_norm[l]),
                              w_kv[:, :HEADS * MLA_NOPE], w_kv[:, HEADS * MLA_NOPE:].T,
                              mla_q_norm[l].reshape(-1, 1).astype(F32), row(mla_k_norm[l]), bsz, seq)
        y_c = lax.cond(_score_bound(mla_q_norm[l], mla_k_norm[l]) <= SCORE_BOUND,
                       _attention_bounded, _attention, qt, k, vt).reshape(t, D_MODEL)
        x2 = _merge(main, y_b, y_c, x2, conv_w[l].astype(F32), w_branch[l].astype(BF16),
                    w_out[l].astype(BF16), seq, tm)
        x2 = _mlp(x2, row(mlp_norm[l]), w_up[l].astype(BF16), w_down[l].astype(BF16), tm)
    return x2.reshape(bsz, seq, D_MODEL)
```

```python
import functools

import jax
import jax.numpy as jnp
import numpy as np
from jax import lax
from jax.experimental import pallas as pl
from jax.experimental.pallas import tpu as pltpu

F32 = jnp.float32
BF16 = jnp.bfloat16

D_MODEL = 1024
N_BRANCH = 3
CONV_WIDTH = 3
HEADS = 8
MLSTM_DK = 64
MLSTM_DV = 128
CHUNK = 128
GATE_CAP = 15.0
MLA_NOPE = 128
MLA_ROPE = 64
MLA_V = 128
Q_LORA = 256
KV_LORA = 128
ROPE_BASE = 10000.0
D_FF = 4 * D_MODEL
NORM_EPS = 1e-6
QK_PAD = 256

MAIN_COLS = 9 * D_MODEL
TAIL_COLS = 512
GATE_COL0 = 6 * D_MODEL
TAIL_KROPE = Q_LORA + KV_LORA
IGATE_LANE = MLA_ROPE
FGATE_LANE = MLA_ROPE + HEADS

IN_PROJ_TN = 3072
IN_PROJ_CHUNK = 1536
MLSTM_GROUP = 8
ATT_T = 512
LOG2E = 1.4426950408889634
Q_SCALE = (MLA_NOPE + MLA_ROPE) ** -0.5 * LOG2E
MASK_VALUE = -1e30
SCORE_BOUND = 80.0
BOUND_SLACK = 1.02
ATT_UNROLL = 8
V_ROWS = MLA_V + 16

VMEM_LIMIT = 56 * 1024 * 1024


def _rms(x, g):
    return x * lax.rsqrt(jnp.mean(x * x, axis=-1, keepdims=True) + NORM_EPS) * g


def _dot(a, b):
    return jnp.dot(a, b, preferred_element_type=F32)


def _dot_nt(a, b):
    return lax.dot_general(a, b, (((1,), (1,)), ((), ())), preferred_element_type=F32)


def _rope_table_kernel(pos_ref, freq_ref, cs_ref, cst_ref):
    ang = freq_ref[...] * pos_ref[...].astype(F32)
    c, s = jnp.cos(ang), jnp.sin(ang)
    cst_ref[...] = jnp.concatenate([c, s], axis=0)
    cs_ref[...] = jnp.concatenate([c, c, -s, s], axis=0).T


def _rope_table(positions, tm):
    t = positions.size
    half = MLA_ROPE // 2
    inv_freq = (np.float32(ROPE_BASE) ** (-np.arange(0, MLA_ROPE, 2, dtype=np.float32) / np.float32(MLA_ROPE)))
    inv_freq = jnp.asarray(inv_freq.astype(np.float32))
    return pl.pallas_call(
        _rope_table_kernel,
        grid=(t // tm,),
        in_specs=[pl.BlockSpec((1, tm), lambda i: (0, i)),
                  pl.BlockSpec((half, 1), lambda i: (0, 0))],
        out_specs=[pl.BlockSpec((tm, 2 * MLA_ROPE), lambda i: (i, 0)),
                   pl.BlockSpec((MLA_ROPE, tm), lambda i: (0, i))],
        out_shape=[jax.ShapeDtypeStruct((t, 2 * MLA_ROPE), F32),
                   jax.ShapeDtypeStruct((MLA_ROPE, t), F32)],
        name="rope_table",
    )(positions.reshape(1, t), inv_freq.reshape(half, 1))


def _in_proj_kernel(x_ref, g_ref, w_ref, wt_ref, main_ref, tail_ref, xn_ref, *, n_gate0):
    j = pl.program_id(1)

    @pl.when(j == 0)
    def _():
        xn = _rms(x_ref[...], g_ref[...]).astype(BF16)
        xn_ref[...] = xn
        tail_ref[...] = _dot(xn, wt_ref[...])

    tn = w_ref.shape[1]
    cols = [slice(c, c + IN_PROJ_CHUNK) for c in range(0, tn, IN_PROJ_CHUNK)]

    @pl.when(j < n_gate0)
    def _():
        for c in cols:
            main_ref[:, c] = _dot(xn_ref[...], w_ref[:, c]).astype(BF16)

    @pl.when(j >= n_gate0)
    def _():
        for c in cols:
            main_ref[:, c] = jax.nn.sigmoid(_dot(xn_ref[...], w_ref[:, c])).astype(BF16)


def _in_proj(x2, g, w_main, w_tail, tm, tn):
    t = x2.shape[0]
    assert GATE_COL0 % tn == 0 and MAIN_COLS % tn == 0
    kern = functools.partial(_in_proj_kernel, n_gate0=GATE_COL0 // tn)
    return pl.pallas_call(
        kern,
        grid=(t // tm, MAIN_COLS // tn),
        in_specs=[pl.BlockSpec((tm, D_MODEL), lambda i, j: (i, 0)),
                  pl.BlockSpec((1, D_MODEL), lambda i, j: (0, 0)),
                  pl.BlockSpec((D_MODEL, tn), lambda i, j: (0, j)),
                  pl.BlockSpec((D_MODEL, TAIL_COLS), lambda i, j: (0, 0))],
        out_specs=[pl.BlockSpec((tm, tn), lambda i, j: (i, j)),
                   pl.BlockSpec((tm, TAIL_COLS), lambda i, j: (i, 0))],
        out_shape=[jax.ShapeDtypeStruct((t, MAIN_COLS), BF16),
                   jax.ShapeDtypeStruct((t, TAIL_COLS), F32)],
        scratch_shapes=[pltpu.VMEM((tm, D_MODEL), BF16)],
        compiler_params=pltpu.CompilerParams(
            dimension_semantics=("parallel", "arbitrary"), vmem_limit_bytes=VMEM_LIMIT),
        name="in_proj",
    )(x2, g, w_main, w_tail)


def _time_scan(x, op, row):
    s = 1
    while s < CHUNK:
        x = jnp.where(row >= s, op(x, pltpu.roll(x, s, 0)), x)
        s *= 2
    return x


def _mlstm_kernel(q_ref, k_ref, v_ref, o_ref, gate_ref, bias_ref, hn_ref, y_ref, c_ref, m_ref, *, n_chunk):
    L = CHUNK

    @pl.when(pl.program_id(1) == 0)
    def _():
        c_ref[...] = jnp.zeros_like(c_ref)
        m_ref[...] = jnp.zeros_like(m_ref)

    row = lax.broadcasted_iota(jnp.int32, (L, 128), 0)
    lane = lax.broadcasted_iota(jnp.int32, (L, 128), 1)
    gate_lanes = jnp.logical_and(lane >= IGATE_LANE, lane < FGATE_LANE + HEADS)
    tri = (lax.broadcasted_iota(jnp.int32, (L, L), 0) >= lax.broadcasted_iota(jnp.int32, (L, L), 1))
    ones_col = (lax.broadcasted_iota(jnp.int32, (L, MLSTM_DV), 1) == 0).astype(BF16)
    m_prev = m_ref[...]
    c_state = [c_ref[h] for h in range(HEADS)]

    for g in range(n_chunk):
        rows = slice(g * L, (g + 1) * L)
        pre = jnp.where(gate_lanes, gate_ref[rows, :] + bias_ref[...], 0.0)
        capped = GATE_CAP * jnp.tanh(pre * (1.0 / GATE_CAP))
        log_i = capped * LOG2E
        log_f = (jnp.minimum(capped, 0.0) - jnp.log1p(jnp.exp(-jnp.abs(capped)))) * LOG2E
        log_f = pltpu.roll(log_f, 128 - HEADS, 1)
        b = _time_scan(log_f, jnp.add, row)
        b_last = b[L - 1:L, :]
        rowd = log_i - b
        big_m = jnp.maximum(_time_scan(rowd, jnp.maximum, row), m_prev)
        e_neg_m = jnp.exp2(-(b + big_m))
        a = b_last + rowd
        m_loc = jnp.max(a, axis=0, keepdims=True)
        m_new = jnp.maximum(b_last + m_prev, m_loc)
        s_old = jnp.exp2(b_last + m_prev - m_new)
        w = jnp.exp2(a - m_new)
        m_prev_g, m_prev = m_prev, m_new
        rowd_t = rowd.T
        w_t = w.T
        q_all = q_ref[rows, :] * (MLSTM_DK ** -0.5)
        k_t = k_ref[rows, :].astype(F32).T

        hk = [slice(h * MLSTM_DK, (h + 1) * MLSTM_DK) for h in range(HEADS)]
        hv = [slice(h * MLSTM_DV, (h + 1) * MLSTM_DV) for h in range(HEADS)]
        gl = [slice(IGATE_LANE + h, IGATE_LANE + h + 1) for h in range(HEADS)]
        v_aug = [jnp.concatenate([v_ref[rows, hv[h]], ones_col], axis=1) for h in range(HEADS)]
        scores = [_dot(q_all[:, hk[h]], k_t[hk[h], :].astype(BF16)) for h in range(HEADS)]
        c_loc = [_dot((k_t[hk[h], :] * w_t[gl[h], :]).astype(BF16), v_aug[h]) for h in range(HEADS)]
        out_gain = (0.5 * jnp.tanh(0.5 * o_ref[rows, :].astype(F32)) + 0.5) * hn_ref[...]
        big_m_b = [jnp.broadcast_to(big_m[:, gl[h]], (L, L)) for h in range(HEADS)]
        dmat = [jnp.where(tri, jnp.exp2(rowd_t[gl[h], :] - big_m_b[h]), 0.0) for h in range(HEADS)]
        q_state = [(q_all[:, hk[h]].astype(F32)
                    * jnp.exp2(m_prev_g[:, gl[h]] - big_m_b[h][:, :MLSTM_DK])).astype(BF16) for h in range(HEADS)]
        num_aug = []
        for h in range(HEADS):
            lhs = jnp.concatenate([(scores[h] * dmat[h]).astype(BF16), q_state[h],
                                   jnp.zeros((L, L - MLSTM_DK), BF16)], axis=1)
            rhs = jnp.concatenate([v_aug[h], c_state[h].astype(BF16),
                                   jnp.zeros((L - MLSTM_DK, 2 * MLSTM_DV), BF16)], axis=0)
            num_aug.append(_dot(lhs, rhs))
        for h in range(HEADS):
            num = num_aug[h][:, :MLSTM_DV]
            d = jnp.maximum(jnp.abs(num_aug[h][:, MLSTM_DV:MLSTM_DV + 1]), e_neg_m[:, gl[h]])
            scale = lax.rsqrt(jnp.mean(num * num, axis=-1, keepdims=True) + NORM_EPS * d * d)
            y_ref[rows, hv[h]] = (num * scale * out_gain[:, hv[h]]).astype(y_ref.dtype)
            c_state[h] = s_old[:, gl[h]] * c_state[h] + c_loc[h]

    m_ref[...] = m_prev
    for h in range(HEADS):
        c_ref[h] = c_state[h]


def _mlstm(main, tail, gate_bias, head_norm, bsz, seq, n_chunk):
    t = main.shape[0]
    rows = n_chunk * CHUNK
    nc = seq // rows
    row = lambda b, c: b * nc + c
    return pl.pallas_call(
        functools.partial(_mlstm_kernel, n_chunk=n_chunk),
        grid=(bsz, nc),
        in_specs=[pl.BlockSpec((rows, 512), lambda b, c: (row(b, c), 6)),
                  pl.BlockSpec((rows, 512), lambda b, c: (row(b, c), 7)),
                  pl.BlockSpec((rows, D_MODEL), lambda b, c: (row(b, c), 4)),
                  pl.BlockSpec((rows, D_MODEL), lambda b, c: (row(b, c), 5)),
                  pl.BlockSpec((rows, 128), lambda b, c: (row(b, c), TAIL_KROPE // 128)),
                  pl.BlockSpec((1, 128), lambda b, c: (0, 0)),
                  pl.BlockSpec((1, D_MODEL), lambda b, c: (0, 0))],
        out_specs=pl.BlockSpec((rows, D_MODEL), lambda b, c: (row(b, c), 0)),
        out_shape=jax.ShapeDtypeStruct((t, D_MODEL), BF16),
        scratch_shapes=[pltpu.VMEM((HEADS, MLSTM_DK, 2 * MLSTM_DV), F32),
                        pltpu.VMEM((1, 128), F32)],
        compiler_params=pltpu.CompilerParams(dimension_semantics=("parallel", "arbitrary")),
        name="mlstm",
    )(main, main, main, main, tail, gate_bias, head_norm)


def _rms0(x, g_col):
    return x * lax.rsqrt(jnp.mean(x * x, axis=0, keepdims=True) + NORM_EPS) * g_col


def _rope(x, cs):
    half = MLA_ROPE // 2
    rot = jnp.concatenate([x[:, half:], x[:, :half]], axis=-1)
    return x * cs[:, :MLA_ROPE] + rot * cs[:, MLA_ROPE:]


def _mla_prep_kernel(tail_ref, cs_ref, cst_ref, qa_ref, wqt_ref, kva_ref, wk_ref, wvt_ref, qn_ref, kn_ref,
                     qt_out, k_out, vt_out):
    tm = ATT_T
    half = MLA_ROPE // 2
    qn_g, qp_g = qn_ref[:MLA_NOPE, :], qn_ref[MLA_NOPE:, :]
    kn_g, kp_g = kn_ref[:, :MLA_NOPE], kn_ref[:, MLA_NOPE:]
    pe0 = HEADS * MLA_NOPE
    ones_rows = (lax.broadcasted_iota(jnp.int32, (V_ROWS - MLA_V, tm), 0) == 0).astype(BF16)
    q_pad = jnp.zeros((QK_PAD - MLA_NOPE - MLA_ROPE, tm), BF16)
    k_pad = jnp.zeros((tm, QK_PAD - MLA_NOPE - MLA_ROPE), BF16)
    for s in range(tail_ref.shape[0] // tm):
        rows = slice(s * tm, (s + 1) * tm)
        cq = _rms(tail_ref[rows, :Q_LORA], qa_ref[...]).astype(BF16)
        ckv = _rms(tail_ref[rows, Q_LORA:Q_LORA + KV_LORA], kva_ref[...]).astype(BF16)
        qft = _dot_nt(wqt_ref[...], cq)
        kf = _dot(ckv, wk_ref[...])
        vft = _dot_nt(wvt_ref[...], ckv)
        cos_t, sin_t = cst_ref[:half, rows], cst_ref[half:, rows]
        k_pe = _rope(_rms(tail_ref[rows, TAIL_KROPE:TAIL_KROPE + MLA_ROPE], kp_g), cs_ref[rows, :]).astype(BF16)
        for h in range(HEADS):
            q_nope = _rms0(qft[h * MLA_NOPE:(h + 1) * MLA_NOPE], qn_g)
            q_pe = _rms0(qft[pe0 + h * MLA_ROPE:pe0 + (h + 1) * MLA_ROPE], qp_g)
            x1, x2 = q_pe[:half], q_pe[half:]
            q_rot = jnp.concatenate([x1 * cos_t - x2 * sin_t, x1 * sin_t + x2 * cos_t], axis=0)
            qt_out[0, h, s] = jnp.concatenate([q_nope.astype(BF16), q_rot.astype(BF16), q_pad], axis=0)
            k_nope = _rms(kf[:, h * MLA_NOPE:(h + 1) * MLA_NOPE], kn_g)
            k_out[0, h, rows, :] = jnp.concatenate([k_nope.astype(BF16), k_pe, k_pad], axis=-1)
            vt_out[0, h, s] = jnp.concatenate([vft[h * MLA_V:(h + 1) * MLA_V].astype(BF16), ones_rows], axis=0)


def _mla_prep(tail, cs, cst, qa, wqt, kva, wk, wvt, qn_col, kn, bsz, seq):
    tiles = 2 if (seq // ATT_T) % 2 == 0 else 1
    tm = tiles * ATT_T
    nt = seq // tm
    full = lambda shape: pl.BlockSpec(shape, lambda i: (0,) * len(shape))
    blocked = lambda rows: pl.BlockSpec((1, HEADS, tiles, rows, ATT_T), lambda i: (i // nt, 0, i % nt, 0, 0))
    return pl.pallas_call(
        _mla_prep_kernel,
        grid=(bsz * nt,),
        in_specs=[pl.BlockSpec((tm, TAIL_COLS), lambda i: (i, 0)),
                  pl.BlockSpec((tm, 2 * MLA_ROPE), lambda i: (i, 0)),
                  pl.BlockSpec((MLA_ROPE, tm), lambda i: (0, i)),
                  full((1, Q_LORA)), full(wqt.shape), full((1, KV_LORA)), full(wk.shape), full(wvt.shape),
                  full((MLA_NOPE + MLA_ROPE, 1)), full((1, MLA_NOPE + MLA_ROPE))],
        out_specs=[blocked(QK_PAD),
                   pl.BlockSpec((1, HEADS, tm, QK_PAD), lambda i: (i // nt, 0, i % nt, 0)),
                   blocked(V_ROWS)],
        out_shape=[jax.ShapeDtypeStruct((bsz, HEADS, seq // ATT_T, QK_PAD, ATT_T), BF16),
                   jax.ShapeDtypeStruct((bsz, HEADS, seq, QK_PAD), BF16),
                   jax.ShapeDtypeStruct((bsz, HEADS, seq // ATT_T, V_ROWS, ATT_T), BF16)],
        compiler_params=pltpu.CompilerParams(dimension_semantics=("parallel",),
                                             vmem_limit_bytes=VMEM_LIMIT),
        name="mla_prep",
    )(tail, cs, cst, qa, wqt, kva, wk, wvt, qn_col, kn)


def _attn_kernel(qt_ref, k_ref, vt_ref, o_ref, s_scr, p_scr, a_scr, m_all, acc_all, *, nq):
    t = ATT_T

    def qk(qi, j):
        k0 = pl.multiple_of(j * t, t)
        return _dot(k_ref[0, 0, pl.ds(k0, t), :], qt_ref[0, 0, qi])

    def accumulate(tile, alpha, p):
        q, j = tile
        acc_all[q] = alpha * acc_all[q] + _dot(vt_ref[0, 0, j], p)

    def stage(prev, cur, nxt, s_cur, p_prev, alpha_prev, masked):
        accumulate(prev, alpha_prev, p_prev)
        qi = cur[0]
        if masked:
            keep = lax.broadcasted_iota(jnp.int32, (t, t), 0) <= lax.broadcasted_iota(jnp.int32, (t, t), 1)
            s_cur = jnp.where(keep, s_cur, MASK_VALUE)
        m_old = m_all[qi]
        m_new = jnp.maximum(m_old, jnp.max(s_cur, axis=0, keepdims=True))
        m_all[qi] = m_new
        alpha = jnp.exp2(m_old - m_new)
        p = jnp.exp2((s_cur - m_new).astype(BF16))
        s_next = qk(jnp.minimum(nxt[0], nq - 1), jnp.minimum(nxt[1], nq - 1))
        return s_next, p, alpha

    def advance_lower(qi, j):
        wrap = j + 1 == qi
        return jnp.where(wrap, qi + 1, qi), jnp.where(wrap, 0, j + 1)

    def advance_diag(qi, j):
        return qi + 1, j + 1

    def run_phase(n_steps, first, advance, masked):
        if n_steps == 0:
            return
        s_scr[...] = qk(*first)
        p_scr[...] = jnp.zeros_like(p_scr)
        a_scr[...] = jnp.ones_like(a_scr)

        def steps(count, prev, cur):
            s_cur, p_prev, alpha_prev = s_scr[...], p_scr[...], a_scr[...]
            for _ in range(count):
                nxt = advance(*cur)
                s_cur, p_prev, alpha_prev = stage(prev, cur, nxt, s_cur, p_prev, alpha_prev, masked)
                prev, cur = cur, nxt
            s_scr[...] = s_cur
            p_scr[...] = p_prev
            a_scr[...] = alpha_prev
            return prev, cur

        first = (jnp.int32(first[0]), jnp.int32(first[1]))
        prev, cur = lax.fori_loop(0, n_steps // ATT_UNROLL, lambda _, c: steps(ATT_UNROLL, *c), (first, first))
        if n_steps % ATT_UNROLL:
            prev, cur = steps(n_steps % ATT_UNROLL, prev, cur)
        accumulate(prev, a_scr[...], p_scr[...])

    m_all[...] = jnp.full_like(m_all, MASK_VALUE)
    acc_all[...] = jnp.zeros_like(acc_all)
    run_phase(nq * (nq - 1) // 2, (1, 0), advance_lower, False)
    run_phase(nq, (0, 0), advance_diag, True)
    for qi in range(nq):
        acc = acc_all[qi]
        o_ref[0, qi * t:(qi + 1) * t, :] = (acc[:MLA_V] / acc[MLA_V:MLA_V + 1]).T.astype(o_ref.dtype)


def _attn_bounded_kernel(qt_ref, k_ref, vt_ref, o_ref, s_scr, p_scr, acc_all, *, nq):
    t = ATT_T

    def qk(qi, j):
        k0 = pl.multiple_of(j * t, t)
        return _dot(k_ref[0, 0, pl.ds(k0, t), :], qt_ref[0, 0, qi])

    def accumulate(tile, p):
        q, j = tile
        acc_all[q] += _dot(vt_ref[0, 0, j], p)

    def stage(prev, cur, nxt, s_cur, p_prev, masked):
        accumulate(prev, p_prev)
        p = jnp.exp2(s_cur)
        if masked:
            keep = lax.broadcasted_iota(jnp.int32, (t, t), 0) <= lax.broadcasted_iota(jnp.int32, (t, t), 1)
            p = jnp.where(keep, p, 0.0)
        s_next = qk(jnp.minimum(nxt[0], nq - 1), jnp.minimum(nxt[1], nq - 1))
        return s_next, p.astype(BF16)

    def advance_lower(qi, j):
        wrap = j + 1 == qi
        return jnp.where(wrap, qi + 1, qi), jnp.where(wrap, 0, j + 1)

    def advance_diag(qi, j):
        return qi + 1, j + 1

    def run_phase(n_steps, first, advance, masked):
        if n_steps == 0:
            return
        s_scr[...] = qk(*first)
        p_scr[...] = jnp.zeros_like(p_scr)

        def steps(count, prev, cur):
            s_cur, p_prev = s_scr[...], p_scr[...]
            for _ in range(count):
                nxt = advance(*cur)
                s_cur, p_prev = stage(prev, cur, nxt, s_cur, p_prev, masked)
                prev, cur = cur, nxt
            s_scr[...] = s_cur
            p_scr[...] = p_prev
            return prev, cur

        first = (jnp.int32(first[0]), jnp.int32(first[1]))
        prev, cur = lax.fori_loop(0, n_steps // ATT_UNROLL, lambda _, c: steps(ATT_UNROLL, *c), (first, first))
        if n_steps % ATT_UNROLL:
            prev, cur = steps(n_steps % ATT_UNROLL, prev, cur)
        accumulate(prev, p_scr[...])

    acc_all[...] = jnp.zeros_like(acc_all)
    run_phase(nq * (nq - 1) // 2, (1, 0), advance_lower, False)
    run_phase(nq, (0, 0), advance_diag, True)
    for qi in range(nq):
        acc = acc_all[qi]
        o_ref[0, qi * t:(qi + 1) * t, :] = (acc[:MLA_V] / acc[MLA_V:MLA_V + 1]).T.astype(o_ref.dtype)


def _attention_bounded(qt, k, vt):
    bsz, _, seq, _ = k.shape
    nq = seq // ATT_T
    kern = functools.partial(_attn_bounded_kernel, nq=nq)
    return pl.pallas_call(
        kern,
        grid=(bsz, HEADS),
        in_specs=[pl.BlockSpec((1, 1, nq, QK_PAD, ATT_T), lambda b, h: (b, h, 0, 0, 0)),
                  pl.BlockSpec((1, 1, seq, QK_PAD), lambda b, h: (b, h, 0, 0)),
                  pl.BlockSpec((1, 1, nq, V_ROWS, ATT_T), lambda b, h: (b, h, 0, 0, 0))],
        out_specs=pl.BlockSpec((1, seq, MLA_V), lambda b, h: (b, 0, h)),
        out_shape=jax.ShapeDtypeStruct((bsz, seq, HEADS * MLA_V), BF16),
        scratch_shapes=[pltpu.VMEM((ATT_T, ATT_T), F32), pltpu.VMEM((ATT_T, ATT_T), BF16),
                        pltpu.VMEM((nq, V_ROWS, ATT_T), F32)],
        compiler_params=pltpu.CompilerParams(dimension_semantics=("parallel", "parallel"),
                                             vmem_limit_bytes=VMEM_LIMIT),
        name="attn_bounded",
    )(qt, k, vt)


def _attention(qt, k, vt):
    bsz, _, seq, _ = k.shape
    nq = seq // ATT_T
    kern = functools.partial(_attn_kernel, nq=nq)
    return pl.pallas_call(
        kern,
        grid=(bsz, HEADS),
        in_specs=[pl.BlockSpec((1, 1, nq, QK_PAD, ATT_T), lambda b, h: (b, h, 0, 0, 0)),
                  pl.BlockSpec((1, 1, seq, QK_PAD), lambda b, h: (b, h, 0, 0)),
                  pl.BlockSpec((1, 1, nq, V_ROWS, ATT_T), lambda b, h: (b, h, 0, 0, 0))],
        out_specs=pl.BlockSpec((1, seq, MLA_V), lambda b, h: (b, 0, h)),
        out_shape=jax.ShapeDtypeStruct((bsz, seq, HEADS * MLA_V), BF16),
        scratch_shapes=[pltpu.VMEM((ATT_T, ATT_T), F32), pltpu.VMEM((ATT_T, ATT_T), BF16),
                        pltpu.VMEM((1, ATT_T), F32), pltpu.VMEM((nq, 1, ATT_T), F32),
                        pltpu.VMEM((nq, V_ROWS, ATT_T), F32)],
        compiler_params=pltpu.CompilerParams(dimension_semantics=("parallel", "parallel"),
                                             vmem_limit_bytes=VMEM_LIMIT),
        name="attn",
    )(qt, k, vt)


def _merge_kernel(cb_ref, cc_ref, cu_ref, gate_ref, yb_ref, yc_ref, x_ref, cw_ref, wb_ref, wo_ref,
                  out_ref, zprev_ref, *, tiles_per_seq):
    tm = x_ref.shape[0]

    @pl.when(pl.program_id(0) % tiles_per_seq == 0)
    def _():
        zprev_ref[...] = jnp.zeros_like(zprev_ref)

    z = cc_ref[...].astype(F32) * cu_ref[...].astype(F32)
    row = lax.broadcasted_iota(jnp.int32, (tm, D_MODEL), 0)
    p1 = zprev_ref[7:8, :]
    p2 = zprev_ref[6:7, :]
    z1 = jnp.where(row == 0, p1, pltpu.roll(z, 1, 0))
    z2 = jnp.where(row == 0, p2, jnp.where(row == 1, p1, pltpu.roll(z, 2, 0)))
    zprev_ref[...] = z[tm - 8:, :]
    cw = cw_ref[...]
    y_a = cb_ref[...].astype(F32) * (cw[0:1] * z2 + cw[1:2] * z1 + cw[2:3] * z)

    g = gate_ref[...]
    merged = g[:, :D_MODEL].astype(F32) * _dot(y_a.astype(BF16), wb_ref[0])
    merged += g[:, D_MODEL:2 * D_MODEL].astype(F32) * _dot(yb_ref[...], wb_ref[1])
    merged += g[:, 2 * D_MODEL:].astype(F32) * _dot(yc_ref[...], wb_ref[2])
    out_ref[...] = x_ref[...] + _dot(merged.astype(BF16), wo_ref[...])


def _merge(main, y_b, y_c, x2, conv_w, w_branch, w_out, seq, tm):
    t = x2.shape[0]
    kern = functools.partial(_merge_kernel, tiles_per_seq=seq // tm)
    col = lambda c: pl.BlockSpec((tm, D_MODEL), lambda i: (i, c))
    return pl.pallas_call(
        kern,
        grid=(t // tm,),
        in_specs=[col(0), col(1), col(2),
                  pl.BlockSpec((tm, N_BRANCH * D_MODEL), lambda i: (i, GATE_COL0 // (N_BRANCH * D_MODEL))),
                  col(0), col(0), col(0),
                  pl.BlockSpec((CONV_WIDTH, D_MODEL), lambda i: (0, 0)),
                  pl.BlockSpec((N_BRANCH, D_MODEL, D_MODEL), lambda i: (0, 0, 0), pipeline_mode=pl.Buffered(1)),
                  pl.BlockSpec((D_MODEL, D_MODEL), lambda i: (0, 0), pipeline_mode=pl.Buffered(1))],
        out_specs=col(0),
        out_shape=jax.ShapeDtypeStruct((t, D_MODEL), F32),
        scratch_shapes=[pltpu.VMEM((8, D_MODEL), F32)],
        compiler_params=pltpu.CompilerParams(dimension_semantics=("arbitrary",),
                                             vmem_limit_bytes=VMEM_LIMIT),
        name="merge",
    )(main, main, main, main, y_b, y_c, x2, conv_w, w_branch, w_out)


def _mlp_kernel(x_ref, g_ref, wu_ref, wd_ref, out_ref, *, n_chunk):
    x = x_ref[...]
    h = _rms(x, g_ref[...]).astype(BF16)
    ck = D_FF // n_chunk
    acc = x
    for c in range(n_chunk):
        u = jnp.maximum(_dot(h, wu_ref[:, c * ck:(c + 1) * ck]), 0.0)
        acc = acc + _dot((u * u).astype(BF16), wd_ref[c * ck:(c + 1) * ck, :])
    out_ref[...] = acc


def _mlp(x2, g, w_up, w_down, tm):
    t = x2.shape[0]
    kern = functools.partial(_mlp_kernel, n_chunk=4)
    return pl.pallas_call(
        kern,
        grid=(t // tm,),
        in_specs=[pl.BlockSpec((tm, D_MODEL), lambda i: (i, 0)),
                  pl.BlockSpec((1, D_MODEL), lambda i: (0, 0)),
                  pl.BlockSpec((D_MODEL, D_FF), lambda i: (0, 0), pipeline_mode=pl.Buffered(1)),
                  pl.BlockSpec((D_FF, D_MODEL), lambda i: (0, 0), pipeline_mode=pl.Buffered(1))],
        out_specs=pl.BlockSpec((tm, D_MODEL), lambda i: (i, 0)),
        out_shape=jax.ShapeDtypeStruct((t, D_MODEL), F32),
        compiler_params=pltpu.CompilerParams(dimension_semantics=("parallel",),
                                             vmem_limit_bytes=VMEM_LIMIT),
        name="mlp",
    )(x2, g, w_up, w_down)


def _regroup_w_in(w_in):
    c = D_MODEL
    q0 = 3 * c
    i0 = q0 + 2 * HEADS * MLSTM_DK + 2 * HEADS * MLSTM_DV
    cq0 = i0 + 2 * HEADS
    ckv0 = cq0 + Q_LORA
    kr0 = ckv0 + KV_LORA
    g0 = kr0 + MLA_ROPE
    pad = jnp.zeros((D_MODEL, TAIL_COLS - (Q_LORA + KV_LORA + MLA_ROPE + 2 * HEADS)), w_in.dtype)
    w_main = jnp.concatenate([w_in[:, :i0], w_in[:, g0:]], axis=1).astype(BF16)
    w_tail = jnp.concatenate([w_in[:, cq0:g0], w_in[:, i0:cq0], pad], axis=1).astype(BF16)
    return w_main, w_tail


def _heads_last_split(w, first, second):
    k = w.shape[0]
    w3 = w.reshape(k, HEADS, first + second)
    return jnp.concatenate([w3[:, :, :first].reshape(k, HEADS * first),
                            w3[:, :, first:].reshape(k, HEADS * second)], axis=1).astype(BF16)


def _score_bound(q_gain, k_gain):
    def sq_norm(g):
        g = g.astype(F32)
        return MLA_NOPE * jnp.max(g[:MLA_NOPE] ** 2) + MLA_ROPE * jnp.max(g[MLA_NOPE:] ** 2)
    return jnp.sqrt(sq_norm(q_gain) * sq_norm(k_gain)) * (Q_SCALE * BOUND_SLACK)


def _pick(n, candidates):
    for c in candidates:
        if n % c == 0:
            return c
    raise ValueError(f"no tile size for {n}")


def kernel(x, positions, mix_norm, w_in, conv_w, mlstm_igate_bias, mlstm_fgate_bias, mlstm_head_norm,
           mla_q_a_norm, mla_w_uq, mla_kv_a_norm, mla_w_ukv, mla_q_norm, mla_k_norm, w_branch, w_out,
           mlp_norm, w_up, w_down):
    bsz, seq, _ = x.shape
    depth = w_in.shape[0]
    t = bsz * seq
    assert seq % ATT_T == 0
    n_chunk = _pick(seq // CHUNK, tuple(g for g in (8, 4, 2, 1) if g <= MLSTM_GROUP))
    tm_proj = _pick(t, (1024, 512, 256, 128))
    tm = _pick(seq, (512, 256, 128))
    tm_mlp = _pick(t, (1024, 512, 256, 128))

    x2 = x.reshape(t, D_MODEL)
    cs, cst = _rope_table(positions, ATT_T)
    row = lambda a: a.reshape(1, -1).astype(F32)

    for l in range(depth):
        gate_bias = jnp.zeros((1, 128), F32)
        gate_bias = gate_bias.at[0, IGATE_LANE:IGATE_LANE + HEADS].set(mlstm_igate_bias[l])
        gate_bias = gate_bias.at[0, FGATE_LANE:FGATE_LANE + HEADS].set(mlstm_fgate_bias[l])
        w_q = _heads_last_split(mla_w_uq[l], MLA_NOPE, MLA_ROPE)
        w_kv = _heads_last_split(mla_w_ukv[l], MLA_NOPE, MLA_V)

        main, tail = _in_proj(x2, row(mix_norm[l]), *_regroup_w_in(w_in[l]), tm_proj, IN_PROJ_TN)
        y_b = _mlstm(main, tail, gate_bias, row(mlstm_head_norm[l]), bsz, seq, n_chunk)
        qt, k, vt = _mla_prep(tail, cs, cst, row(mla_q_a_norm[l]), w_q.T, row(mla_kv_a_norm[l]),
                              w_kv[:, :HEADS * MLA_NOPE], w_kv[:, HEADS * MLA_NOPE:].T,
                              mla_q_norm[l].reshape(-1, 1).astype(F32) * Q_SCALE, row(mla_k_norm[l]), bsz, seq)
        y_c = lax.cond(_score_bound(mla_q_norm[l], mla_k_norm[l]) <= SCORE_BOUND,
                       _attention_bounded, _attention, qt, k, vt).reshape(t, D_MODEL)
        x2 = _merge(main, y_b, y_c, x2, conv_w[l].astype(F32), w_branch[l].astype(BF16),
                    w_out[l].astype(BF16), seq, tm)
        x2 = _mlp(x2, row(mlp_norm[l]), w_up[l].astype(BF16), w_down[l].astype(BF16), tm_mlp)
    return x2.reshape(bsz, seq, D_MODEL)
```

```python
import functools

import jax
import jax.numpy as jnp
import numpy as np
from jax import lax
from jax.experimental import pallas as pl
from jax.experimental.pallas import tpu as pltpu

F32 = jnp.float32
BF16 = jnp.bfloat16

D_MODEL = 1024
N_BRANCH = 3
CONV_WIDTH = 3
HEADS = 8
MLSTM_DK = 64
MLSTM_DV = 128
CHUNK = 128
GATE_CAP = 15.0
MLA_NOPE = 128
MLA_ROPE = 64
MLA_V = 128
Q_LORA = 256
KV_LORA = 128
ROPE_BASE = 10000.0
D_FF = 4 * D_MODEL
NORM_EPS = 1e-6
QK_PAD = 256

MAIN_COLS = 9 * D_MODEL
TAIL_COLS = 512
COL_MQ = 3 * D_MODEL
COL_MK = COL_MQ + HEADS * MLSTM_DK
COL_MV = COL_MK + HEADS * MLSTM_DK
COL_MO = COL_MV + HEADS * MLSTM_DV
GATE_COL0 = 6 * D_MODEL
LANES = 128
TAIL_KROPE = Q_LORA + KV_LORA
IGATE_LANE = MLA_ROPE
FGATE_LANE = MLA_ROPE + HEADS

IN_PROJ_TN = 3072
IN_PROJ_CHUNK = 1536
MLSTM_GROUP = 8
ATT_T = 512
LOG2E = 1.4426950408889634
Q_SCALE = (MLA_NOPE + MLA_ROPE) ** -0.5 * LOG2E
MASK_VALUE = -1e30
SCORE_BOUND = 70.0
SCORE_SHIFT = 48.0
BOUND_SLACK = 1.02
ATT_UNROLL = 8
V_ROWS = MLA_V + 16

VMEM_LIMIT = 56 * 1024 * 1024


def _rms(x, g):
    return x * lax.rsqrt(jnp.mean(x * x, axis=-1, keepdims=True) + NORM_EPS) * g


def _dot(a, b):
    return jnp.dot(a, b, preferred_element_type=F32)


def _dot_nt(a, b):
    return lax.dot_general(a, b, (((1,), (1,)), ((), ())), preferred_element_type=F32)


def _rope_table_kernel(pos_ref, freq_ref, cs_ref, cst_ref):
    ang = freq_ref[...] * pos_ref[...].astype(F32)
    c, s = jnp.cos(ang), jnp.sin(ang)
    cst_ref[...] = jnp.concatenate([c, s], axis=0)
    cs_ref[...] = jnp.concatenate([c, c, -s, s], axis=0).T


def _rope_table(positions, tm):
    t = positions.size
    half = MLA_ROPE // 2
    inv_freq = (np.float32(ROPE_BASE) ** (-np.arange(0, MLA_ROPE, 2, dtype=np.float32) / np.float32(MLA_ROPE)))
    inv_freq = jnp.asarray(inv_freq.astype(np.float32))
    return pl.pallas_call(
        _rope_table_kernel,
        grid=(t // tm,),
        in_specs=[pl.BlockSpec((1, tm), lambda i: (0, i)),
                  pl.BlockSpec((half, 1), lambda i: (0, 0))],
        out_specs=[pl.BlockSpec((tm, 2 * MLA_ROPE), lambda i: (i, 0)),
                   pl.BlockSpec((MLA_ROPE, tm), lambda i: (0, i))],
        out_shape=[jax.ShapeDtypeStruct((t, 2 * MLA_ROPE), F32),
                   jax.ShapeDtypeStruct((MLA_ROPE, t), F32)],
        name="rope_table",
    )(positions.reshape(1, t), inv_freq.reshape(half, 1))


def _in_proj_kernel(x_ref, g_ref, w_ref, wt_ref, main_ref, tail_ref, xn_ref, *, n_gate0):
    j = pl.program_id(1)

    @pl.when(j == 0)
    def _():
        xn = _rms(x_ref[...], g_ref[...]).astype(BF16)
        xn_ref[...] = xn
        tail_ref[...] = _dot(xn, wt_ref[...])

    tn = w_ref.shape[1]
    cols = [slice(c, c + IN_PROJ_CHUNK) for c in range(0, tn, IN_PROJ_CHUNK)]

    @pl.when(j < n_gate0)
    def _():
        for c in cols:
            main_ref[:, c] = _dot(xn_ref[...], w_ref[:, c]).astype(BF16)

    @pl.when(j >= n_gate0)
    def _():
        for c in cols:
            main_ref[:, c] = jax.nn.sigmoid(_dot(xn_ref[...], w_ref[:, c])).astype(BF16)


def _in_proj(x2, g, w_main, w_tail, layer, tm, tn):
    t = x2.shape[0]
    assert GATE_COL0 % tn == 0 and MAIN_COLS % tn == 0
    kern = functools.partial(_in_proj_kernel, n_gate0=GATE_COL0 // tn)
    return pl.pallas_call(
        kern,
        grid=(t // tm, MAIN_COLS // tn),
        in_specs=[pl.BlockSpec((tm, D_MODEL), lambda i, j: (i, 0)),
                  pl.BlockSpec((1, D_MODEL), lambda i, j: (0, 0)),
                  pl.BlockSpec((None, D_MODEL, tn), lambda i, j: (layer, 0, j)),
                  pl.BlockSpec((None, D_MODEL, TAIL_COLS), lambda i, j: (layer, 0, 0))],
        out_specs=[pl.BlockSpec((tm, tn), lambda i, j: (i, j)),
                   pl.BlockSpec((tm, TAIL_COLS), lambda i, j: (i, 0))],
        out_shape=[jax.ShapeDtypeStruct((t, MAIN_COLS), BF16),
                   jax.ShapeDtypeStruct((t, TAIL_COLS), F32)],
        scratch_shapes=[pltpu.VMEM((tm, D_MODEL), BF16)],
        compiler_params=pltpu.CompilerParams(
            dimension_semantics=("parallel", "arbitrary"), vmem_limit_bytes=VMEM_LIMIT),
        name="in_proj",
    )(x2, g, w_main, w_tail)


def _time_scan(x, op, row):
    s = 1
    while s < CHUNK:
        x = jnp.where(row >= s, op(x, pltpu.roll(x, s, 0)), x)
        s *= 2
    return x


def _mlstm_kernel(q_ref, k_ref, v_ref, o_ref, gate_ref, bias_ref, hn_ref, y_ref, c_ref, m_ref, *, n_chunk):
    L = CHUNK

    @pl.when(pl.program_id(1) == 0)
    def _():
        c_ref[...] = jnp.zeros_like(c_ref)
        m_ref[...] = jnp.zeros_like(m_ref)

    row = lax.broadcasted_iota(jnp.int32, (L, LANES), 0)
    lane = lax.broadcasted_iota(jnp.int32, (L, LANES), 1)
    gate_lanes = jnp.logical_and(lane >= IGATE_LANE, lane < FGATE_LANE + HEADS)
    tri = (lax.broadcasted_iota(jnp.int32, (L, L), 0) >= lax.broadcasted_iota(jnp.int32, (L, L), 1))
    ones_col = (lax.broadcasted_iota(jnp.int32, (L, MLSTM_DV), 1) == 0).astype(BF16)
    m_prev = m_ref[...]
    c_state = [c_ref[h] for h in range(HEADS)]

    for g in range(n_chunk):
        rows = slice(g * L, (g + 1) * L)
        pre = jnp.where(gate_lanes, gate_ref[rows, :] + bias_ref[...], 0.0)
        capped = GATE_CAP * jnp.tanh(pre * (1.0 / GATE_CAP))
        log_i = capped * LOG2E
        log_f = (jnp.minimum(capped, 0.0) - jnp.log1p(jnp.exp(-jnp.abs(capped)))) * LOG2E
        log_f = pltpu.roll(log_f, LANES - HEADS, 1)
        b = _time_scan(log_f, jnp.add, row)
        b_last = b[L - 1:L, :]
        rowd = log_i - b
        big_m = jnp.maximum(_time_scan(rowd, jnp.maximum, row), m_prev)
        e_neg_m = jnp.exp2(-(b + big_m))
        a = b_last + rowd
        m_loc = jnp.max(a, axis=0, keepdims=True)
        m_new = jnp.maximum(b_last + m_prev, m_loc)
        s_old = jnp.exp2(b_last + m_prev - m_new)
        w = jnp.exp2(a - m_new)
        m_prev_g, m_prev = m_prev, m_new
        rowd_t = rowd.T
        w_t = w.T
        q_all = q_ref[rows, :] * (MLSTM_DK ** -0.5)
        k_t = k_ref[rows, :].astype(F32).T

        hk = [slice(h * MLSTM_DK, (h + 1) * MLSTM_DK) for h in range(HEADS)]
        hv = [slice(h * MLSTM_DV, (h + 1) * MLSTM_DV) for h in range(HEADS)]
        gl = [slice(IGATE_LANE + h, IGATE_LANE + h + 1) for h in range(HEADS)]
        v_aug = [jnp.concatenate([v_ref[rows, hv[h]], ones_col], axis=1) for h in range(HEADS)]
        scores = [_dot(q_all[:, hk[h]], k_t[hk[h], :].astype(BF16)) for h in range(HEADS)]
        c_loc = [_dot((k_t[hk[h], :] * w_t[gl[h], :]).astype(BF16), v_aug[h]) for h in range(HEADS)]
        out_gain = (0.5 * jnp.tanh(0.5 * o_ref[rows, :].astype(F32)) + 0.5) * hn_ref[...]
        big_m_b = [jnp.broadcast_to(big_m[:, gl[h]], (L, L)) for h in range(HEADS)]
        dmat = [jnp.where(tri, jnp.exp2(rowd_t[gl[h], :] - big_m_b[h]), 0.0) for h in range(HEADS)]
        q_state = [(q_all[:, hk[h]].astype(F32)
                    * jnp.exp2(m_prev_g[:, gl[h]] - big_m_b[h][:, :MLSTM_DK])).astype(BF16) for h in range(HEADS)]
        num_aug = []
        for h in range(HEADS):
            lhs = jnp.concatenate([(scores[h] * dmat[h]).astype(BF16), q_state[h],
                                   jnp.zeros((L, L - MLSTM_DK), BF16)], axis=1)
            rhs = jnp.concatenate([v_aug[h], c_state[h].astype(BF16),
                                   jnp.zeros((L - MLSTM_DK, 2 * MLSTM_DV), BF16)], axis=0)
            num_aug.append(_dot(lhs, rhs))
        for h in range(HEADS):
            num = num_aug[h][:, :MLSTM_DV]
            d = jnp.maximum(jnp.abs(num_aug[h][:, MLSTM_DV:MLSTM_DV + 1]), e_neg_m[:, gl[h]])
            scale = lax.rsqrt(jnp.mean(num * num, axis=-1, keepdims=True) + NORM_EPS * d * d)
            y_ref[rows, hv[h]] = (num * scale * out_gain[:, hv[h]]).astype(y_ref.dtype)
            c_state[h] = s_old[:, gl[h]] * c_state[h] + c_loc[h]

    m_ref[...] = m_prev
    for h in range(HEADS):
        c_ref[h] = c_state[h]


def _mlstm(main, tail, gate_bias, head_norm, bsz, seq, n_chunk):
    t = main.shape[0]
    rows = n_chunk * CHUNK
    nc = seq // rows
    row = lambda b, c: b * nc + c
    col_block = lambda col0, width: pl.BlockSpec((rows, width), lambda b, c: (row(b, c), col0 // width))
    return pl.pallas_call(
        functools.partial(_mlstm_kernel, n_chunk=n_chunk),
        grid=(bsz, nc),
        in_specs=[col_block(COL_MQ, HEADS * MLSTM_DK), col_block(COL_MK, HEADS * MLSTM_DK),
                  col_block(COL_MV, HEADS * MLSTM_DV), col_block(COL_MO, HEADS * MLSTM_DV),
                  col_block(TAIL_KROPE, LANES),
                  pl.BlockSpec((1, LANES), lambda b, c: (0, 0)),
                  pl.BlockSpec((1, D_MODEL), lambda b, c: (0, 0))],
        out_specs=pl.BlockSpec((rows, D_MODEL), lambda b, c: (row(b, c), 0)),
        out_shape=jax.ShapeDtypeStruct((t, D_MODEL), BF16),
        scratch_shapes=[pltpu.VMEM((HEADS, MLSTM_DK, 2 * MLSTM_DV), F32),
                        pltpu.VMEM((1, LANES), F32)],
        compiler_params=pltpu.CompilerParams(dimension_semantics=("parallel", "arbitrary")),
        name="mlstm",
    )(main, main, main, main, tail, gate_bias, head_norm)


def _rms0(x, g_col):
    return x * lax.rsqrt(jnp.mean(x * x, axis=0, keepdims=True) + NORM_EPS) * g_col


def _rope(x, cs):
    half = MLA_ROPE // 2
    rot = jnp.concatenate([x[:, half:], x[:, :half]], axis=-1)
    return x * cs[:, :MLA_ROPE] + rot * cs[:, MLA_ROPE:]


def _mla_prep_kernel(tail_ref, cs_ref, cst_ref, qa_ref, wqt_ref, kva_ref, wk_ref, wvt_ref, qn_ref, kn_ref,
                     qt_out, k_out, vt_out):
    tm = ATT_T
    half = MLA_ROPE // 2
    qn_g, qp_g = qn_ref[:MLA_NOPE, :], qn_ref[MLA_NOPE:, :]
    kn_g, kp_g = kn_ref[:, :MLA_NOPE], kn_ref[:, MLA_NOPE:]
    pe0 = HEADS * MLA_NOPE
    ones_rows = (lax.broadcasted_iota(jnp.int32, (V_ROWS - MLA_V, tm), 0) == 0).astype(BF16)
    pad = QK_PAD - MLA_NOPE - MLA_ROPE
    q_pad = jnp.where(lax.broadcasted_iota(jnp.int32, (pad, tm), 0) == 0, -SCORE_SHIFT, 0.0).astype(BF16)
    k_pad = (lax.broadcasted_iota(jnp.int32, (tm, pad), 1) == 0).astype(BF16)
    for s in range(tail_ref.shape[0] // tm):
        rows = slice(s * tm, (s + 1) * tm)
        cq = _rms(tail_ref[rows, :Q_LORA], qa_ref[...]).astype(BF16)
        ckv = _rms(tail_ref[rows, Q_LORA:Q_LORA + KV_LORA], kva_ref[...]).astype(BF16)
        qft = _dot_nt(wqt_ref[...], cq)
        kf = _dot(ckv, wk_ref[...])
        vft = _dot_nt(wvt_ref[...], ckv)
        cos_t, sin_t = cst_ref[:half, rows], cst_ref[half:, rows]
        k_pe = _rope(_rms(tail_ref[rows, TAIL_KROPE:TAIL_KROPE + MLA_ROPE], kp_g), cs_ref[rows, :]).astype(BF16)
        for h in range(HEADS):
            q_nope = _rms0(qft[h * MLA_NOPE:(h + 1) * MLA_NOPE], qn_g)
            q_pe = _rms0(qft[pe0 + h * MLA_ROPE:pe0 + (h + 1) * MLA_ROPE], qp_g)
            x1, x2 = q_pe[:half], q_pe[half:]
            q_rot = jnp.concatenate([x1 * cos_t - x2 * sin_t, x1 * sin_t + x2 * cos_t], axis=0)
            qt_out[0, h, s] = jnp.concatenate([q_nope.astype(BF16), q_rot.astype(BF16), q_pad], axis=0)
            k_nope = _rms(kf[:, h * MLA_NOPE:(h + 1) * MLA_NOPE], kn_g)
            k_out[0, h, rows, :] = jnp.concatenate([k_nope.astype(BF16), k_pe, k_pad], axis=-1)
            vt_out[0, h, s] = jnp.concatenate([vft[h * MLA_V:(h + 1) * MLA_V].astype(BF16), ones_rows], axis=0)


def _mla_prep(tail, cs, cst, qa, wqt, kva, wk, wvt, qn_col, kn, bsz, seq):
    tiles = 2 if (seq // ATT_T) % 2 == 0 else 1
    tm = tiles * ATT_T
    nt = seq // tm
    full = lambda shape: pl.BlockSpec(shape, lambda i: (0,) * len(shape))
    blocked = lambda rows: pl.BlockSpec((1, HEADS, tiles, rows, ATT_T), lambda i: (i // nt, 0, i % nt, 0, 0))
    return pl.pallas_call(
        _mla_prep_kernel,
        grid=(bsz * nt,),
        in_specs=[pl.BlockSpec((tm, TAIL_COLS), lambda i: (i, 0)),
                  pl.BlockSpec((tm, 2 * MLA_ROPE), lambda i: (i, 0)),
                  pl.BlockSpec((MLA_ROPE, tm), lambda i: (0, i)),
                  full((1, Q_LORA)), full(wqt.shape), full((1, KV_LORA)), full(wk.shape), full(wvt.shape),
                  full((MLA_NOPE + MLA_ROPE, 1)), full((1, MLA_NOPE + MLA_ROPE))],
        out_specs=[blocked(QK_PAD),
                   pl.BlockSpec((1, HEADS, tm, QK_PAD), lambda i: (i // nt, 0, i % nt, 0)),
                   blocked(V_ROWS)],
        out_shape=[jax.ShapeDtypeStruct((bsz, HEADS, seq // ATT_T, QK_PAD, ATT_T), BF16),
                   jax.ShapeDtypeStruct((bsz, HEADS, seq, QK_PAD), BF16),
                   jax.ShapeDtypeStruct((bsz, HEADS, seq // ATT_T, V_ROWS, ATT_T), BF16)],
        compiler_params=pltpu.CompilerParams(dimension_semantics=("parallel",),
                                             vmem_limit_bytes=VMEM_LIMIT),
        name="mla_prep",
    )(tail, cs, cst, qa, wqt, kva, wk, wvt, qn_col, kn)


def _attn_kernel(qt_ref, k_ref, vt_ref, o_ref, s_scr, p_scr, a_scr, m_all, acc_all, *, nq):
    t = ATT_T

    def qk(qi, j):
        k0 = pl.multiple_of(j * t, t)
        return _dot(k_ref[0, 0, pl.ds(k0, t), :], qt_ref[0, 0, qi])

    def accumulate(tile, alpha, p):
        q, j = tile
        acc_all[q] = alpha * acc_all[q] + _dot(vt_ref[0, 0, j], p)

    def stage(prev, cur, nxt, s_cur, p_prev, alpha_prev, masked):
        accumulate(prev, alpha_prev, p_prev)
        qi = cur[0]
        if masked:
            keep = lax.broadcasted_iota(jnp.int32, (t, t), 0) <= lax.broadcasted_iota(jnp.int32, (t, t), 1)
            s_cur = jnp.where(keep, s_cur, MASK_VALUE)
        m_old = m_all[qi]
        m_new = jnp.maximum(m_old, jnp.max(s_cur, axis=0, keepdims=True))
        m_all[qi] = m_new
        alpha = jnp.exp2(m_old - m_new)
        p = jnp.exp2((s_cur - m_new).astype(BF16))
        s_next = qk(jnp.minimum(nxt[0], nq - 1), jnp.minimum(nxt[1], nq - 1))
        return s_next, p, alpha

    def advance_lower(qi, j):
        wrap = j + 1 == qi
        return jnp.where(wrap, qi + 1, qi), jnp.where(wrap, 0, j + 1)

    def advance_diag(qi, j):
        return qi + 1, j + 1

    def run_phase(n_steps, first, advance, masked):
        if n_steps == 0:
            return
        s_scr[...] = qk(*first)
        p_scr[...] = jnp.zeros_like(p_scr)
        a_scr[...] = jnp.ones_like(a_scr)

        def steps(count, prev, cur):
            s_cur, p_prev, alpha_prev = s_scr[...], p_scr[...], a_scr[...]
            for _ in range(count):
                nxt = advance(*cur)
                s_cur, p_prev, alpha_prev = stage(prev, cur, nxt, s_cur, p_prev, alpha_prev, masked)
                prev, cur = cur, nxt
            s_scr[...] = s_cur
            p_scr[...] = p_prev
            a_scr[...] = alpha_prev
            return prev, cur

        first = (jnp.int32(first[0]), jnp.int32(first[1]))
        prev, cur = lax.fori_loop(0, n_steps // ATT_UNROLL, lambda _, c: steps(ATT_UNROLL, *c), (first, first))
        if n_steps % ATT_UNROLL:
            prev, cur = steps(n_steps % ATT_UNROLL, prev, cur)
        accumulate(prev, a_scr[...], p_scr[...])

    m_all[...] = jnp.full_like(m_all, MASK_VALUE)
    acc_all[...] = jnp.zeros_like(acc_all)
    run_phase(nq * (nq - 1) // 2, (1, 0), advance_lower, False)
    run_phase(nq, (0, 0), advance_diag, True)
    for qi in range(nq):
        acc = acc_all[qi]
        o_ref[0, qi * t:(qi + 1) * t, :] = (acc[:MLA_V] / acc[MLA_V:MLA_V + 1]).T.astype(o_ref.dtype)


def _attn_bounded_kernel(qt_ref, k_ref, vt_ref, o_ref, s_scr, p_scr, acc_all, *, nq):
    t = ATT_T

    def qk(qi, j):
        k0 = pl.multiple_of(j * t, t)
        return _dot(k_ref[0, 0, pl.ds(k0, t), :], qt_ref[0, 0, qi])

    def accumulate(tile, p):
        q, j = tile
        acc_all[q] += _dot(vt_ref[0, 0, j], p)

    def stage(prev, cur, nxt, s_cur, p_prev, masked):
        accumulate(prev, p_prev)
        p = jnp.exp2(s_cur)
        if masked:
            keep = lax.broadcasted_iota(jnp.int32, (t, t), 0) <= lax.broadcasted_iota(jnp.int32, (t, t), 1)
            p = jnp.where(keep, p, 0.0)
        s_next = qk(jnp.minimum(nxt[0], nq - 1), jnp.minimum(nxt[1], nq - 1))
        return s_next, p.astype(BF16)

    def advance_lower(qi, j):
        wrap = j + 1 == qi
        return jnp.where(wrap, qi + 1, qi), jnp.where(wrap, 0, j + 1)

    def advance_diag(qi, j):
        return qi + 1, j + 1

    def run_phase(n_steps, first, advance, masked):
        if n_steps == 0:
            return
        s_scr[...] = qk(*first)
        p_scr[...] = jnp.zeros_like(p_scr)

        def steps(count, prev, cur):
            s_cur, p_prev = s_scr[...], p_scr[...]
            for _ in range(count):
                nxt = advance(*cur)
                s_cur, p_prev = stage(prev, cur, nxt, s_cur, p_prev, masked)
                prev, cur = cur, nxt
            s_scr[...] = s_cur
            p_scr[...] = p_prev
            return prev, cur

        first = (jnp.int32(first[0]), jnp.int32(first[1]))
        prev, cur = lax.fori_loop(0, n_steps // ATT_UNROLL, lambda _, c: steps(ATT_UNROLL, *c), (first, first))
        if n_steps % ATT_UNROLL:
            prev, cur = steps(n_steps % ATT_UNROLL, prev, cur)
        accumulate(prev, p_scr[...])

    acc_all[...] = jnp.zeros_like(acc_all)
    run_phase(nq * (nq - 1) // 2, (1, 0), advance_lower, False)
    run_phase(nq, (0, 0), advance_diag, True)
    for qi in range(nq):
        acc = acc_all[qi]
        o_ref[0, qi * t:(qi + 1) * t, :] = (acc[:MLA_V] / acc[MLA_V:MLA_V + 1]).T.astype(o_ref.dtype)


def _attention_bounded(qt, k, vt):
    bsz, _, seq, _ = k.shape
    nq = seq // ATT_T
    kern = functools.partial(_attn_bounded_kernel, nq=nq)
    return pl.pallas_call(
        kern,
        grid=(bsz, HEADS),
        in_specs=[pl.BlockSpec((1, 1, nq, QK_PAD, ATT_T), lambda b, h: (b, h, 0, 0, 0)),
                  pl.BlockSpec((1, 1, seq, QK_PAD), lambda b, h: (b, h, 0, 0)),
                  pl.BlockSpec((1, 1, nq, V_ROWS, ATT_T), lambda b, h: (b, h, 0, 0, 0))],
        out_specs=pl.BlockSpec((1, seq, MLA_V), lambda b, h: (b, 0, h)),
        out_shape=jax.ShapeDtypeStruct((bsz, seq, HEADS * MLA_V), BF16),
        scratch_shapes=[pltpu.VMEM((ATT_T, ATT_T), F32), pltpu.VMEM((ATT_T, ATT_T), BF16),
                        pltpu.VMEM((nq, V_ROWS, ATT_T), F32)],
        compiler_params=pltpu.CompilerParams(dimension_semantics=("parallel", "parallel"),
                                             vmem_limit_bytes=VMEM_LIMIT),
        name="attn_bounded",
    )(qt, k, vt)


def _attention(qt, k, vt):
    bsz, _, seq, _ = k.shape
    nq = seq // ATT_T
    kern = functools.partial(_attn_kernel, nq=nq)
    return pl.pallas_call(
        kern,
        grid=(bsz, HEADS),
        in_specs=[pl.BlockSpec((1, 1, nq, QK_PAD, ATT_T), lambda b, h: (b, h, 0, 0, 0)),
                  pl.BlockSpec((1, 1, seq, QK_PAD), lambda b, h: (b, h, 0, 0)),
                  pl.BlockSpec((1, 1, nq, V_ROWS, ATT_T), lambda b, h: (b, h, 0, 0, 0))],
        out_specs=pl.BlockSpec((1, seq, MLA_V), lambda b, h: (b, 0, h)),
        out_shape=jax.ShapeDtypeStruct((bsz, seq, HEADS * MLA_V), BF16),
        scratch_shapes=[pltpu.VMEM((ATT_T, ATT_T), F32), pltpu.VMEM((ATT_T, ATT_T), BF16),
                        pltpu.VMEM((1, ATT_T), F32), pltpu.VMEM((nq, 1, ATT_T), F32),
                        pltpu.VMEM((nq, V_ROWS, ATT_T), F32)],
        compiler_params=pltpu.CompilerParams(dimension_semantics=("parallel", "parallel"),
                                             vmem_limit_bytes=VMEM_LIMIT),
        name="attn",
    )(qt, k, vt)


def _merge_kernel(cb_ref, cc_ref, cu_ref, gate_ref, yb_ref, yc_ref, x_ref, cw_ref, wb_ref, wo_ref,
                  out_ref, zprev_ref, *, tiles_per_seq):
    tm = x_ref.shape[0]

    @pl.when(pl.program_id(0) % tiles_per_seq == 0)
    def _():
        zprev_ref[...] = jnp.zeros_like(zprev_ref)

    z = cc_ref[...].astype(F32) * cu_ref[...].astype(F32)
    row = lax.broadcasted_iota(jnp.int32, (tm, D_MODEL), 0)
    p1 = zprev_ref[7:8, :]
    p2 = zprev_ref[6:7, :]
    z1 = jnp.where(row == 0, p1, pltpu.roll(z, 1, 0))
    z2 = jnp.where(row == 0, p2, jnp.where(row == 1, p1, pltpu.roll(z, 2, 0)))
    zprev_ref[...] = z[tm - 8:, :]
    cw = cw_ref[...]
    y_a = cb_ref[...].astype(F32) * (cw[0:1] * z2 + cw[1:2] * z1 + cw[2:3] * z)

    g = gate_ref[...]
    merged = g[:, :D_MODEL].astype(F32) * _dot(y_a.astype(BF16), wb_ref[0])
    merged += g[:, D_MODEL:2 * D_MODEL].astype(F32) * _dot(yb_ref[...], wb_ref[1])
    merged += g[:, 2 * D_MODEL:].astype(F32) * _dot(yc_ref[...], wb_ref[2])
    out_ref[...] = x_ref[...] + _dot(merged.astype(BF16), wo_ref[...])


def _merge(main, y_b, y_c, x2, conv_w, w_branch, w_out, layer, seq, tm):
    t = x2.shape[0]
    kern = functools.partial(_merge_kernel, tiles_per_seq=seq // tm)
    col = lambda c: pl.BlockSpec((tm, D_MODEL), lambda i: (i, c))
    return pl.pallas_call(
        kern,
        grid=(t // tm,),
        in_specs=[col(0), col(1), col(2),
                  pl.BlockSpec((tm, N_BRANCH * D_MODEL), lambda i: (i, GATE_COL0 // (N_BRANCH * D_MODEL))),
                  col(0), col(0), col(0),
                  pl.BlockSpec((CONV_WIDTH, D_MODEL), lambda i: (0, 0)),
                  pl.BlockSpec((None, N_BRANCH, D_MODEL, D_MODEL), lambda i: (layer, 0, 0, 0),
                               pipeline_mode=pl.Buffered(1)),
                  pl.BlockSpec((None, D_MODEL, D_MODEL), lambda i: (layer, 0, 0), pipeline_mode=pl.Buffered(1))],
        out_specs=col(0),
        out_shape=jax.ShapeDtypeStruct((t, D_MODEL), F32),
        scratch_shapes=[pltpu.VMEM((8, D_MODEL), F32)],
        compiler_params=pltpu.CompilerParams(dimension_semantics=("arbitrary",),
                                             vmem_limit_bytes=VMEM_LIMIT),
        name="merge",
    )(main, main, main, main, y_b, y_c, x2, conv_w, w_branch, w_out)


def _mlp_kernel(x_ref, g_ref, wu_ref, wd_ref, out_ref, *, n_chunk):
    x = x_ref[...]
    h = _rms(x, g_ref[...]).astype(BF16)
    ck = D_FF // n_chunk
    acc = x
    for c in range(n_chunk):
        u = jnp.maximum(_dot(h, wu_ref[:, c * ck:(c + 1) * ck]), 0.0)
        acc = acc + _dot((u * u).astype(BF16), wd_ref[c * ck:(c + 1) * ck, :])
    out_ref[...] = acc


def _mlp(x2, g, w_up, w_down, layer, tm):
    t = x2.shape[0]
    kern = functools.partial(_mlp_kernel, n_chunk=4)
    return pl.pallas_call(
        kern,
        grid=(t // tm,),
        in_specs=[pl.BlockSpec((tm, D_MODEL), lambda i: (i, 0)),
                  pl.BlockSpec((1, D_MODEL), lambda i: (0, 0)),
                  pl.BlockSpec((None, D_MODEL, D_FF), lambda i: (layer, 0, 0), pipeline_mode=pl.Buffered(1)),
                  pl.BlockSpec((None, D_FF, D_MODEL), lambda i: (layer, 0, 0), pipeline_mode=pl.Buffered(1))],
        out_specs=pl.BlockSpec((tm, D_MODEL), lambda i: (i, 0)),
        out_shape=jax.ShapeDtypeStruct((t, D_MODEL), F32),
        compiler_params=pltpu.CompilerParams(dimension_semantics=("parallel",),
                                             vmem_limit_bytes=VMEM_LIMIT),
        name="mlp",
    )(x2, g, w_up, w_down)


def _regroup_w_in(w_in):
    i0 = COL_MO + HEADS * MLSTM_DV
    cq0 = i0 + 2 * HEADS
    g0 = cq0 + Q_LORA + KV_LORA + MLA_ROPE
    pad = jnp.zeros(w_in.shape[:2] + (TAIL_COLS - (Q_LORA + KV_LORA + MLA_ROPE + 2 * HEADS),), w_in.dtype)
    w_main = jnp.concatenate([w_in[..., :i0], w_in[..., g0:]], axis=-1).astype(BF16)
    w_tail = jnp.concatenate([w_in[..., cq0:g0], w_in[..., i0:cq0], pad], axis=-1).astype(BF16)
    return w_main, w_tail


def _heads_last_split(w, first, second):
    k = w.shape[0]
    w3 = w.reshape(k, HEADS, first + second)
    return jnp.concatenate([w3[:, :, :first].reshape(k, HEADS * first),
                            w3[:, :, first:].reshape(k, HEADS * second)], axis=1).astype(BF16)


def _score_bound(q_gain, k_gain):
    def sq_norm(g):
        g = g.astype(F32)
        return MLA_NOPE * jnp.max(g[:MLA_NOPE] ** 2) + MLA_ROPE * jnp.max(g[MLA_NOPE:] ** 2)
    return jnp.sqrt(sq_norm(q_gain) * sq_norm(k_gain)) * (Q_SCALE * BOUND_SLACK)


def _pick(n, candidates):
    for c in candidates:
        if n % c == 0:
            return c
    raise ValueError(f"no tile size for {n}")


def kernel(x, positions, mix_norm, w_in, conv_w, mlstm_igate_bias, mlstm_fgate_bias, mlstm_head_norm,
           mla_q_a_norm, mla_w_uq, mla_kv_a_norm, mla_w_ukv, mla_q_norm, mla_k_norm, w_branch, w_out,
           mlp_norm, w_up, w_down):
    bsz, seq, _ = x.shape
    depth = w_in.shape[0]
    t = bsz * seq
    assert seq % ATT_T == 0
    n_chunk = _pick(seq // CHUNK, tuple(g for g in (8, 4, 2, 1) if g <= MLSTM_GROUP))
    tm_proj = _pick(t, (1024, 512, 256, 128))
    tm = _pick(seq, (512, 256, 128))
    tm_mlp = _pick(t, (1024, 512, 256, 128))

    x2 = x.reshape(t, D_MODEL)
    cs, cst = _rope_table(positions, ATT_T)
    row = lambda a: a.reshape(1, -1).astype(F32)
    w_main, w_tail = _regroup_w_in(w_in)
    w_branch_b, w_out_b, w_up_b, w_down_b = (w.astype(BF16) for w in (w_branch, w_out, w_up, w_down))

    for l in range(depth):
        gate_bias = jnp.zeros((1, LANES), F32)
        gate_bias = gate_bias.at[0, IGATE_LANE:IGATE_LANE + HEADS].set(mlstm_igate_bias[l])
        gate_bias = gate_bias.at[0, FGATE_LANE:FGATE_LANE + HEADS].set(mlstm_fgate_bias[l])
        w_q = _heads_last_split(mla_w_uq[l], MLA_NOPE, MLA_ROPE)
        w_kv = _heads_last_split(mla_w_ukv[l], MLA_NOPE, MLA_V)

        main, tail = _in_proj(x2, row(mix_norm[l]), w_main, w_tail, l, tm_proj, IN_PROJ_TN)
        y_b = _mlstm(main, tail, gate_bias, row(mlstm_head_norm[l]), bsz, seq, n_chunk)
        qt, k, vt = _mla_prep(tail, cs, cst, row(mla_q_a_norm[l]), w_q.T, row(mla_kv_a_norm[l]),
                              w_kv[:, :HEADS * MLA_NOPE], w_kv[:, HEADS * MLA_NOPE:].T,
                              mla_q_norm[l].reshape(-1, 1).astype(F32) * Q_SCALE, row(mla_k_norm[l]), bsz, seq)
        y_c = lax.cond(_score_bound(mla_q_norm[l], mla_k_norm[l]) <= SCORE_BOUND,
                       _attention_bounded, _attention, qt, k, vt).reshape(t, D_MODEL)
        x2 = _merge(main, y_b, y_c, x2, conv_w[l].astype(F32), w_branch_b, w_out_b, l, seq, tm)
        x2 = _mlp(x2, row(mlp_norm[l]), w_up_b, w_down_b, l, tm_mlp)
    return x2.reshape(bsz, seq, D_MODEL)
```

```python
import functools

import jax
import jax.numpy as jnp
import numpy as np
from jax import lax
from jax.experimental import pallas as pl
from jax.experimental.pallas import tpu as pltpu

F32 = jnp.float32
BF16 = jnp.bfloat16

D_MODEL = 1024
N_BRANCH = 3
CONV_WIDTH = 3
HEADS = 8
MLSTM_DK = 64
MLSTM_DV = 128
CHUNK = 128
GATE_CAP = 15.0
MLA_NOPE = 128
MLA_ROPE = 64
MLA_V = 128
Q_LORA = 256
KV_LORA = 128
ROPE_BASE = 10000.0
D_FF = 4 * D_MODEL
NORM_EPS = 1e-6
QK_PAD = 256

MAIN_COLS = 9 * D_MODEL
TAIL_COLS = 512
COL_MQ = 3 * D_MODEL
COL_MK = COL_MQ + HEADS * MLSTM_DK
COL_MV = COL_MK + HEADS * MLSTM_DK
COL_MO = COL_MV + HEADS * MLSTM_DV
GATE_COL0 = 6 * D_MODEL
LANES = 128
TAIL_KROPE = Q_LORA + KV_LORA
IGATE_LANE = MLA_ROPE
FGATE_LANE = MLA_ROPE + HEADS

IN_PROJ_TN = 3072
IN_PROJ_CHUNK = 1536
MLSTM_GROUP = 8
ATT_T = 512
LOG2E = 1.4426950408889634
Q_SCALE = (MLA_NOPE + MLA_ROPE) ** -0.5 * LOG2E
MASK_VALUE = -1e30
SCORE_BOUND = 70.0
SCORE_SHIFT = 48.0
BOUND_SLACK = 1.02
ATT_UNROLL = 8
V_ROWS = MLA_V + 16

VMEM_LIMIT = 56 * 1024 * 1024


def _rms(x, g):
    return x * lax.rsqrt(jnp.mean(x * x, axis=-1, keepdims=True) + NORM_EPS) * g


def _dot(a, b):
    return jnp.dot(a, b, preferred_element_type=F32)


def _dot_nt(a, b):
    return lax.dot_general(a, b, (((1,), (1,)), ((), ())), preferred_element_type=F32)


def _rope_table_kernel(pos_ref, freq_ref, cs_ref, cst_ref):
    ang = freq_ref[...] * pos_ref[...].astype(F32)
    c, s = jnp.cos(ang), jnp.sin(ang)
    cst_ref[...] = jnp.concatenate([c, s], axis=0)
    cs_ref[...] = jnp.concatenate([c, c, -s, s], axis=0).T


def _rope_table(positions, tm):
    t = positions.size
    half = MLA_ROPE // 2
    inv_freq = (np.float32(ROPE_BASE) ** (-np.arange(0, MLA_ROPE, 2, dtype=np.float32) / np.float32(MLA_ROPE)))
    inv_freq = jnp.asarray(inv_freq.astype(np.float32))
    return pl.pallas_call(
        _rope_table_kernel,
        grid=(t // tm,),
        in_specs=[pl.BlockSpec((1, tm), lambda i: (0, i)),
                  pl.BlockSpec((half, 1), lambda i: (0, 0))],
        out_specs=[pl.BlockSpec((tm, 2 * MLA_ROPE), lambda i: (i, 0)),
                   pl.BlockSpec((MLA_ROPE, tm), lambda i: (0, i))],
        out_shape=[jax.ShapeDtypeStruct((t, 2 * MLA_ROPE), F32),
                   jax.ShapeDtypeStruct((MLA_ROPE, t), F32)],
        name="rope_table",
    )(positions.reshape(1, t), inv_freq.reshape(half, 1))


def _in_proj_kernel(x_ref, g_ref, w_ref, wt_ref, main_ref, tail_ref, xn_ref, *, n_gate0):
    j = pl.program_id(1)

    @pl.when(j == 0)
    def _():
        xn = _rms(x_ref[...], g_ref[...]).astype(BF16)
        xn_ref[...] = xn
        tail_ref[...] = _dot(xn, wt_ref[...])

    tn = w_ref.shape[1]
    cols = [slice(c, c + IN_PROJ_CHUNK) for c in range(0, tn, IN_PROJ_CHUNK)]

    @pl.when(j < n_gate0)
    def _():
        for c in cols:
            main_ref[:, c] = _dot(xn_ref[...], w_ref[:, c]).astype(BF16)

    @pl.when(j >= n_gate0)
    def _():
        for c in cols:
            main_ref[:, c] = jax.nn.sigmoid(_dot(xn_ref[...], w_ref[:, c])).astype(BF16)


def _in_proj(x2, g, w_main, w_tail, layer, tm, tn):
    t = x2.shape[0]
    assert GATE_COL0 % tn == 0 and MAIN_COLS % tn == 0
    kern = functools.partial(_in_proj_kernel, n_gate0=GATE_COL0 // tn)
    return pl.pallas_call(
        kern,
        grid=(t // tm, MAIN_COLS // tn),
        in_specs=[pl.BlockSpec((tm, D_MODEL), lambda i, j: (i, 0)),
                  pl.BlockSpec((1, D_MODEL), lambda i, j: (0, 0)),
                  pl.BlockSpec((None, D_MODEL, tn), lambda i, j: (layer, 0, j)),
                  pl.BlockSpec((None, D_MODEL, TAIL_COLS), lambda i, j: (layer, 0, 0))],
        out_specs=[pl.BlockSpec((tm, tn), lambda i, j: (i, j)),
                   pl.BlockSpec((tm, TAIL_COLS), lambda i, j: (i, 0))],
        out_shape=[jax.ShapeDtypeStruct((t, MAIN_COLS), BF16),
                   jax.ShapeDtypeStruct((t, TAIL_COLS), F32)],
        scratch_shapes=[pltpu.VMEM((tm, D_MODEL), BF16)],
        compiler_params=pltpu.CompilerParams(
            dimension_semantics=("parallel", "arbitrary"), vmem_limit_bytes=VMEM_LIMIT),
        name="in_proj",
    )(x2, g, w_main, w_tail)


def _time_scan(x, op, row):
    s = 1
    while s < CHUNK:
        x = jnp.where(row >= s, op(x, pltpu.roll(x, s, 0)), x)
        s *= 2
    return x


def _mlstm_kernel(q_ref, k_ref, v_ref, gate_ref, bias_ref, num_ref, dst_ref, c_ref, m_ref, *, n_chunk):
    L = CHUNK

    @pl.when(pl.program_id(1) == 0)
    def _():
        c_ref[...] = jnp.zeros_like(c_ref)
        m_ref[...] = jnp.zeros_like(m_ref)

    row = lax.broadcasted_iota(jnp.int32, (L, LANES), 0)
    lane = lax.broadcasted_iota(jnp.int32, (L, LANES), 1)
    gate_lanes = jnp.logical_and(lane >= IGATE_LANE, lane < FGATE_LANE + HEADS)
    tri = (lax.broadcasted_iota(jnp.int32, (L, L), 0) >= lax.broadcasted_iota(jnp.int32, (L, L), 1))
    lane_v = lax.broadcasted_iota(jnp.int32, (L, MLSTM_DV), 1)
    m_prev = m_ref[...]
    c_state = [c_ref[h] for h in range(HEADS)]

    for g in range(n_chunk):
        rows = slice(g * L, (g + 1) * L)
        pre = jnp.where(gate_lanes, gate_ref[rows, :] + bias_ref[...], 0.0)
        capped = GATE_CAP * jnp.tanh(pre * (1.0 / GATE_CAP))
        log_i = capped * LOG2E
        log_f = (jnp.minimum(capped, 0.0) - jnp.log1p(jnp.exp(-jnp.abs(capped)))) * LOG2E
        log_f = pltpu.roll(log_f, LANES - HEADS, 1)
        b = _time_scan(log_f, jnp.add, row)
        b_last = b[L - 1:L, :]
        rowd = log_i - b
        big_m = jnp.maximum(_time_scan(rowd, jnp.maximum, row), m_prev)
        e_neg_m = jnp.exp2(-(b + big_m))
        a = b_last + rowd
        m_loc = jnp.max(a, axis=0, keepdims=True)
        m_new = jnp.maximum(b_last + m_prev, m_loc)
        s_old = jnp.exp2(b_last + m_prev - m_new)
        w = jnp.exp2(a - m_new)
        m_prev_g, m_prev = m_prev, m_new
        rowd_t = rowd.T
        w_t = w.T
        q_all = q_ref[rows, :] * (MLSTM_DK ** -0.5)
        k_t = k_ref[rows, :].astype(F32).T

        hk = [slice(h * MLSTM_DK, (h + 1) * MLSTM_DK) for h in range(HEADS)]
        hv = [slice(h * MLSTM_DV, (h + 1) * MLSTM_DV) for h in range(HEADS)]
        gl = [slice(IGATE_LANE + h, IGATE_LANE + h + 1) for h in range(HEADS)]
        v_aug = [jnp.concatenate([v_ref[rows, hv[h]], (lane_v == IGATE_LANE + h).astype(BF16)], axis=1)
                 for h in range(HEADS)]
        scores = [_dot(q_all[:, hk[h]], k_t[hk[h], :].astype(BF16)) for h in range(HEADS)]
        c_loc = [_dot((k_t[hk[h], :] * w_t[gl[h], :]).astype(BF16), v_aug[h]) for h in range(HEADS)]
        big_m_b = [jnp.broadcast_to(big_m[:, gl[h]], (L, L)) for h in range(HEADS)]
        dmat = [jnp.where(tri, jnp.exp2(rowd_t[gl[h], :] - big_m_b[h]), 0.0) for h in range(HEADS)]
        q_state = [(q_all[:, hk[h]].astype(F32)
                    * jnp.exp2(m_prev_g[:, gl[h]] - big_m_b[h][:, :MLSTM_DK])).astype(BF16) for h in range(HEADS)]
        num_aug = []
        for h in range(HEADS):
            lhs = jnp.concatenate([(scores[h] * dmat[h]).astype(BF16), q_state[h],
                                   jnp.zeros((L, L - MLSTM_DK), BF16)], axis=1)
            rhs = jnp.concatenate([v_aug[h], c_state[h].astype(BF16),
                                   jnp.zeros((L - MLSTM_DK, 2 * MLSTM_DV), BF16)], axis=0)
            num_aug.append(_dot(lhs, rhs))
        for h in range(HEADS):
            num_ref[rows, hv[h]] = num_aug[h][:, :MLSTM_DV].astype(num_ref.dtype)
            c_state[h] = s_old[:, gl[h]] * c_state[h] + c_loc[h]
        den = functools.reduce(jnp.add, [n[:, MLSTM_DV:] for n in num_aug])
        dst_ref[rows, :] = jnp.maximum(jnp.abs(den), e_neg_m)

    m_ref[...] = m_prev
    for h in range(HEADS):
        c_ref[h] = c_state[h]


def _mlstm(main, tail, gate_bias, bsz, seq, n_chunk):
    t = main.shape[0]
    rows = n_chunk * CHUNK
    nc = seq // rows
    row = lambda b, c: b * nc + c
    col_block = lambda col0, width: pl.BlockSpec((rows, width), lambda b, c: (row(b, c), col0 // width))
    return pl.pallas_call(
        functools.partial(_mlstm_kernel, n_chunk=n_chunk),
        grid=(bsz, nc),
        in_specs=[col_block(COL_MQ, HEADS * MLSTM_DK), col_block(COL_MK, HEADS * MLSTM_DK),
                  col_block(COL_MV, HEADS * MLSTM_DV),
                  col_block(TAIL_KROPE, LANES),
                  pl.BlockSpec((1, LANES), lambda b, c: (0, 0))],
        out_specs=[pl.BlockSpec((rows, D_MODEL), lambda b, c: (row(b, c), 0)),
                   pl.BlockSpec((rows, LANES), lambda b, c: (row(b, c), 0))],
        out_shape=[jax.ShapeDtypeStruct((t, D_MODEL), BF16), jax.ShapeDtypeStruct((t, LANES), F32)],
        scratch_shapes=[pltpu.VMEM((HEADS, MLSTM_DK, 2 * MLSTM_DV), F32),
                        pltpu.VMEM((1, LANES), F32)],
        compiler_params=pltpu.CompilerParams(dimension_semantics=("parallel", "arbitrary")),
        name="mlstm",
    )(main, main, main, tail, gate_bias)


def _rms0(x, g_col):
    return x * lax.rsqrt(jnp.mean(x * x, axis=0, keepdims=True) + NORM_EPS) * g_col


def _rope(x, cs):
    half = MLA_ROPE // 2
    rot = jnp.concatenate([x[:, half:], x[:, :half]], axis=-1)
    return x * cs[:, :MLA_ROPE] + rot * cs[:, MLA_ROPE:]


def _mla_prep_kernel(tail_ref, cs_ref, cst_ref, qa_ref, wqt_ref, kva_ref, wk_ref, wvt_ref, qn_ref, kn_ref,
                     qt_out, k_out, vt_out):
    tm = ATT_T
    half = MLA_ROPE // 2
    qn_g, qp_g = qn_ref[:MLA_NOPE, :], qn_ref[MLA_NOPE:, :]
    kn_g, kp_g = kn_ref[:, :MLA_NOPE], kn_ref[:, MLA_NOPE:]
    pe0 = HEADS * MLA_NOPE
    ones_rows = (lax.broadcasted_iota(jnp.int32, (V_ROWS - MLA_V, tm), 0) == 0).astype(BF16)
    pad = QK_PAD - MLA_NOPE - MLA_ROPE
    q_pad = jnp.where(lax.broadcasted_iota(jnp.int32, (pad, tm), 0) == 0, -SCORE_SHIFT, 0.0).astype(BF16)
    k_pad = (lax.broadcasted_iota(jnp.int32, (tm, pad), 1) == 0).astype(BF16)
    for s in range(tail_ref.shape[0] // tm):
        rows = slice(s * tm, (s + 1) * tm)
        cq = _rms(tail_ref[rows, :Q_LORA], qa_ref[...]).astype(BF16)
        ckv = _rms(tail_ref[rows, Q_LORA:Q_LORA + KV_LORA], kva_ref[...]).astype(BF16)
        qft = _dot_nt(wqt_ref[...], cq)
        kf = _dot(ckv, wk_ref[...])
        vft = _dot_nt(wvt_ref[...], ckv)
        cos_t, sin_t = cst_ref[:half, rows], cst_ref[half:, rows]
        k_pe = _rope(_rms(tail_ref[rows, TAIL_KROPE:TAIL_KROPE + MLA_ROPE], kp_g), cs_ref[rows, :]).astype(BF16)
        for h in range(HEADS):
            q_nope = _rms0(qft[h * MLA_NOPE:(h + 1) * MLA_NOPE], qn_g)
            q_pe = _rms0(qft[pe0 + h * MLA_ROPE:pe0 + (h + 1) * MLA_ROPE], qp_g)
            x1, x2 = q_pe[:half], q_pe[half:]
            q_rot = jnp.concatenate([x1 * cos_t - x2 * sin_t, x1 * sin_t + x2 * cos_t], axis=0)
            qt_out[0, h, s] = jnp.concatenate([q_nope.astype(BF16), q_rot.astype(BF16), q_pad], axis=0)
            k_nope = _rms(kf[:, h * MLA_NOPE:(h + 1) * MLA_NOPE], kn_g)
            k_out[0, h, rows, :] = jnp.concatenate([k_nope.astype(BF16), k_pe, k_pad], axis=-1)
            vt_out[0, h, s] = jnp.concatenate([vft[h * MLA_V:(h + 1) * MLA_V].astype(BF16), ones_rows], axis=0)


def _mla_prep(tail, cs, cst, qa, wqt, kva, wk, wvt, qn_col, kn, bsz, seq):
    tiles = 2 if (seq // ATT_T) % 2 == 0 else 1
    tm = tiles * ATT_T
    nt = seq // tm
    full = lambda shape: pl.BlockSpec(shape, lambda i: (0,) * len(shape))
    blocked = lambda rows: pl.BlockSpec((1, HEADS, tiles, rows, ATT_T), lambda i: (i // nt, 0, i % nt, 0, 0))
    return pl.pallas_call(
        _mla_prep_kernel,
        grid=(bsz * nt,),
        in_specs=[pl.BlockSpec((tm, TAIL_COLS), lambda i: (i, 0)),
                  pl.BlockSpec((tm, 2 * MLA_ROPE), lambda i: (i, 0)),
                  pl.BlockSpec((MLA_ROPE, tm), lambda i: (0, i)),
                  full((1, Q_LORA)), full(wqt.shape), full((1, KV_LORA)), full(wk.shape), full(wvt.shape),
                  full((MLA_NOPE + MLA_ROPE, 1)), full((1, MLA_NOPE + MLA_ROPE))],
        out_specs=[blocked(QK_PAD),
                   pl.BlockSpec((1, HEADS, tm, QK_PAD), lambda i: (i // nt, 0, i % nt, 0)),
                   blocked(V_ROWS)],
        out_shape=[jax.ShapeDtypeStruct((bsz, HEADS, seq // ATT_T, QK_PAD, ATT_T), BF16),
                   jax.ShapeDtypeStruct((bsz, HEADS, seq, QK_PAD), BF16),
                   jax.ShapeDtypeStruct((bsz, HEADS, seq // ATT_T, V_ROWS, ATT_T), BF16)],
        compiler_params=pltpu.CompilerParams(dimension_semantics=("parallel",),
                                             vmem_limit_bytes=VMEM_LIMIT),
        name="mla_prep",
    )(tail, cs, cst, qa, wqt, kva, wk, wvt, qn_col, kn)


def _attn_kernel(qt_ref, k_ref, vt_ref, o_ref, s_scr, p_scr, a_scr, m_all, acc_all, *, nq):
    t = ATT_T

    def qk(qi, j):
        k0 = pl.multiple_of(j * t, t)
        return _dot(k_ref[0, 0, pl.ds(k0, t), :], qt_ref[0, 0, qi])

    def accumulate(tile, alpha, p):
        q, j = tile
        acc_all[q] = alpha * acc_all[q] + _dot(vt_ref[0, 0, j], p)

    def stage(prev, cur, nxt, s_cur, p_prev, alpha_prev, masked):
        accumulate(prev, alpha_prev, p_prev)
        qi = cur[0]
        if masked:
            keep = lax.broadcasted_iota(jnp.int32, (t, t), 0) <= lax.broadcasted_iota(jnp.int32, (t, t), 1)
            s_cur = jnp.where(keep, s_cur, MASK_VALUE)
        m_old = m_all[qi]
        m_new = jnp.maximum(m_old, jnp.max(s_cur, axis=0, keepdims=True))
        m_all[qi] = m_new
        alpha = jnp.exp2(m_old - m_new)
        p = jnp.exp2((s_cur - m_new).astype(BF16))
        s_next = qk(jnp.minimum(nxt[0], nq - 1), jnp.minimum(nxt[1], nq - 1))
        return s_next, p, alpha

    def advance_lower(qi, j):
        wrap = j + 1 == qi
        return jnp.where(wrap, qi + 1, qi), jnp.where(wrap, 0, j + 1)

    def advance_diag(qi, j):
        return qi + 1, j + 1

    def run_phase(n_steps, first, advance, masked):
        if n_steps == 0:
            return
        s_scr[...] = qk(*first)
        p_scr[...] = jnp.zeros_like(p_scr)
        a_scr[...] = jnp.ones_like(a_scr)

        def steps(count, prev, cur):
            s_cur, p_prev, alpha_prev = s_scr[...], p_scr[...], a_scr[...]
            for _ in range(count):
                nxt = advance(*cur)
                s_cur, p_prev, alpha_prev = stage(prev, cur, nxt, s_cur, p_prev, alpha_prev, masked)
                prev, cur = cur, nxt
            s_scr[...] = s_cur
            p_scr[...] = p_prev
            a_scr[...] = alpha_prev
            return prev, cur

        first = (jnp.int32(first[0]), jnp.int32(first[1]))
        prev, cur = lax.fori_loop(0, n_steps // ATT_UNROLL, lambda _, c: steps(ATT_UNROLL, *c), (first, first))
        if n_steps % ATT_UNROLL:
            prev, cur = steps(n_steps % ATT_UNROLL, prev, cur)
        accumulate(prev, a_scr[...], p_scr[...])

    m_all[...] = jnp.full_like(m_all, MASK_VALUE)
    acc_all[...] = jnp.zeros_like(acc_all)
    run_phase(nq * (nq - 1) // 2, (1, 0), advance_lower, False)
    run_phase(nq, (0, 0), advance_diag, True)
    for qi in range(nq):
        acc = acc_all[qi]
        o_ref[0, qi * t:(qi + 1) * t, :] = (acc[:MLA_V] / acc[MLA_V:MLA_V + 1]).T.astype(o_ref.dtype)


def _attn_bounded_kernel(qt_ref, k_ref, vt_ref, o_ref, s_scr, p_scr, acc_all, *, nq):
    t = ATT_T

    def qk(qi, j):
        k0 = pl.multiple_of(j * t, t)
        return _dot(k_ref[0, 0, pl.ds(k0, t), :], qt_ref[0, 0, qi])

    def accumulate(tile, p):
        q, j = tile
        acc_all[q] += _dot(vt_ref[0, 0, j], p)

    def stage(prev, cur, nxt, s_cur, p_prev, masked):
        accumulate(prev, p_prev)
        p = jnp.exp2(s_cur)
        if masked:
            keep = lax.broadcasted_iota(jnp.int32, (t, t), 0) <= lax.broadcasted_iota(jnp.int32, (t, t), 1)
            p = jnp.where(keep, p, 0.0)
        s_next = qk(jnp.minimum(nxt[0], nq - 1), jnp.minimum(nxt[1], nq - 1))
        return s_next, p.astype(BF16)

    def advance_lower(qi, j):
        wrap = j + 1 == qi
        return jnp.where(wrap, qi + 1, qi), jnp.where(wrap, 0, j + 1)

    def advance_diag(qi, j):
        return qi + 1, j + 1

    def run_phase(n_steps, first, advance, masked):
        if n_steps == 0:
            return
        s_scr[...] = qk(*first)
        p_scr[...] = jnp.zeros_like(p_scr)

        def steps(count, prev, cur):
            s_cur, p_prev = s_scr[...], p_scr[...]
            for _ in range(count):
                nxt = advance(*cur)
                s_cur, p_prev = stage(prev, cur, nxt, s_cur, p_prev, masked)
                prev, cur = cur, nxt
            s_scr[...] = s_cur
            p_scr[...] = p_prev
            return prev, cur

        first = (jnp.int32(first[0]), jnp.int32(first[1]))
        prev, cur = lax.fori_loop(0, n_steps // ATT_UNROLL, lambda _, c: steps(ATT_UNROLL, *c), (first, first))
        if n_steps % ATT_UNROLL:
            prev, cur = steps(n_steps % ATT_UNROLL, prev, cur)
        accumulate(prev, p_scr[...])

    acc_all[...] = jnp.zeros_like(acc_all)
    run_phase(nq * (nq - 1) // 2, (1, 0), advance_lower, False)
    run_phase(nq, (0, 0), advance_diag, True)
    for qi in range(nq):
        acc = acc_all[qi]
        o_ref[0, qi * t:(qi + 1) * t, :] = (acc[:MLA_V] / acc[MLA_V:MLA_V + 1]).T.astype(o_ref.dtype)


def _attention_bounded(qt, k, vt):
    bsz, _, seq, _ = k.shape
    nq = seq // ATT_T
    kern = functools.partial(_attn_bounded_kernel, nq=nq)
    return pl.pallas_call(
        kern,
        grid=(bsz, HEADS),
        in_specs=[pl.BlockSpec((1, 1, nq, QK_PAD, ATT_T), lambda b, h: (b, h, 0, 0, 0)),
                  pl.BlockSpec((1, 1, seq, QK_PAD), lambda b, h: (b, h, 0, 0)),
                  pl.BlockSpec((1, 1, nq, V_ROWS, ATT_T), lambda b, h: (b, h, 0, 0, 0))],
        out_specs=pl.BlockSpec((1, seq, MLA_V), lambda b, h: (b, 0, h)),
        out_shape=jax.ShapeDtypeStruct((bsz, seq, HEADS * MLA_V), BF16),
        scratch_shapes=[pltpu.VMEM((ATT_T, ATT_T), F32), pltpu.VMEM((ATT_T, ATT_T), BF16),
                        pltpu.VMEM((nq, V_ROWS, ATT_T), F32)],
        compiler_params=pltpu.CompilerParams(dimension_semantics=("parallel", "parallel"),
                                             vmem_limit_bytes=VMEM_LIMIT),
        name="attn_bounded",
    )(qt, k, vt)


def _attention(qt, k, vt):
    bsz, _, seq, _ = k.shape
    nq = seq // ATT_T
    kern = functools.partial(_attn_kernel, nq=nq)
    return pl.pallas_call(
        kern,
        grid=(bsz, HEADS),
        in_specs=[pl.BlockSpec((1, 1, nq, QK_PAD, ATT_T), lambda b, h: (b, h, 0, 0, 0)),
                  pl.BlockSpec((1, 1, seq, QK_PAD), lambda b, h: (b, h, 0, 0)),
                  pl.BlockSpec((1, 1, nq, V_ROWS, ATT_T), lambda b, h: (b, h, 0, 0, 0))],
        out_specs=pl.BlockSpec((1, seq, MLA_V), lambda b, h: (b, 0, h)),
        out_shape=jax.ShapeDtypeStruct((bsz, seq, HEADS * MLA_V), BF16),
        scratch_shapes=[pltpu.VMEM((ATT_T, ATT_T), F32), pltpu.VMEM((ATT_T, ATT_T), BF16),
                        pltpu.VMEM((1, ATT_T), F32), pltpu.VMEM((nq, 1, ATT_T), F32),
                        pltpu.VMEM((nq, V_ROWS, ATT_T), F32)],
        compiler_params=pltpu.CompilerParams(dimension_semantics=("parallel", "parallel"),
                                             vmem_limit_bytes=VMEM_LIMIT),
        name="attn",
    )(qt, k, vt)


def _mlstm_head_out(num_ref, dst_ref, o_ref, hn_ref):
    out_gain = (0.5 * jnp.tanh(0.5 * o_ref[...].astype(F32)) + 0.5) * hn_ref[...]
    heads = []
    for h in range(HEADS):
        hv = slice(h * MLSTM_DV, (h + 1) * MLSTM_DV)
        num = num_ref[:, hv].astype(F32)
        d = dst_ref[:, IGATE_LANE + h:IGATE_LANE + h + 1]
        scale = lax.rsqrt(jnp.mean(num * num, axis=-1, keepdims=True) + NORM_EPS * d * d)
        heads.append((num * scale * out_gain[:, hv]).astype(BF16))
    return jnp.concatenate(heads, axis=1)


def _merge_kernel(cb_ref, cc_ref, cu_ref, gate_ref, num_ref, dst_ref, o_ref, hn_ref, yc_ref, x_ref, cw_ref,
                  wb_ref, wo_ref, out_ref, zprev_ref, *, tiles_per_seq):
    tm = x_ref.shape[0]

    @pl.when(pl.program_id(0) % tiles_per_seq == 0)
    def _():
        zprev_ref[...] = jnp.zeros_like(zprev_ref)

    z = cc_ref[...].astype(F32) * cu_ref[...].astype(F32)
    row = lax.broadcasted_iota(jnp.int32, (tm, D_MODEL), 0)
    p1 = zprev_ref[7:8, :]
    p2 = zprev_ref[6:7, :]
    z1 = jnp.where(row == 0, p1, pltpu.roll(z, 1, 0))
    z2 = jnp.where(row == 0, p2, jnp.where(row == 1, p1, pltpu.roll(z, 2, 0)))
    zprev_ref[...] = z[tm - 8:, :]
    cw = cw_ref[...]
    y_a = cb_ref[...].astype(F32) * (cw[0:1] * z2 + cw[1:2] * z1 + cw[2:3] * z)

    g = gate_ref[...]
    merged = g[:, :D_MODEL].astype(F32) * _dot(y_a.astype(BF16), wb_ref[0])
    y_b = _mlstm_head_out(num_ref, dst_ref, o_ref, hn_ref)
    merged += g[:, D_MODEL:2 * D_MODEL].astype(F32) * _dot(y_b, wb_ref[1])
    merged += g[:, 2 * D_MODEL:].astype(F32) * _dot(yc_ref[...], wb_ref[2])
    out_ref[...] = x_ref[...] + _dot(merged.astype(BF16), wo_ref[...])


def _merge(main, num, dstat, head_norm, y_c, x2, conv_w, w_branch, w_out, layer, seq, tm):
    t = x2.shape[0]
    kern = functools.partial(_merge_kernel, tiles_per_seq=seq // tm)
    col = lambda c: pl.BlockSpec((tm, D_MODEL), lambda i: (i, c))
    return pl.pallas_call(
        kern,
        grid=(t // tm,),
        in_specs=[col(0), col(1), col(2),
                  pl.BlockSpec((tm, N_BRANCH * D_MODEL), lambda i: (i, GATE_COL0 // (N_BRANCH * D_MODEL))),
                  col(0), pl.BlockSpec((tm, LANES), lambda i: (i, 0)), col(COL_MO // D_MODEL),
                  pl.BlockSpec((1, D_MODEL), lambda i: (0, 0)),
                  col(0), col(0),
                  pl.BlockSpec((CONV_WIDTH, D_MODEL), lambda i: (0, 0)),
                  pl.BlockSpec((None, N_BRANCH, D_MODEL, D_MODEL), lambda i: (layer, 0, 0, 0),
                               pipeline_mode=pl.Buffered(1)),
                  pl.BlockSpec((None, D_MODEL, D_MODEL), lambda i: (layer, 0, 0), pipeline_mode=pl.Buffered(1))],
        out_specs=col(0),
        out_shape=jax.ShapeDtypeStruct((t, D_MODEL), F32),
        scratch_shapes=[pltpu.VMEM((8, D_MODEL), F32)],
        compiler_params=pltpu.CompilerParams(dimension_semantics=("arbitrary",),
                                             vmem_limit_bytes=VMEM_LIMIT),
        name="merge",
    )(main, main, main, main, num, dstat, main, head_norm, y_c, x2, conv_w, w_branch, w_out)


def _mlp_kernel(x_ref, g_ref, wu_ref, wd_ref, out_ref, *, n_chunk):
    x = x_ref[...]
    h = _rms(x, g_ref[...]).astype(BF16)
    ck = D_FF // n_chunk
    acc = x
    for c in range(n_chunk):
        u = jnp.maximum(_dot(h, wu_ref[:, c * ck:(c + 1) * ck]), 0.0)
        acc = acc + _dot((u * u).astype(BF16), wd_ref[c * ck:(c + 1) * ck, :])
    out_ref[...] = acc


def _mlp(x2, g, w_up, w_down, layer, tm):
    t = x2.shape[0]
    kern = functools.partial(_mlp_kernel, n_chunk=4)
    return pl.pallas_call(
        kern,
        grid=(t // tm,),
        in_specs=[pl.BlockSpec((tm, D_MODEL), lambda i: (i, 0)),
                  pl.BlockSpec((1, D_MODEL), lambda i: (0, 0)),
                  pl.BlockSpec((None, D_MODEL, D_FF), lambda i: (layer, 0, 0), pipeline_mode=pl.Buffered(1)),
                  pl.BlockSpec((None, D_FF, D_MODEL), lambda i: (layer, 0, 0), pipeline_mode=pl.Buffered(1))],
        out_specs=pl.BlockSpec((tm, D_MODEL), lambda i: (i, 0)),
        out_shape=jax.ShapeDtypeStruct((t, D_MODEL), F32),
        compiler_params=pltpu.CompilerParams(dimension_semantics=("parallel",),
                                             vmem_limit_bytes=VMEM_LIMIT),
        name="mlp",
    )(x2, g, w_up, w_down)


def _regroup_w_in(w_in):
    i0 = COL_MO + HEADS * MLSTM_DV
    cq0 = i0 + 2 * HEADS
    g0 = cq0 + Q_LORA + KV_LORA + MLA_ROPE
    pad = jnp.zeros(w_in.shape[:2] + (TAIL_COLS - (Q_LORA + KV_LORA + MLA_ROPE + 2 * HEADS),), w_in.dtype)
    w_main = jnp.concatenate([w_in[..., :i0], w_in[..., g0:]], axis=-1).astype(BF16)
    w_tail = jnp.concatenate([w_in[..., cq0:g0], w_in[..., i0:cq0], pad], axis=-1).astype(BF16)
    return w_main, w_tail


def _heads_last_split(w, first, second):
    k = w.shape[0]
    w3 = w.reshape(k, HEADS, first + second)
    return jnp.concatenate([w3[:, :, :first].reshape(k, HEADS * first),
                            w3[:, :, first:].reshape(k, HEADS * second)], axis=1).astype(BF16)


def _score_bound(q_gain, k_gain):
    def sq_norm(g):
        g = g.astype(F32)
        return MLA_NOPE * jnp.max(g[:MLA_NOPE] ** 2) + MLA_ROPE * jnp.max(g[MLA_NOPE:] ** 2)
    return jnp.sqrt(sq_norm(q_gain) * sq_norm(k_gain)) * (Q_SCALE * BOUND_SLACK)


def _pick(n, candidates):
    for c in candidates:
        if n % c == 0:
            return c
    raise ValueError(f"no tile size for {n}")


def kernel(x, positions, mix_norm, w_in, conv_w, mlstm_igate_bias, mlstm_fgate_bias, mlstm_head_norm,
           mla_q_a_norm, mla_w_uq, mla_kv_a_norm, mla_w_ukv, mla_q_norm, mla_k_norm, w_branch, w_out,
           mlp_norm, w_up, w_down):
    bsz, seq, _ = x.shape
    depth = w_in.shape[0]
    t = bsz * seq
    assert seq % ATT_T == 0
    n_chunk = _pick(seq // CHUNK, tuple(g for g in (8, 4, 2, 1) if g <= MLSTM_GROUP))
    tm_proj = _pick(t, (1024, 512, 256, 128))
    tm = _pick(seq, (512, 256, 128))
    tm_mlp = _pick(t, (1024, 512, 256, 128))

    x2 = x.reshape(t, D_MODEL)
    cs, cst = _rope_table(positions, ATT_T)
    row = lambda a: a.reshape(1, -1).astype(F32)
    w_main, w_tail = _regroup_w_in(w_in)
    w_branch_b, w_out_b, w_up_b, w_down_b = (w.astype(BF16) for w in (w_branch, w_out, w_up, w_down))

    for l in range(depth):
        gate_bias = jnp.zeros((1, LANES), F32)
        gate_bias = gate_bias.at[0, IGATE_LANE:IGATE_LANE + HEADS].set(mlstm_igate_bias[l])
        gate_bias = gate_bias.at[0, FGATE_LANE:FGATE_LANE + HEADS].set(mlstm_fgate_bias[l])
        w_q = _heads_last_split(mla_w_uq[l], MLA_NOPE, MLA_ROPE)
        w_kv = _heads_last_split(mla_w_ukv[l], MLA_NOPE, MLA_V)

        main, tail = _in_proj(x2, row(mix_norm[l]), w_main, w_tail, l, tm_proj, IN_PROJ_TN)
        num, dstat = _mlstm(main, tail, gate_bias, bsz, seq, n_chunk)
        qt, k, vt = _mla_prep(tail, cs, cst, row(mla_q_a_norm[l]), w_q.T, row(mla_kv_a_norm[l]),
                              w_kv[:, :HEADS * MLA_NOPE], w_kv[:, HEADS * MLA_NOPE:].T,
                              mla_q_norm[l].reshape(-1, 1).astype(F32) * Q_SCALE, row(mla_k_norm[l]), bsz, seq)
        y_c = lax.cond(_score_bound(mla_q_norm[l], mla_k_norm[l]) <= SCORE_BOUND,
                       _attention_bounded, _attention, qt, k, vt).reshape(t, D_MODEL)
        x2 = _merge(main, num, dstat, row(mlstm_head_norm[l]), y_c, x2, conv_w[l].astype(F32), w_branch_b, w_out_b,
                    l, seq, tm)
        x2 = _mlp(x2, row(mlp_norm[l]), w_up_b, w_down_b, l, tm_mlp)
    return x2.reshape(bsz, seq, D_MODEL)
```

```python
import functools

import jax
import jax.numpy as jnp
import numpy as np
from jax import lax
from jax.experimental import pallas as pl
from jax.experimental.pallas import tpu as pltpu

F32 = jnp.float32
BF16 = jnp.bfloat16

D_MODEL = 1024
N_BRANCH = 3
CONV_WIDTH = 3
HEADS = 8
MLSTM_DK = 64
MLSTM_DV = 128
CHUNK = 128
GATE_CAP = 15.0
MLA_NOPE = 128
MLA_ROPE = 64
MLA_V = 128
Q_LORA = 256
KV_LORA = 128
ROPE_BASE = 10000.0
D_FF = 4 * D_MODEL
NORM_EPS = 1e-6
QK_PAD = 256

CONV_COLS = 3 * D_MODEL
PROJ_COLS = 9 * D_MODEL
MAIN_COLS = PROJ_COLS - CONV_COLS
TAIL_COLS = 512
COL_MQ = 0
COL_MK = COL_MQ + HEADS * MLSTM_DK
COL_MV = COL_MK + HEADS * MLSTM_DK
COL_MO = COL_MV + HEADS * MLSTM_DV
GATE_COL0 = COL_MO + HEADS * MLSTM_DV
LANES = 128
TAIL_KROPE = Q_LORA + KV_LORA
IGATE_LANE = MLA_ROPE
FGATE_LANE = MLA_ROPE + HEADS

IN_PROJ_TN = 3 * D_MODEL
IN_PROJ_CHUNK = 1536
CONV_CHUNK = 256
MLSTM_GROUP = 8
ATT_T = 512
LOG2E = 1.4426950408889634
Q_SCALE = (MLA_NOPE + MLA_ROPE) ** -0.5 * LOG2E
MASK_VALUE = -1e30
SCORE_BOUND = 70.0
SCORE_SHIFT = 48.0
BOUND_SLACK = 1.02
ATT_UNROLL = 8
V_ROWS = MLA_V + 16

VMEM_LIMIT = 56 * 1024 * 1024


def _rms(x, g):
    return x * lax.rsqrt(jnp.mean(x * x, axis=-1, keepdims=True) + NORM_EPS) * g


def _dot(a, b):
    return jnp.dot(a, b, preferred_element_type=F32)


def _dot_nt(a, b):
    return lax.dot_general(a, b, (((1,), (1,)), ((), ())), preferred_element_type=F32)


def _rope_table_kernel(pos_ref, freq_ref, cs_ref, cst_ref):
    ang = freq_ref[...] * pos_ref[...].astype(F32)
    c, s = jnp.cos(ang), jnp.sin(ang)
    cst_ref[...] = jnp.concatenate([c, s], axis=0)
    cs_ref[...] = jnp.concatenate([c, c, -s, s], axis=0).T


def _rope_table(positions, tm):
    t = positions.size
    half = MLA_ROPE // 2
    inv_freq = (np.float32(ROPE_BASE) ** (-np.arange(0, MLA_ROPE, 2, dtype=np.float32) / np.float32(MLA_ROPE)))
    inv_freq = jnp.asarray(inv_freq.astype(np.float32))
    return pl.pallas_call(
        _rope_table_kernel,
        grid=(t // tm,),
        in_specs=[pl.BlockSpec((1, tm), lambda i: (0, i)),
                  pl.BlockSpec((half, 1), lambda i: (0, 0))],
        out_specs=[pl.BlockSpec((tm, 2 * MLA_ROPE), lambda i: (i, 0)),
                   pl.BlockSpec((MLA_ROPE, tm), lambda i: (0, i))],
        out_shape=[jax.ShapeDtypeStruct((t, 2 * MLA_ROPE), F32),
                   jax.ShapeDtypeStruct((MLA_ROPE, t), F32)],
        name="rope_table",
    )(positions.reshape(1, t), inv_freq.reshape(half, 1))


def _in_proj_kernel(x_ref, g_ref, w_ref, wt_ref, cw_ref, ya_ref, main_ref, tail_ref, xn_ref, zprev_ref, *,
                    tiles_per_seq):
    i, j = pl.program_id(0), pl.program_id(1)
    tm = x_ref.shape[0]

    @pl.when(jnp.logical_and(j == 0, i % tiles_per_seq == 0))
    def _():
        zprev_ref[...] = jnp.zeros_like(zprev_ref)

    @pl.when(j == 0)
    def _():
        xn = _rms(x_ref[...], g_ref[...]).astype(BF16)
        xn_ref[...] = xn
        tail_ref[...] = _dot(xn, wt_ref[...])
        row = lax.broadcasted_iota(jnp.int32, (tm, CONV_CHUNK), 0)
        for c0 in range(0, D_MODEL, CONV_CHUNK):
            c = slice(c0, c0 + CONV_CHUNK)
            gate_b, gate_c, u = (_dot(xn, w_ref[:, n * D_MODEL + c0:n * D_MODEL + c0 + CONV_CHUNK]) for n in range(3))
            z = gate_c * u
            p1, p2 = zprev_ref[7:8, c], zprev_ref[6:7, c]
            z1 = jnp.where(row == 0, p1, pltpu.roll(z, 1, 0))
            z2 = jnp.where(row == 0, p2, jnp.where(row == 1, p1, pltpu.roll(z, 2, 0)))
            zprev_ref[:, c] = z[tm - 8:, :]
            ya_ref[:, c] = (gate_b * (cw_ref[0:1, c] * z2 + cw_ref[1:2, c] * z1 + cw_ref[2:3, c] * z)).astype(BF16)

    cols = [slice(c, c + IN_PROJ_CHUNK) for c in range(0, w_ref.shape[1], IN_PROJ_CHUNK)]

    @pl.when(j == 1)
    def _():
        for c in cols:
            main_ref[:, c] = _dot(xn_ref[...], w_ref[:, c]).astype(BF16)

    @pl.when(j == 2)
    def _():
        for c in cols:
            main_ref[:, c] = jax.nn.sigmoid(_dot(xn_ref[...], w_ref[:, c])).astype(BF16)


def _in_proj(x2, g, w_main, w_tail, conv_w, layer, seq, tm):
    t = x2.shape[0]
    tn = IN_PROJ_TN
    assert CONV_COLS == tn and GATE_COL0 == tn and MAIN_COLS == 2 * tn
    kern = functools.partial(_in_proj_kernel, tiles_per_seq=seq // tm)
    return pl.pallas_call(
        kern,
        grid=(t // tm, PROJ_COLS // tn),
        in_specs=[pl.BlockSpec((tm, D_MODEL), lambda i, j: (i, 0)),
                  pl.BlockSpec((1, D_MODEL), lambda i, j: (0, 0)),
                  pl.BlockSpec((None, D_MODEL, tn), lambda i, j: (layer, 0, j)),
                  pl.BlockSpec((None, D_MODEL, TAIL_COLS), lambda i, j: (layer, 0, 0)),
                  pl.BlockSpec((CONV_WIDTH, D_MODEL), lambda i, j: (0, 0))],
        out_specs=[pl.BlockSpec((tm, D_MODEL), lambda i, j: (i, 0)),
                   pl.BlockSpec((tm, tn), lambda i, j: (i, jnp.maximum(j - 1, 0))),
                   pl.BlockSpec((tm, TAIL_COLS), lambda i, j: (i, 0))],
        out_shape=[jax.ShapeDtypeStruct((t, D_MODEL), BF16),
                   jax.ShapeDtypeStruct((t, MAIN_COLS), BF16),
                   jax.ShapeDtypeStruct((t, TAIL_COLS), F32)],
        scratch_shapes=[pltpu.VMEM((tm, D_MODEL), BF16), pltpu.VMEM((8, D_MODEL), F32)],
        compiler_params=pltpu.CompilerParams(
            dimension_semantics=("arbitrary", "arbitrary"), vmem_limit_bytes=VMEM_LIMIT),
        name="in_proj",
    )(x2, g, w_main, w_tail, conv_w)


def _time_scan(x, op, row):
    s = 1
    while s < CHUNK:
        x = jnp.where(row >= s, op(x, pltpu.roll(x, s, 0)), x)
        s *= 2
    return x


def _mlstm_kernel(q_ref, k_ref, v_ref, gate_ref, bias_ref, num_ref, dst_ref, c_ref, m_ref, *, n_chunk):
    L = CHUNK

    @pl.when(pl.program_id(1) == 0)
    def _():
        c_ref[...] = jnp.zeros_like(c_ref)
        m_ref[...] = jnp.zeros_like(m_ref)

    row = lax.broadcasted_iota(jnp.int32, (L, LANES), 0)
    lane = lax.broadcasted_iota(jnp.int32, (L, LANES), 1)
    gate_lanes = jnp.logical_and(lane >= IGATE_LANE, lane < FGATE_LANE + HEADS)
    tri = (lax.broadcasted_iota(jnp.int32, (L, L), 0) >= lax.broadcasted_iota(jnp.int32, (L, L), 1))
    lane_v = lax.broadcasted_iota(jnp.int32, (L, MLSTM_DV), 1)
    m_prev = m_ref[...]
    c_state = [c_ref[h] for h in range(HEADS)]

    for g in range(n_chunk):
        rows = slice(g * L, (g + 1) * L)
        pre = jnp.where(gate_lanes, gate_ref[rows, :] + bias_ref[...], 0.0)
        capped = GATE_CAP * jnp.tanh(pre * (1.0 / GATE_CAP))
        log_i = capped * LOG2E
        log_f = (jnp.minimum(capped, 0.0) - jnp.log1p(jnp.exp(-jnp.abs(capped)))) * LOG2E
        log_f = pltpu.roll(log_f, LANES - HEADS, 1)
        b = _time_scan(log_f, jnp.add, row)
        b_last = b[L - 1:L, :]
        rowd = log_i - b
        big_m = jnp.maximum(_time_scan(rowd, jnp.maximum, row), m_prev)
        e_neg_m = jnp.exp2(-(b + big_m))
        a = b_last + rowd
        m_loc = jnp.max(a, axis=0, keepdims=True)
        m_new = jnp.maximum(b_last + m_prev, m_loc)
        s_old = jnp.exp2(b_last + m_prev - m_new)
        w = jnp.exp2(a - m_new)
        m_prev_g, m_prev = m_prev, m_new
        rowd_t = rowd.T
        w_t = w.T
        q_all = q_ref[rows, :] * (MLSTM_DK ** -0.5)
        k_t = k_ref[rows, :].astype(F32).T

        hk = [slice(h * MLSTM_DK, (h + 1) * MLSTM_DK) for h in range(HEADS)]
        hv = [slice(h * MLSTM_DV, (h + 1) * MLSTM_DV) for h in range(HEADS)]
        gl = [slice(IGATE_LANE + h, IGATE_LANE + h + 1) for h in range(HEADS)]
        v_aug = [jnp.concatenate([v_ref[rows, hv[h]], (lane_v == IGATE_LANE + h).astype(BF16)], axis=1)
                 for h in range(HEADS)]
        scores = [_dot(q_all[:, hk[h]], k_t[hk[h], :].astype(BF16)) for h in range(HEADS)]
        c_loc = [_dot((k_t[hk[h], :] * w_t[gl[h], :]).astype(BF16), v_aug[h]) for h in range(HEADS)]
        big_m_b = [jnp.broadcast_to(big_m[:, gl[h]], (L, L)) for h in range(HEADS)]
        dmat = [jnp.where(tri, jnp.exp2(rowd_t[gl[h], :] - big_m_b[h]), 0.0) for h in range(HEADS)]
        q_state = [(q_all[:, hk[h]].astype(F32)
                    * jnp.exp2(m_prev_g[:, gl[h]] - big_m_b[h][:, :MLSTM_DK])).astype(BF16) for h in range(HEADS)]
        num_aug = []
        for h in range(HEADS):
            lhs = jnp.concatenate([(scores[h] * dmat[h]).astype(BF16), q_state[h],
                                   jnp.zeros((L, L - MLSTM_DK), BF16)], axis=1)
            rhs = jnp.concatenate([v_aug[h], c_state[h].astype(BF16),
                                   jnp.zeros((L - MLSTM_DK, 2 * MLSTM_DV), BF16)], axis=0)
            num_aug.append(_dot(lhs, rhs))
        for h in range(HEADS):
            num_ref[rows, hv[h]] = num_aug[h][:, :MLSTM_DV].astype(num_ref.dtype)
            c_state[h] = s_old[:, gl[h]] * c_state[h] + c_loc[h]
        den = functools.reduce(jnp.add, [n[:, MLSTM_DV:] for n in num_aug])
        dst_ref[rows, :] = jnp.maximum(jnp.abs(den), e_neg_m)

    m_ref[...] = m_prev
    for h in range(HEADS):
        c_ref[h] = c_state[h]


def _mlstm(main, tail, gate_bias, bsz, seq, n_chunk):
    t = main.shape[0]
    rows = n_chunk * CHUNK
    nc = seq // rows
    row = lambda b, c: b * nc + c
    col_block = lambda col0, width: pl.BlockSpec((rows, width), lambda b, c: (row(b, c), col0 // width))
    return pl.pallas_call(
        functools.partial(_mlstm_kernel, n_chunk=n_chunk),
        grid=(bsz, nc),
        in_specs=[col_block(COL_MQ, HEADS * MLSTM_DK), col_block(COL_MK, HEADS * MLSTM_DK),
                  col_block(COL_MV, HEADS * MLSTM_DV),
                  col_block(TAIL_KROPE, LANES),
                  pl.BlockSpec((1, LANES), lambda b, c: (0, 0))],
        out_specs=[pl.BlockSpec((rows, D_MODEL), lambda b, c: (row(b, c), 0)),
                   pl.BlockSpec((rows, LANES), lambda b, c: (row(b, c), 0))],
        out_shape=[jax.ShapeDtypeStruct((t, D_MODEL), BF16), jax.ShapeDtypeStruct((t, LANES), F32)],
        scratch_shapes=[pltpu.VMEM((HEADS, MLSTM_DK, 2 * MLSTM_DV), F32),
                        pltpu.VMEM((1, LANES), F32)],
        compiler_params=pltpu.CompilerParams(dimension_semantics=("parallel", "arbitrary")),
        name="mlstm",
    )(main, main, main, tail, gate_bias)


def _rms0(x, g_col):
    return x * lax.rsqrt(jnp.mean(x * x, axis=0, keepdims=True) + NORM_EPS) * g_col


def _rope(x, cs):
    half = MLA_ROPE // 2
    rot = jnp.concatenate([x[:, half:], x[:, :half]], axis=-1)
    return x * cs[:, :MLA_ROPE] + rot * cs[:, MLA_ROPE:]


def _mla_prep_kernel(tail_ref, cs_ref, cst_ref, qa_ref, wqt_ref, kva_ref, wk_ref, wvt_ref, qn_ref, kn_ref,
                     qt_out, k_out, vt_out):
    tm = ATT_T
    half = MLA_ROPE // 2
    qn_g, qp_g = qn_ref[:MLA_NOPE, :], qn_ref[MLA_NOPE:, :]
    kn_g, kp_g = kn_ref[:, :MLA_NOPE], kn_ref[:, MLA_NOPE:]
    pe0 = HEADS * MLA_NOPE
    ones_rows = (lax.broadcasted_iota(jnp.int32, (V_ROWS - MLA_V, tm), 0) == 0).astype(BF16)
    pad = QK_PAD - MLA_NOPE - MLA_ROPE
    q_pad = jnp.where(lax.broadcasted_iota(jnp.int32, (pad, tm), 0) == 0, -SCORE_SHIFT, 0.0).astype(BF16)
    k_pad = (lax.broadcasted_iota(jnp.int32, (tm, pad), 1) == 0).astype(BF16)
    for s in range(tail_ref.shape[0] // tm):
        rows = slice(s * tm, (s + 1) * tm)
        cq = _rms(tail_ref[rows, :Q_LORA], qa_ref[...]).astype(BF16)
        ckv = _rms(tail_ref[rows, Q_LORA:Q_LORA + KV_LORA], kva_ref[...]).astype(BF16)
        qft = _dot_nt(wqt_ref[...], cq)
        kf = _dot(ckv, wk_ref[...])
        vft = _dot_nt(wvt_ref[...], ckv)
        cos_t, sin_t = cst_ref[:half, rows], cst_ref[half:, rows]
        k_pe = _rope(_rms(tail_ref[rows, TAIL_KROPE:TAIL_KROPE + MLA_ROPE], kp_g), cs_ref[rows, :]).astype(BF16)
        for h in range(HEADS):
            q_nope = _rms0(qft[h * MLA_NOPE:(h + 1) * MLA_NOPE], qn_g)
            q_pe = _rms0(qft[pe0 + h * MLA_ROPE:pe0 + (h + 1) * MLA_ROPE], qp_g)
            x1, x2 = q_pe[:half], q_pe[half:]
            q_rot = jnp.concatenate([x1 * cos_t - x2 * sin_t, x1 * sin_t + x2 * cos_t], axis=0)
            qt_out[0, h, s] = jnp.concatenate([q_nope.astype(BF16), q_rot.astype(BF16), q_pad], axis=0)
            k_nope = _rms(kf[:, h * MLA_NOPE:(h + 1) * MLA_NOPE], kn_g)
            k_out[0, h, rows, :] = jnp.concatenate([k_nope.astype(BF16), k_pe, k_pad], axis=-1)
            vt_out[0, h, s] = jnp.concatenate([vft[h * MLA_V:(h + 1) * MLA_V].astype(BF16), ones_rows], axis=0)


def _mla_prep(tail, cs, cst, qa, wqt, kva, wk, wvt, qn_col, kn, bsz, seq):
    tiles = 2 if (seq // ATT_T) % 2 == 0 else 1
    tm = tiles * ATT_T
    nt = seq // tm
    full = lambda shape: pl.BlockSpec(shape, lambda i: (0,) * len(shape))
    blocked = lambda rows: pl.BlockSpec((1, HEADS, tiles, rows, ATT_T), lambda i: (i // nt, 0, i % nt, 0, 0))
    return pl.pallas_call(
        _mla_prep_kernel,
        grid=(bsz * nt,),
        in_specs=[pl.BlockSpec((tm, TAIL_COLS), lambda i: (i, 0)),
                  pl.BlockSpec((tm, 2 * MLA_ROPE), lambda i: (i, 0)),
                  pl.BlockSpec((MLA_ROPE, tm), lambda i: (0, i)),
                  full((1, Q_LORA)), full(wqt.shape), full((1, KV_LORA)), full(wk.shape), full(wvt.shape),
                  full((MLA_NOPE + MLA_ROPE, 1)), full((1, MLA_NOPE + MLA_ROPE))],
        out_specs=[blocked(QK_PAD),
                   pl.BlockSpec((1, HEADS, tm, QK_PAD), lambda i: (i // nt, 0, i % nt, 0)),
                   blocked(V_ROWS)],
        out_shape=[jax.ShapeDtypeStruct((bsz, HEADS, seq // ATT_T, QK_PAD, ATT_T), BF16),
                   jax.ShapeDtypeStruct((bsz, HEADS, seq, QK_PAD), BF16),
                   jax.ShapeDtypeStruct((bsz, HEADS, seq // ATT_T, V_ROWS, ATT_T), BF16)],
        compiler_params=pltpu.CompilerParams(dimension_semantics=("parallel",),
                                             vmem_limit_bytes=VMEM_LIMIT),
        name="mla_prep",
    )(tail, cs, cst, qa, wqt, kva, wk, wvt, qn_col, kn)


def _attn_kernel(qt_ref, k_ref, vt_ref, o_ref, s_scr, p_scr, a_scr, m_all, acc_all, *, nq):
    t = ATT_T

    def qk(qi, j):
        k0 = pl.multiple_of(j * t, t)
        return _dot(k_ref[0, 0, pl.ds(k0, t), :], qt_ref[0, 0, qi])

    def accumulate(tile, alpha, p):
        q, j = tile
        acc_all[q] = alpha * acc_all[q] + _dot(vt_ref[0, 0, j], p)

    def stage(prev, cur, nxt, s_cur, p_prev, alpha_prev, masked):
        accumulate(prev, alpha_prev, p_prev)
        qi = cur[0]
        if masked:
            keep = lax.broadcasted_iota(jnp.int32, (t, t), 0) <= lax.broadcasted_iota(jnp.int32, (t, t), 1)
            s_cur = jnp.where(keep, s_cur, MASK_VALUE)
        m_old = m_all[qi]
        m_new = jnp.maximum(m_old, jnp.max(s_cur, axis=0, keepdims=True))
        m_all[qi] = m_new
        alpha = jnp.exp2(m_old - m_new)
        p = jnp.exp2((s_cur - m_new).astype(BF16))
        s_next = qk(jnp.minimum(nxt[0], nq - 1), jnp.minimum(nxt[1], nq - 1))
        return s_next, p, alpha

    def advance_lower(qi, j):
        wrap = j + 1 == qi
        return jnp.where(wrap, qi + 1, qi), jnp.where(wrap, 0, j + 1)

    def advance_diag(qi, j):
        return qi + 1, j + 1

    def run_phase(n_steps, first, advance, masked):
        if n_steps == 0:
            return
        s_scr[...] = qk(*first)
        p_scr[...] = jnp.zeros_like(p_scr)
        a_scr[...] = jnp.ones_like(a_scr)

        def steps(count, prev, cur):
            s_cur, p_prev, alpha_prev = s_scr[...], p_scr[...], a_scr[...]
            for _ in range(count):
                nxt = advance(*cur)
                s_cur, p_prev, alpha_prev = stage(prev, cur, nxt, s_cur, p_prev, alpha_prev, masked)
                prev, cur = cur, nxt
            s_scr[...] = s_cur
            p_scr[...] = p_prev
            a_scr[...] = alpha_prev
            return prev, cur

        first = (jnp.int32(first[0]), jnp.int32(first[1]))
        prev, cur = lax.fori_loop(0, n_steps // ATT_UNROLL, lambda _, c: steps(ATT_UNROLL, *c), (first, first))
        if n_steps % ATT_UNROLL:
            prev, cur = steps(n_steps % ATT_UNROLL, prev, cur)
        accumulate(prev, a_scr[...], p_scr[...])

    m_all[...] = jnp.full_like(m_all, MASK_VALUE)
    acc_all[...] = jnp.zeros_like(acc_all)
    run_phase(nq * (nq - 1) // 2, (1, 0), advance_lower, False)
    run_phase(nq, (0, 0), advance_diag, True)
    for qi in range(nq):
        acc = acc_all[qi]
        o_ref[0, qi * t:(qi + 1) * t, :] = (acc[:MLA_V] / acc[MLA_V:MLA_V + 1]).T.astype(o_ref.dtype)


def _attn_bounded_kernel(qt_ref, k_ref, vt_ref, o_ref, s_scr, p_scr, acc_all, *, nq):
    t = ATT_T

    def qk(qi, j):
        k0 = pl.multiple_of(j * t, t)
        return _dot(k_ref[0, 0, pl.ds(k0, t), :], qt_ref[0, 0, qi])

    def accumulate(tile, p):
        q, j = tile
        acc_all[q] += _dot(vt_ref[0, 0, j], p)

    def stage(prev, cur, nxt, s_cur, p_prev, masked):
        accumulate(prev, p_prev)
        p = jnp.exp2(s_cur)
        if masked:
            keep = lax.broadcasted_iota(jnp.int32, (t, t), 0) <= lax.broadcasted_iota(jnp.int32, (t, t), 1)
            p = jnp.where(keep, p, 0.0)
        s_next = qk(jnp.minimum(nxt[0], nq - 1), jnp.minimum(nxt[1], nq - 1))
        return s_next, p.astype(BF16)

    def advance_lower(qi, j):
        wrap = j + 1 == qi
        return jnp.where(wrap, qi + 1, qi), jnp.where(wrap, 0, j + 1)

    def advance_diag(qi, j):
        return qi + 1, j + 1

    def run_phase(n_steps, first, advance, masked):
        if n_steps == 0:
            return
        s_scr[...] = qk(*first)
        p_scr[...] = jnp.zeros_like(p_scr)

        def steps(count, prev, cur):
            s_cur, p_prev = s_scr[...], p_scr[...]
            for _ in range(count):
                nxt = advance(*cur)
                s_cur, p_prev = stage(prev, cur, nxt, s_cur, p_prev, masked)
                prev, cur = cur, nxt
            s_scr[...] = s_cur
            p_scr[...] = p_prev
            return prev, cur

        first = (jnp.int32(first[0]), jnp.int32(first[1]))
        prev, cur = lax.fori_loop(0, n_steps // ATT_UNROLL, lambda _, c: steps(ATT_UNROLL, *c), (first, first))
        if n_steps % ATT_UNROLL:
            prev, cur = steps(n_steps % ATT_UNROLL, prev, cur)
        accumulate(prev, p_scr[...])

    acc_all[...] = jnp.zeros_like(acc_all)
    run_phase(nq * (nq - 1) // 2, (1, 0), advance_lower, False)
    run_phase(nq, (0, 0), advance_diag, True)
    for qi in range(nq):
        acc = acc_all[qi]
        o_ref[0, qi * t:(qi + 1) * t, :] = (acc[:MLA_V] / acc[MLA_V:MLA_V + 1]).T.astype(o_ref.dtype)


def _attention_bounded(qt, k, vt):
    bsz, _, seq, _ = k.shape
    nq = seq // ATT_T
    kern = functools.partial(_attn_bounded_kernel, nq=nq)
    return pl.pallas_call(
        kern,
        grid=(bsz, HEADS),
        in_specs=[pl.BlockSpec((1, 1, nq, QK_PAD, ATT_T), lambda b, h: (b, h, 0, 0, 0)),
                  pl.BlockSpec((1, 1, seq, QK_PAD), lambda b, h: (b, h, 0, 0)),
                  pl.BlockSpec((1, 1, nq, V_ROWS, ATT_T), lambda b, h: (b, h, 0, 0, 0))],
        out_specs=pl.BlockSpec((1, seq, MLA_V), lambda b, h: (b, 0, h)),
        out_shape=jax.ShapeDtypeStruct((bsz, seq, HEADS * MLA_V), BF16),
        scratch_shapes=[pltpu.VMEM((ATT_T, ATT_T), F32), pltpu.VMEM((ATT_T, ATT_T), BF16),
                        pltpu.VMEM((nq, V_ROWS, ATT_T), F32)],
        compiler_params=pltpu.CompilerParams(dimension_semantics=("parallel", "parallel"),
                                             vmem_limit_bytes=VMEM_LIMIT),
        name="attn_bounded",
    )(qt, k, vt)


def _attention(qt, k, vt):
    bsz, _, seq, _ = k.shape
    nq = seq // ATT_T
    kern = functools.partial(_attn_kernel, nq=nq)
    return pl.pallas_call(
        kern,
        grid=(bsz, HEADS),
        in_specs=[pl.BlockSpec((1, 1, nq, QK_PAD, ATT_T), lambda b, h: (b, h, 0, 0, 0)),
                  pl.BlockSpec((1, 1, seq, QK_PAD), lambda b, h: (b, h, 0, 0)),
                  pl.BlockSpec((1, 1, nq, V_ROWS, ATT_T), lambda b, h: (b, h, 0, 0, 0))],
        out_specs=pl.BlockSpec((1, seq, MLA_V), lambda b, h: (b, 0, h)),
        out_shape=jax.ShapeDtypeStruct((bsz, seq, HEADS * MLA_V), BF16),
        scratch_shapes=[pltpu.VMEM((ATT_T, ATT_T), F32), pltpu.VMEM((ATT_T, ATT_T), BF16),
                        pltpu.VMEM((1, ATT_T), F32), pltpu.VMEM((nq, 1, ATT_T), F32),
                        pltpu.VMEM((nq, V_ROWS, ATT_T), F32)],
        compiler_params=pltpu.CompilerParams(dimension_semantics=("parallel", "parallel"),
                                             vmem_limit_bytes=VMEM_LIMIT),
        name="attn",
    )(qt, k, vt)


def _mlstm_head_out(num_ref, dst_ref, o_ref, hn_ref):
    out_gain = (0.5 * jnp.tanh(0.5 * o_ref[...].astype(F32)) + 0.5) * hn_ref[...]
    heads = []
    for h in range(HEADS):
        hv = slice(h * MLSTM_DV, (h + 1) * MLSTM_DV)
        num = num_ref[:, hv].astype(F32)
        d = dst_ref[:, IGATE_LANE + h:IGATE_LANE + h + 1]
        scale = lax.rsqrt(jnp.mean(num * num, axis=-1, keepdims=True) + NORM_EPS * d * d)
        heads.append((num * scale * out_gain[:, hv]).astype(BF16))
    return jnp.concatenate(heads, axis=1)


def _merge_kernel(ya_ref, gate_ref, num_ref, dst_ref, o_ref, hn_ref, yc_ref, x_ref, wb_ref, wo_ref, out_ref):
    g = gate_ref[...]
    merged = g[:, :D_MODEL].astype(F32) * _dot(ya_ref[...], wb_ref[0])
    y_b = _mlstm_head_out(num_ref, dst_ref, o_ref, hn_ref)
    merged += g[:, D_MODEL:2 * D_MODEL].astype(F32) * _dot(y_b, wb_ref[1])
    merged += g[:, 2 * D_MODEL:].astype(F32) * _dot(yc_ref[...], wb_ref[2])
    out_ref[...] = x_ref[...] + _dot(merged.astype(BF16), wo_ref[...])


def _merge(y_a, main, num, dstat, head_norm, y_c, x2, w_branch, w_out, layer, tm):
    t = x2.shape[0]
    col = lambda c: pl.BlockSpec((tm, D_MODEL), lambda i: (i, c))
    return pl.pallas_call(
        _merge_kernel,
        grid=(t // tm,),
        in_specs=[col(0),
                  pl.BlockSpec((tm, N_BRANCH * D_MODEL), lambda i: (i, GATE_COL0 // (N_BRANCH * D_MODEL))),
                  col(0), pl.BlockSpec((tm, LANES), lambda i: (i, 0)), col(COL_MO // D_MODEL),
                  pl.BlockSpec((1, D_MODEL), lambda i: (0, 0)),
                  col(0), col(0),
                  pl.BlockSpec((None, N_BRANCH, D_MODEL, D_MODEL), lambda i: (layer, 0, 0, 0),
                               pipeline_mode=pl.Buffered(1)),
                  pl.BlockSpec((None, D_MODEL, D_MODEL), lambda i: (layer, 0, 0), pipeline_mode=pl.Buffered(1))],
        out_specs=col(0),
        out_shape=jax.ShapeDtypeStruct((t, D_MODEL), F32),
        compiler_params=pltpu.CompilerParams(dimension_semantics=("parallel",),
                                             vmem_limit_bytes=VMEM_LIMIT),
        name="merge",
    )(y_a, main, num, dstat, main, head_norm, y_c, x2, w_branch, w_out)


def _mlp_kernel(x_ref, g_ref, wu_ref, wd_ref, out_ref, *, n_chunk):
    x = x_ref[...]
    h = _rms(x, g_ref[...]).astype(BF16)
    ck = D_FF // n_chunk
    acc = x
    for c in range(n_chunk):
        u = jnp.maximum(_dot(h, wu_ref[:, c * ck:(c + 1) * ck]), 0.0)
        acc = acc + _dot((u * u).astype(BF16), wd_ref[c * ck:(c + 1) * ck, :])
    out_ref[...] = acc


def _mlp(x2, g, w_up, w_down, layer, tm):
    t = x2.shape[0]
    kern = functools.partial(_mlp_kernel, n_chunk=4)
    return pl.pallas_call(
        kern,
        grid=(t // tm,),
        in_specs=[pl.BlockSpec((tm, D_MODEL), lambda i: (i, 0)),
                  pl.BlockSpec((1, D_MODEL), lambda i: (0, 0)),
                  pl.BlockSpec((None, D_MODEL, D_FF), lambda i: (layer, 0, 0), pipeline_mode=pl.Buffered(1)),
                  pl.BlockSpec((None, D_FF, D_MODEL), lambda i: (layer, 0, 0), pipeline_mode=pl.Buffered(1))],
        out_specs=pl.BlockSpec((tm, D_MODEL), lambda i: (i, 0)),
        out_shape=jax.ShapeDtypeStruct((t, D_MODEL), F32),
        compiler_params=pltpu.CompilerParams(dimension_semantics=("parallel",),
                                             vmem_limit_bytes=VMEM_LIMIT),
        name="mlp",
    )(x2, g, w_up, w_down)


def _regroup_w_in(w_in):
    i0 = CONV_COLS + GATE_COL0
    cq0 = i0 + 2 * HEADS
    g0 = cq0 + Q_LORA + KV_LORA + MLA_ROPE
    pad = jnp.zeros(w_in.shape[:2] + (TAIL_COLS - (Q_LORA + KV_LORA + MLA_ROPE + 2 * HEADS),), w_in.dtype)
    w_main = jnp.concatenate([w_in[..., :i0], w_in[..., g0:]], axis=-1).astype(BF16)
    w_tail = jnp.concatenate([w_in[..., cq0:g0], w_in[..., i0:cq0], pad], axis=-1).astype(BF16)
    return w_main, w_tail


def _heads_last_split(w, first, second):
    k = w.shape[0]
    w3 = w.reshape(k, HEADS, first + second)
    return jnp.concatenate([w3[:, :, :first].reshape(k, HEADS * first),
                            w3[:, :, first:].reshape(k, HEADS * second)], axis=1).astype(BF16)


def _score_bound(q_gain, k_gain):
    def sq_norm(g):
        g = g.astype(F32)
        return MLA_NOPE * jnp.max(g[:MLA_NOPE] ** 2) + MLA_ROPE * jnp.max(g[MLA_NOPE:] ** 2)
    return jnp.sqrt(sq_norm(q_gain) * sq_norm(k_gain)) * (Q_SCALE * BOUND_SLACK)


def _pick(n, candidates):
    for c in candidates:
        if n % c == 0:
            return c
    raise ValueError(f"no tile size for {n}")


def kernel(x, positions, mix_norm, w_in, conv_w, mlstm_igate_bias, mlstm_fgate_bias, mlstm_head_norm,
           mla_q_a_norm, mla_w_uq, mla_kv_a_norm, mla_w_ukv, mla_q_norm, mla_k_norm, w_branch, w_out,
           mlp_norm, w_up, w_down):
    bsz, seq, _ = x.shape
    depth = w_in.shape[0]
    t = bsz * seq
    assert seq % ATT_T == 0
    n_chunk = _pick(seq // CHUNK, tuple(g for g in (8, 4, 2, 1) if g <= MLSTM_GROUP))
    tm_proj = _pick(seq, (1024, 512, 256, 128))
    tm = _pick(seq, (512, 256, 128))
    tm_mlp = _pick(t, (1024, 512, 256, 128))

    x2 = x.reshape(t, D_MODEL)
    cs, cst = _rope_table(positions, ATT_T)
    row = lambda a: a.reshape(1, -1).astype(F32)
    w_main, w_tail = _regroup_w_in(w_in)
    w_branch_b, w_out_b, w_up_b, w_down_b = (w.astype(BF16) for w in (w_branch, w_out, w_up, w_down))

    for l in range(depth):
        gate_bias = jnp.zeros((1, LANES), F32)
        gate_bias = gate_bias.at[0, IGATE_LANE:IGATE_LANE + HEADS].set(mlstm_igate_bias[l])
        gate_bias = gate_bias.at[0, FGATE_LANE:FGATE_LANE + HEADS].set(mlstm_fgate_bias[l])
        w_q = _heads_last_split(mla_w_uq[l], MLA_NOPE, MLA_ROPE)
        w_kv = _heads_last_split(mla_w_ukv[l], MLA_NOPE, MLA_V)

        y_a, main, tail = _in_proj(x2, row(mix_norm[l]), w_main, w_tail, conv_w[l].astype(F32), l, seq, tm_proj)
        num, dstat = _mlstm(main, tail, gate_bias, bsz, seq, n_chunk)
        qt, k, vt = _mla_prep(tail, cs, cst, row(mla_q_a_norm[l]), w_q.T, row(mla_kv_a_norm[l]),
                              w_kv[:, :HEADS * MLA_NOPE], w_kv[:, HEADS * MLA_NOPE:].T,
                              mla_q_norm[l].reshape(-1, 1).astype(F32) * Q_SCALE, row(mla_k_norm[l]), bsz, seq)
        y_c = lax.cond(_score_bound(mla_q_norm[l], mla_k_norm[l]) <= SCORE_BOUND,
                       _attention_bounded, _attention, qt, k, vt).reshape(t, D_MODEL)
        x2 = _merge(y_a, main, num, dstat, row(mlstm_head_norm[l]), y_c, x2, w_branch_b, w_out_b, l, tm)
        x2 = _mlp(x2, row(mlp_norm[l]), w_up_b, w_down_b, l, tm_mlp)
    return x2.reshape(bsz, seq, D_MODEL)
```

```python
import functools

import jax
import jax.numpy as jnp
import numpy as np
from jax import lax
from jax.experimental import pallas as pl
from jax.experimental.pallas import tpu as pltpu

F32 = jnp.float32
BF16 = jnp.bfloat16

D_MODEL = 1024
N_BRANCH = 3
CONV_WIDTH = 3
HEADS = 8
MLSTM_DK = 64
MLSTM_DV = 128
CHUNK = 128
GATE_CAP = 15.0
MLA_NOPE = 128
MLA_ROPE = 64
MLA_V = 128
Q_LORA = 256
KV_LORA = 128
ROPE_BASE = 10000.0
D_FF = 4 * D_MODEL
NORM_EPS = 1e-6
QK_PAD = 256

CONV_COLS = 3 * D_MODEL
PROJ_COLS = 9 * D_MODEL
MAIN_COLS = PROJ_COLS - CONV_COLS
TAIL_COLS = 512
COL_MQ = 0
COL_MK = COL_MQ + HEADS * MLSTM_DK
COL_MV = COL_MK + HEADS * MLSTM_DK
COL_MO = COL_MV + HEADS * MLSTM_DV
GATE_COL0 = COL_MO + HEADS * MLSTM_DV
LANES = 128
TAIL_KROPE = Q_LORA + KV_LORA
IGATE_LANE = MLA_ROPE
FGATE_LANE = MLA_ROPE + HEADS

IN_PROJ_TN = 3 * D_MODEL
IN_PROJ_CHUNK = 1536
CONV_CHUNK = 256
MLSTM_GROUP = 8
ATT_T = 512
LOG2E = 1.4426950408889634
Q_SCALE = (MLA_NOPE + MLA_ROPE) ** -0.5 * LOG2E
MASK_VALUE = -1e30
SCORE_BOUND = 70.0
SCORE_SHIFT = 48.0
BOUND_SLACK = 1.02
ATT_UNROLL = 8
ATT_UNROLL_BOUNDED = 24
V_ROWS = MLA_V + 16

VMEM_LIMIT = 56 * 1024 * 1024


def _rms(x, g):
    return x * lax.rsqrt(jnp.mean(x * x, axis=-1, keepdims=True) + NORM_EPS) * g


def _dot(a, b):
    return jnp.dot(a, b, preferred_element_type=F32)


def _dot_nt(a, b):
    return lax.dot_general(a, b, (((1,), (1,)), ((), ())), preferred_element_type=F32)


def _rope_table_kernel(pos_ref, freq_ref, cs_ref, cst_ref):
    ang = freq_ref[...] * pos_ref[...].astype(F32)
    c, s = jnp.cos(ang), jnp.sin(ang)
    cst_ref[...] = jnp.concatenate([c, s], axis=0)
    cs_ref[...] = jnp.concatenate([c, c, -s, s], axis=0).T


def _rope_table(positions, tm):
    t = positions.size
    half = MLA_ROPE // 2
    inv_freq = (np.float32(ROPE_BASE) ** (-np.arange(0, MLA_ROPE, 2, dtype=np.float32) / np.float32(MLA_ROPE)))
    inv_freq = jnp.asarray(inv_freq.astype(np.float32))
    return pl.pallas_call(
        _rope_table_kernel,
        grid=(t // tm,),
        in_specs=[pl.BlockSpec((1, tm), lambda i: (0, i)),
                  pl.BlockSpec((half, 1), lambda i: (0, 0))],
        out_specs=[pl.BlockSpec((tm, 2 * MLA_ROPE), lambda i: (i, 0)),
                   pl.BlockSpec((MLA_ROPE, tm), lambda i: (0, i))],
        out_shape=[jax.ShapeDtypeStruct((t, 2 * MLA_ROPE), F32),
                   jax.ShapeDtypeStruct((MLA_ROPE, t), F32)],
        name="rope_table",
    )(positions.reshape(1, t), inv_freq.reshape(half, 1))


def _in_proj_kernel(x_ref, g_ref, w_ref, wt_ref, cw_ref, ya_ref, main_ref, tail_ref, xn_ref, zprev_ref, *,
                    tiles_per_seq):
    i, j = pl.program_id(0), pl.program_id(1)
    tm = x_ref.shape[0]

    @pl.when(jnp.logical_and(j == 0, i % tiles_per_seq == 0))
    def _():
        zprev_ref[...] = jnp.zeros_like(zprev_ref)

    @pl.when(j == 0)
    def _():
        xn = _rms(x_ref[...], g_ref[...]).astype(BF16)
        xn_ref[...] = xn
        tail_ref[...] = _dot(xn, wt_ref[...])
        row = lax.broadcasted_iota(jnp.int32, (tm, CONV_CHUNK), 0)
        for c0 in range(0, D_MODEL, CONV_CHUNK):
            c = slice(c0, c0 + CONV_CHUNK)
            gate_b, gate_c, u = (_dot(xn, w_ref[:, n * D_MODEL + c0:n * D_MODEL + c0 + CONV_CHUNK]) for n in range(3))
            z = gate_c * u
            p1, p2 = zprev_ref[7:8, c], zprev_ref[6:7, c]
            z1 = jnp.where(row == 0, p1, pltpu.roll(z, 1, 0))
            z2 = jnp.where(row == 0, p2, jnp.where(row == 1, p1, pltpu.roll(z, 2, 0)))
            zprev_ref[:, c] = z[tm - 8:, :]
            ya_ref[:, c] = (gate_b * (cw_ref[0:1, c] * z2 + cw_ref[1:2, c] * z1 + cw_ref[2:3, c] * z)).astype(BF16)

    cols = [slice(c, c + IN_PROJ_CHUNK) for c in range(0, w_ref.shape[1], IN_PROJ_CHUNK)]

    @pl.when(j == 1)
    def _():
        for c in cols:
            main_ref[:, c] = _dot(xn_ref[...], w_ref[:, c]).astype(BF16)

    @pl.when(j == 2)
    def _():
        for c in cols:
            main_ref[:, c] = jax.nn.sigmoid(_dot(xn_ref[...], w_ref[:, c])).astype(BF16)


def _in_proj(x2, g, w_main, w_tail, conv_w, layer, seq, tm):
    t = x2.shape[0]
    tn = IN_PROJ_TN
    assert CONV_COLS == tn and GATE_COL0 == tn and MAIN_COLS == 2 * tn
    kern = functools.partial(_in_proj_kernel, tiles_per_seq=seq // tm)
    return pl.pallas_call(
        kern,
        grid=(t // tm, PROJ_COLS // tn),
        in_specs=[pl.BlockSpec((tm, D_MODEL), lambda i, j: (i, 0)),
                  pl.BlockSpec((1, D_MODEL), lambda i, j: (0, 0)),
                  pl.BlockSpec((None, D_MODEL, tn), lambda i, j: (layer, 0, j)),
                  pl.BlockSpec((None, D_MODEL, TAIL_COLS), lambda i, j: (layer, 0, 0)),
                  pl.BlockSpec((CONV_WIDTH, D_MODEL), lambda i, j: (0, 0))],
        out_specs=[pl.BlockSpec((tm, D_MODEL), lambda i, j: (i, 0)),
                   pl.BlockSpec((tm, tn), lambda i, j: (i, jnp.maximum(j - 1, 0))),
                   pl.BlockSpec((tm, TAIL_COLS), lambda i, j: (i, 0))],
        out_shape=[jax.ShapeDtypeStruct((t, D_MODEL), BF16),
                   jax.ShapeDtypeStruct((t, MAIN_COLS), BF16),
                   jax.ShapeDtypeStruct((t, TAIL_COLS), F32)],
        scratch_shapes=[pltpu.VMEM((tm, D_MODEL), BF16), pltpu.VMEM((8, D_MODEL), F32)],
        compiler_params=pltpu.CompilerParams(
            dimension_semantics=("arbitrary", "arbitrary"), vmem_limit_bytes=VMEM_LIMIT),
        name="in_proj",
    )(x2, g, w_main, w_tail, conv_w)


def _time_scan(x, op, row):
    s = 1
    while s < CHUNK:
        x = jnp.where(row >= s, op(x, pltpu.roll(x, s, 0)), x)
        s *= 2
    return x


def _mlstm_kernel(q_ref, k_ref, v_ref, gate_ref, bias_ref, num_ref, dst_ref, c_ref, m_ref, *, n_chunk):
    L = CHUNK

    @pl.when(pl.program_id(1) == 0)
    def _():
        c_ref[...] = jnp.zeros_like(c_ref)
        m_ref[...] = jnp.zeros_like(m_ref)

    row = lax.broadcasted_iota(jnp.int32, (L, LANES), 0)
    lane = lax.broadcasted_iota(jnp.int32, (L, LANES), 1)
    gate_lanes = jnp.logical_and(lane >= IGATE_LANE, lane < FGATE_LANE + HEADS)
    tri = (lax.broadcasted_iota(jnp.int32, (L, L), 0) >= lax.broadcasted_iota(jnp.int32, (L, L), 1))
    lane_v = lax.broadcasted_iota(jnp.int32, (L, MLSTM_DV), 1)
    m_prev = m_ref[...]
    c_state = [c_ref[h] for h in range(HEADS)]

    for g in range(n_chunk):
        rows = slice(g * L, (g + 1) * L)
        pre = jnp.where(gate_lanes, gate_ref[rows, :] + bias_ref[...], 0.0)
        capped = GATE_CAP * jnp.tanh(pre * (1.0 / GATE_CAP))
        log_i = capped * LOG2E
        log_f = (jnp.minimum(capped, 0.0) - jnp.log1p(jnp.exp(-jnp.abs(capped)))) * LOG2E
        log_f = pltpu.roll(log_f, LANES - HEADS, 1)
        b = _time_scan(log_f, jnp.add, row)
        b_last = b[L - 1:L, :]
        rowd = log_i - b
        big_m = jnp.maximum(_time_scan(rowd, jnp.maximum, row), m_prev)
        e_neg_m = jnp.exp2(-(b + big_m))
        a = b_last + rowd
        m_loc = jnp.max(a, axis=0, keepdims=True)
        m_new = jnp.maximum(b_last + m_prev, m_loc)
        s_old = jnp.exp2(b_last + m_prev - m_new)
        w = jnp.exp2(a - m_new)
        m_prev_g, m_prev = m_prev, m_new
        rowd_t = rowd.T
        w_t = w.T
        q_all = q_ref[rows, :] * (MLSTM_DK ** -0.5)
        k_t = k_ref[rows, :].astype(F32).T

        hk = [slice(h * MLSTM_DK, (h + 1) * MLSTM_DK) for h in range(HEADS)]
        hv = [slice(h * MLSTM_DV, (h + 1) * MLSTM_DV) for h in range(HEADS)]
        gl = [slice(IGATE_LANE + h, IGATE_LANE + h + 1) for h in range(HEADS)]
        v_aug = [jnp.concatenate([v_ref[rows, hv[h]], (lane_v == IGATE_LANE + h).astype(BF16)], axis=1)
                 for h in range(HEADS)]
        scores = [_dot(q_all[:, hk[h]], k_t[hk[h], :].astype(BF16)) for h in range(HEADS)]
        c_loc = [_dot((k_t[hk[h], :] * w_t[gl[h], :]).astype(BF16), v_aug[h]) for h in range(HEADS)]
        big_m_b = [jnp.broadcast_to(big_m[:, gl[h]], (L, L)) for h in range(HEADS)]
        dmat = [jnp.where(tri, jnp.exp2(rowd_t[gl[h], :] - big_m_b[h]), 0.0) for h in range(HEADS)]
        q_state = [(q_all[:, hk[h]].astype(F32)
                    * jnp.exp2(m_prev_g[:, gl[h]] - big_m_b[h][:, :MLSTM_DK])).astype(BF16) for h in range(HEADS)]
        num_aug = []
        for h in range(HEADS):
            lhs = jnp.concatenate([(scores[h] * dmat[h]).astype(BF16), q_state[h],
                                   jnp.zeros((L, L - MLSTM_DK), BF16)], axis=1)
            rhs = jnp.concatenate([v_aug[h], c_state[h].astype(BF16),
                                   jnp.zeros((L - MLSTM_DK, 2 * MLSTM_DV), BF16)], axis=0)
            num_aug.append(_dot(lhs, rhs))
        for h in range(HEADS):
            num_ref[rows, hv[h]] = num_aug[h][:, :MLSTM_DV].astype(num_ref.dtype)
            c_state[h] = s_old[:, gl[h]] * c_state[h] + c_loc[h]
        den = functools.reduce(jnp.add, [n[:, MLSTM_DV:] for n in num_aug])
        dst_ref[rows, :] = jnp.maximum(jnp.abs(den), e_neg_m)

    m_ref[...] = m_prev
    for h in range(HEADS):
        c_ref[h] = c_state[h]


def _mlstm(main, tail, gate_bias, bsz, seq, n_chunk):
    t = main.shape[0]
    rows = n_chunk * CHUNK
    nc = seq // rows
    row = lambda b, c: b * nc + c
    col_block = lambda col0, width: pl.BlockSpec((rows, width), lambda b, c: (row(b, c), col0 // width))
    return pl.pallas_call(
        functools.partial(_mlstm_kernel, n_chunk=n_chunk),
        grid=(bsz, nc),
        in_specs=[col_block(COL_MQ, HEADS * MLSTM_DK), col_block(COL_MK, HEADS * MLSTM_DK),
                  col_block(COL_MV, HEADS * MLSTM_DV),
                  col_block(TAIL_KROPE, LANES),
                  pl.BlockSpec((1, LANES), lambda b, c: (0, 0))],
        out_specs=[pl.BlockSpec((rows, D_MODEL), lambda b, c: (row(b, c), 0)),
                   pl.BlockSpec((rows, LANES), lambda b, c: (row(b, c), 0))],
        out_shape=[jax.ShapeDtypeStruct((t, D_MODEL), BF16), jax.ShapeDtypeStruct((t, LANES), F32)],
        scratch_shapes=[pltpu.VMEM((HEADS, MLSTM_DK, 2 * MLSTM_DV), F32),
                        pltpu.VMEM((1, LANES), F32)],
        compiler_params=pltpu.CompilerParams(dimension_semantics=("parallel", "arbitrary")),
        name="mlstm",
    )(main, main, main, tail, gate_bias)


def _rms0(x, g_col):
    return x * lax.rsqrt(jnp.mean(x * x, axis=0, keepdims=True) + NORM_EPS) * g_col


def _rope(x, cs):
    half = MLA_ROPE // 2
    rot = jnp.concatenate([x[:, half:], x[:, :half]], axis=-1)
    return x * cs[:, :MLA_ROPE] + rot * cs[:, MLA_ROPE:]


def _mla_prep_kernel(tail_ref, cs_ref, cst_ref, qa_ref, wqt_ref, kva_ref, wk_ref, wvt_ref, qn_ref, kn_ref,
                     qt_out, k_out, vt_out):
    tm = ATT_T
    half = MLA_ROPE // 2
    qn_g, qp_g = qn_ref[:MLA_NOPE, :], qn_ref[MLA_NOPE:, :]
    kn_g, kp_g = kn_ref[:, :MLA_NOPE], kn_ref[:, MLA_NOPE:]
    pe0 = HEADS * MLA_NOPE
    ones_rows = (lax.broadcasted_iota(jnp.int32, (V_ROWS - MLA_V, tm), 0) == 0).astype(BF16)
    pad = QK_PAD - MLA_NOPE - MLA_ROPE
    q_pad = jnp.where(lax.broadcasted_iota(jnp.int32, (pad, tm), 0) == 0, -SCORE_SHIFT, 0.0).astype(BF16)
    k_pad = (lax.broadcasted_iota(jnp.int32, (tm, pad), 1) == 0).astype(BF16)
    for s in range(tail_ref.shape[0] // tm):
        rows = slice(s * tm, (s + 1) * tm)
        cq = _rms(tail_ref[rows, :Q_LORA], qa_ref[...]).astype(BF16)
        ckv = _rms(tail_ref[rows, Q_LORA:Q_LORA + KV_LORA], kva_ref[...]).astype(BF16)
        qft = _dot_nt(wqt_ref[...], cq)
        kf = _dot(ckv, wk_ref[...])
        vft = _dot_nt(wvt_ref[...], ckv)
        cos_t, sin_t = cst_ref[:half, rows], cst_ref[half:, rows]
        k_pe = _rope(_rms(tail_ref[rows, TAIL_KROPE:TAIL_KROPE + MLA_ROPE], kp_g), cs_ref[rows, :]).astype(BF16)
        for h in range(HEADS):
            q_nope = _rms0(qft[h * MLA_NOPE:(h + 1) * MLA_NOPE], qn_g)
            q_pe = _rms0(qft[pe0 + h * MLA_ROPE:pe0 + (h + 1) * MLA_ROPE], qp_g)
            x1, x2 = q_pe[:half], q_pe[half:]
            q_rot = jnp.concatenate([x1 * cos_t - x2 * sin_t, x1 * sin_t + x2 * cos_t], axis=0)
            qt_out[0, h, s] = jnp.concatenate([q_nope.astype(BF16), q_rot.astype(BF16), q_pad], axis=0)
            k_nope = _rms(kf[:, h * MLA_NOPE:(h + 1) * MLA_NOPE], kn_g)
            k_out[0, h, rows, :] = jnp.concatenate([k_nope.astype(BF16), k_pe, k_pad], axis=-1)
            vt_out[0, h, s] = jnp.concatenate([vft[h * MLA_V:(h + 1) * MLA_V].astype(BF16), ones_rows], axis=0)


def _mla_prep(tail, cs, cst, qa, wqt, kva, wk, wvt, qn_col, kn, bsz, seq):
    tiles = 2 if (seq // ATT_T) % 2 == 0 else 1
    tm = tiles * ATT_T
    nt = seq // tm
    full = lambda shape: pl.BlockSpec(shape, lambda i: (0,) * len(shape))
    blocked = lambda rows: pl.BlockSpec((1, HEADS, tiles, rows, ATT_T), lambda i: (i // nt, 0, i % nt, 0, 0))
    return pl.pallas_call(
        _mla_prep_kernel,
        grid=(bsz * nt,),
        in_specs=[pl.BlockSpec((tm, TAIL_COLS), lambda i: (i, 0)),
                  pl.BlockSpec((tm, 2 * MLA_ROPE), lambda i: (i, 0)),
                  pl.BlockSpec((MLA_ROPE, tm), lambda i: (0, i)),
                  full((1, Q_LORA)), full(wqt.shape), full((1, KV_LORA)), full(wk.shape), full(wvt.shape),
                  full((MLA_NOPE + MLA_ROPE, 1)), full((1, MLA_NOPE + MLA_ROPE))],
        out_specs=[blocked(QK_PAD),
                   pl.BlockSpec((1, HEADS, tm, QK_PAD), lambda i: (i // nt, 0, i % nt, 0)),
                   blocked(V_ROWS)],
        out_shape=[jax.ShapeDtypeStruct((bsz, HEADS, seq // ATT_T, QK_PAD, ATT_T), BF16),
                   jax.ShapeDtypeStruct((bsz, HEADS, seq, QK_PAD), BF16),
                   jax.ShapeDtypeStruct((bsz, HEADS, seq // ATT_T, V_ROWS, ATT_T), BF16)],
        compiler_params=pltpu.CompilerParams(dimension_semantics=("parallel",),
                                             vmem_limit_bytes=VMEM_LIMIT),
        name="mla_prep",
    )(tail, cs, cst, qa, wqt, kva, wk, wvt, qn_col, kn)


def _attn_kernel(qt_ref, k_ref, vt_ref, o_ref, s_scr, p_scr, a_scr, m_all, acc_all, *, nq):
    t = ATT_T

    def qk(qi, j):
        k0 = pl.multiple_of(j * t, t)
        return _dot(k_ref[0, 0, pl.ds(k0, t), :], qt_ref[0, 0, qi])

    def accumulate(tile, alpha, p):
        q, j = tile
        acc_all[q] = alpha * acc_all[q] + _dot(vt_ref[0, 0, j], p)

    def stage(prev, cur, nxt, s_cur, p_prev, alpha_prev, masked):
        accumulate(prev, alpha_prev, p_prev)
        qi = cur[0]
        if masked:
            keep = lax.broadcasted_iota(jnp.int32, (t, t), 0) <= lax.broadcasted_iota(jnp.int32, (t, t), 1)
            s_cur = jnp.where(keep, s_cur, MASK_VALUE)
        m_old = m_all[qi]
        m_new = jnp.maximum(m_old, jnp.max(s_cur, axis=0, keepdims=True))
        m_all[qi] = m_new
        alpha = jnp.exp2(m_old - m_new)
        p = jnp.exp2((s_cur - m_new).astype(BF16))
        s_next = qk(jnp.minimum(nxt[0], nq - 1), jnp.minimum(nxt[1], nq - 1))
        return s_next, p, alpha

    def advance_lower(qi, j):
        wrap = j + 1 == qi
        return jnp.where(wrap, qi + 1, qi), jnp.where(wrap, 0, j + 1)

    def advance_diag(qi, j):
        return qi + 1, j + 1

    def run_phase(n_steps, first, advance, masked):
        if n_steps == 0:
            return
        s_scr[...] = qk(*first)
        p_scr[...] = jnp.zeros_like(p_scr)
        a_scr[...] = jnp.ones_like(a_scr)

        def steps(count, prev, cur):
            s_cur, p_prev, alpha_prev = s_scr[...], p_scr[...], a_scr[...]
            for _ in range(count):
                nxt = advance(*cur)
                s_cur, p_prev, alpha_prev = stage(prev, cur, nxt, s_cur, p_prev, alpha_prev, masked)
                prev, cur = cur, nxt
            s_scr[...] = s_cur
            p_scr[...] = p_prev
            a_scr[...] = alpha_prev
            return prev, cur

        first = (jnp.int32(first[0]), jnp.int32(first[1]))
        prev, cur = lax.fori_loop(0, n_steps // ATT_UNROLL, lambda _, c: steps(ATT_UNROLL, *c), (first, first))
        if n_steps % ATT_UNROLL:
            prev, cur = steps(n_steps % ATT_UNROLL, prev, cur)
        accumulate(prev, a_scr[...], p_scr[...])

    m_all[...] = jnp.full_like(m_all, MASK_VALUE)
    acc_all[...] = jnp.zeros_like(acc_all)
    run_phase(nq * (nq - 1) // 2, (1, 0), advance_lower, False)
    run_phase(nq, (0, 0), advance_diag, True)
    for qi in range(nq):
        acc = acc_all[qi]
        o_ref[0, qi * t:(qi + 1) * t, :] = (acc[:MLA_V] / acc[MLA_V:MLA_V + 1]).T.astype(o_ref.dtype)


def _attn_bounded_kernel(qt_ref, k_ref, vt_ref, o_ref, s_scr, p_scr, acc_all, *, nq):
    t = ATT_T

    def qk(qi, j):
        k0 = pl.multiple_of(j * t, t)
        return _dot(k_ref[0, 0, pl.ds(k0, t), :], qt_ref[0, 0, qi])

    def accumulate(tile, p):
        q, j = tile
        acc_all[q] += _dot(vt_ref[0, 0, j], p)

    def stage(prev, cur, nxt, s_cur, p_prev, masked):
        accumulate(prev, p_prev)
        p = jnp.exp2(s_cur)
        if masked:
            keep = lax.broadcasted_iota(jnp.int32, (t, t), 0) <= lax.broadcasted_iota(jnp.int32, (t, t), 1)
            p = jnp.where(keep, p, 0.0)
        s_next = qk(jnp.minimum(nxt[0], nq - 1), jnp.minimum(nxt[1], nq - 1))
        return s_next, p.astype(BF16)

    def advance_lower(qi, j):
        wrap = j + 1 == qi
        return jnp.where(wrap, qi + 1, qi), jnp.where(wrap, 0, j + 1)

    def advance_diag(qi, j):
        return qi + 1, j + 1

    def run_phase(n_steps, first, advance, masked):
        if n_steps == 0:
            return
        s_scr[...] = qk(*first)
        p_scr[...] = jnp.zeros_like(p_scr)

        def steps(count, prev, cur):
            s_cur, p_prev = s_scr[...], p_scr[...]
            for _ in range(count):
                nxt = advance(*cur)
                s_cur, p_prev = stage(prev, cur, nxt, s_cur, p_prev, masked)
                prev, cur = cur, nxt
            s_scr[...] = s_cur
            p_scr[...] = p_prev
            return prev, cur

        first = (jnp.int32(first[0]), jnp.int32(first[1]))
        unroll = ATT_UNROLL_BOUNDED
        prev, cur = lax.fori_loop(0, n_steps // unroll, lambda _, c: steps(unroll, *c), (first, first))
        if n_steps % unroll:
            prev, cur = steps(n_steps % unroll, prev, cur)
        accumulate(prev, p_scr[...])

    acc_all[...] = jnp.zeros_like(acc_all)
    run_phase(nq * (nq - 1) // 2, (1, 0), advance_lower, False)
    run_phase(nq, (0, 0), advance_diag, True)
    for qi in range(nq):
        acc = acc_all[qi]
        o_ref[0, qi * t:(qi + 1) * t, :] = (acc[:MLA_V] / acc[MLA_V:MLA_V + 1]).T.astype(o_ref.dtype)


def _attention_bounded(qt, k, vt):
    bsz, _, seq, _ = k.shape
    nq = seq // ATT_T
    kern = functools.partial(_attn_bounded_kernel, nq=nq)
    return pl.pallas_call(
        kern,
        grid=(bsz, HEADS),
        in_specs=[pl.BlockSpec((1, 1, nq, QK_PAD, ATT_T), lambda b, h: (b, h, 0, 0, 0)),
                  pl.BlockSpec((1, 1, seq, QK_PAD), lambda b, h: (b, h, 0, 0)),
                  pl.BlockSpec((1, 1, nq, V_ROWS, ATT_T), lambda b, h: (b, h, 0, 0, 0))],
        out_specs=pl.BlockSpec((1, seq, MLA_V), lambda b, h: (b, 0, h)),
        out_shape=jax.ShapeDtypeStruct((bsz, seq, HEADS * MLA_V), BF16),
        scratch_shapes=[pltpu.VMEM((ATT_T, ATT_T), F32), pltpu.VMEM((ATT_T, ATT_T), BF16),
                        pltpu.VMEM((nq, V_ROWS, ATT_T), F32)],
        compiler_params=pltpu.CompilerParams(dimension_semantics=("parallel", "parallel"),
                                             vmem_limit_bytes=VMEM_LIMIT),
        name="attn_bounded",
    )(qt, k, vt)


def _attention(qt, k, vt):
    bsz, _, seq, _ = k.shape
    nq = seq // ATT_T
    kern = functools.partial(_attn_kernel, nq=nq)
    return pl.pallas_call(
        kern,
        grid=(bsz, HEADS),
        in_specs=[pl.BlockSpec((1, 1, nq, QK_PAD, ATT_T), lambda b, h: (b, h, 0, 0, 0)),
                  pl.BlockSpec((1, 1, seq, QK_PAD), lambda b, h: (b, h, 0, 0)),
                  pl.BlockSpec((1, 1, nq, V_ROWS, ATT_T), lambda b, h: (b, h, 0, 0, 0))],
        out_specs=pl.BlockSpec((1, seq, MLA_V), lambda b, h: (b, 0, h)),
        out_shape=jax.ShapeDtypeStruct((bsz, seq, HEADS * MLA_V), BF16),
        scratch_shapes=[pltpu.VMEM((ATT_T, ATT_T), F32), pltpu.VMEM((ATT_T, ATT_T), BF16),
                        pltpu.VMEM((1, ATT_T), F32), pltpu.VMEM((nq, 1, ATT_T), F32),
                        pltpu.VMEM((nq, V_ROWS, ATT_T), F32)],
        compiler_params=pltpu.CompilerParams(dimension_semantics=("parallel", "parallel"),
                                             vmem_limit_bytes=VMEM_LIMIT),
        name="attn",
    )(qt, k, vt)


def _mlstm_head_out(num_ref, dst_ref, o_ref, hn_ref):
    out_gain = (0.5 * jnp.tanh(0.5 * o_ref[...].astype(F32)) + 0.5) * hn_ref[...]
    heads = []
    for h in range(HEADS):
        hv = slice(h * MLSTM_DV, (h + 1) * MLSTM_DV)
        num = num_ref[:, hv].astype(F32)
        d = dst_ref[:, IGATE_LANE + h:IGATE_LANE + h + 1]
        scale = lax.rsqrt(jnp.mean(num * num, axis=-1, keepdims=True) + NORM_EPS * d * d)
        heads.append((num * scale * out_gain[:, hv]).astype(BF16))
    return jnp.concatenate(heads, axis=1)


def _merge_kernel(ya_ref, gate_ref, num_ref, dst_ref, o_ref, hn_ref, yc_ref, x_ref, wb_ref, wo_ref, out_ref):
    g = gate_ref[...]
    merged = g[:, :D_MODEL].astype(F32) * _dot(ya_ref[...], wb_ref[0])
    y_b = _mlstm_head_out(num_ref, dst_ref, o_ref, hn_ref)
    merged += g[:, D_MODEL:2 * D_MODEL].astype(F32) * _dot(y_b, wb_ref[1])
    merged += g[:, 2 * D_MODEL:].astype(F32) * _dot(yc_ref[...], wb_ref[2])
    out_ref[...] = x_ref[...] + _dot(merged.astype(BF16), wo_ref[...])


def _merge(y_a, main, num, dstat, head_norm, y_c, x2, w_branch, w_out, layer, tm):
    t = x2.shape[0]
    col = lambda c: pl.BlockSpec((tm, D_MODEL), lambda i: (i, c))
    return pl.pallas_call(
        _merge_kernel,
        grid=(t // tm,),
        in_specs=[col(0),
                  pl.BlockSpec((tm, N_BRANCH * D_MODEL), lambda i: (i, GATE_COL0 // (N_BRANCH * D_MODEL))),
                  col(0), pl.BlockSpec((tm, LANES), lambda i: (i, 0)), col(COL_MO // D_MODEL),
                  pl.BlockSpec((1, D_MODEL), lambda i: (0, 0)),
                  col(0), col(0),
                  pl.BlockSpec((None, N_BRANCH, D_MODEL, D_MODEL), lambda i: (layer, 0, 0, 0),
                               pipeline_mode=pl.Buffered(1)),
                  pl.BlockSpec((None, D_MODEL, D_MODEL), lambda i: (layer, 0, 0), pipeline_mode=pl.Buffered(1))],
        out_specs=col(0),
        out_shape=jax.ShapeDtypeStruct((t, D_MODEL), F32),
        compiler_params=pltpu.CompilerParams(dimension_semantics=("parallel",),
                                             vmem_limit_bytes=VMEM_LIMIT),
        name="merge",
    )(y_a, main, num, dstat, main, head_norm, y_c, x2, w_branch, w_out)


def _mlp_kernel(x_ref, g_ref, wu_ref, wd_ref, out_ref, *, n_chunk):
    x = x_ref[...]
    h = _rms(x, g_ref[...]).astype(BF16)
    ck = D_FF // n_chunk
    acc = x
    for c in range(n_chunk):
        u = jnp.maximum(_dot(h, wu_ref[:, c * ck:(c + 1) * ck]), 0.0)
        acc = acc + _dot((u * u).astype(BF16), wd_ref[c * ck:(c + 1) * ck, :])
    out_ref[...] = acc


def _mlp(x2, g, w_up, w_down, layer, tm):
    t = x2.shape[0]
    kern = functools.partial(_mlp_kernel, n_chunk=4)
    return pl.pallas_call(
        kern,
        grid=(t // tm,),
        in_specs=[pl.BlockSpec((tm, D_MODEL), lambda i: (i, 0)),
                  pl.BlockSpec((1, D_MODEL), lambda i: (0, 0)),
                  pl.BlockSpec((None, D_MODEL, D_FF), lambda i: (layer, 0, 0), pipeline_mode=pl.Buffered(1)),
                  pl.BlockSpec((None, D_FF, D_MODEL), lambda i: (layer, 0, 0), pipeline_mode=pl.Buffered(1))],
        out_specs=pl.BlockSpec((tm, D_MODEL), lambda i: (i, 0)),
        out_shape=jax.ShapeDtypeStruct((t, D_MODEL), F32),
        compiler_params=pltpu.CompilerParams(dimension_semantics=("parallel",),
                                             vmem_limit_bytes=VMEM_LIMIT),
        name="mlp",
    )(x2, g, w_up, w_down)


def _regroup_w_in(w_in):
    i0 = CONV_COLS + GATE_COL0
    cq0 = i0 + 2 * HEADS
    g0 = cq0 + Q_LORA + KV_LORA + MLA_ROPE
    pad = jnp.zeros(w_in.shape[:2] + (TAIL_COLS - (Q_LORA + KV_LORA + MLA_ROPE + 2 * HEADS),), w_in.dtype)
    w_main = jnp.concatenate([w_in[..., :i0], w_in[..., g0:]], axis=-1).astype(BF16)
    w_tail = jnp.concatenate([w_in[..., cq0:g0], w_in[..., i0:cq0], pad], axis=-1).astype(BF16)
    return w_main, w_tail


def _heads_last_split(w, first, second):
    k = w.shape[0]
    w3 = w.reshape(k, HEADS, first + second)
    return jnp.concatenate([w3[:, :, :first].reshape(k, HEADS * first),
                            w3[:, :, first:].reshape(k, HEADS * second)], axis=1).astype(BF16)


def _score_bound(q_gain, k_gain):
    def sq_norm(g):
        g = g.astype(F32)
        return MLA_NOPE * jnp.max(g[:MLA_NOPE] ** 2) + MLA_ROPE * jnp.max(g[MLA_NOPE:] ** 2)
    return jnp.sqrt(sq_norm(q_gain) * sq_norm(k_gain)) * (Q_SCALE * BOUND_SLACK)


def _pick(n, candidates):
    for c in candidates:
        if n % c == 0:
            return c
    raise ValueError(f"no tile size for {n}")


def kernel(x, positions, mix_norm, w_in, conv_w, mlstm_igate_bias, mlstm_fgate_bias, mlstm_head_norm,
           mla_q_a_norm, mla_w_uq, mla_kv_a_norm, mla_w_ukv, mla_q_norm, mla_k_norm, w_branch, w_out,
           mlp_norm, w_up, w_down):
    bsz, seq, _ = x.shape
    depth = w_in.shape[0]
    t = bsz * seq
    assert seq % ATT_T == 0
    n_chunk = _pick(seq // CHUNK, tuple(g for g in (8, 4, 2, 1) if g <= MLSTM_GROUP))
    tm_proj = _pick(seq, (1024, 512, 256, 128))
    tm = _pick(seq, (512, 256, 128))
    tm_mlp = _pick(t, (1024, 512, 256, 128))

    x2 = x.reshape(t, D_MODEL)
    cs, cst = _rope_table(positions, ATT_T)
    row = lambda a: a.reshape(1, -1).astype(F32)
    w_main, w_tail = _regroup_w_in(w_in)
    w_branch_b, w_out_b, w_up_b, w_down_b = (w.astype(BF16) for w in (w_branch, w_out, w_up, w_down))

    for l in range(depth):
        gate_bias = jnp.zeros((1, LANES), F32)
        gate_bias = gate_bias.at[0, IGATE_LANE:IGATE_LANE + HEADS].set(mlstm_igate_bias[l])
        gate_bias = gate_bias.at[0, FGATE_LANE:FGATE_LANE + HEADS].set(mlstm_fgate_bias[l])
        w_q = _heads_last_split(mla_w_uq[l], MLA_NOPE, MLA_ROPE)
        w_kv = _heads_last_split(mla_w_ukv[l], MLA_NOPE, MLA_V)

        y_a, main, tail = _in_proj(x2, row(mix_norm[l]), w_main, w_tail, conv_w[l].astype(F32), l, seq, tm_proj)
        num, dstat = _mlstm(main, tail, gate_bias, bsz, seq, n_chunk)
        qt, k, vt = _mla_prep(tail, cs, cst, row(mla_q_a_norm[l]), w_q.T, row(mla_kv_a_norm[l]),
                              w_kv[:, :HEADS * MLA_NOPE], w_kv[:, HEADS * MLA_NOPE:].T,
                              mla_q_norm[l].reshape(-1, 1).astype(F32) * Q_SCALE, row(mla_k_norm[l]), bsz, seq)
        y_c = lax.cond(_score_bound(mla_q_norm[l], mla_k_norm[l]) <= SCORE_BOUND,
                       _attention_bounded, _attention, qt, k, vt).reshape(t, D_MODEL)
        x2 = _merge(y_a, main, num, dstat, row(mlstm_head_norm[l]), y_c, x2, w_branch_b, w_out_b, l, tm)
        x2 = _mlp(x2, row(mlp_norm[l]), w_up_b, w_down_b, l, tm_mlp)
    return x2.reshape(bsz, seq, D_MODEL)
```

```python
import functools

import jax
import jax.numpy as jnp
import numpy as np
from jax import lax
from jax.experimental import pallas as pl
from jax.experimental.pallas import tpu as pltpu

F32 = jnp.float32
BF16 = jnp.bfloat16

D_MODEL = 1024
N_BRANCH = 3
CONV_WIDTH = 3
HEADS = 8
MLSTM_DK = 64
MLSTM_DV = 128
CHUNK = 128
GATE_CAP = 15.0
MLA_NOPE = 128
MLA_ROPE = 64
MLA_V = 128
Q_LORA = 256
KV_LORA = 128
ROPE_BASE = 10000.0
D_FF = 4 * D_MODEL
NORM_EPS = 1e-6
QK_PAD = 256

CONV_COLS = 3 * D_MODEL
PROJ_COLS = 9 * D_MODEL
MAIN_COLS = PROJ_COLS - CONV_COLS
TAIL_COLS = 512
COL_MQ = 0
COL_MK = COL_MQ + HEADS * MLSTM_DK
COL_MV = COL_MK + HEADS * MLSTM_DK
COL_MO = COL_MV + HEADS * MLSTM_DV
GATE_COL0 = COL_MO + HEADS * MLSTM_DV
LANES = 128
TAIL_KROPE = Q_LORA + KV_LORA
IGATE_LANE = MLA_ROPE
FGATE_LANE = MLA_ROPE + HEADS

IN_PROJ_TN = 3 * D_MODEL
IN_PROJ_CHUNK = 1536
CONV_CHUNK = 256
MLSTM_GROUP = 8
ATT_T = 512
LOG2E = 1.4426950408889634
Q_SCALE = (MLA_NOPE + MLA_ROPE) ** -0.5 * LOG2E
MASK_VALUE = -1e30
SCORE_BOUND = 70.0
SCORE_SHIFT = 48.0
BOUND_SLACK = 1.02
ATT_UNROLL = 8
ATT_UNROLL_BOUNDED = 24

VMEM_LIMIT = 56 * 1024 * 1024


def _rms(x, g):
    return x * lax.rsqrt(jnp.mean(x * x, axis=-1, keepdims=True) + NORM_EPS) * g


def _dot(a, b):
    return jnp.dot(a, b, preferred_element_type=F32)


def _dot_nt(a, b):
    return lax.dot_general(a, b, (((1,), (1,)), ((), ())), preferred_element_type=F32)


def _rope_table_kernel(pos_ref, freq_ref, cs_ref, cst_ref):
    ang = freq_ref[...] * pos_ref[...].astype(F32)
    c, s = jnp.cos(ang), jnp.sin(ang)
    cst_ref[...] = jnp.concatenate([c, s], axis=0)
    cs_ref[...] = jnp.concatenate([c, c, -s, s], axis=0).T


def _rope_table(positions, tm):
    t = positions.size
    half = MLA_ROPE // 2
    inv_freq = (np.float32(ROPE_BASE) ** (-np.arange(0, MLA_ROPE, 2, dtype=np.float32) / np.float32(MLA_ROPE)))
    inv_freq = jnp.asarray(inv_freq.astype(np.float32))
    return pl.pallas_call(
        _rope_table_kernel,
        grid=(t // tm,),
        in_specs=[pl.BlockSpec((1, tm), lambda i: (0, i)),
                  pl.BlockSpec((half, 1), lambda i: (0, 0))],
        out_specs=[pl.BlockSpec((tm, 2 * MLA_ROPE), lambda i: (i, 0)),
                   pl.BlockSpec((MLA_ROPE, tm), lambda i: (0, i))],
        out_shape=[jax.ShapeDtypeStruct((t, 2 * MLA_ROPE), F32),
                   jax.ShapeDtypeStruct((MLA_ROPE, t), F32)],
        name="rope_table",
    )(positions.reshape(1, t), inv_freq.reshape(half, 1))


def _in_proj_kernel(x_ref, g_ref, w_ref, wt_ref, cw_ref, ya_ref, main_ref, tail_ref, xn_ref, zprev_ref, *,
                    tiles_per_seq):
    i, j = pl.program_id(0), pl.program_id(1)
    tm = x_ref.shape[0]

    @pl.when(jnp.logical_and(j == 0, i % tiles_per_seq == 0))
    def _():
        zprev_ref[...] = jnp.zeros_like(zprev_ref)

    @pl.when(j == 0)
    def _():
        xn = _rms(x_ref[...], g_ref[...]).astype(BF16)
        xn_ref[...] = xn
        tail_ref[...] = _dot(xn, wt_ref[...])
        row = lax.broadcasted_iota(jnp.int32, (tm, CONV_CHUNK), 0)
        for c0 in range(0, D_MODEL, CONV_CHUNK):
            c = slice(c0, c0 + CONV_CHUNK)
            gate_b, gate_c, u = (_dot(xn, w_ref[:, n * D_MODEL + c0:n * D_MODEL + c0 + CONV_CHUNK]) for n in range(3))
            z = gate_c * u
            p1, p2 = zprev_ref[7:8, c], zprev_ref[6:7, c]
            z1 = jnp.where(row == 0, p1, pltpu.roll(z, 1, 0))
            z2 = jnp.where(row == 0, p2, jnp.where(row == 1, p1, pltpu.roll(z, 2, 0)))
            zprev_ref[:, c] = z[tm - 8:, :]
            ya_ref[:, c] = (gate_b * (cw_ref[0:1, c] * z2 + cw_ref[1:2, c] * z1 + cw_ref[2:3, c] * z)).astype(BF16)

    cols = [slice(c, c + IN_PROJ_CHUNK) for c in range(0, w_ref.shape[1], IN_PROJ_CHUNK)]

    @pl.when(j == 1)
    def _():
        for c in cols:
            main_ref[:, c] = _dot(xn_ref[...], w_ref[:, c]).astype(BF16)

    @pl.when(j == 2)
    def _():
        for c in cols:
            main_ref[:, c] = jax.nn.sigmoid(_dot(xn_ref[...], w_ref[:, c])).astype(BF16)


def _in_proj(x2, g, w_main, w_tail, conv_w, layer, seq, tm):
    t = x2.shape[0]
    tn = IN_PROJ_TN
    assert CONV_COLS == tn and GATE_COL0 == tn and MAIN_COLS == 2 * tn
    kern = functools.partial(_in_proj_kernel, tiles_per_seq=seq // tm)
    return pl.pallas_call(
        kern,
        grid=(t // tm, PROJ_COLS // tn),
        in_specs=[pl.BlockSpec((tm, D_MODEL), lambda i, j: (i, 0)),
                  pl.BlockSpec((1, D_MODEL), lambda i, j: (0, 0)),
                  pl.BlockSpec((None, D_MODEL, tn), lambda i, j: (layer, 0, j)),
                  pl.BlockSpec((None, D_MODEL, TAIL_COLS), lambda i, j: (layer, 0, 0)),
                  pl.BlockSpec((CONV_WIDTH, D_MODEL), lambda i, j: (0, 0))],
        out_specs=[pl.BlockSpec((tm, D_MODEL), lambda i, j: (i, 0)),
                   pl.BlockSpec((tm, tn), lambda i, j: (i, jnp.maximum(j - 1, 0))),
                   pl.BlockSpec((tm, TAIL_COLS), lambda i, j: (i, 0))],
        out_shape=[jax.ShapeDtypeStruct((t, D_MODEL), BF16),
                   jax.ShapeDtypeStruct((t, MAIN_COLS), BF16),
                   jax.ShapeDtypeStruct((t, TAIL_COLS), F32)],
        scratch_shapes=[pltpu.VMEM((tm, D_MODEL), BF16), pltpu.VMEM((8, D_MODEL), F32)],
        compiler_params=pltpu.CompilerParams(
            dimension_semantics=("arbitrary", "arbitrary"), vmem_limit_bytes=VMEM_LIMIT),
        name="in_proj",
    )(x2, g, w_main, w_tail, conv_w)


def _time_scan(x, op, row):
    s = 1
    while s < CHUNK:
        x = jnp.where(row >= s, op(x, pltpu.roll(x, s, 0)), x)
        s *= 2
    return x


def _mlstm_kernel(q_ref, k_ref, v_ref, gate_ref, bias_ref, num_ref, dst_ref, c_ref, m_ref, *, n_chunk):
    L = CHUNK

    @pl.when(pl.program_id(1) == 0)
    def _():
        c_ref[...] = jnp.zeros_like(c_ref)
        m_ref[...] = jnp.zeros_like(m_ref)

    row = lax.broadcasted_iota(jnp.int32, (L, LANES), 0)
    lane = lax.broadcasted_iota(jnp.int32, (L, LANES), 1)
    gate_lanes = jnp.logical_and(lane >= IGATE_LANE, lane < FGATE_LANE + HEADS)
    tri = (lax.broadcasted_iota(jnp.int32, (L, L), 0) >= lax.broadcasted_iota(jnp.int32, (L, L), 1))
    lane_v = lax.broadcasted_iota(jnp.int32, (L, MLSTM_DV), 1)
    m_prev = m_ref[...]
    c_state = [c_ref[h] for h in range(HEADS)]

    for g in range(n_chunk):
        rows = slice(g * L, (g + 1) * L)
        pre = jnp.where(gate_lanes, gate_ref[rows, :] + bias_ref[...], 0.0)
        capped = GATE_CAP * jnp.tanh(pre * (1.0 / GATE_CAP))
        log_i = capped * LOG2E
        log_f = (jnp.minimum(capped, 0.0) - jnp.log1p(jnp.exp(-jnp.abs(capped)))) * LOG2E
        log_f = pltpu.roll(log_f, LANES - HEADS, 1)
        b = _time_scan(log_f, jnp.add, row)
        b_last = b[L - 1:L, :]
        rowd = log_i - b
        big_m = jnp.maximum(_time_scan(rowd, jnp.maximum, row), m_prev)
        e_neg_m = jnp.exp2(-(b + big_m))
        a = b_last + rowd
        m_loc = jnp.max(a, axis=0, keepdims=True)
        m_new = jnp.maximum(b_last + m_prev, m_loc)
        s_old = jnp.exp2(b_last + m_prev - m_new)
        w = jnp.exp2(a - m_new)
        m_prev_g, m_prev = m_prev, m_new
        rowd_t = rowd.T
        w_t = w.T
        q_all = q_ref[rows, :] * (MLSTM_DK ** -0.5)
        k_t = k_ref[rows, :].astype(F32).T

        hk = [slice(h * MLSTM_DK, (h + 1) * MLSTM_DK) for h in range(HEADS)]
        hv = [slice(h * MLSTM_DV, (h + 1) * MLSTM_DV) for h in range(HEADS)]
        gl = [slice(IGATE_LANE + h, IGATE_LANE + h + 1) for h in range(HEADS)]
        v_aug = [jnp.concatenate([v_ref[rows, hv[h]], (lane_v == IGATE_LANE + h).astype(BF16)], axis=1)
                 for h in range(HEADS)]
        scores = [_dot(q_all[:, hk[h]], k_t[hk[h], :].astype(BF16)) for h in range(HEADS)]
        c_loc = [_dot((k_t[hk[h], :] * w_t[gl[h], :]).astype(BF16), v_aug[h]) for h in range(HEADS)]
        big_m_b = [jnp.broadcast_to(big_m[:, gl[h]], (L, L)) for h in range(HEADS)]
        dmat = [jnp.where(tri, jnp.exp2(rowd_t[gl[h], :] - big_m_b[h]), 0.0) for h in range(HEADS)]
        q_state = [(q_all[:, hk[h]].astype(F32)
                    * jnp.exp2(m_prev_g[:, gl[h]] - big_m_b[h][:, :MLSTM_DK])).astype(BF16) for h in range(HEADS)]
        num_aug = []
        for h in range(HEADS):
            lhs = jnp.concatenate([(scores[h] * dmat[h]).astype(BF16), q_state[h],
                                   jnp.zeros((L, L - MLSTM_DK), BF16)], axis=1)
            rhs = jnp.concatenate([v_aug[h], c_state[h].astype(BF16),
                                   jnp.zeros((L - MLSTM_DK, 2 * MLSTM_DV), BF16)], axis=0)
            num_aug.append(_dot(lhs, rhs))
        for h in range(HEADS):
            num_ref[rows, hv[h]] = num_aug[h][:, :MLSTM_DV].astype(num_ref.dtype)
            c_state[h] = s_old[:, gl[h]] * c_state[h] + c_loc[h]
        den = functools.reduce(jnp.add, [n[:, MLSTM_DV:] for n in num_aug])
        dst_ref[rows, :] = jnp.maximum(jnp.abs(den), e_neg_m)

    m_ref[...] = m_prev
    for h in range(HEADS):
        c_ref[h] = c_state[h]


def _mlstm(main, tail, gate_bias, bsz, seq, n_chunk):
    t = main.shape[0]
    rows = n_chunk * CHUNK
    nc = seq // rows
    row = lambda b, c: b * nc + c
    col_block = lambda col0, width: pl.BlockSpec((rows, width), lambda b, c: (row(b, c), col0 // width))
    return pl.pallas_call(
        functools.partial(_mlstm_kernel, n_chunk=n_chunk),
        grid=(bsz, nc),
        in_specs=[col_block(COL_MQ, HEADS * MLSTM_DK), col_block(COL_MK, HEADS * MLSTM_DK),
                  col_block(COL_MV, HEADS * MLSTM_DV),
                  col_block(TAIL_KROPE, LANES),
                  pl.BlockSpec((1, LANES), lambda b, c: (0, 0))],
        out_specs=[pl.BlockSpec((rows, D_MODEL), lambda b, c: (row(b, c), 0)),
                   pl.BlockSpec((rows, LANES), lambda b, c: (row(b, c), 0))],
        out_shape=[jax.ShapeDtypeStruct((t, D_MODEL), BF16), jax.ShapeDtypeStruct((t, LANES), F32)],
        scratch_shapes=[pltpu.VMEM((HEADS, MLSTM_DK, 2 * MLSTM_DV), F32),
                        pltpu.VMEM((1, LANES), F32)],
        compiler_params=pltpu.CompilerParams(dimension_semantics=("parallel", "arbitrary")),
        name="mlstm",
    )(main, main, main, tail, gate_bias)


def _rms0(x, g_col):
    return x * lax.rsqrt(jnp.mean(x * x, axis=0, keepdims=True) + NORM_EPS) * g_col


def _rope(x, cs):
    half = MLA_ROPE // 2
    rot = jnp.concatenate([x[:, half:], x[:, :half]], axis=-1)
    return x * cs[:, :MLA_ROPE] + rot * cs[:, MLA_ROPE:]


def _mla_prep_kernel(tail_ref, cs_ref, cst_ref, qa_ref, wqt_ref, kva_ref, wk_ref, wvt_ref, qn_ref, kn_ref,
                     qt_out, k_out, vt_out):
    tm = ATT_T
    half = MLA_ROPE // 2
    qn_g, qp_g = qn_ref[:MLA_NOPE, :], qn_ref[MLA_NOPE:, :]
    kn_g, kp_g = kn_ref[:, :MLA_NOPE], kn_ref[:, MLA_NOPE:]
    pe0 = HEADS * MLA_NOPE
    pad = QK_PAD - MLA_NOPE - MLA_ROPE
    q_pad = jnp.where(lax.broadcasted_iota(jnp.int32, (pad, tm), 0) == 0, -SCORE_SHIFT, 0.0).astype(BF16)
    k_pad = (lax.broadcasted_iota(jnp.int32, (tm, pad), 1) == 0).astype(BF16)
    for s in range(tail_ref.shape[0] // tm):
        rows = slice(s * tm, (s + 1) * tm)
        cq = _rms(tail_ref[rows, :Q_LORA], qa_ref[...]).astype(BF16)
        ckv = _rms(tail_ref[rows, Q_LORA:Q_LORA + KV_LORA], kva_ref[...]).astype(BF16)
        qft = _dot_nt(wqt_ref[...], cq)
        kf = _dot(ckv, wk_ref[...])
        vft = _dot_nt(wvt_ref[...], ckv)
        cos_t, sin_t = cst_ref[:half, rows], cst_ref[half:, rows]
        k_pe = _rope(_rms(tail_ref[rows, TAIL_KROPE:TAIL_KROPE + MLA_ROPE], kp_g), cs_ref[rows, :]).astype(BF16)
        for h in range(HEADS):
            q_nope = _rms0(qft[h * MLA_NOPE:(h + 1) * MLA_NOPE], qn_g)
            q_pe = _rms0(qft[pe0 + h * MLA_ROPE:pe0 + (h + 1) * MLA_ROPE], qp_g)
            x1, x2 = q_pe[:half], q_pe[half:]
            q_rot = jnp.concatenate([x1 * cos_t - x2 * sin_t, x1 * sin_t + x2 * cos_t], axis=0)
            qt_out[0, h, s] = jnp.concatenate([q_nope.astype(BF16), q_rot.astype(BF16), q_pad], axis=0)
            k_nope = _rms(kf[:, h * MLA_NOPE:(h + 1) * MLA_NOPE], kn_g)
            k_out[0, h, rows, :] = jnp.concatenate([k_nope.astype(BF16), k_pe, k_pad], axis=-1)
            vt_out[0, h, s] = vft[h * MLA_V:(h + 1) * MLA_V].astype(BF16)


def _mla_prep(tail, cs, cst, qa, wqt, kva, wk, wvt, qn_col, kn, bsz, seq):
    tiles = 2 if (seq // ATT_T) % 2 == 0 else 1
    tm = tiles * ATT_T
    nt = seq // tm
    full = lambda shape: pl.BlockSpec(shape, lambda i: (0,) * len(shape))
    blocked = lambda rows: pl.BlockSpec((1, HEADS, tiles, rows, ATT_T), lambda i: (i // nt, 0, i % nt, 0, 0))
    return pl.pallas_call(
        _mla_prep_kernel,
        grid=(bsz * nt,),
        in_specs=[pl.BlockSpec((tm, TAIL_COLS), lambda i: (i, 0)),
                  pl.BlockSpec((tm, 2 * MLA_ROPE), lambda i: (i, 0)),
                  pl.BlockSpec((MLA_ROPE, tm), lambda i: (0, i)),
                  full((1, Q_LORA)), full(wqt.shape), full((1, KV_LORA)), full(wk.shape), full(wvt.shape),
                  full((MLA_NOPE + MLA_ROPE, 1)), full((1, MLA_NOPE + MLA_ROPE))],
        out_specs=[blocked(QK_PAD),
                   pl.BlockSpec((1, HEADS, tm, QK_PAD), lambda i: (i // nt, 0, i % nt, 0)),
                   blocked(MLA_V)],
        out_shape=[jax.ShapeDtypeStruct((bsz, HEADS, seq // ATT_T, QK_PAD, ATT_T), BF16),
                   jax.ShapeDtypeStruct((bsz, HEADS, seq, QK_PAD), BF16),
                   jax.ShapeDtypeStruct((bsz, HEADS, seq // ATT_T, MLA_V, ATT_T), BF16)],
        compiler_params=pltpu.CompilerParams(dimension_semantics=("parallel",),
                                             vmem_limit_bytes=VMEM_LIMIT),
        name="mla_prep",
    )(tail, cs, cst, qa, wqt, kva, wk, wvt, qn_col, kn)


def _attn_kernel(qt_ref, k_ref, vt_ref, o_ref, s_scr, p_scr, a_scr, m_all, l_all, acc_all, *, nq):
    t = ATT_T

    def qk(qi, j):
        k0 = pl.multiple_of(j * t, t)
        return _dot(k_ref[0, 0, pl.ds(k0, t), :], qt_ref[0, 0, qi])

    def accumulate(tile, alpha, p):
        q, j = tile
        acc_all[q] = alpha * acc_all[q] + _dot(vt_ref[0, 0, j], p)

    def stage(prev, cur, nxt, s_cur, p_prev, alpha_prev, masked):
        accumulate(prev, alpha_prev, p_prev)
        qi = cur[0]
        if masked:
            keep = lax.broadcasted_iota(jnp.int32, (t, t), 0) <= lax.broadcasted_iota(jnp.int32, (t, t), 1)
            s_cur = jnp.where(keep, s_cur, MASK_VALUE)
        m_old = m_all[qi]
        m_new = jnp.maximum(m_old, jnp.max(s_cur, axis=0, keepdims=True))
        m_all[qi] = m_new
        alpha = jnp.exp2(m_old - m_new)
        p = jnp.exp2((s_cur - m_new).astype(BF16))
        l_all[qi] = alpha * l_all[qi] + jnp.sum(p.astype(F32), axis=0, keepdims=True)
        s_next = qk(jnp.minimum(nxt[0], nq - 1), jnp.minimum(nxt[1], nq - 1))
        return s_next, p, alpha

    def advance_lower(qi, j):
        wrap = j + 1 == qi
        return jnp.where(wrap, qi + 1, qi), jnp.where(wrap, 0, j + 1)

    def advance_diag(qi, j):
        return qi + 1, j + 1

    def run_phase(n_steps, first, advance, masked):
        if n_steps == 0:
            return
        s_scr[...] = qk(*first)
        p_scr[...] = jnp.zeros_like(p_scr)
        a_scr[...] = jnp.ones_like(a_scr)

        def steps(count, prev, cur):
            s_cur, p_prev, alpha_prev = s_scr[...], p_scr[...], a_scr[...]
            for _ in range(count):
                nxt = advance(*cur)
                s_cur, p_prev, alpha_prev = stage(prev, cur, nxt, s_cur, p_prev, alpha_prev, masked)
                prev, cur = cur, nxt
            s_scr[...] = s_cur
            p_scr[...] = p_prev
            a_scr[...] = alpha_prev
            return prev, cur

        first = (jnp.int32(first[0]), jnp.int32(first[1]))
        prev, cur = lax.fori_loop(0, n_steps // ATT_UNROLL, lambda _, c: steps(ATT_UNROLL, *c), (first, first))
        if n_steps % ATT_UNROLL:
            prev, cur = steps(n_steps % ATT_UNROLL, prev, cur)
        accumulate(prev, a_scr[...], p_scr[...])

    m_all[...] = jnp.full_like(m_all, MASK_VALUE)
    l_all[...] = jnp.zeros_like(l_all)
    acc_all[...] = jnp.zeros_like(acc_all)
    run_phase(nq * (nq - 1) // 2, (1, 0), advance_lower, False)
    run_phase(nq, (0, 0), advance_diag, True)
    for qi in range(nq):
        o_ref[0, qi * t:(qi + 1) * t, :] = (acc_all[qi] / l_all[qi]).T.astype(o_ref.dtype)


def _attn_bounded_kernel(qt_ref, k_ref, vt_ref, o_ref, s_scr, p_scr, l_all, acc_all, *, nq):
    t = ATT_T

    def qk(qi, j):
        k0 = pl.multiple_of(j * t, t)
        return _dot(k_ref[0, 0, pl.ds(k0, t), :], qt_ref[0, 0, qi])

    def accumulate(tile, p):
        q, j = tile
        acc_all[q] += _dot(vt_ref[0, 0, j], p)

    def stage(prev, cur, nxt, s_cur, p_prev, masked):
        accumulate(prev, p_prev)
        p = jnp.exp2(s_cur)
        if masked:
            keep = lax.broadcasted_iota(jnp.int32, (t, t), 0) <= lax.broadcasted_iota(jnp.int32, (t, t), 1)
            p = jnp.where(keep, p, 0.0)
        l_all[cur[0]] += jnp.sum(p, axis=0, keepdims=True)
        s_next = qk(jnp.minimum(nxt[0], nq - 1), jnp.minimum(nxt[1], nq - 1))
        return s_next, p.astype(BF16)

    def advance_lower(qi, j):
        wrap = j + 1 == qi
        return jnp.where(wrap, qi + 1, qi), jnp.where(wrap, 0, j + 1)

    def advance_diag(qi, j):
        return qi + 1, j + 1

    def run_phase(n_steps, first, advance, masked):
        if n_steps == 0:
            return
        s_scr[...] = qk(*first)
        p_scr[...] = jnp.zeros_like(p_scr)

        def steps(count, prev, cur):
            s_cur, p_prev = s_scr[...], p_scr[...]
            for _ in range(count):
                nxt = advance(*cur)
                s_cur, p_prev = stage(prev, cur, nxt, s_cur, p_prev, masked)
                prev, cur = cur, nxt
            s_scr[...] = s_cur
            p_scr[...] = p_prev
            return prev, cur

        first = (jnp.int32(first[0]), jnp.int32(first[1]))
        unroll = ATT_UNROLL_BOUNDED
        prev, cur = lax.fori_loop(0, n_steps // unroll, lambda _, c: steps(unroll, *c), (first, first))
        if n_steps % unroll:
            prev, cur = steps(n_steps % unroll, prev, cur)
        accumulate(prev, p_scr[...])

    l_all[...] = jnp.zeros_like(l_all)
    acc_all[...] = jnp.zeros_like(acc_all)
    run_phase(nq * (nq - 1) // 2, (1, 0), advance_lower, False)
    run_phase(nq, (0, 0), advance_diag, True)
    for qi in range(nq):
        o_ref[0, qi * t:(qi + 1) * t, :] = (acc_all[qi] / l_all[qi]).T.astype(o_ref.dtype)


def _attention_bounded(qt, k, vt):
    bsz, _, seq, _ = k.shape
    nq = seq // ATT_T
    kern = functools.partial(_attn_bounded_kernel, nq=nq)
    return pl.pallas_call(
        kern,
        grid=(bsz, HEADS),
        in_specs=[pl.BlockSpec((1, 1, nq, QK_PAD, ATT_T), lambda b, h: (b, h, 0, 0, 0)),
                  pl.BlockSpec((1, 1, seq, QK_PAD), lambda b, h: (b, h, 0, 0)),
                  pl.BlockSpec((1, 1, nq, MLA_V, ATT_T), lambda b, h: (b, h, 0, 0, 0))],
        out_specs=pl.BlockSpec((1, seq, MLA_V), lambda b, h: (b, 0, h)),
        out_shape=jax.ShapeDtypeStruct((bsz, seq, HEADS * MLA_V), BF16),
        scratch_shapes=[pltpu.VMEM((ATT_T, ATT_T), F32), pltpu.VMEM((ATT_T, ATT_T), BF16),
                        pltpu.VMEM((nq, 1, ATT_T), F32), pltpu.VMEM((nq, MLA_V, ATT_T), F32)],
        compiler_params=pltpu.CompilerParams(dimension_semantics=("parallel", "parallel"),
                                             vmem_limit_bytes=VMEM_LIMIT),
        name="attn_bounded",
    )(qt, k, vt)


def _attention(qt, k, vt):
    bsz, _, seq, _ = k.shape
    nq = seq // ATT_T
    kern = functools.partial(_attn_kernel, nq=nq)
    return pl.pallas_call(
        kern,
        grid=(bsz, HEADS),
        in_specs=[pl.BlockSpec((1, 1, nq, QK_PAD, ATT_T), lambda b, h: (b, h, 0, 0, 0)),
                  pl.BlockSpec((1, 1, seq, QK_PAD), lambda b, h: (b, h, 0, 0)),
                  pl.BlockSpec((1, 1, nq, MLA_V, ATT_T), lambda b, h: (b, h, 0, 0, 0))],
        out_specs=pl.BlockSpec((1, seq, MLA_V), lambda b, h: (b, 0, h)),
        out_shape=jax.ShapeDtypeStruct((bsz, seq, HEADS * MLA_V), BF16),
        scratch_shapes=[pltpu.VMEM((ATT_T, ATT_T), F32), pltpu.VMEM((ATT_T, ATT_T), BF16),
                        pltpu.VMEM((1, ATT_T), F32), pltpu.VMEM((nq, 1, ATT_T), F32),
                        pltpu.VMEM((nq, 1, ATT_T), F32), pltpu.VMEM((nq, MLA_V, ATT_T), F32)],
        compiler_params=pltpu.CompilerParams(dimension_semantics=("parallel", "parallel"),
                                             vmem_limit_bytes=VMEM_LIMIT),
        name="attn",
    )(qt, k, vt)


def _mlstm_head_out(num_ref, dst_ref, o_ref, hn_ref):
    out_gain = (0.5 * jnp.tanh(0.5 * o_ref[...].astype(F32)) + 0.5) * hn_ref[...]
    heads = []
    for h in range(HEADS):
        hv = slice(h * MLSTM_DV, (h + 1) * MLSTM_DV)
        num = num_ref[:, hv].astype(F32)
        d = dst_ref[:, IGATE_LANE + h:IGATE_LANE + h + 1]
        scale = lax.rsqrt(jnp.mean(num * num, axis=-1, keepdims=True) + NORM_EPS * d * d)
        heads.append((num * scale * out_gain[:, hv]).astype(BF16))
    return jnp.concatenate(heads, axis=1)


def _merge_kernel(ya_ref, gate_ref, num_ref, dst_ref, o_ref, hn_ref, yc_ref, x_ref, wb_ref, wo_ref, out_ref):
    g = gate_ref[...]
    merged = g[:, :D_MODEL].astype(F32) * _dot(ya_ref[...], wb_ref[0])
    y_b = _mlstm_head_out(num_ref, dst_ref, o_ref, hn_ref)
    merged += g[:, D_MODEL:2 * D_MODEL].astype(F32) * _dot(y_b, wb_ref[1])
    merged += g[:, 2 * D_MODEL:].astype(F32) * _dot(yc_ref[...], wb_ref[2])
    out_ref[...] = x_ref[...] + _dot(merged.astype(BF16), wo_ref[...])


def _merge(y_a, main, num, dstat, head_norm, y_c, x2, w_branch, w_out, layer, tm):
    t = x2.shape[0]
    col = lambda c: pl.BlockSpec((tm, D_MODEL), lambda i: (i, c))
    return pl.pallas_call(
        _merge_kernel,
        grid=(t // tm,),
        in_specs=[col(0),
                  pl.BlockSpec((tm, N_BRANCH * D_MODEL), lambda i: (i, GATE_COL0 // (N_BRANCH * D_MODEL))),
                  col(0), pl.BlockSpec((tm, LANES), lambda i: (i, 0)), col(COL_MO // D_MODEL),
                  pl.BlockSpec((1, D_MODEL), lambda i: (0, 0)),
                  col(0), col(0),
                  pl.BlockSpec((None, N_BRANCH, D_MODEL, D_MODEL), lambda i: (layer, 0, 0, 0),
                               pipeline_mode=pl.Buffered(1)),
                  pl.BlockSpec((None, D_MODEL, D_MODEL), lambda i: (layer, 0, 0), pipeline_mode=pl.Buffered(1))],
        out_specs=col(0),
        out_shape=jax.ShapeDtypeStruct((t, D_MODEL), F32),
        compiler_params=pltpu.CompilerParams(dimension_semantics=("parallel",),
                                             vmem_limit_bytes=VMEM_LIMIT),
        name="merge",
    )(y_a, main, num, dstat, main, head_norm, y_c, x2, w_branch, w_out)


def _mlp_kernel(x_ref, g_ref, wu_ref, wd_ref, out_ref, *, n_chunk):
    x = x_ref[...]
    h = _rms(x, g_ref[...]).astype(BF16)
    ck = D_FF // n_chunk
    acc = x
    for c in range(n_chunk):
        u = jnp.maximum(_dot(h, wu_ref[:, c * ck:(c + 1) * ck]), 0.0)
        acc = acc + _dot((u * u).astype(BF16), wd_ref[c * ck:(c + 1) * ck, :])
    out_ref[...] = acc


def _mlp(x2, g, w_up, w_down, layer, tm):
    t = x2.shape[0]
    kern = functools.partial(_mlp_kernel, n_chunk=4)
    return pl.pallas_call(
        kern,
        grid=(t // tm,),
        in_specs=[pl.BlockSpec((tm, D_MODEL), lambda i: (i, 0)),
                  pl.BlockSpec((1, D_MODEL), lambda i: (0, 0)),
                  pl.BlockSpec((None, D_MODEL, D_FF), lambda i: (layer, 0, 0), pipeline_mode=pl.Buffered(1)),
                  pl.BlockSpec((None, D_FF, D_MODEL), lambda i: (layer, 0, 0), pipeline_mode=pl.Buffered(1))],
        out_specs=pl.BlockSpec((tm, D_MODEL), lambda i: (i, 0)),
        out_shape=jax.ShapeDtypeStruct((t, D_MODEL), F32),
        compiler_params=pltpu.CompilerParams(dimension_semantics=("parallel",),
                                             vmem_limit_bytes=VMEM_LIMIT),
        name="mlp",
    )(x2, g, w_up, w_down)


def _regroup_w_in(w_in):
    i0 = CONV_COLS + GATE_COL0
    cq0 = i0 + 2 * HEADS
    g0 = cq0 + Q_LORA + KV_LORA + MLA_ROPE
    pad = jnp.zeros(w_in.shape[:2] + (TAIL_COLS - (Q_LORA + KV_LORA + MLA_ROPE + 2 * HEADS),), w_in.dtype)
    w_main = jnp.concatenate([w_in[..., :i0], w_in[..., g0:]], axis=-1).astype(BF16)
    w_tail = jnp.concatenate([w_in[..., cq0:g0], w_in[..., i0:cq0], pad], axis=-1).astype(BF16)
    return w_main, w_tail


def _heads_last_split(w, first, second):
    k = w.shape[0]
    w3 = w.reshape(k, HEADS, first + second)
    return jnp.concatenate([w3[:, :, :first].reshape(k, HEADS * first),
                            w3[:, :, first:].reshape(k, HEADS * second)], axis=1).astype(BF16)


def _score_bound(q_gain, k_gain):
    def sq_norm(g):
        g = g.astype(F32)
        return MLA_NOPE * jnp.max(g[:MLA_NOPE] ** 2) + MLA_ROPE * jnp.max(g[MLA_NOPE:] ** 2)
    return jnp.sqrt(sq_norm(q_gain) * sq_norm(k_gain)) * (Q_SCALE * BOUND_SLACK)


def _pick(n, candidates):
    for c in candidates:
        if n % c == 0:
            return c
    raise ValueError(f"no tile size for {n}")


def kernel(x, positions, mix_norm, w_in, conv_w, mlstm_igate_bias, mlstm_fgate_bias, mlstm_head_norm,
           mla_q_a_norm, mla_w_uq, mla_kv_a_norm, mla_w_ukv, mla_q_norm, mla_k_norm, w_branch, w_out,
           mlp_norm, w_up, w_down):
    bsz, seq, _ = x.shape
    depth = w_in.shape[0]
    t = bsz * seq
    assert seq % ATT_T == 0
    n_chunk = _pick(seq // CHUNK, tuple(g for g in (8, 4, 2, 1) if g <= MLSTM_GROUP))
    tm_proj = _pick(seq, (1024, 512, 256, 128))
    tm = _pick(seq, (512, 256, 128))
    tm_mlp = _pick(t, (1024, 512, 256, 128))

    x2 = x.reshape(t, D_MODEL)
    cs, cst = _rope_table(positions, ATT_T)
    row = lambda a: a.reshape(1, -1).astype(F32)
    w_main, w_tail = _regroup_w_in(w_in)
    w_branch_b, w_out_b, w_up_b, w_down_b = (w.astype(BF16) for w in (w_branch, w_out, w_up, w_down))

    for l in range(depth):
        gate_bias = jnp.zeros((1, LANES), F32)
        gate_bias = gate_bias.at[0, IGATE_LANE:IGATE_LANE + HEADS].set(mlstm_igate_bias[l])
        gate_bias = gate_bias.at[0, FGATE_LANE:FGATE_LANE + HEADS].set(mlstm_fgate_bias[l])
        w_q = _heads_last_split(mla_w_uq[l], MLA_NOPE, MLA_ROPE)
        w_kv = _heads_last_split(mla_w_ukv[l], MLA_NOPE, MLA_V)

        y_a, main, tail = _in_proj(x2, row(mix_norm[l]), w_main, w_tail, conv_w[l].astype(F32), l, seq, tm_proj)
        num, dstat = _mlstm(main, tail, gate_bias, bsz, seq, n_chunk)
        qt, k, vt = _mla_prep(tail, cs, cst, row(mla_q_a_norm[l]), w_q.T, row(mla_kv_a_norm[l]),
                              w_kv[:, :HEADS * MLA_NOPE], w_kv[:, HEADS * MLA_NOPE:].T,
                              mla_q_norm[l].reshape(-1, 1).astype(F32) * Q_SCALE, row(mla_k_norm[l]), bsz, seq)
        y_c = lax.cond(_score_bound(mla_q_norm[l], mla_k_norm[l]) <= SCORE_BOUND,
                       _attention_bounded, _attention, qt, k, vt).reshape(t, D_MODEL)
        x2 = _merge(y_a, main, num, dstat, row(mlstm_head_norm[l]), y_c, x2, w_branch_b, w_out_b, l, tm)
        x2 = _mlp(x2, row(mlp_norm[l]), w_up_b, w_down_b, l, tm_mlp)
    return x2.reshape(bsz, seq, D_MODEL)
```

```python
import functools

import jax
import jax.numpy as jnp
import numpy as np
from jax import lax
from jax.experimental import pallas as pl
from jax.experimental.pallas import tpu as pltpu

F32 = jnp.float32
BF16 = jnp.bfloat16

D_MODEL = 1024
N_BRANCH = 3
CONV_WIDTH = 3
HEADS = 8
MLSTM_DK = 64
MLSTM_DV = 128
CHUNK = 128
GATE_CAP = 15.0
MLA_NOPE = 128
MLA_ROPE = 64
MLA_V = 128
Q_LORA = 256
KV_LORA = 128
ROPE_BASE = 10000.0
D_FF = 4 * D_MODEL
NORM_EPS = 1e-6
QK_PAD = 256

CONV_COLS = 3 * D_MODEL
PROJ_COLS = 9 * D_MODEL
MAIN_COLS = PROJ_COLS - CONV_COLS
TAIL_COLS = 512
COL_MQ = 0
COL_MK = COL_MQ + HEADS * MLSTM_DK
COL_MV = COL_MK + HEADS * MLSTM_DK
COL_MO = COL_MV + HEADS * MLSTM_DV
GATE_COL0 = COL_MO + HEADS * MLSTM_DV
LANES = 128
TAIL_KROPE = Q_LORA + KV_LORA
IGATE_LANE = MLA_ROPE
FGATE_LANE = MLA_ROPE + HEADS

IN_PROJ_TN = 3 * D_MODEL
IN_PROJ_CHUNK = 1536
CONV_CHUNK = 256
MLSTM_GROUP = 8
ATT_T = 512
LOG2E = 1.4426950408889634
Q_SCALE = (MLA_NOPE + MLA_ROPE) ** -0.5 * LOG2E
MASK_VALUE = -1e30
SCORE_BOUND = 70.0
SCORE_SHIFT = 48.0
BOUND_SLACK = 1.02
ATT_UNROLL = 8
ATT_UNROLL_BOUNDED = 24

VMEM_LIMIT = 56 * 1024 * 1024


def _rms(x, g):
    return x * lax.rsqrt(jnp.mean(x * x, axis=-1, keepdims=True) + NORM_EPS) * g


def _dot(a, b):
    return jnp.dot(a, b, preferred_element_type=F32)


def _dot_nt(a, b):
    return lax.dot_general(a, b, (((1,), (1,)), ((), ())), preferred_element_type=F32)


def _rope_table_kernel(pos_ref, freq_ref, cs_ref, cst_ref):
    ang = freq_ref[...] * pos_ref[...].astype(F32)
    c, s = jnp.cos(ang), jnp.sin(ang)
    cst_ref[...] = jnp.concatenate([c, s], axis=0)
    cs_ref[...] = jnp.concatenate([c, c, -s, s], axis=0).T


def _rope_table(positions, tm):
    t = positions.size
    half = MLA_ROPE // 2
    inv_freq = (np.float32(ROPE_BASE) ** (-np.arange(0, MLA_ROPE, 2, dtype=np.float32) / np.float32(MLA_ROPE)))
    inv_freq = jnp.asarray(inv_freq.astype(np.float32))
    return pl.pallas_call(
        _rope_table_kernel,
        grid=(t // tm,),
        in_specs=[pl.BlockSpec((1, tm), lambda i: (0, i)),
                  pl.BlockSpec((half, 1), lambda i: (0, 0))],
        out_specs=[pl.BlockSpec((tm, 2 * MLA_ROPE), lambda i: (i, 0)),
                   pl.BlockSpec((MLA_ROPE, tm), lambda i: (0, i))],
        out_shape=[jax.ShapeDtypeStruct((t, 2 * MLA_ROPE), F32),
                   jax.ShapeDtypeStruct((MLA_ROPE, t), F32)],
        name="rope_table",
    )(positions.reshape(1, t), inv_freq.reshape(half, 1))


def _in_proj_kernel(x_ref, g_ref, w_ref, wt_ref, cw_ref, ya_ref, main_ref, tail_ref, xn_ref, zprev_ref, *,
                    tiles_per_seq):
    i, j = pl.program_id(0), pl.program_id(1)
    tm = x_ref.shape[0]

    @pl.when(jnp.logical_and(j == 0, i % tiles_per_seq == 0))
    def _():
        zprev_ref[...] = jnp.zeros_like(zprev_ref)

    @pl.when(j == 0)
    def _():
        xn = _rms(x_ref[...], g_ref[...]).astype(BF16)
        xn_ref[...] = xn
        tail_ref[...] = _dot(xn, wt_ref[...])
        row = lax.broadcasted_iota(jnp.int32, (tm, CONV_CHUNK), 0)
        for c0 in range(0, D_MODEL, CONV_CHUNK):
            c = slice(c0, c0 + CONV_CHUNK)
            gate_b, gate_c, u = (_dot(xn, w_ref[:, n * D_MODEL + c0:n * D_MODEL + c0 + CONV_CHUNK]) for n in range(3))
            z = gate_c * u
            p1, p2 = zprev_ref[7:8, c], zprev_ref[6:7, c]
            z1 = jnp.where(row == 0, p1, pltpu.roll(z, 1, 0))
            z2 = jnp.where(row == 0, p2, jnp.where(row == 1, p1, pltpu.roll(z, 2, 0)))
            zprev_ref[:, c] = z[tm - 8:, :]
            ya_ref[:, c] = (gate_b * (cw_ref[0:1, c] * z2 + cw_ref[1:2, c] * z1 + cw_ref[2:3, c] * z)).astype(BF16)

    cols = [slice(c, c + IN_PROJ_CHUNK) for c in range(0, w_ref.shape[1], IN_PROJ_CHUNK)]

    @pl.when(j == 1)
    def _():
        for c in cols:
            main_ref[:, c] = _dot(xn_ref[...], w_ref[:, c]).astype(BF16)

    @pl.when(j == 2)
    def _():
        for c in cols:
            main_ref[:, c] = jax.nn.sigmoid(_dot(xn_ref[...], w_ref[:, c])).astype(BF16)


def _in_proj(x2, g, w_main, w_tail, conv_w, layer, seq, tm):
    t = x2.shape[0]
    tn = IN_PROJ_TN
    assert CONV_COLS == tn and GATE_COL0 == tn and MAIN_COLS == 2 * tn
    kern = functools.partial(_in_proj_kernel, tiles_per_seq=seq // tm)
    return pl.pallas_call(
        kern,
        grid=(t // tm, PROJ_COLS // tn),
        in_specs=[pl.BlockSpec((tm, D_MODEL), lambda i, j: (i, 0)),
                  pl.BlockSpec((1, D_MODEL), lambda i, j: (0, 0)),
                  pl.BlockSpec((None, D_MODEL, tn), lambda i, j: (layer, 0, j)),
                  pl.BlockSpec((None, D_MODEL, TAIL_COLS), lambda i, j: (layer, 0, 0)),
                  pl.BlockSpec((CONV_WIDTH, D_MODEL), lambda i, j: (0, 0))],
        out_specs=[pl.BlockSpec((tm, D_MODEL), lambda i, j: (i, 0)),
                   pl.BlockSpec((tm, tn), lambda i, j: (i, jnp.maximum(j - 1, 0))),
                   pl.BlockSpec((tm, TAIL_COLS), lambda i, j: (i, 0))],
        out_shape=[jax.ShapeDtypeStruct((t, D_MODEL), BF16),
                   jax.ShapeDtypeStruct((t, MAIN_COLS), BF16),
                   jax.ShapeDtypeStruct((t, TAIL_COLS), F32)],
        scratch_shapes=[pltpu.VMEM((tm, D_MODEL), BF16), pltpu.VMEM((8, D_MODEL), F32)],
        compiler_params=pltpu.CompilerParams(
            dimension_semantics=("arbitrary", "arbitrary"), vmem_limit_bytes=VMEM_LIMIT),
        name="in_proj",
    )(x2, g, w_main, w_tail, conv_w)


def _time_scan(x, op, row):
    s = 1
    while s < CHUNK:
        x = jnp.where(row >= s, op(x, pltpu.roll(x, s, 0)), x)
        s *= 2
    return x


def _mlstm_chunk(rows, q_ref, k_ref, v_ref, gate_ref, bias_ref, num_ref, dst_ref, c_state, m_prev):
    L = CHUNK
    row = lax.broadcasted_iota(jnp.int32, (L, LANES), 0)
    lane = lax.broadcasted_iota(jnp.int32, (L, LANES), 1)
    gate_lanes = jnp.logical_and(lane >= IGATE_LANE, lane < FGATE_LANE + HEADS)
    tri = (lax.broadcasted_iota(jnp.int32, (L, L), 0) >= lax.broadcasted_iota(jnp.int32, (L, L), 1))
    lane_v = lax.broadcasted_iota(jnp.int32, (L, MLSTM_DV), 1)
    c_state = list(c_state)
    pre = jnp.where(gate_lanes, gate_ref[rows, :] + bias_ref[...], 0.0)
    capped = GATE_CAP * jnp.tanh(pre * (1.0 / GATE_CAP))
    log_i = capped * LOG2E
    log_f = (jnp.minimum(capped, 0.0) - jnp.log1p(jnp.exp(-jnp.abs(capped)))) * LOG2E
    log_f = pltpu.roll(log_f, LANES - HEADS, 1)
    b = _time_scan(log_f, jnp.add, row)
    b_last = b[L - 1:L, :]
    rowd = log_i - b
    big_m = jnp.maximum(_time_scan(rowd, jnp.maximum, row), m_prev)
    e_neg_m = jnp.exp2(-(b + big_m))
    a = b_last + rowd
    m_loc = jnp.max(a, axis=0, keepdims=True)
    m_new = jnp.maximum(b_last + m_prev, m_loc)
    s_old = jnp.exp2(b_last + m_prev - m_new)
    w = jnp.exp2(a - m_new)
    rowd_t = rowd.T
    w_t = w.T
    q_all = q_ref[rows, :] * (MLSTM_DK ** -0.5)
    k_t = k_ref[rows, :].astype(F32).T

    hk = [slice(h * MLSTM_DK, (h + 1) * MLSTM_DK) for h in range(HEADS)]
    hv = [slice(h * MLSTM_DV, (h + 1) * MLSTM_DV) for h in range(HEADS)]
    gl = [slice(IGATE_LANE + h, IGATE_LANE + h + 1) for h in range(HEADS)]
    v_aug = [jnp.concatenate([v_ref[rows, hv[h]], (lane_v == IGATE_LANE + h).astype(BF16)], axis=1)
             for h in range(HEADS)]
    scores = [_dot(q_all[:, hk[h]], k_t[hk[h], :].astype(BF16)) for h in range(HEADS)]
    c_loc = [_dot((k_t[hk[h], :] * w_t[gl[h], :]).astype(BF16), v_aug[h]) for h in range(HEADS)]
    big_m_b = [jnp.broadcast_to(big_m[:, gl[h]], (L, L)) for h in range(HEADS)]
    dmat = [jnp.where(tri, jnp.exp2(rowd_t[gl[h], :] - big_m_b[h]), 0.0) for h in range(HEADS)]
    q_state = [(q_all[:, hk[h]].astype(F32)
                * jnp.exp2(m_prev[:, gl[h]] - big_m_b[h][:, :MLSTM_DK])).astype(BF16) for h in range(HEADS)]
    num_aug = []
    for h in range(HEADS):
        lhs = jnp.concatenate([(scores[h] * dmat[h]).astype(BF16), q_state[h],
                               jnp.zeros((L, L - MLSTM_DK), BF16)], axis=1)
        rhs = jnp.concatenate([v_aug[h], c_state[h].astype(BF16),
                               jnp.zeros((L - MLSTM_DK, 2 * MLSTM_DV), BF16)], axis=0)
        num_aug.append(_dot(lhs, rhs))
    for h in range(HEADS):
        num_ref[rows, hv[h]] = num_aug[h][:, :MLSTM_DV].astype(num_ref.dtype)
        c_state[h] = s_old[:, gl[h]] * c_state[h] + c_loc[h]
    den = functools.reduce(jnp.add, [n[:, MLSTM_DV:] for n in num_aug])
    dst_ref[rows, :] = jnp.maximum(jnp.abs(den), e_neg_m)
    return c_state, m_new


def _mlstm_kernel(q_ref, k_ref, v_ref, gate_ref, bias_ref, num_ref, dst_ref, c_ref, m_ref, *, n_chunk):
    @pl.when(pl.program_id(1) == 0)
    def _():
        c_ref[...] = jnp.zeros_like(c_ref)
        m_ref[...] = jnp.zeros_like(m_ref)

    m_prev = m_ref[...]
    c_state = [c_ref[h] for h in range(HEADS)]
    for g in range(n_chunk):
        c_state, m_prev = _mlstm_chunk(slice(g * CHUNK, (g + 1) * CHUNK), q_ref, k_ref, v_ref, gate_ref, bias_ref,
                                       num_ref, dst_ref, c_state, m_prev)
    m_ref[...] = m_prev
    for h in range(HEADS):
        c_ref[h] = c_state[h]


def _mlstm(main, tail, gate_bias, bsz, seq, n_chunk):
    t = main.shape[0]
    rows = n_chunk * CHUNK
    nc = seq // rows
    row = lambda b, c: b * nc + c
    col_block = lambda col0, width: pl.BlockSpec((rows, width), lambda b, c: (row(b, c), col0 // width))
    return pl.pallas_call(
        functools.partial(_mlstm_kernel, n_chunk=n_chunk),
        grid=(bsz, nc),
        in_specs=[col_block(COL_MQ, HEADS * MLSTM_DK), col_block(COL_MK, HEADS * MLSTM_DK),
                  col_block(COL_MV, HEADS * MLSTM_DV),
                  col_block(TAIL_KROPE, LANES),
                  pl.BlockSpec((1, LANES), lambda b, c: (0, 0))],
        out_specs=[pl.BlockSpec((rows, D_MODEL), lambda b, c: (row(b, c), 0)),
                   pl.BlockSpec((rows, LANES), lambda b, c: (row(b, c), 0))],
        out_shape=[jax.ShapeDtypeStruct((t, D_MODEL), BF16), jax.ShapeDtypeStruct((t, LANES), F32)],
        scratch_shapes=[pltpu.VMEM((HEADS, MLSTM_DK, 2 * MLSTM_DV), F32),
                        pltpu.VMEM((1, LANES), F32)],
        compiler_params=pltpu.CompilerParams(dimension_semantics=("parallel", "arbitrary")),
        name="mlstm",
    )(main, main, main, tail, gate_bias)


def _rms0(x, g_col):
    return x * lax.rsqrt(jnp.mean(x * x, axis=0, keepdims=True) + NORM_EPS) * g_col


def _rope(x, cs):
    half = MLA_ROPE // 2
    rot = jnp.concatenate([x[:, half:], x[:, :half]], axis=-1)
    return x * cs[:, :MLA_ROPE] + rot * cs[:, MLA_ROPE:]


def _mla_prep_kernel(tail_ref, cs_ref, cst_ref, qa_ref, wqt_ref, kva_ref, wk_ref, wvt_ref, qn_ref, kn_ref,
                     qt_out, k_out, vt_out):
    tm = ATT_T
    half = MLA_ROPE // 2
    qn_g, qp_g = qn_ref[:MLA_NOPE, :], qn_ref[MLA_NOPE:, :]
    kn_g, kp_g = kn_ref[:, :MLA_NOPE], kn_ref[:, MLA_NOPE:]
    pe0 = HEADS * MLA_NOPE
    pad = QK_PAD - MLA_NOPE - MLA_ROPE
    q_pad = jnp.where(lax.broadcasted_iota(jnp.int32, (pad, tm), 0) == 0, -SCORE_SHIFT, 0.0).astype(BF16)
    k_pad = (lax.broadcasted_iota(jnp.int32, (tm, pad), 1) == 0).astype(BF16)
    for s in range(tail_ref.shape[0] // tm):
        rows = slice(s * tm, (s + 1) * tm)
        cq = _rms(tail_ref[rows, :Q_LORA], qa_ref[...]).astype(BF16)
        ckv = _rms(tail_ref[rows, Q_LORA:Q_LORA + KV_LORA], kva_ref[...]).astype(BF16)
        qft = _dot_nt(wqt_ref[...], cq)
        kf = _dot(ckv, wk_ref[...])
        vft = _dot_nt(wvt_ref[...], ckv)
        cos_t, sin_t = cst_ref[:half, rows], cst_ref[half:, rows]
        k_pe = _rope(_rms(tail_ref[rows, TAIL_KROPE:TAIL_KROPE + MLA_ROPE], kp_g), cs_ref[rows, :]).astype(BF16)
        for h in range(HEADS):
            q_nope = _rms0(qft[h * MLA_NOPE:(h + 1) * MLA_NOPE], qn_g)
            q_pe = _rms0(qft[pe0 + h * MLA_ROPE:pe0 + (h + 1) * MLA_ROPE], qp_g)
            x1, x2 = q_pe[:half], q_pe[half:]
            q_rot = jnp.concatenate([x1 * cos_t - x2 * sin_t, x1 * sin_t + x2 * cos_t], axis=0)
            qt_out[0, h, s] = jnp.concatenate([q_nope.astype(BF16), q_rot.astype(BF16), q_pad], axis=0)
            k_nope = _rms(kf[:, h * MLA_NOPE:(h + 1) * MLA_NOPE], kn_g)
            k_out[0, h, rows, :] = jnp.concatenate([k_nope.astype(BF16), k_pe, k_pad], axis=-1)
            vt_out[0, h, s] = vft[h * MLA_V:(h + 1) * MLA_V].astype(BF16)


def _mla_prep(tail, cs, cst, qa, wqt, kva, wk, wvt, qn_col, kn, bsz, seq):
    tiles = 2 if (seq // ATT_T) % 2 == 0 else 1
    tm = tiles * ATT_T
    nt = seq // tm
    full = lambda shape: pl.BlockSpec(shape, lambda i: (0,) * len(shape))
    blocked = lambda rows: pl.BlockSpec((1, HEADS, tiles, rows, ATT_T), lambda i: (i // nt, 0, i % nt, 0, 0))
    return pl.pallas_call(
        _mla_prep_kernel,
        grid=(bsz * nt,),
        in_specs=[pl.BlockSpec((tm, TAIL_COLS), lambda i: (i, 0)),
                  pl.BlockSpec((tm, 2 * MLA_ROPE), lambda i: (i, 0)),
                  pl.BlockSpec((MLA_ROPE, tm), lambda i: (0, i)),
                  full((1, Q_LORA)), full(wqt.shape), full((1, KV_LORA)), full(wk.shape), full(wvt.shape),
                  full((MLA_NOPE + MLA_ROPE, 1)), full((1, MLA_NOPE + MLA_ROPE))],
        out_specs=[blocked(QK_PAD),
                   pl.BlockSpec((1, HEADS, tm, QK_PAD), lambda i: (i // nt, 0, i % nt, 0)),
                   blocked(MLA_V)],
        out_shape=[jax.ShapeDtypeStruct((bsz, HEADS, seq // ATT_T, QK_PAD, ATT_T), BF16),
                   jax.ShapeDtypeStruct((bsz, HEADS, seq, QK_PAD), BF16),
                   jax.ShapeDtypeStruct((bsz, HEADS, seq // ATT_T, MLA_V, ATT_T), BF16)],
        compiler_params=pltpu.CompilerParams(dimension_semantics=("parallel",),
                                             vmem_limit_bytes=VMEM_LIMIT),
        name="mla_prep",
    )(tail, cs, cst, qa, wqt, kva, wk, wvt, qn_col, kn)


def _attn_kernel(qt_ref, k_ref, vt_ref, o_ref, s_scr, p_scr, a_scr, m_all, l_all, acc_all, *, nq):
    t = ATT_T

    def qk(qi, j):
        k0 = pl.multiple_of(j * t, t)
        return _dot(k_ref[0, 0, pl.ds(k0, t), :], qt_ref[0, 0, qi])

    def accumulate(tile, alpha, p):
        q, j = tile
        acc_all[q] = alpha * acc_all[q] + _dot(vt_ref[0, 0, j], p)

    def stage(prev, cur, nxt, s_cur, p_prev, alpha_prev, masked):
        accumulate(prev, alpha_prev, p_prev)
        qi = cur[0]
        if masked:
            keep = lax.broadcasted_iota(jnp.int32, (t, t), 0) <= lax.broadcasted_iota(jnp.int32, (t, t), 1)
            s_cur = jnp.where(keep, s_cur, MASK_VALUE)
        m_old = m_all[qi]
        m_new = jnp.maximum(m_old, jnp.max(s_cur, axis=0, keepdims=True))
        m_all[qi] = m_new
        alpha = jnp.exp2(m_old - m_new)
        p = jnp.exp2((s_cur - m_new).astype(BF16))
        l_all[qi] = alpha * l_all[qi] + jnp.sum(p.astype(F32), axis=0, keepdims=True)
        s_next = qk(jnp.minimum(nxt[0], nq - 1), jnp.minimum(nxt[1], nq - 1))
        return s_next, p, alpha

    def advance_lower(qi, j):
        wrap = j + 1 == qi
        return jnp.where(wrap, qi + 1, qi), jnp.where(wrap, 0, j + 1)

    def advance_diag(qi, j):
        return qi + 1, j + 1

    def run_phase(n_steps, first, advance, masked):
        if n_steps == 0:
            return
        s_scr[...] = qk(*first)
        p_scr[...] = jnp.zeros_like(p_scr)
        a_scr[...] = jnp.ones_like(a_scr)

        def steps(count, prev, cur):
            s_cur, p_prev, alpha_prev = s_scr[...], p_scr[...], a_scr[...]
            for _ in range(count):
                nxt = advance(*cur)
                s_cur, p_prev, alpha_prev = stage(prev, cur, nxt, s_cur, p_prev, alpha_prev, masked)
                prev, cur = cur, nxt
            s_scr[...] = s_cur
            p_scr[...] = p_prev
            a_scr[...] = alpha_prev
            return prev, cur

        first = (jnp.int32(first[0]), jnp.int32(first[1]))
        prev, cur = lax.fori_loop(0, n_steps // ATT_UNROLL, lambda _, c: steps(ATT_UNROLL, *c), (first, first))
        if n_steps % ATT_UNROLL:
            prev, cur = steps(n_steps % ATT_UNROLL, prev, cur)
        accumulate(prev, a_scr[...], p_scr[...])

    m_all[...] = jnp.full_like(m_all, MASK_VALUE)
    l_all[...] = jnp.zeros_like(l_all)
    acc_all[...] = jnp.zeros_like(acc_all)
    run_phase(nq * (nq - 1) // 2, (1, 0), advance_lower, False)
    run_phase(nq, (0, 0), advance_diag, True)
    for qi in range(nq):
        o_ref[0, qi * t:(qi + 1) * t, :] = (acc_all[qi] / l_all[qi]).T.astype(o_ref.dtype)


def _attn_bounded_kernel(qt_ref, k_ref, vt_ref, o_ref, s_scr, p_scr, l_all, acc_all, *, nq):
    t = ATT_T

    def qk(qi, j):
        k0 = pl.multiple_of(j * t, t)
        return _dot(k_ref[0, 0, pl.ds(k0, t), :], qt_ref[0, 0, qi])

    def accumulate(tile, p):
        q, j = tile
        acc_all[q] += _dot(vt_ref[0, 0, j], p)

    def stage(prev, cur, nxt, s_cur, p_prev, masked):
        accumulate(prev, p_prev)
        p = jnp.exp2(s_cur)
        if masked:
            keep = lax.broadcasted_iota(jnp.int32, (t, t), 0) <= lax.broadcasted_iota(jnp.int32, (t, t), 1)
            p = jnp.where(keep, p, 0.0)
        l_all[cur[0]] += jnp.sum(p, axis=0, keepdims=True)
        s_next = qk(jnp.minimum(nxt[0], nq - 1), jnp.minimum(nxt[1], nq - 1))
        return s_next, p.astype(BF16)

    def advance_lower(qi, j):
        wrap = j + 1 == qi
        return jnp.where(wrap, qi + 1, qi), jnp.where(wrap, 0, j + 1)

    def advance_diag(qi, j):
        return qi + 1, j + 1

    def run_phase(n_steps, first, advance, masked):
        if n_steps == 0:
            return
        s_scr[...] = qk(*first)
        p_scr[...] = jnp.zeros_like(p_scr)

        def steps(count, prev, cur):
            s_cur, p_prev = s_scr[...], p_scr[...]
            for _ in range(count):
                nxt = advance(*cur)
                s_cur, p_prev = stage(prev, cur, nxt, s_cur, p_prev, masked)
                prev, cur = cur, nxt
            s_scr[...] = s_cur
            p_scr[...] = p_prev
            return prev, cur

        first = (jnp.int32(first[0]), jnp.int32(first[1]))
        unroll = ATT_UNROLL_BOUNDED
        prev, cur = lax.fori_loop(0, n_steps // unroll, lambda _, c: steps(unroll, *c), (first, first))
        if n_steps % unroll:
            prev, cur = steps(n_steps % unroll, prev, cur)
        accumulate(prev, p_scr[...])

    l_all[...] = jnp.zeros_like(l_all)
    acc_all[...] = jnp.zeros_like(acc_all)
    run_phase(nq * (nq - 1) // 2, (1, 0), advance_lower, False)
    run_phase(nq, (0, 0), advance_diag, True)
    for qi in range(nq):
        o_ref[0, qi * t:(qi + 1) * t, :] = (acc_all[qi] / l_all[qi]).T.astype(o_ref.dtype)


def _attn_mlstm_kernel(qt_ref, k_ref, vt_ref, mq_ref, mk_ref, mv_ref, gate_ref, bias_ref,
                       o_ref, num_ref, dst_ref, s_scr, p_scr, l_all, acc_all, c_ref, m_ref, *, nq):
    t = ATT_T

    @pl.when(pl.program_id(1) == 0)
    def _():
        c_ref[...] = jnp.zeros_like(c_ref)
        m_ref[...] = jnp.zeros_like(m_ref)

    def qk(qi, j):
        k0 = pl.multiple_of(j * t, t)
        return _dot(k_ref[0, 0, pl.ds(k0, t), :], qt_ref[0, 0, qi])

    def accumulate(tile, p):
        q, j = tile
        acc_all[q] += _dot(vt_ref[0, 0, j], p)

    def stage(prev, cur, nxt, s_cur, p_prev, masked):
        accumulate(prev, p_prev)
        p = jnp.exp2(s_cur)
        if masked:
            keep = lax.broadcasted_iota(jnp.int32, (t, t), 0) <= lax.broadcasted_iota(jnp.int32, (t, t), 1)
            p = jnp.where(keep, p, 0.0)
        l_all[cur[0]] += jnp.sum(p, axis=0, keepdims=True)
        s_next = qk(jnp.minimum(nxt[0], nq - 1), jnp.minimum(nxt[1], nq - 1))
        return s_next, p.astype(BF16)

    def advance_lower(qi, j):
        wrap = j + 1 == qi
        return jnp.where(wrap, qi + 1, qi), jnp.where(wrap, 0, j + 1)

    def advance_diag(qi, j):
        return qi + 1, j + 1

    def steps(count, prev, cur, advance, masked):
        s_cur, p_prev = s_scr[...], p_scr[...]
        for _ in range(count):
            nxt = advance(*cur)
            s_cur, p_prev = stage(prev, cur, nxt, s_cur, p_prev, masked)
            prev, cur = cur, nxt
        s_scr[...] = s_cur
        p_scr[...] = p_prev
        return prev, cur

    l_all[...] = jnp.zeros_like(l_all)
    acc_all[...] = jnp.zeros_like(acc_all)

    s_scr[...] = qk(1, 0)
    p_scr[...] = jnp.zeros_like(p_scr)

    def trip(i, carry):
        prev, cur = steps(nq - 1, *carry, advance_lower, False)
        rows = pl.ds(pl.multiple_of(i * CHUNK, CHUNK), CHUNK)
        c_state, m_new = _mlstm_chunk(rows, mq_ref, mk_ref, mv_ref, gate_ref, bias_ref, num_ref, dst_ref,
                                      [c_ref[h] for h in range(HEADS)], m_ref[...])
        m_ref[...] = m_new
        for h in range(HEADS):
            c_ref[h] = c_state[h]
        return prev, cur

    first = (jnp.int32(1), jnp.int32(0))
    prev, _ = lax.fori_loop(0, nq // 2, trip, (first, first))
    accumulate(prev, p_scr[...])

    s_scr[...] = qk(0, 0)
    p_scr[...] = jnp.zeros_like(p_scr)
    zero = (jnp.int32(0), jnp.int32(0))
    prev, _ = steps(nq, zero, zero, advance_diag, True)
    accumulate(prev, p_scr[...])

    for qi in range(nq):
        o_ref[0, qi * t:(qi + 1) * t, :] = (acc_all[qi] / l_all[qi]).T.astype(o_ref.dtype)


def _attention_mlstm(qt, k, vt, main, tail, gate_bias):
    bsz, _, seq, _ = k.shape
    nq = seq // ATT_T
    rows = seq // HEADS
    assert nq % 2 == 0 and rows == (nq // 2) * CHUNK
    t = bsz * seq
    row = lambda b, h: b * HEADS + h
    col_block = lambda col0, width: pl.BlockSpec((rows, width), lambda b, h: (row(b, h), col0 // width))
    return pl.pallas_call(
        functools.partial(_attn_mlstm_kernel, nq=nq),
        grid=(bsz, HEADS),
        in_specs=[pl.BlockSpec((1, 1, nq, QK_PAD, ATT_T), lambda b, h: (b, h, 0, 0, 0)),
                  pl.BlockSpec((1, 1, seq, QK_PAD), lambda b, h: (b, h, 0, 0)),
                  pl.BlockSpec((1, 1, nq, MLA_V, ATT_T), lambda b, h: (b, h, 0, 0, 0)),
                  col_block(COL_MQ, HEADS * MLSTM_DK), col_block(COL_MK, HEADS * MLSTM_DK),
                  col_block(COL_MV, HEADS * MLSTM_DV), col_block(TAIL_KROPE, LANES),
                  pl.BlockSpec((1, LANES), lambda b, h: (0, 0))],
        out_specs=[pl.BlockSpec((1, seq, MLA_V), lambda b, h: (b, 0, h)),
                   pl.BlockSpec((rows, D_MODEL), lambda b, h: (row(b, h), 0)),
                   pl.BlockSpec((rows, LANES), lambda b, h: (row(b, h), 0))],
        out_shape=[jax.ShapeDtypeStruct((bsz, seq, HEADS * MLA_V), BF16),
                   jax.ShapeDtypeStruct((t, D_MODEL), BF16), jax.ShapeDtypeStruct((t, LANES), F32)],
        scratch_shapes=[pltpu.VMEM((ATT_T, ATT_T), F32), pltpu.VMEM((ATT_T, ATT_T), BF16),
                        pltpu.VMEM((nq, 1, ATT_T), F32), pltpu.VMEM((nq, MLA_V, ATT_T), F32),
                        pltpu.VMEM((HEADS, MLSTM_DK, 2 * MLSTM_DV), F32), pltpu.VMEM((1, LANES), F32)],
        compiler_params=pltpu.CompilerParams(dimension_semantics=("parallel", "arbitrary"),
                                             vmem_limit_bytes=VMEM_LIMIT),
        name="attn_mlstm",
    )(qt, k, vt, main, main, main, tail, gate_bias)


def _attention_bounded(qt, k, vt):
    bsz, _, seq, _ = k.shape
    nq = seq // ATT_T
    kern = functools.partial(_attn_bounded_kernel, nq=nq)
    return pl.pallas_call(
        kern,
        grid=(bsz, HEADS),
        in_specs=[pl.BlockSpec((1, 1, nq, QK_PAD, ATT_T), lambda b, h: (b, h, 0, 0, 0)),
                  pl.BlockSpec((1, 1, seq, QK_PAD), lambda b, h: (b, h, 0, 0)),
                  pl.BlockSpec((1, 1, nq, MLA_V, ATT_T), lambda b, h: (b, h, 0, 0, 0))],
        out_specs=pl.BlockSpec((1, seq, MLA_V), lambda b, h: (b, 0, h)),
        out_shape=jax.ShapeDtypeStruct((bsz, seq, HEADS * MLA_V), BF16),
        scratch_shapes=[pltpu.VMEM((ATT_T, ATT_T), F32), pltpu.VMEM((ATT_T, ATT_T), BF16),
                        pltpu.VMEM((nq, 1, ATT_T), F32), pltpu.VMEM((nq, MLA_V, ATT_T), F32)],
        compiler_params=pltpu.CompilerParams(dimension_semantics=("parallel", "parallel"),
                                             vmem_limit_bytes=VMEM_LIMIT),
        name="attn_bounded",
    )(qt, k, vt)


def _attention(qt, k, vt):
    bsz, _, seq, _ = k.shape
    nq = seq // ATT_T
    kern = functools.partial(_attn_kernel, nq=nq)
    return pl.pallas_call(
        kern,
        grid=(bsz, HEADS),
        in_specs=[pl.BlockSpec((1, 1, nq, QK_PAD, ATT_T), lambda b, h: (b, h, 0, 0, 0)),
                  pl.BlockSpec((1, 1, seq, QK_PAD), lambda b, h: (b, h, 0, 0)),
                  pl.BlockSpec((1, 1, nq, MLA_V, ATT_T), lambda b, h: (b, h, 0, 0, 0))],
        out_specs=pl.BlockSpec((1, seq, MLA_V), lambda b, h: (b, 0, h)),
        out_shape=jax.ShapeDtypeStruct((bsz, seq, HEADS * MLA_V), BF16),
        scratch_shapes=[pltpu.VMEM((ATT_T, ATT_T), F32), pltpu.VMEM((ATT_T, ATT_T), BF16),
                        pltpu.VMEM((1, ATT_T), F32), pltpu.VMEM((nq, 1, ATT_T), F32),
                        pltpu.VMEM((nq, 1, ATT_T), F32), pltpu.VMEM((nq, MLA_V, ATT_T), F32)],
        compiler_params=pltpu.CompilerParams(dimension_semantics=("parallel", "parallel"),
                                             vmem_limit_bytes=VMEM_LIMIT),
        name="attn",
    )(qt, k, vt)


def _mlstm_head_out(num_ref, dst_ref, o_ref, hn_ref):
    out_gain = (0.5 * jnp.tanh(0.5 * o_ref[...].astype(F32)) + 0.5) * hn_ref[...]
    heads = []
    for h in range(HEADS):
        hv = slice(h * MLSTM_DV, (h + 1) * MLSTM_DV)
        num = num_ref[:, hv].astype(F32)
        d = dst_ref[:, IGATE_LANE + h:IGATE_LANE + h + 1]
        scale = lax.rsqrt(jnp.mean(num * num, axis=-1, keepdims=True) + NORM_EPS * d * d)
        heads.append((num * scale * out_gain[:, hv]).astype(BF16))
    return jnp.concatenate(heads, axis=1)


def _merge_kernel(ya_ref, gate_ref, num_ref, dst_ref, o_ref, hn_ref, yc_ref, x_ref, wb_ref, wo_ref, out_ref):
    g = gate_ref[...]
    merged = g[:, :D_MODEL].astype(F32) * _dot(ya_ref[...], wb_ref[0])
    y_b = _mlstm_head_out(num_ref, dst_ref, o_ref, hn_ref)
    merged += g[:, D_MODEL:2 * D_MODEL].astype(F32) * _dot(y_b, wb_ref[1])
    merged += g[:, 2 * D_MODEL:].astype(F32) * _dot(yc_ref[...], wb_ref[2])
    out_ref[...] = x_ref[...] + _dot(merged.astype(BF16), wo_ref[...])


def _merge(y_a, main, num, dstat, head_norm, y_c, x2, w_branch, w_out, layer, tm):
    t = x2.shape[0]
    col = lambda c: pl.BlockSpec((tm, D_MODEL), lambda i: (i, c))
    return pl.pallas_call(
        _merge_kernel,
        grid=(t // tm,),
        in_specs=[col(0),
                  pl.BlockSpec((tm, N_BRANCH * D_MODEL), lambda i: (i, GATE_COL0 // (N_BRANCH * D_MODEL))),
                  col(0), pl.BlockSpec((tm, LANES), lambda i: (i, 0)), col(COL_MO // D_MODEL),
                  pl.BlockSpec((1, D_MODEL), lambda i: (0, 0)),
                  col(0), col(0),
                  pl.BlockSpec((None, N_BRANCH, D_MODEL, D_MODEL), lambda i: (layer, 0, 0, 0),
                               pipeline_mode=pl.Buffered(1)),
                  pl.BlockSpec((None, D_MODEL, D_MODEL), lambda i: (layer, 0, 0), pipeline_mode=pl.Buffered(1))],
        out_specs=col(0),
        out_shape=jax.ShapeDtypeStruct((t, D_MODEL), F32),
        compiler_params=pltpu.CompilerParams(dimension_semantics=("parallel",),
                                             vmem_limit_bytes=VMEM_LIMIT),
        name="merge",
    )(y_a, main, num, dstat, main, head_norm, y_c, x2, w_branch, w_out)


def _mlp_kernel(x_ref, g_ref, wu_ref, wd_ref, out_ref, *, n_chunk):
    x = x_ref[...]
    h = _rms(x, g_ref[...]).astype(BF16)
    ck = D_FF // n_chunk
    acc = x
    for c in range(n_chunk):
        u = jnp.maximum(_dot(h, wu_ref[:, c * ck:(c + 1) * ck]), 0.0)
        acc = acc + _dot((u * u).astype(BF16), wd_ref[c * ck:(c + 1) * ck, :])
    out_ref[...] = acc


def _mlp(x2, g, w_up, w_down, layer, tm):
    t = x2.shape[0]
    kern = functools.partial(_mlp_kernel, n_chunk=4)
    return pl.pallas_call(
        kern,
        grid=(t // tm,),
        in_specs=[pl.BlockSpec((tm, D_MODEL), lambda i: (i, 0)),
                  pl.BlockSpec((1, D_MODEL), lambda i: (0, 0)),
                  pl.BlockSpec((None, D_MODEL, D_FF), lambda i: (layer, 0, 0), pipeline_mode=pl.Buffered(1)),
                  pl.BlockSpec((None, D_FF, D_MODEL), lambda i: (layer, 0, 0), pipeline_mode=pl.Buffered(1))],
        out_specs=pl.BlockSpec((tm, D_MODEL), lambda i: (i, 0)),
        out_shape=jax.ShapeDtypeStruct((t, D_MODEL), F32),
        compiler_params=pltpu.CompilerParams(dimension_semantics=("parallel",),
                                             vmem_limit_bytes=VMEM_LIMIT),
        name="mlp",
    )(x2, g, w_up, w_down)


def _regroup_w_in(w_in):
    i0 = CONV_COLS + GATE_COL0
    cq0 = i0 + 2 * HEADS
    g0 = cq0 + Q_LORA + KV_LORA + MLA_ROPE
    pad = jnp.zeros(w_in.shape[:2] + (TAIL_COLS - (Q_LORA + KV_LORA + MLA_ROPE + 2 * HEADS),), w_in.dtype)
    w_main = jnp.concatenate([w_in[..., :i0], w_in[..., g0:]], axis=-1).astype(BF16)
    w_tail = jnp.concatenate([w_in[..., cq0:g0], w_in[..., i0:cq0], pad], axis=-1).astype(BF16)
    return w_main, w_tail


def _heads_last_split(w, first, second):
    k = w.shape[0]
    w3 = w.reshape(k, HEADS, first + second)
    return jnp.concatenate([w3[:, :, :first].reshape(k, HEADS * first),
                            w3[:, :, first:].reshape(k, HEADS * second)], axis=1).astype(BF16)


def _score_bound(q_gain, k_gain):
    def sq_norm(g):
        g = g.astype(F32)
        return MLA_NOPE * jnp.max(g[:MLA_NOPE] ** 2) + MLA_ROPE * jnp.max(g[MLA_NOPE:] ** 2)
    return jnp.sqrt(sq_norm(q_gain) * sq_norm(k_gain)) * (Q_SCALE * BOUND_SLACK)


def _pick(n, candidates):
    for c in candidates:
        if n % c == 0:
            return c
    raise ValueError(f"no tile size for {n}")


def kernel(x, positions, mix_norm, w_in, conv_w, mlstm_igate_bias, mlstm_fgate_bias, mlstm_head_norm,
           mla_q_a_norm, mla_w_uq, mla_kv_a_norm, mla_w_ukv, mla_q_norm, mla_k_norm, w_branch, w_out,
           mlp_norm, w_up, w_down):
    bsz, seq, _ = x.shape
    depth = w_in.shape[0]
    t = bsz * seq
    assert seq % ATT_T == 0
    n_chunk = _pick(seq // CHUNK, tuple(g for g in (8, 4, 2, 1) if g <= MLSTM_GROUP))
    tm_proj = _pick(seq, (1024, 512, 256, 128))
    tm = _pick(seq, (512, 256, 128))
    tm_mlp = _pick(t, (1024, 512, 256, 128))

    x2 = x.reshape(t, D_MODEL)
    cs, cst = _rope_table(positions, ATT_T)
    row = lambda a: a.reshape(1, -1).astype(F32)
    w_main, w_tail = _regroup_w_in(w_in)
    w_branch_b, w_out_b, w_up_b, w_down_b = (w.astype(BF16) for w in (w_branch, w_out, w_up, w_down))

    for l in range(depth):
        gate_bias = jnp.zeros((1, LANES), F32)
        gate_bias = gate_bias.at[0, IGATE_LANE:IGATE_LANE + HEADS].set(mlstm_igate_bias[l])
        gate_bias = gate_bias.at[0, FGATE_LANE:FGATE_LANE + HEADS].set(mlstm_fgate_bias[l])
        w_q = _heads_last_split(mla_w_uq[l], MLA_NOPE, MLA_ROPE)
        w_kv = _heads_last_split(mla_w_ukv[l], MLA_NOPE, MLA_V)

        y_a, main, tail = _in_proj(x2, row(mix_norm[l]), w_main, w_tail, conv_w[l].astype(F32), l, seq, tm_proj)
        qt, k, vt = _mla_prep(tail, cs, cst, row(mla_q_a_norm[l]), w_q.T, row(mla_kv_a_norm[l]),
                              w_kv[:, :HEADS * MLA_NOPE], w_kv[:, HEADS * MLA_NOPE:].T,
                              mla_q_norm[l].reshape(-1, 1).astype(F32) * Q_SCALE, row(mla_k_norm[l]), bsz, seq)
        bounded = _score_bound(mla_q_norm[l], mla_k_norm[l]) <= SCORE_BOUND
        if (seq // ATT_T) % 2 == 0:
            def separate(qt, k, vt, main, tail, gate_bias):
                return (_attention(qt, k, vt), *_mlstm(main, tail, gate_bias, bsz, seq, n_chunk))
            y_c, num, dstat = lax.cond(bounded, _attention_mlstm, separate, qt, k, vt, main, tail, gate_bias)
        else:
            num, dstat = _mlstm(main, tail, gate_bias, bsz, seq, n_chunk)
            y_c = lax.cond(bounded, _attention_bounded, _attention, qt, k, vt)
        y_c = y_c.reshape(t, D_MODEL)
        x2 = _merge(y_a, main, num, dstat, row(mlstm_head_norm[l]), y_c, x2, w_branch_b, w_out_b, l, tm)
        x2 = _mlp(x2, row(mlp_norm[l]), w_up_b, w_down_b, l, tm_mlp)
    return x2.reshape(bsz, seq, D_MODEL)
```

```python
import functools

import jax
import jax.numpy as jnp
import numpy as np
from jax import lax
from jax.experimental import pallas as pl
from jax.experimental.pallas import tpu as pltpu

F32 = jnp.float32
BF16 = jnp.bfloat16

D_MODEL = 1024
N_BRANCH = 3
CONV_WIDTH = 3
HEADS = 8
MLSTM_DK = 64
MLSTM_DV = 128
CHUNK = 128
GATE_CAP = 15.0
MLA_NOPE = 128
MLA_ROPE = 64
MLA_V = 128
Q_LORA = 256
KV_LORA = 128
ROPE_BASE = 10000.0
D_FF = 4 * D_MODEL
NORM_EPS = 1e-6
QK_PAD = 256

CONV_COLS = 3 * D_MODEL
PROJ_COLS = 9 * D_MODEL
MAIN_COLS = PROJ_COLS - CONV_COLS
TAIL_COLS = 512
COL_MQ = 0
COL_MK = COL_MQ + HEADS * MLSTM_DK
COL_MV = COL_MK + HEADS * MLSTM_DK
COL_MO = COL_MV + HEADS * MLSTM_DV
GATE_COL0 = COL_MO + HEADS * MLSTM_DV
LANES = 128
TAIL_KROPE = Q_LORA + KV_LORA
IGATE_LANE = MLA_ROPE
FGATE_LANE = MLA_ROPE + HEADS

IN_PROJ_TN = 3 * D_MODEL
IN_PROJ_CHUNK = 1536
CONV_CHUNK = 256
MLSTM_GROUP = 8
ATT_T = 512
LOG2E = 1.4426950408889634
Q_SCALE = (MLA_NOPE + MLA_ROPE) ** -0.5 * LOG2E
MASK_VALUE = -1e30
SCORE_BOUND = 70.0
SCORE_SHIFT = 48.0
BOUND_SLACK = 1.02
ATT_UNROLL = 8
ATT_UNROLL_BOUNDED = 24

VMEM_LIMIT = 56 * 1024 * 1024


def _rms(x, g):
    return x * lax.rsqrt(jnp.mean(x * x, axis=-1, keepdims=True) + NORM_EPS) * g


def _dot(a, b):
    return jnp.dot(a, b, preferred_element_type=F32)


def _dot_nt(a, b):
    return lax.dot_general(a, b, (((1,), (1,)), ((), ())), preferred_element_type=F32)


def _rope_table_kernel(pos_ref, freq_ref, cs_ref, cst_ref):
    ang = freq_ref[...] * pos_ref[...].astype(F32)
    c, s = jnp.cos(ang), jnp.sin(ang)
    cst_ref[...] = jnp.concatenate([c, s], axis=0)
    cs_ref[...] = jnp.concatenate([c, c, -s, s], axis=0).T


def _rope_table(positions, tm):
    t = positions.size
    half = MLA_ROPE // 2
    inv_freq = (np.float32(ROPE_BASE) ** (-np.arange(0, MLA_ROPE, 2, dtype=np.float32) / np.float32(MLA_ROPE)))
    inv_freq = jnp.asarray(inv_freq.astype(np.float32))
    return pl.pallas_call(
        _rope_table_kernel,
        grid=(t // tm,),
        in_specs=[pl.BlockSpec((1, tm), lambda i: (0, i)),
                  pl.BlockSpec((half, 1), lambda i: (0, 0))],
        out_specs=[pl.BlockSpec((tm, 2 * MLA_ROPE), lambda i: (i, 0)),
                   pl.BlockSpec((MLA_ROPE, tm), lambda i: (0, i))],
        out_shape=[jax.ShapeDtypeStruct((t, 2 * MLA_ROPE), F32),
                   jax.ShapeDtypeStruct((MLA_ROPE, t), F32)],
        name="rope_table",
    )(positions.reshape(1, t), inv_freq.reshape(half, 1))


def _in_proj_kernel(x_ref, g_ref, w_ref, wt_ref, cw_ref, ya_ref, main_ref, tail_ref, xn_ref, zprev_ref, *,
                    tiles_per_seq):
    i, j = pl.program_id(0), pl.program_id(1)
    tm = x_ref.shape[0]

    @pl.when(jnp.logical_and(j == 0, i % tiles_per_seq == 0))
    def _():
        zprev_ref[...] = jnp.zeros_like(zprev_ref)

    @pl.when(j == 0)
    def _():
        xn = _rms(x_ref[...], g_ref[...]).astype(BF16)
        xn_ref[...] = xn
        tail_ref[...] = _dot(xn, wt_ref[...])
        row = lax.broadcasted_iota(jnp.int32, (tm, CONV_CHUNK), 0)
        for c0 in range(0, D_MODEL, CONV_CHUNK):
            c = slice(c0, c0 + CONV_CHUNK)
            gate_b, gate_c, u = (_dot(xn, w_ref[:, n * D_MODEL + c0:n * D_MODEL + c0 + CONV_CHUNK]) for n in range(3))
            z = gate_c * u
            p1, p2 = zprev_ref[7:8, c], zprev_ref[6:7, c]
            z1 = jnp.where(row == 0, p1, pltpu.roll(z, 1, 0))
            z2 = jnp.where(row == 0, p2, jnp.where(row == 1, p1, pltpu.roll(z, 2, 0)))
            zprev_ref[:, c] = z[tm - 8:, :]
            ya_ref[:, c] = (gate_b * (cw_ref[0:1, c] * z2 + cw_ref[1:2, c] * z1 + cw_ref[2:3, c] * z)).astype(BF16)

    cols = [slice(c, c + IN_PROJ_CHUNK) for c in range(0, w_ref.shape[1], IN_PROJ_CHUNK)]

    @pl.when(j == 1)
    def _():
        for c in cols:
            main_ref[:, c] = _dot(xn_ref[...], w_ref[:, c]).astype(BF16)

    @pl.when(j == 2)
    def _():
        for c in cols:
            main_ref[:, c] = jax.nn.sigmoid(_dot(xn_ref[...], w_ref[:, c])).astype(BF16)


def _in_proj(x2, g, w_main, w_tail, conv_w, layer, seq, tm):
    t = x2.shape[0]
    tn = IN_PROJ_TN
    assert CONV_COLS == tn and GATE_COL0 == tn and MAIN_COLS == 2 * tn
    kern = functools.partial(_in_proj_kernel, tiles_per_seq=seq // tm)
    return pl.pallas_call(
        kern,
        grid=(t // tm, PROJ_COLS // tn),
        in_specs=[pl.BlockSpec((tm, D_MODEL), lambda i, j: (i, 0)),
                  pl.BlockSpec((1, D_MODEL), lambda i, j: (0, 0)),
                  pl.BlockSpec((None, D_MODEL, tn), lambda i, j: (layer, 0, j)),
                  pl.BlockSpec((None, D_MODEL, TAIL_COLS), lambda i, j: (layer, 0, 0)),
                  pl.BlockSpec((CONV_WIDTH, D_MODEL), lambda i, j: (0, 0))],
        out_specs=[pl.BlockSpec((tm, D_MODEL), lambda i, j: (i, 0)),
                   pl.BlockSpec((tm, tn), lambda i, j: (i, jnp.maximum(j - 1, 0))),
                   pl.BlockSpec((tm, TAIL_COLS), lambda i, j: (i, 0))],
        out_shape=[jax.ShapeDtypeStruct((t, D_MODEL), BF16),
                   jax.ShapeDtypeStruct((t, MAIN_COLS), BF16),
                   jax.ShapeDtypeStruct((t, TAIL_COLS), F32)],
        scratch_shapes=[pltpu.VMEM((tm, D_MODEL), BF16), pltpu.VMEM((8, D_MODEL), F32)],
        compiler_params=pltpu.CompilerParams(
            dimension_semantics=("arbitrary", "arbitrary"), vmem_limit_bytes=VMEM_LIMIT),
        name="in_proj",
    )(x2, g, w_main, w_tail, conv_w)


def _time_scan(x, op, row):
    s = 1
    while s < CHUNK:
        x = jnp.where(row >= s, op(x, pltpu.roll(x, s, 0)), x)
        s *= 2
    return x


def _mlstm_chunk(rows, q_ref, k_ref, v_ref, gate_ref, bias_ref, num_ref, dst_ref, c_state, m_prev):
    L = CHUNK
    row = lax.broadcasted_iota(jnp.int32, (L, LANES), 0)
    lane = lax.broadcasted_iota(jnp.int32, (L, LANES), 1)
    gate_lanes = jnp.logical_and(lane >= IGATE_LANE, lane < FGATE_LANE + HEADS)
    tri = (lax.broadcasted_iota(jnp.int32, (L, L), 0) >= lax.broadcasted_iota(jnp.int32, (L, L), 1))
    lane_v = lax.broadcasted_iota(jnp.int32, (L, MLSTM_DV), 1)
    c_state = list(c_state)
    pre = jnp.where(gate_lanes, gate_ref[rows, :] + bias_ref[...], 0.0)
    capped = GATE_CAP * jnp.tanh(pre * (1.0 / GATE_CAP))
    log_i = capped * LOG2E
    log_f = (jnp.minimum(capped, 0.0) - jnp.log1p(jnp.exp(-jnp.abs(capped)))) * LOG2E
    log_f = pltpu.roll(log_f, LANES - HEADS, 1)
    b = _time_scan(log_f, jnp.add, row)
    b_last = b[L - 1:L, :]
    rowd = log_i - b
    big_m = jnp.maximum(_time_scan(rowd, jnp.maximum, row), m_prev)
    e_neg_m = jnp.exp2(-(b + big_m))
    a = b_last + rowd
    m_loc = jnp.max(a, axis=0, keepdims=True)
    m_new = jnp.maximum(b_last + m_prev, m_loc)
    s_old = jnp.exp2(b_last + m_prev - m_new)
    w = jnp.exp2(a - m_new)
    rowd_t = rowd.T
    w_t = w.T
    q_all = q_ref[rows, :] * (MLSTM_DK ** -0.5)
    k_t = k_ref[rows, :].astype(F32).T

    hk = [slice(h * MLSTM_DK, (h + 1) * MLSTM_DK) for h in range(HEADS)]
    hv = [slice(h * MLSTM_DV, (h + 1) * MLSTM_DV) for h in range(HEADS)]
    gl = [slice(IGATE_LANE + h, IGATE_LANE + h + 1) for h in range(HEADS)]
    v_aug = [jnp.concatenate([v_ref[rows, hv[h]], (lane_v == IGATE_LANE + h).astype(BF16)], axis=1)
             for h in range(HEADS)]
    scores = [_dot(q_all[:, hk[h]], k_t[hk[h], :].astype(BF16)) for h in range(HEADS)]
    c_loc = [_dot((k_t[hk[h], :] * w_t[gl[h], :]).astype(BF16), v_aug[h]) for h in range(HEADS)]
    big_m_b = [jnp.broadcast_to(big_m[:, gl[h]], (L, L)) for h in range(HEADS)]
    dmat = [jnp.where(tri, jnp.exp2(rowd_t[gl[h], :] - big_m_b[h]), 0.0) for h in range(HEADS)]
    q_state = [(q_all[:, hk[h]].astype(F32)
                * jnp.exp2(m_prev[:, gl[h]] - big_m_b[h][:, :MLSTM_DK])).astype(BF16) for h in range(HEADS)]
    num_aug = []
    for h in range(HEADS):
        lhs = jnp.concatenate([(scores[h] * dmat[h]).astype(BF16), q_state[h],
                               jnp.zeros((L, L - MLSTM_DK), BF16)], axis=1)
        rhs = jnp.concatenate([v_aug[h], c_state[h].astype(BF16),
                               jnp.zeros((L - MLSTM_DK, 2 * MLSTM_DV), BF16)], axis=0)
        num_aug.append(_dot(lhs, rhs))
    for h in range(HEADS):
        num_ref[rows, hv[h]] = num_aug[h][:, :MLSTM_DV].astype(num_ref.dtype)
        c_state[h] = s_old[:, gl[h]] * c_state[h] + c_loc[h]
    den = functools.reduce(jnp.add, [n[:, MLSTM_DV:] for n in num_aug])
    dst_ref[rows, :] = jnp.maximum(jnp.abs(den), e_neg_m)
    return c_state, m_new


def _mlstm_kernel(q_ref, k_ref, v_ref, gate_ref, bias_ref, num_ref, dst_ref, c_ref, m_ref, *, n_chunk):
    @pl.when(pl.program_id(1) == 0)
    def _():
        c_ref[...] = jnp.zeros_like(c_ref)
        m_ref[...] = jnp.zeros_like(m_ref)

    m_prev = m_ref[...]
    c_state = [c_ref[h] for h in range(HEADS)]
    for g in range(n_chunk):
        c_state, m_prev = _mlstm_chunk(slice(g * CHUNK, (g + 1) * CHUNK), q_ref, k_ref, v_ref, gate_ref, bias_ref,
                                       num_ref, dst_ref, c_state, m_prev)
    m_ref[...] = m_prev
    for h in range(HEADS):
        c_ref[h] = c_state[h]


def _mlstm(main, tail, gate_bias, bsz, seq, n_chunk):
    t = main.shape[0]
    rows = n_chunk * CHUNK
    nc = seq // rows
    row = lambda b, c: b * nc + c
    col_block = lambda col0, width: pl.BlockSpec((rows, width), lambda b, c: (row(b, c), col0 // width))
    return pl.pallas_call(
        functools.partial(_mlstm_kernel, n_chunk=n_chunk),
        grid=(bsz, nc),
        in_specs=[col_block(COL_MQ, HEADS * MLSTM_DK), col_block(COL_MK, HEADS * MLSTM_DK),
                  col_block(COL_MV, HEADS * MLSTM_DV),
                  col_block(TAIL_KROPE, LANES),
                  pl.BlockSpec((1, LANES), lambda b, c: (0, 0))],
        out_specs=[pl.BlockSpec((rows, D_MODEL), lambda b, c: (row(b, c), 0)),
                   pl.BlockSpec((rows, LANES), lambda b, c: (row(b, c), 0))],
        out_shape=[jax.ShapeDtypeStruct((t, D_MODEL), BF16), jax.ShapeDtypeStruct((t, LANES), F32)],
        scratch_shapes=[pltpu.VMEM((HEADS, MLSTM_DK, 2 * MLSTM_DV), F32),
                        pltpu.VMEM((1, LANES), F32)],
        compiler_params=pltpu.CompilerParams(dimension_semantics=("parallel", "arbitrary")),
        name="mlstm",
    )(main, main, main, tail, gate_bias)


def _rms0(x, g_col):
    return x * lax.rsqrt(jnp.mean(x * x, axis=0, keepdims=True) + NORM_EPS) * g_col


def _rope(x, cs):
    half = MLA_ROPE // 2
    rot = jnp.concatenate([x[:, half:], x[:, :half]], axis=-1)
    return x * cs[:, :MLA_ROPE] + rot * cs[:, MLA_ROPE:]


def _mla_prep_kernel(tail_ref, cs_ref, cst_ref, qa_ref, wqt_ref, kva_ref, wk_ref, wvt_ref, qn_ref, kn_ref,
                     qt_out, k_out, vt_out):
    tm = ATT_T
    half = MLA_ROPE // 2
    qn_g, qp_g = qn_ref[:MLA_NOPE, :], qn_ref[MLA_NOPE:, :]
    kn_g, kp_g = kn_ref[:, :MLA_NOPE], kn_ref[:, MLA_NOPE:]
    pe0 = HEADS * MLA_NOPE
    pad = QK_PAD - MLA_NOPE - MLA_ROPE
    q_pad = jnp.where(lax.broadcasted_iota(jnp.int32, (pad, tm), 0) == 0, -SCORE_SHIFT, 0.0).astype(BF16)
    k_pad = (lax.broadcasted_iota(jnp.int32, (tm, pad), 1) == 0).astype(BF16)
    for s in range(tail_ref.shape[0] // tm):
        rows = slice(s * tm, (s + 1) * tm)
        cq = _rms(tail_ref[rows, :Q_LORA], qa_ref[...]).astype(BF16)
        ckv = _rms(tail_ref[rows, Q_LORA:Q_LORA + KV_LORA], kva_ref[...]).astype(BF16)
        qft = _dot_nt(wqt_ref[...], cq)
        kf = _dot(ckv, wk_ref[...])
        vft = _dot_nt(wvt_ref[...], ckv)
        cos_t, sin_t = cst_ref[:half, rows], cst_ref[half:, rows]
        k_pe = _rope(_rms(tail_ref[rows, TAIL_KROPE:TAIL_KROPE + MLA_ROPE], kp_g), cs_ref[rows, :]).astype(BF16)
        for h in range(HEADS):
            q_nope = _rms0(qft[h * MLA_NOPE:(h + 1) * MLA_NOPE], qn_g)
            q_pe = _rms0(qft[pe0 + h * MLA_ROPE:pe0 + (h + 1) * MLA_ROPE], qp_g)
            x1, x2 = q_pe[:half], q_pe[half:]
            q_rot = jnp.concatenate([x1 * cos_t - x2 * sin_t, x1 * sin_t + x2 * cos_t], axis=0)
            qt_out[0, h, s] = jnp.concatenate([q_nope.astype(BF16), q_rot.astype(BF16), q_pad], axis=0)
            k_nope = _rms(kf[:, h * MLA_NOPE:(h + 1) * MLA_NOPE], kn_g)
            k_out[0, h, rows, :] = jnp.concatenate([k_nope.astype(BF16), k_pe, k_pad], axis=-1)
            vt_out[0, h, s] = vft[h * MLA_V:(h + 1) * MLA_V].astype(BF16)


def _mla_prep(tail, cs, cst, qa, wqt, kva, wk, wvt, qn_col, kn, bsz, seq):
    tiles = 2 if (seq // ATT_T) % 2 == 0 else 1
    tm = tiles * ATT_T
    nt = seq // tm
    full = lambda shape: pl.BlockSpec(shape, lambda i: (0,) * len(shape))
    blocked = lambda rows: pl.BlockSpec((1, HEADS, tiles, rows, ATT_T), lambda i: (i // nt, 0, i % nt, 0, 0))
    return pl.pallas_call(
        _mla_prep_kernel,
        grid=(bsz * nt,),
        in_specs=[pl.BlockSpec((tm, TAIL_COLS), lambda i: (i, 0)),
                  pl.BlockSpec((tm, 2 * MLA_ROPE), lambda i: (i, 0)),
                  pl.BlockSpec((MLA_ROPE, tm), lambda i: (0, i)),
                  full((1, Q_LORA)), full(wqt.shape), full((1, KV_LORA)), full(wk.shape), full(wvt.shape),
                  full((MLA_NOPE + MLA_ROPE, 1)), full((1, MLA_NOPE + MLA_ROPE))],
        out_specs=[blocked(QK_PAD),
                   pl.BlockSpec((1, HEADS, tm, QK_PAD), lambda i: (i // nt, 0, i % nt, 0)),
                   blocked(MLA_V)],
        out_shape=[jax.ShapeDtypeStruct((bsz, HEADS, seq // ATT_T, QK_PAD, ATT_T), BF16),
                   jax.ShapeDtypeStruct((bsz, HEADS, seq, QK_PAD), BF16),
                   jax.ShapeDtypeStruct((bsz, HEADS, seq // ATT_T, MLA_V, ATT_T), BF16)],
        compiler_params=pltpu.CompilerParams(dimension_semantics=("parallel",),
                                             vmem_limit_bytes=VMEM_LIMIT),
        name="mla_prep",
    )(tail, cs, cst, qa, wqt, kva, wk, wvt, qn_col, kn)


def _attn_kernel(qt_ref, k_ref, vt_ref, o_ref, s_scr, p_scr, a_scr, m_all, l_all, acc_all, *, nq):
    t = ATT_T

    def qk(qi, j):
        k0 = pl.multiple_of(j * t, t)
        return _dot(k_ref[0, 0, pl.ds(k0, t), :], qt_ref[0, 0, qi])

    def accumulate(tile, alpha, p):
        q, j = tile
        acc_all[q] = alpha * acc_all[q] + _dot(vt_ref[0, 0, j], p)

    def stage(prev, cur, nxt, s_cur, p_prev, alpha_prev, masked):
        accumulate(prev, alpha_prev, p_prev)
        qi = cur[0]
        if masked:
            keep = lax.broadcasted_iota(jnp.int32, (t, t), 0) <= lax.broadcasted_iota(jnp.int32, (t, t), 1)
            s_cur = jnp.where(keep, s_cur, MASK_VALUE)
        m_old = m_all[qi]
        m_new = jnp.maximum(m_old, jnp.max(s_cur, axis=0, keepdims=True))
        m_all[qi] = m_new
        alpha = jnp.exp2(m_old - m_new)
        p = jnp.exp2((s_cur - m_new).astype(BF16))
        l_all[qi] = alpha * l_all[qi] + jnp.sum(p.astype(F32), axis=0, keepdims=True)
        s_next = qk(jnp.minimum(nxt[0], nq - 1), jnp.minimum(nxt[1], nq - 1))
        return s_next, p, alpha

    def advance_lower(qi, j):
        wrap = j + 1 == qi
        return jnp.where(wrap, qi + 1, qi), jnp.where(wrap, 0, j + 1)

    def advance_diag(qi, j):
        return qi + 1, j + 1

    def run_phase(n_steps, first, advance, masked):
        if n_steps == 0:
            return
        s_scr[...] = qk(*first)
        p_scr[...] = jnp.zeros_like(p_scr)
        a_scr[...] = jnp.ones_like(a_scr)

        def steps(count, prev, cur):
            s_cur, p_prev, alpha_prev = s_scr[...], p_scr[...], a_scr[...]
            for _ in range(count):
                nxt = advance(*cur)
                s_cur, p_prev, alpha_prev = stage(prev, cur, nxt, s_cur, p_prev, alpha_prev, masked)
                prev, cur = cur, nxt
            s_scr[...] = s_cur
            p_scr[...] = p_prev
            a_scr[...] = alpha_prev
            return prev, cur

        first = (jnp.int32(first[0]), jnp.int32(first[1]))
        prev, cur = lax.fori_loop(0, n_steps // ATT_UNROLL, lambda _, c: steps(ATT_UNROLL, *c), (first, first))
        if n_steps % ATT_UNROLL:
            prev, cur = steps(n_steps % ATT_UNROLL, prev, cur)
        accumulate(prev, a_scr[...], p_scr[...])

    m_all[...] = jnp.full_like(m_all, MASK_VALUE)
    l_all[...] = jnp.zeros_like(l_all)
    acc_all[...] = jnp.zeros_like(acc_all)
    run_phase(nq * (nq - 1) // 2, (1, 0), advance_lower, False)
    run_phase(nq, (0, 0), advance_diag, True)
    for qi in range(nq):
        o_ref[0, qi * t:(qi + 1) * t, :] = (acc_all[qi] / l_all[qi]).T.astype(o_ref.dtype)


def _attn_bounded_kernel(qt_ref, k_ref, vt_ref, o_ref, s_scr, p_scr, l_all, acc_all, *, nq):
    t = ATT_T

    def qk(qi, j):
        k0 = pl.multiple_of(j * t, t)
        return _dot(k_ref[0, 0, pl.ds(k0, t), :], qt_ref[0, 0, qi])

    def accumulate(tile, p):
        q, j = tile
        acc_all[q] += _dot(vt_ref[0, 0, j], p)

    def stage(prev, cur, nxt, s_cur, p_prev, masked):
        accumulate(prev, p_prev)
        p = jnp.exp2(s_cur)
        if masked:
            keep = lax.broadcasted_iota(jnp.int32, (t, t), 0) <= lax.broadcasted_iota(jnp.int32, (t, t), 1)
            p = jnp.where(keep, p, 0.0)
        l_all[cur[0]] += jnp.sum(p, axis=0, keepdims=True)
        s_next = qk(jnp.minimum(nxt[0], nq - 1), jnp.minimum(nxt[1], nq - 1))
        return s_next, p.astype(BF16)

    def advance_lower(qi, j):
        wrap = j + 1 == qi
        return jnp.where(wrap, qi + 1, qi), jnp.where(wrap, 0, j + 1)

    def advance_diag(qi, j):
        return qi + 1, j + 1

    def run_phase(n_steps, first, advance, masked):
        if n_steps == 0:
            return
        s_scr[...] = qk(*first)
        p_scr[...] = jnp.zeros_like(p_scr)

        def steps(count, prev, cur):
            s_cur, p_prev = s_scr[...], p_scr[...]
            for _ in range(count):
                nxt = advance(*cur)
                s_cur, p_prev = stage(prev, cur, nxt, s_cur, p_prev, masked)
                prev, cur = cur, nxt
            s_scr[...] = s_cur
            p_scr[...] = p_prev
            return prev, cur

        first = (jnp.int32(first[0]), jnp.int32(first[1]))
        unroll = ATT_UNROLL_BOUNDED
        prev, cur = lax.fori_loop(0, n_steps // unroll, lambda _, c: steps(unroll, *c), (first, first))
        if n_steps % unroll:
            prev, cur = steps(n_steps % unroll, prev, cur)
        accumulate(prev, p_scr[...])

    l_all[...] = jnp.zeros_like(l_all)
    acc_all[...] = jnp.zeros_like(acc_all)
    run_phase(nq * (nq - 1) // 2, (1, 0), advance_lower, False)
    run_phase(nq, (0, 0), advance_diag, True)
    for qi in range(nq):
        o_ref[0, qi * t:(qi + 1) * t, :] = (acc_all[qi] / l_all[qi]).T.astype(o_ref.dtype)


def _attention_bounded(qt, k, vt):
    bsz, _, seq, _ = k.shape
    nq = seq // ATT_T
    kern = functools.partial(_attn_bounded_kernel, nq=nq)
    return pl.pallas_call(
        kern,
        grid=(bsz, HEADS),
        in_specs=[pl.BlockSpec((1, 1, nq, QK_PAD, ATT_T), lambda b, h: (b, h, 0, 0, 0)),
                  pl.BlockSpec((1, 1, seq, QK_PAD), lambda b, h: (b, h, 0, 0)),
                  pl.BlockSpec((1, 1, nq, MLA_V, ATT_T), lambda b, h: (b, h, 0, 0, 0))],
        out_specs=pl.BlockSpec((1, seq, MLA_V), lambda b, h: (b, 0, h)),
        out_shape=jax.ShapeDtypeStruct((bsz, seq, HEADS * MLA_V), BF16),
        scratch_shapes=[pltpu.VMEM((ATT_T, ATT_T), F32), pltpu.VMEM((ATT_T, ATT_T), BF16),
                        pltpu.VMEM((nq, 1, ATT_T), F32), pltpu.VMEM((nq, MLA_V, ATT_T), F32)],
        compiler_params=pltpu.CompilerParams(dimension_semantics=("parallel", "parallel"),
                                             vmem_limit_bytes=VMEM_LIMIT),
        name="attn_bounded",
    )(qt, k, vt)


def _attention(qt, k, vt):
    bsz, _, seq, _ = k.shape
    nq = seq // ATT_T
    kern = functools.partial(_attn_kernel, nq=nq)
    return pl.pallas_call(
        kern,
        grid=(bsz, HEADS),
        in_specs=[pl.BlockSpec((1, 1, nq, QK_PAD, ATT_T), lambda b, h: (b, h, 0, 0, 0)),
                  pl.BlockSpec((1, 1, seq, QK_PAD), lambda b, h: (b, h, 0, 0)),
                  pl.BlockSpec((1, 1, nq, MLA_V, ATT_T), lambda b, h: (b, h, 0, 0, 0))],
        out_specs=pl.BlockSpec((1, seq, MLA_V), lambda b, h: (b, 0, h)),
        out_shape=jax.ShapeDtypeStruct((bsz, seq, HEADS * MLA_V), BF16),
        scratch_shapes=[pltpu.VMEM((ATT_T, ATT_T), F32), pltpu.VMEM((ATT_T, ATT_T), BF16),
                        pltpu.VMEM((1, ATT_T), F32), pltpu.VMEM((nq, 1, ATT_T), F32),
                        pltpu.VMEM((nq, 1, ATT_T), F32), pltpu.VMEM((nq, MLA_V, ATT_T), F32)],
        compiler_params=pltpu.CompilerParams(dimension_semantics=("parallel", "parallel"),
                                             vmem_limit_bytes=VMEM_LIMIT),
        name="attn",
    )(qt, k, vt)


def _mlstm_head_out(num_ref, dst_ref, o_ref, hn_ref):
    out_gain = (0.5 * jnp.tanh(0.5 * o_ref[...].astype(F32)) + 0.5) * hn_ref[...]
    heads = []
    for h in range(HEADS):
        hv = slice(h * MLSTM_DV, (h + 1) * MLSTM_DV)
        num = num_ref[:, hv].astype(F32)
        d = dst_ref[:, IGATE_LANE + h:IGATE_LANE + h + 1]
        scale = lax.rsqrt(jnp.mean(num * num, axis=-1, keepdims=True) + NORM_EPS * d * d)
        heads.append((num * scale * out_gain[:, hv]).astype(BF16))
    return jnp.concatenate(heads, axis=1)


def _merge_kernel(ya_ref, gate_ref, num_ref, dst_ref, o_ref, hn_ref, yc_ref, x_ref, wb_ref, wo_ref, out_ref):
    g = gate_ref[...]
    merged = g[:, :D_MODEL].astype(F32) * _dot(ya_ref[...], wb_ref[0])
    y_b = _mlstm_head_out(num_ref, dst_ref, o_ref, hn_ref)
    merged += g[:, D_MODEL:2 * D_MODEL].astype(F32) * _dot(y_b, wb_ref[1])
    merged += g[:, 2 * D_MODEL:].astype(F32) * _dot(yc_ref[...], wb_ref[2])
    out_ref[...] = x_ref[...] + _dot(merged.astype(BF16), wo_ref[...])


def _merge(y_a, main, num, dstat, head_norm, y_c, x2, w_branch, w_out, layer, tm):
    t = x2.shape[0]
    col = lambda c: pl.BlockSpec((tm, D_MODEL), lambda i: (i, c))
    return pl.pallas_call(
        _merge_kernel,
        grid=(t // tm,),
        in_specs=[col(0),
                  pl.BlockSpec((tm, N_BRANCH * D_MODEL), lambda i: (i, GATE_COL0 // (N_BRANCH * D_MODEL))),
                  col(0), pl.BlockSpec((tm, LANES), lambda i: (i, 0)), col(COL_MO // D_MODEL),
                  pl.BlockSpec((1, D_MODEL), lambda i: (0, 0)),
                  col(0), col(0),
                  pl.BlockSpec((None, N_BRANCH, D_MODEL, D_MODEL), lambda i: (layer, 0, 0, 0),
                               pipeline_mode=pl.Buffered(1)),
                  pl.BlockSpec((None, D_MODEL, D_MODEL), lambda i: (layer, 0, 0), pipeline_mode=pl.Buffered(1))],
        out_specs=col(0),
        out_shape=jax.ShapeDtypeStruct((t, D_MODEL), F32),
        compiler_params=pltpu.CompilerParams(dimension_semantics=("parallel",),
                                             vmem_limit_bytes=VMEM_LIMIT),
        name="merge",
    )(y_a, main, num, dstat, main, head_norm, y_c, x2, w_branch, w_out)


def _mlp_kernel(x_ref, g_ref, wu_ref, wd_ref, out_ref, *, n_chunk):
    x = x_ref[...]
    h = _rms(x, g_ref[...]).astype(BF16)
    ck = D_FF // n_chunk
    acc = x
    for c in range(n_chunk):
        u = jnp.maximum(_dot(h, wu_ref[:, c * ck:(c + 1) * ck]), 0.0)
        acc = acc + _dot((u * u).astype(BF16), wd_ref[c * ck:(c + 1) * ck, :])
    out_ref[...] = acc


def _mlp(x2, g, w_up, w_down, layer, tm):
    t = x2.shape[0]
    kern = functools.partial(_mlp_kernel, n_chunk=4)
    return pl.pallas_call(
        kern,
        grid=(t // tm,),
        in_specs=[pl.BlockSpec((tm, D_MODEL), lambda i: (i, 0)),
                  pl.BlockSpec((1, D_MODEL), lambda i: (0, 0)),
                  pl.BlockSpec((None, D_MODEL, D_FF), lambda i: (layer, 0, 0), pipeline_mode=pl.Buffered(1)),
                  pl.BlockSpec((None, D_FF, D_MODEL), lambda i: (layer, 0, 0), pipeline_mode=pl.Buffered(1))],
        out_specs=pl.BlockSpec((tm, D_MODEL), lambda i: (i, 0)),
        out_shape=jax.ShapeDtypeStruct((t, D_MODEL), F32),
        compiler_params=pltpu.CompilerParams(dimension_semantics=("parallel",),
                                             vmem_limit_bytes=VMEM_LIMIT),
        name="mlp",
    )(x2, g, w_up, w_down)


def _regroup_w_in(w_in):
    i0 = CONV_COLS + GATE_COL0
    cq0 = i0 + 2 * HEADS
    g0 = cq0 + Q_LORA + KV_LORA + MLA_ROPE
    pad = jnp.zeros(w_in.shape[:2] + (TAIL_COLS - (Q_LORA + KV_LORA + MLA_ROPE + 2 * HEADS),), w_in.dtype)
    w_main = jnp.concatenate([w_in[..., :i0], w_in[..., g0:]], axis=-1).astype(BF16)
    w_tail = jnp.concatenate([w_in[..., cq0:g0], w_in[..., i0:cq0], pad], axis=-1).astype(BF16)
    return w_main, w_tail


def _heads_last_split(w, first, second):
    k = w.shape[0]
    w3 = w.reshape(k, HEADS, first + second)
    return jnp.concatenate([w3[:, :, :first].reshape(k, HEADS * first),
                            w3[:, :, first:].reshape(k, HEADS * second)], axis=1).astype(BF16)


def _score_bound(q_gain, k_gain):
    def sq_norm(g):
        g = g.astype(F32)
        return MLA_NOPE * jnp.max(g[:MLA_NOPE] ** 2) + MLA_ROPE * jnp.max(g[MLA_NOPE:] ** 2)
    return jnp.sqrt(sq_norm(q_gain) * sq_norm(k_gain)) * (Q_SCALE * BOUND_SLACK)


def _pick(n, candidates):
    for c in candidates:
        if n % c == 0:
            return c
    raise ValueError(f"no tile size for {n}")


def kernel(x, positions, mix_norm, w_in, conv_w, mlstm_igate_bias, mlstm_fgate_bias, mlstm_head_norm,
           mla_q_a_norm, mla_w_uq, mla_kv_a_norm, mla_w_ukv, mla_q_norm, mla_k_norm, w_branch, w_out,
           mlp_norm, w_up, w_down):
    bsz, seq, _ = x.shape
    depth = w_in.shape[0]
    t = bsz * seq
    assert seq % ATT_T == 0
    n_chunk = _pick(seq // CHUNK, tuple(g for g in (8, 4, 2, 1) if g <= MLSTM_GROUP))
    tm_proj = _pick(seq, (1024, 512, 256, 128))
    tm = _pick(seq, (512, 256, 128))
    tm_mlp = _pick(t, (1024, 512, 256, 128))

    x2 = x.reshape(t, D_MODEL)
    cs, cst = _rope_table(positions, ATT_T)
    row = lambda a: a.reshape(1, -1).astype(F32)
    w_main, w_tail = _regroup_w_in(w_in)
    w_branch_b, w_out_b, w_up_b, w_down_b = (w.astype(BF16) for w in (w_branch, w_out, w_up, w_down))

    for l in range(depth):
        gate_bias = jnp.zeros((1, LANES), F32)
        gate_bias = gate_bias.at[0, IGATE_LANE:IGATE_LANE + HEADS].set(mlstm_igate_bias[l])
        gate_bias = gate_bias.at[0, FGATE_LANE:FGATE_LANE + HEADS].set(mlstm_fgate_bias[l])
        w_q = _heads_last_split(mla_w_uq[l], MLA_NOPE, MLA_ROPE)
        w_kv = _heads_last_split(mla_w_ukv[l], MLA_NOPE, MLA_V)

        y_a, main, tail = _in_proj(x2, row(mix_norm[l]), w_main, w_tail, conv_w[l].astype(F32), l, seq, tm_proj)
        qt, k, vt = _mla_prep(tail, cs, cst, row(mla_q_a_norm[l]), w_q.T, row(mla_kv_a_norm[l]),
                              w_kv[:, :HEADS * MLA_NOPE], w_kv[:, HEADS * MLA_NOPE:].T,
                              mla_q_norm[l].reshape(-1, 1).astype(F32) * Q_SCALE, row(mla_k_norm[l]), bsz, seq)
        num, dstat = _mlstm(main, tail, gate_bias, bsz, seq, n_chunk)
        y_c = lax.cond(_score_bound(mla_q_norm[l], mla_k_norm[l]) <= SCORE_BOUND,
                       _attention_bounded, _attention, qt, k, vt).reshape(t, D_MODEL)
        x2 = _merge(y_a, main, num, dstat, row(mlstm_head_norm[l]), y_c, x2, w_branch_b, w_out_b, l, tm)
        x2 = _mlp(x2, row(mlp_norm[l]), w_up_b, w_down_b, l, tm_mlp)
    return x2.reshape(bsz, seq, D_MODEL)
```

```python
import functools

import jax
import jax.numpy as jnp
import numpy as np
from jax import lax
from jax.experimental import pallas as pl
from jax.experimental.pallas import tpu as pltpu

F32 = jnp.float32
BF16 = jnp.bfloat16

D_MODEL = 1024
N_BRANCH = 3
CONV_WIDTH = 3
HEADS = 8
MLSTM_DK = 64
MLSTM_DV = 128
CHUNK = 128
GATE_CAP = 15.0
MLA_NOPE = 128
MLA_ROPE = 64
MLA_V = 128
Q_LORA = 256
KV_LORA = 128
ROPE_BASE = 10000.0
D_FF = 4 * D_MODEL
NORM_EPS = 1e-6
QK_PAD = 256

CONV_COLS = 3 * D_MODEL
PROJ_COLS = 9 * D_MODEL
MAIN_COLS = PROJ_COLS - CONV_COLS
TAIL_COLS = 512
COL_MQ = 0
COL_MK = COL_MQ + HEADS * MLSTM_DK
COL_MV = COL_MK + HEADS * MLSTM_DK
COL_MO = COL_MV + HEADS * MLSTM_DV
GATE_COL0 = COL_MO + HEADS * MLSTM_DV
LANES = 128
SUBLANES = 8
TAIL_KROPE = Q_LORA + KV_LORA
IGATE_LANE = MLA_ROPE
FGATE_LANE = MLA_ROPE + HEADS

IN_PROJ_TN = 3 * D_MODEL
IN_PROJ_CHUNK = 1536
CONV_CHUNK = 256
MLSTM_GROUP = 8
ATT_T = 512
LOG2E = 1.4426950408889634
Q_SCALE = (MLA_NOPE + MLA_ROPE) ** -0.5 * LOG2E
MASK_VALUE = -1e30
SCORE_BOUND = 70.0
SCORE_SHIFT = 48.0
BOUND_SLACK = 1.02
ATT_UNROLL = 8
ATT_UNROLL_BOUNDED = 24

VMEM_LIMIT = 56 * 1024 * 1024


def _rms(x, g):
    return x * lax.rsqrt(jnp.mean(x * x, axis=-1, keepdims=True) + NORM_EPS) * g


def _dot(a, b):
    return jnp.dot(a, b, preferred_element_type=F32)


def _dot_nt(a, b):
    return lax.dot_general(a, b, (((1,), (1,)), ((), ())), preferred_element_type=F32)


def _rope_table_kernel(pos_ref, freq_ref, cs_ref, cst_ref):
    ang = freq_ref[...] * pos_ref[...].astype(F32)
    c, s = jnp.cos(ang), jnp.sin(ang)
    cst_ref[...] = jnp.concatenate([c, s], axis=0)
    cs_ref[...] = jnp.concatenate([c, c, -s, s], axis=0).T


def _rope_table(positions, tm):
    t = positions.size
    half = MLA_ROPE // 2
    inv_freq = (np.float32(ROPE_BASE) ** (-np.arange(0, MLA_ROPE, 2, dtype=np.float32) / np.float32(MLA_ROPE)))
    inv_freq = jnp.asarray(inv_freq.astype(np.float32))
    return pl.pallas_call(
        _rope_table_kernel,
        grid=(t // tm,),
        in_specs=[pl.BlockSpec((1, tm), lambda i: (0, i)),
                  pl.BlockSpec((half, 1), lambda i: (0, 0))],
        out_specs=[pl.BlockSpec((tm, 2 * MLA_ROPE), lambda i: (i, 0)),
                   pl.BlockSpec((MLA_ROPE, tm), lambda i: (0, i))],
        out_shape=[jax.ShapeDtypeStruct((t, 2 * MLA_ROPE), F32),
                   jax.ShapeDtypeStruct((MLA_ROPE, t), F32)],
        name="rope_table",
    )(positions.reshape(1, t), inv_freq.reshape(half, 1))


def _in_proj_kernel(x_ref, g_ref, w_ref, wt_ref, cw_ref, ya_ref, main_ref, tail_ref, xn_ref, zprev_ref, *,
                    tiles_per_seq):
    i, j = pl.program_id(0), pl.program_id(1)
    tm = x_ref.shape[0]

    @pl.when(jnp.logical_and(j == 0, i % tiles_per_seq == 0))
    def _():
        zprev_ref[...] = jnp.zeros_like(zprev_ref)

    @pl.when(j == 0)
    def _():
        xn = _rms(x_ref[...], g_ref[...]).astype(BF16)
        xn_ref[...] = xn
        tail_ref[...] = _dot(xn, wt_ref[...])
        row = lax.broadcasted_iota(jnp.int32, (tm, CONV_CHUNK), 0)
        for c0 in range(0, D_MODEL, CONV_CHUNK):
            c = slice(c0, c0 + CONV_CHUNK)
            gate_b, gate_c, u = (_dot(xn, w_ref[:, n * D_MODEL + c0:n * D_MODEL + c0 + CONV_CHUNK]) for n in range(3))
            z = gate_c * u
            p1, p2 = zprev_ref[SUBLANES - 1:SUBLANES, c], zprev_ref[SUBLANES - 2:SUBLANES - 1, c]
            z1 = jnp.where(row == 0, p1, pltpu.roll(z, 1, 0))
            z2 = jnp.where(row == 0, p2, jnp.where(row == 1, p1, pltpu.roll(z, 2, 0)))
            zprev_ref[:, c] = z[tm - SUBLANES:, :]
            ya_ref[:, c] = (gate_b * (cw_ref[0:1, c] * z2 + cw_ref[1:2, c] * z1 + cw_ref[2:3, c] * z)).astype(BF16)

    cols = [slice(c, c + IN_PROJ_CHUNK) for c in range(0, w_ref.shape[1], IN_PROJ_CHUNK)]

    @pl.when(j == 1)
    def _():
        for c in cols:
            main_ref[:, c] = _dot(xn_ref[...], w_ref[:, c]).astype(BF16)

    @pl.when(j == 2)
    def _():
        for c in cols:
            main_ref[:, c] = jax.nn.sigmoid(_dot(xn_ref[...], w_ref[:, c])).astype(BF16)


def _in_proj(x2, g, w_main, w_tail, conv_w, layer, seq, tm):
    t = x2.shape[0]
    tn = IN_PROJ_TN
    assert CONV_COLS == tn and GATE_COL0 == tn and MAIN_COLS == 2 * tn
    kern = functools.partial(_in_proj_kernel, tiles_per_seq=seq // tm)
    return pl.pallas_call(
        kern,
        grid=(t // tm, PROJ_COLS // tn),
        in_specs=[pl.BlockSpec((tm, D_MODEL), lambda i, j: (i, 0)),
                  pl.BlockSpec((1, D_MODEL), lambda i, j: (0, 0)),
                  pl.BlockSpec((None, D_MODEL, tn), lambda i, j: (layer, 0, j)),
                  pl.BlockSpec((None, D_MODEL, TAIL_COLS), lambda i, j: (layer, 0, 0)),
                  pl.BlockSpec((CONV_WIDTH, D_MODEL), lambda i, j: (0, 0))],
        out_specs=[pl.BlockSpec((tm, D_MODEL), lambda i, j: (i, 0)),
                   pl.BlockSpec((tm, tn), lambda i, j: (i, jnp.maximum(j - 1, 0))),
                   pl.BlockSpec((tm, TAIL_COLS), lambda i, j: (i, 0))],
        out_shape=[jax.ShapeDtypeStruct((t, D_MODEL), BF16),
                   jax.ShapeDtypeStruct((t, MAIN_COLS), BF16),
                   jax.ShapeDtypeStruct((t, TAIL_COLS), F32)],
        scratch_shapes=[pltpu.VMEM((tm, D_MODEL), BF16), pltpu.VMEM((SUBLANES, D_MODEL), F32)],
        compiler_params=pltpu.CompilerParams(
            dimension_semantics=("arbitrary", "arbitrary"), vmem_limit_bytes=VMEM_LIMIT),
        name="in_proj",
    )(x2, g, w_main, w_tail, conv_w)


def _time_scan(x, op, row):
    s = 1
    while s < CHUNK:
        x = jnp.where(row >= s, op(x, pltpu.roll(x, s, 0)), x)
        s *= 2
    return x


def _mlstm_chunk(rows, q_ref, k_ref, v_ref, gate_ref, bias_ref, num_ref, dst_ref, c_state, m_prev):
    L = CHUNK
    row = lax.broadcasted_iota(jnp.int32, (L, LANES), 0)
    lane = lax.broadcasted_iota(jnp.int32, (L, LANES), 1)
    gate_lanes = jnp.logical_and(lane >= IGATE_LANE, lane < FGATE_LANE + HEADS)
    tri = (lax.broadcasted_iota(jnp.int32, (L, L), 0) >= lax.broadcasted_iota(jnp.int32, (L, L), 1))
    lane_v = lax.broadcasted_iota(jnp.int32, (L, MLSTM_DV), 1)
    c_state = list(c_state)
    pre = jnp.where(gate_lanes, gate_ref[rows, :] + bias_ref[...], 0.0)
    capped = GATE_CAP * jnp.tanh(pre * (1.0 / GATE_CAP))
    log_i = capped * LOG2E
    log_f = (jnp.minimum(capped, 0.0) - jnp.log1p(jnp.exp(-jnp.abs(capped)))) * LOG2E
    log_f = pltpu.roll(log_f, LANES - HEADS, 1)
    b = _time_scan(log_f, jnp.add, row)
    b_last = b[L - 1:L, :]
    rowd = log_i - b
    big_m = jnp.maximum(_time_scan(rowd, jnp.maximum, row), m_prev)
    e_neg_m = jnp.exp2(-(b + big_m))
    a = b_last + rowd
    m_loc = jnp.max(a, axis=0, keepdims=True)
    m_new = jnp.maximum(b_last + m_prev, m_loc)
    s_old = jnp.exp2(b_last + m_prev - m_new)
    w = jnp.exp2(a - m_new)
    rowd_t = rowd.T
    w_t = w.T
    q_all = q_ref[rows, :] * (MLSTM_DK ** -0.5)
    k_t = k_ref[rows, :].astype(F32).T

    hk = [slice(h * MLSTM_DK, (h + 1) * MLSTM_DK) for h in range(HEADS)]
    hv = [slice(h * MLSTM_DV, (h + 1) * MLSTM_DV) for h in range(HEADS)]
    gl = [slice(IGATE_LANE + h, IGATE_LANE + h + 1) for h in range(HEADS)]
    v_aug = [jnp.concatenate([v_ref[rows, hv[h]], (lane_v == IGATE_LANE + h).astype(BF16)], axis=1)
             for h in range(HEADS)]
    scores = [_dot(q_all[:, hk[h]], k_t[hk[h], :].astype(BF16)) for h in range(HEADS)]
    c_loc = [_dot((k_t[hk[h], :] * w_t[gl[h], :]).astype(BF16), v_aug[h]) for h in range(HEADS)]
    big_m_b = [jnp.broadcast_to(big_m[:, gl[h]], (L, L)) for h in range(HEADS)]
    dmat = [jnp.where(tri, jnp.exp2(rowd_t[gl[h], :] - big_m_b[h]), 0.0) for h in range(HEADS)]
    q_state = [(q_all[:, hk[h]].astype(F32)
                * jnp.exp2(m_prev[:, gl[h]] - big_m_b[h][:, :MLSTM_DK])).astype(BF16) for h in range(HEADS)]
    num_aug = []
    for h in range(HEADS):
        lhs = jnp.concatenate([(scores[h] * dmat[h]).astype(BF16), q_state[h],
                               jnp.zeros((L, L - MLSTM_DK), BF16)], axis=1)
        rhs = jnp.concatenate([v_aug[h], c_state[h].astype(BF16),
                               jnp.zeros((L - MLSTM_DK, 2 * MLSTM_DV), BF16)], axis=0)
        num_aug.append(_dot(lhs, rhs))
    for h in range(HEADS):
        num_ref[rows, hv[h]] = num_aug[h][:, :MLSTM_DV].astype(num_ref.dtype)
        c_state[h] = s_old[:, gl[h]] * c_state[h] + c_loc[h]
    den = functools.reduce(jnp.add, [n[:, MLSTM_DV:] for n in num_aug])
    dst_ref[rows, :] = jnp.maximum(jnp.abs(den), e_neg_m)
    return c_state, m_new


def _mlstm_kernel(q_ref, k_ref, v_ref, gate_ref, bias_ref, num_ref, dst_ref, c_ref, m_ref, *, n_chunk):
    @pl.when(pl.program_id(1) == 0)
    def _():
        c_ref[...] = jnp.zeros_like(c_ref)
        m_ref[...] = jnp.zeros_like(m_ref)

    m_prev = m_ref[...]
    c_state = [c_ref[h] for h in range(HEADS)]
    for g in range(n_chunk):
        c_state, m_prev = _mlstm_chunk(slice(g * CHUNK, (g + 1) * CHUNK), q_ref, k_ref, v_ref, gate_ref, bias_ref,
                                       num_ref, dst_ref, c_state, m_prev)
    m_ref[...] = m_prev
    for h in range(HEADS):
        c_ref[h] = c_state[h]


def _mlstm(main, tail, gate_bias, bsz, seq, n_chunk):
    t = main.shape[0]
    rows = n_chunk * CHUNK
    nc = seq // rows
    row = lambda b, c: b * nc + c
    col_block = lambda col0, width: pl.BlockSpec((rows, width), lambda b, c: (row(b, c), col0 // width))
    return pl.pallas_call(
        functools.partial(_mlstm_kernel, n_chunk=n_chunk),
        grid=(bsz, nc),
        in_specs=[col_block(COL_MQ, HEADS * MLSTM_DK), col_block(COL_MK, HEADS * MLSTM_DK),
                  col_block(COL_MV, HEADS * MLSTM_DV),
                  col_block(TAIL_KROPE, LANES),
                  pl.BlockSpec((1, LANES), lambda b, c: (0, 0))],
        out_specs=[pl.BlockSpec((rows, D_MODEL), lambda b, c: (row(b, c), 0)),
                   pl.BlockSpec((rows, LANES), lambda b, c: (row(b, c), 0))],
        out_shape=[jax.ShapeDtypeStruct((t, D_MODEL), BF16), jax.ShapeDtypeStruct((t, LANES), F32)],
        scratch_shapes=[pltpu.VMEM((HEADS, MLSTM_DK, 2 * MLSTM_DV), F32),
                        pltpu.VMEM((1, LANES), F32)],
        compiler_params=pltpu.CompilerParams(dimension_semantics=("parallel", "arbitrary")),
        name="mlstm",
    )(main, main, main, tail, gate_bias)


def _rms0(x, g_col):
    return x * lax.rsqrt(jnp.mean(x * x, axis=0, keepdims=True) + NORM_EPS) * g_col


def _rope(x, cs):
    half = MLA_ROPE // 2
    rot = jnp.concatenate([x[:, half:], x[:, :half]], axis=-1)
    return x * cs[:, :MLA_ROPE] + rot * cs[:, MLA_ROPE:]


def _mla_prep_kernel(tail_ref, cs_ref, cst_ref, qa_ref, wqt_ref, kva_ref, wk_ref, wvt_ref, qn_ref, kn_ref,
                     qt_out, k_out, vt_out):
    tm = ATT_T
    half = MLA_ROPE // 2
    qn_g, qp_g = qn_ref[:MLA_NOPE, :], qn_ref[MLA_NOPE:, :]
    kn_g, kp_g = kn_ref[:, :MLA_NOPE], kn_ref[:, MLA_NOPE:]
    pe0 = HEADS * MLA_NOPE
    pad = QK_PAD - MLA_NOPE - MLA_ROPE
    q_pad = jnp.where(lax.broadcasted_iota(jnp.int32, (pad, tm), 0) == 0, -SCORE_SHIFT, 0.0).astype(BF16)
    k_pad = (lax.broadcasted_iota(jnp.int32, (tm, pad), 1) == 0).astype(BF16)
    for s in range(tail_ref.shape[0] // tm):
        rows = slice(s * tm, (s + 1) * tm)
        cq = _rms(tail_ref[rows, :Q_LORA], qa_ref[...]).astype(BF16)
        ckv = _rms(tail_ref[rows, Q_LORA:Q_LORA + KV_LORA], kva_ref[...]).astype(BF16)
        qft = _dot_nt(wqt_ref[...], cq)
        kf = _dot(ckv, wk_ref[...])
        vft = _dot_nt(wvt_ref[...], ckv)
        cos_t, sin_t = cst_ref[:half, rows], cst_ref[half:, rows]
        k_pe = _rope(_rms(tail_ref[rows, TAIL_KROPE:TAIL_KROPE + MLA_ROPE], kp_g), cs_ref[rows, :]).astype(BF16)
        for h in range(HEADS):
            q_nope = _rms0(qft[h * MLA_NOPE:(h + 1) * MLA_NOPE], qn_g)
            q_pe = _rms0(qft[pe0 + h * MLA_ROPE:pe0 + (h + 1) * MLA_ROPE], qp_g)
            x1, x2 = q_pe[:half], q_pe[half:]
            q_rot = jnp.concatenate([x1 * cos_t - x2 * sin_t, x1 * sin_t + x2 * cos_t], axis=0)
            qt_out[0, h, s] = jnp.concatenate([q_nope.astype(BF16), q_rot.astype(BF16), q_pad], axis=0)
            k_nope = _rms(kf[:, h * MLA_NOPE:(h + 1) * MLA_NOPE], kn_g)
            k_out[0, h, rows, :] = jnp.concatenate([k_nope.astype(BF16), k_pe, k_pad], axis=-1)
            vt_out[0, h, s] = vft[h * MLA_V:(h + 1) * MLA_V].astype(BF16)


def _mla_prep(tail, cs, cst, qa, wqt, kva, wk, wvt, qn_col, kn, bsz, seq):
    tiles = 2 if (seq // ATT_T) % 2 == 0 else 1
    tm = tiles * ATT_T
    nt = seq // tm
    full = lambda shape: pl.BlockSpec(shape, lambda i: (0,) * len(shape))
    blocked = lambda rows: pl.BlockSpec((1, HEADS, tiles, rows, ATT_T), lambda i: (i // nt, 0, i % nt, 0, 0))
    return pl.pallas_call(
        _mla_prep_kernel,
        grid=(bsz * nt,),
        in_specs=[pl.BlockSpec((tm, TAIL_COLS), lambda i: (i, 0)),
                  pl.BlockSpec((tm, 2 * MLA_ROPE), lambda i: (i, 0)),
                  pl.BlockSpec((MLA_ROPE, tm), lambda i: (0, i)),
                  full((1, Q_LORA)), full(wqt.shape), full((1, KV_LORA)), full(wk.shape), full(wvt.shape),
                  full((MLA_NOPE + MLA_ROPE, 1)), full((1, MLA_NOPE + MLA_ROPE))],
        out_specs=[blocked(QK_PAD),
                   pl.BlockSpec((1, HEADS, tm, QK_PAD), lambda i: (i // nt, 0, i % nt, 0)),
                   blocked(MLA_V)],
        out_shape=[jax.ShapeDtypeStruct((bsz, HEADS, seq // ATT_T, QK_PAD, ATT_T), BF16),
                   jax.ShapeDtypeStruct((bsz, HEADS, seq, QK_PAD), BF16),
                   jax.ShapeDtypeStruct((bsz, HEADS, seq // ATT_T, MLA_V, ATT_T), BF16)],
        compiler_params=pltpu.CompilerParams(dimension_semantics=("parallel",),
                                             vmem_limit_bytes=VMEM_LIMIT),
        name="mla_prep",
    )(tail, cs, cst, qa, wqt, kva, wk, wvt, qn_col, kn)


def _attn_kernel(qt_ref, k_ref, vt_ref, o_ref, s_scr, p_scr, l_all, acc_all, *online_scr, nq, online):
    t = ATT_T
    unroll = ATT_UNROLL if online else ATT_UNROLL_BOUNDED
    a_scr, m_all = online_scr if online else (None, None)

    def qk(qi, j):
        k0 = pl.multiple_of(j * t, t)
        return _dot(k_ref[0, 0, pl.ds(k0, t), :], qt_ref[0, 0, qi])

    def accumulate(tile, alpha, p):
        q, j = tile
        pv = _dot(vt_ref[0, 0, j], p)
        acc_all[q] = alpha * acc_all[q] + pv if online else acc_all[q] + pv

    def stage(prev, cur, nxt, s_cur, p_prev, alpha_prev, masked):
        accumulate(prev, alpha_prev, p_prev)
        qi = cur[0]
        if masked:
            keep = lax.broadcasted_iota(jnp.int32, (t, t), 0) <= lax.broadcasted_iota(jnp.int32, (t, t), 1)
        if online:
            if masked:
                s_cur = jnp.where(keep, s_cur, MASK_VALUE)
            m_old = m_all[qi]
            m_new = jnp.maximum(m_old, jnp.max(s_cur, axis=0, keepdims=True))
            m_all[qi] = m_new
            alpha = jnp.exp2(m_old - m_new)
            p = jnp.exp2((s_cur - m_new).astype(BF16))
            l_all[qi] = alpha * l_all[qi] + jnp.sum(p.astype(F32), axis=0, keepdims=True)
        else:
            alpha = None
            p = jnp.exp2(s_cur)
            if masked:
                p = jnp.where(keep, p, 0.0)
            l_all[qi] += jnp.sum(p, axis=0, keepdims=True)
            p = p.astype(BF16)
        s_next = qk(jnp.minimum(nxt[0], nq - 1), jnp.minimum(nxt[1], nq - 1))
        return s_next, p, alpha

    def advance_lower(qi, j):
        wrap = j + 1 == qi
        return jnp.where(wrap, qi + 1, qi), jnp.where(wrap, 0, j + 1)

    def advance_diag(qi, j):
        return qi + 1, j + 1

    def run_phase(n_steps, first, advance, masked):
        if n_steps == 0:
            return
        s_scr[...] = qk(*first)
        p_scr[...] = jnp.zeros_like(p_scr)
        if online:
            a_scr[...] = jnp.ones_like(a_scr)

        def steps(count, prev, cur):
            s_cur, p_prev = s_scr[...], p_scr[...]
            alpha_prev = a_scr[...] if online else None
            for _ in range(count):
                nxt = advance(*cur)
                s_cur, p_prev, alpha_prev = stage(prev, cur, nxt, s_cur, p_prev, alpha_prev, masked)
                prev, cur = cur, nxt
            s_scr[...] = s_cur
            p_scr[...] = p_prev
            if online:
                a_scr[...] = alpha_prev
            return prev, cur

        first = (jnp.int32(first[0]), jnp.int32(first[1]))
        prev, cur = lax.fori_loop(0, n_steps // unroll, lambda _, c: steps(unroll, *c), (first, first))
        if n_steps % unroll:
            prev, cur = steps(n_steps % unroll, prev, cur)
        accumulate(prev, a_scr[...] if online else None, p_scr[...])

    if online:
        m_all[...] = jnp.full_like(m_all, MASK_VALUE)
    l_all[...] = jnp.zeros_like(l_all)
    acc_all[...] = jnp.zeros_like(acc_all)
    run_phase(nq * (nq - 1) // 2, (1, 0), advance_lower, False)
    run_phase(nq, (0, 0), advance_diag, True)
    for qi in range(nq):
        o_ref[0, qi * t:(qi + 1) * t, :] = (acc_all[qi] / l_all[qi]).T.astype(o_ref.dtype)


def _attention(qt, k, vt, *, online):
    bsz, _, seq, _ = k.shape
    nq = seq // ATT_T
    online_scr = [pltpu.VMEM((1, ATT_T), F32), pltpu.VMEM((nq, 1, ATT_T), F32)] if online else []
    return pl.pallas_call(
        functools.partial(_attn_kernel, nq=nq, online=online),
        grid=(bsz, HEADS),
        in_specs=[pl.BlockSpec((1, 1, nq, QK_PAD, ATT_T), lambda b, h: (b, h, 0, 0, 0)),
                  pl.BlockSpec((1, 1, seq, QK_PAD), lambda b, h: (b, h, 0, 0)),
                  pl.BlockSpec((1, 1, nq, MLA_V, ATT_T), lambda b, h: (b, h, 0, 0, 0))],
        out_specs=pl.BlockSpec((1, seq, MLA_V), lambda b, h: (b, 0, h)),
        out_shape=jax.ShapeDtypeStruct((bsz, seq, HEADS * MLA_V), BF16),
        scratch_shapes=[pltpu.VMEM((ATT_T, ATT_T), F32), pltpu.VMEM((ATT_T, ATT_T), BF16),
                        pltpu.VMEM((nq, 1, ATT_T), F32), pltpu.VMEM((nq, MLA_V, ATT_T), F32)] + online_scr,
        compiler_params=pltpu.CompilerParams(dimension_semantics=("parallel", "parallel"),
                                             vmem_limit_bytes=VMEM_LIMIT),
        name="attn_online" if online else "attn_bounded",
    )(qt, k, vt)


def _mlstm_head_out(num_ref, dst_ref, o_ref, hn_ref):
    out_gain = (0.5 * jnp.tanh(0.5 * o_ref[...].astype(F32)) + 0.5) * hn_ref[...]
    heads = []
    for h in range(HEADS):
        hv = slice(h * MLSTM_DV, (h + 1) * MLSTM_DV)
        num = num_ref[:, hv].astype(F32)
        d = dst_ref[:, IGATE_LANE + h:IGATE_LANE + h + 1]
        scale = lax.rsqrt(jnp.mean(num * num, axis=-1, keepdims=True) + NORM_EPS * d * d)
        heads.append((num * scale * out_gain[:, hv]).astype(BF16))
    return jnp.concatenate(heads, axis=1)


def _merge_kernel(ya_ref, gate_ref, num_ref, dst_ref, o_ref, hn_ref, yc_ref, x_ref, wb_ref, wo_ref, out_ref):
    g = gate_ref[...]
    merged = g[:, :D_MODEL].astype(F32) * _dot(ya_ref[...], wb_ref[0])
    y_b = _mlstm_head_out(num_ref, dst_ref, o_ref, hn_ref)
    merged += g[:, D_MODEL:2 * D_MODEL].astype(F32) * _dot(y_b, wb_ref[1])
    merged += g[:, 2 * D_MODEL:].astype(F32) * _dot(yc_ref[...], wb_ref[2])
    out_ref[...] = x_ref[...] + _dot(merged.astype(BF16), wo_ref[...])


def _merge(y_a, main, num, dstat, head_norm, y_c, x2, w_branch, w_out, layer, tm):
    t = x2.shape[0]
    col = lambda c: pl.BlockSpec((tm, D_MODEL), lambda i: (i, c))
    return pl.pallas_call(
        _merge_kernel,
        grid=(t // tm,),
        in_specs=[col(0),
                  pl.BlockSpec((tm, N_BRANCH * D_MODEL), lambda i: (i, GATE_COL0 // (N_BRANCH * D_MODEL))),
                  col(0), pl.BlockSpec((tm, LANES), lambda i: (i, 0)), col(COL_MO // D_MODEL),
                  pl.BlockSpec((1, D_MODEL), lambda i: (0, 0)),
                  col(0), col(0),
                  pl.BlockSpec((None, N_BRANCH, D_MODEL, D_MODEL), lambda i: (layer, 0, 0, 0),
                               pipeline_mode=pl.Buffered(1)),
                  pl.BlockSpec((None, D_MODEL, D_MODEL), lambda i: (layer, 0, 0), pipeline_mode=pl.Buffered(1))],
        out_specs=col(0),
        out_shape=jax.ShapeDtypeStruct((t, D_MODEL), F32),
        compiler_params=pltpu.CompilerParams(dimension_semantics=("parallel",),
                                             vmem_limit_bytes=VMEM_LIMIT),
        name="merge",
    )(y_a, main, num, dstat, main, head_norm, y_c, x2, w_branch, w_out)


def _mlp_kernel(x_ref, g_ref, wu_ref, wd_ref, out_ref, *, n_chunk):
    x = x_ref[...]
    h = _rms(x, g_ref[...]).astype(BF16)
    ck = D_FF // n_chunk
    acc = x
    for c in range(n_chunk):
        u = jnp.maximum(_dot(h, wu_ref[:, c * ck:(c + 1) * ck]), 0.0)
        acc = acc + _dot((u * u).astype(BF16), wd_ref[c * ck:(c + 1) * ck, :])
    out_ref[...] = acc


def _mlp(x2, g, w_up, w_down, layer, tm):
    t = x2.shape[0]
    kern = functools.partial(_mlp_kernel, n_chunk=4)
    return pl.pallas_call(
        kern,
        grid=(t // tm,),
        in_specs=[pl.BlockSpec((tm, D_MODEL), lambda i: (i, 0)),
                  pl.BlockSpec((1, D_MODEL), lambda i: (0, 0)),
                  pl.BlockSpec((None, D_MODEL, D_FF), lambda i: (layer, 0, 0), pipeline_mode=pl.Buffered(1)),
                  pl.BlockSpec((None, D_FF, D_MODEL), lambda i: (layer, 0, 0), pipeline_mode=pl.Buffered(1))],
        out_specs=pl.BlockSpec((tm, D_MODEL), lambda i: (i, 0)),
        out_shape=jax.ShapeDtypeStruct((t, D_MODEL), F32),
        compiler_params=pltpu.CompilerParams(dimension_semantics=("parallel",),
                                             vmem_limit_bytes=VMEM_LIMIT),
        name="mlp",
    )(x2, g, w_up, w_down)


def _regroup_w_in(w_in):
    i0 = CONV_COLS + GATE_COL0
    cq0 = i0 + 2 * HEADS
    g0 = cq0 + Q_LORA + KV_LORA + MLA_ROPE
    pad = jnp.zeros(w_in.shape[:2] + (TAIL_COLS - (Q_LORA + KV_LORA + MLA_ROPE + 2 * HEADS),), w_in.dtype)
    w_main = jnp.concatenate([w_in[..., :i0], w_in[..., g0:]], axis=-1).astype(BF16)
    w_tail = jnp.concatenate([w_in[..., cq0:g0], w_in[..., i0:cq0], pad], axis=-1).astype(BF16)
    return w_main, w_tail


def _heads_last_split(w, first, second):
    k = w.shape[0]
    w3 = w.reshape(k, HEADS, first + second)
    return jnp.concatenate([w3[:, :, :first].reshape(k, HEADS * first),
                            w3[:, :, first:].reshape(k, HEADS * second)], axis=1).astype(BF16)


def _score_bound(q_gain, k_gain):
    def sq_norm(g):
        g = g.astype(F32)
        return MLA_NOPE * jnp.max(g[:MLA_NOPE] ** 2) + MLA_ROPE * jnp.max(g[MLA_NOPE:] ** 2)
    return jnp.sqrt(sq_norm(q_gain) * sq_norm(k_gain)) * (Q_SCALE * BOUND_SLACK)


def _pick(n, candidates):
    for c in candidates:
        if n % c == 0:
            return c
    raise ValueError(f"no tile size for {n}")


def kernel(x, positions, mix_norm, w_in, conv_w, mlstm_igate_bias, mlstm_fgate_bias, mlstm_head_norm,
           mla_q_a_norm, mla_w_uq, mla_kv_a_norm, mla_w_ukv, mla_q_norm, mla_k_norm, w_branch, w_out,
           mlp_norm, w_up, w_down):
    bsz, seq, _ = x.shape
    depth = w_in.shape[0]
    t = bsz * seq
    assert seq % ATT_T == 0
    n_chunk = _pick(seq // CHUNK, tuple(g for g in (8, 4, 2, 1) if g <= MLSTM_GROUP))
    tm_proj = _pick(seq, (1024, 512, 256, 128))
    tm = _pick(seq, (512, 256, 128))
    tm_mlp = _pick(t, (1024, 512, 256, 128))

    x2 = x.reshape(t, D_MODEL)
    cs, cst = _rope_table(positions, ATT_T)
    row = lambda a: a.reshape(1, -1).astype(F32)
    w_main, w_tail = _regroup_w_in(w_in)
    w_branch_b, w_out_b, w_up_b, w_down_b = (w.astype(BF16) for w in (w_branch, w_out, w_up, w_down))

    for l in range(depth):
        gate_bias = jnp.zeros((1, LANES), F32)
        gate_bias = gate_bias.at[0, IGATE_LANE:IGATE_LANE + HEADS].set(mlstm_igate_bias[l])
        gate_bias = gate_bias.at[0, FGATE_LANE:FGATE_LANE + HEADS].set(mlstm_fgate_bias[l])
        w_q = _heads_last_split(mla_w_uq[l], MLA_NOPE, MLA_ROPE)
        w_kv = _heads_last_split(mla_w_ukv[l], MLA_NOPE, MLA_V)

        y_a, main, tail = _in_proj(x2, row(mix_norm[l]), w_main, w_tail, conv_w[l].astype(F32), l, seq, tm_proj)
        qt, k, vt = _mla_prep(tail, cs, cst, row(mla_q_a_norm[l]), w_q.T, row(mla_kv_a_norm[l]),
                              w_kv[:, :HEADS * MLA_NOPE], w_kv[:, HEADS * MLA_NOPE:].T,
                              mla_q_norm[l].reshape(-1, 1).astype(F32) * Q_SCALE, row(mla_k_norm[l]), bsz, seq)
        num, dstat = _mlstm(main, tail, gate_bias, bsz, seq, n_chunk)
        y_c = lax.cond(_score_bound(mla_q_norm[l], mla_k_norm[l]) <= SCORE_BOUND,
                       functools.partial(_attention, online=False), functools.partial(_attention, online=True),
                       qt, k, vt).reshape(t, D_MODEL)
        x2 = _merge(y_a, main, num, dstat, row(mlstm_head_norm[l]), y_c, x2, w_branch_b, w_out_b, l, tm)
        x2 = _mlp(x2, row(mlp_norm[l]), w_up_b, w_down_b, l, tm_mlp)
    return x2.reshape(bsz, seq, D_MODEL)
```

```python
import functools

import jax
import jax.numpy as jnp
import numpy as np
from jax import lax
from jax.experimental import pallas as pl
from jax.experimental.pallas import tpu as pltpu

F32 = jnp.float32
BF16 = jnp.bfloat16

D_MODEL = 1024
N_BRANCH = 3
CONV_WIDTH = 3
HEADS = 8
MLSTM_DK = 64
MLSTM_DV = 128
CHUNK = 128
GATE_CAP = 15.0
MLA_NOPE = 128
MLA_ROPE = 64
MLA_V = 128
Q_LORA = 256
KV_LORA = 128
ROPE_BASE = 10000.0
D_FF = 4 * D_MODEL
NORM_EPS = 1e-6
QK_PAD = 256

CONV_COLS = 3 * D_MODEL
PROJ_COLS = 9 * D_MODEL
MAIN_COLS = PROJ_COLS - CONV_COLS
TAIL_COLS = 512
COL_MQ = 0
COL_MK = COL_MQ + HEADS * MLSTM_DK
COL_MV = COL_MK + HEADS * MLSTM_DK
COL_MO = COL_MV + HEADS * MLSTM_DV
GATE_COL0 = COL_MO + HEADS * MLSTM_DV
LANES = 128
SUBLANES = 8
TAIL_KROPE = Q_LORA + KV_LORA
IGATE_LANE = MLA_ROPE
FGATE_LANE = MLA_ROPE + HEADS

IN_PROJ_TN = 3 * D_MODEL
IN_PROJ_CHUNK = 1536
CONV_CHUNK = 256
MLSTM_GROUP = 8
ATT_T = 512
LOG2E = 1.4426950408889634
Q_SCALE = (MLA_NOPE + MLA_ROPE) ** -0.5 * LOG2E
MASK_VALUE = -1e30
SCORE_BOUND = 70.0
SCORE_SHIFT = 48.0
BOUND_SLACK = 1.02
ATT_UNROLL = 8
ATT_UNROLL_BOUNDED = 24

VMEM_LIMIT = 56 * 1024 * 1024


def _rms(x, g):
    return x * lax.rsqrt(jnp.mean(x * x, axis=-1, keepdims=True) + NORM_EPS) * g


def _sigmoid(x):
    return 0.5 * jnp.tanh(0.5 * x) + 0.5


def _dot(a, b):
    return jnp.dot(a, b, preferred_element_type=F32)


def _dot_nt(a, b):
    return lax.dot_general(a, b, (((1,), (1,)), ((), ())), preferred_element_type=F32)


def _rope_table_kernel(pos_ref, freq_ref, cs_ref, cst_ref):
    ang = freq_ref[...] * pos_ref[...].astype(F32)
    c, s = jnp.cos(ang), jnp.sin(ang)
    cst_ref[...] = jnp.concatenate([c, s], axis=0)
    cs_ref[...] = jnp.concatenate([c, c, -s, s], axis=0).T


def _rope_table(positions, tm):
    t = positions.size
    half = MLA_ROPE // 2
    inv_freq = (np.float32(ROPE_BASE) ** (-np.arange(0, MLA_ROPE, 2, dtype=np.float32) / np.float32(MLA_ROPE)))
    inv_freq = jnp.asarray(inv_freq.astype(np.float32))
    return pl.pallas_call(
        _rope_table_kernel,
        grid=(t // tm,),
        in_specs=[pl.BlockSpec((1, tm), lambda i: (0, i)),
                  pl.BlockSpec((half, 1), lambda i: (0, 0))],
        out_specs=[pl.BlockSpec((tm, 2 * MLA_ROPE), lambda i: (i, 0)),
                   pl.BlockSpec((MLA_ROPE, tm), lambda i: (0, i))],
        out_shape=[jax.ShapeDtypeStruct((t, 2 * MLA_ROPE), F32),
                   jax.ShapeDtypeStruct((MLA_ROPE, t), F32)],
        name="rope_table",
    )(positions.reshape(1, t), inv_freq.reshape(half, 1))


def _in_proj_kernel(x_ref, g_ref, w_ref, wt_ref, cw_ref, ya_ref, main_ref, tail_ref, xn_ref, zprev_ref, *,
                    tiles_per_seq):
    i, j = pl.program_id(0), pl.program_id(1)
    tm = x_ref.shape[0]

    @pl.when(jnp.logical_and(j == 0, i % tiles_per_seq == 0))
    def _():
        zprev_ref[...] = jnp.zeros_like(zprev_ref)

    @pl.when(j == 0)
    def _():
        xn = _rms(x_ref[...], g_ref[...]).astype(BF16)
        xn_ref[...] = xn
        tail_ref[...] = _dot(xn, wt_ref[...])
        row = lax.broadcasted_iota(jnp.int32, (tm, CONV_CHUNK), 0)
        for c0 in range(0, D_MODEL, CONV_CHUNK):
            c = slice(c0, c0 + CONV_CHUNK)
            gate_b, gate_c, u = (_dot(xn, w_ref[:, n * D_MODEL + c0:n * D_MODEL + c0 + CONV_CHUNK]) for n in range(3))
            z = gate_c * u
            p1, p2 = zprev_ref[SUBLANES - 1:SUBLANES, c], zprev_ref[SUBLANES - 2:SUBLANES - 1, c]
            z1 = jnp.where(row == 0, p1, pltpu.roll(z, 1, 0))
            z2 = jnp.where(row == 0, p2, jnp.where(row == 1, p1, pltpu.roll(z, 2, 0)))
            zprev_ref[:, c] = z[tm - SUBLANES:, :]
            ya_ref[:, c] = (gate_b * (cw_ref[0:1, c] * z2 + cw_ref[1:2, c] * z1 + cw_ref[2:3, c] * z)).astype(BF16)

    cols = [slice(c, c + IN_PROJ_CHUNK) for c in range(0, w_ref.shape[1], IN_PROJ_CHUNK)]

    @pl.when(j == 1)
    def _():
        for c in cols:
            main_ref[:, c] = _dot(xn_ref[...], w_ref[:, c]).astype(BF16)

    @pl.when(j == 2)
    def _():
        for c in cols:
            main_ref[:, c] = _sigmoid(_dot(xn_ref[...], w_ref[:, c])).astype(BF16)


def _in_proj(x2, g, w_main, w_tail, conv_w, layer, seq, tm):
    t = x2.shape[0]
    tn = IN_PROJ_TN
    assert CONV_COLS == tn and GATE_COL0 == tn and MAIN_COLS == 2 * tn
    kern = functools.partial(_in_proj_kernel, tiles_per_seq=seq // tm)
    return pl.pallas_call(
        kern,
        grid=(t // tm, PROJ_COLS // tn),
        in_specs=[pl.BlockSpec((tm, D_MODEL), lambda i, j: (i, 0)),
                  pl.BlockSpec((1, D_MODEL), lambda i, j: (0, 0)),
                  pl.BlockSpec((None, D_MODEL, tn), lambda i, j: (layer, 0, j)),
                  pl.BlockSpec((None, D_MODEL, TAIL_COLS), lambda i, j: (layer, 0, 0)),
                  pl.BlockSpec((CONV_WIDTH, D_MODEL), lambda i, j: (0, 0))],
        out_specs=[pl.BlockSpec((tm, D_MODEL), lambda i, j: (i, 0)),
                   pl.BlockSpec((tm, tn), lambda i, j: (i, jnp.maximum(j - 1, 0))),
                   pl.BlockSpec((tm, TAIL_COLS), lambda i, j: (i, 0))],
        out_shape=[jax.ShapeDtypeStruct((t, D_MODEL), BF16),
                   jax.ShapeDtypeStruct((t, MAIN_COLS), BF16),
                   jax.ShapeDtypeStruct((t, TAIL_COLS), F32)],
        scratch_shapes=[pltpu.VMEM((tm, D_MODEL), BF16), pltpu.VMEM((SUBLANES, D_MODEL), F32)],
        compiler_params=pltpu.CompilerParams(
            dimension_semantics=("arbitrary", "arbitrary"), vmem_limit_bytes=VMEM_LIMIT),
        name="in_proj",
    )(x2, g, w_main, w_tail, conv_w)


def _time_scan(x, op, row):
    s = 1
    while s < CHUNK:
        x = jnp.where(row >= s, op(x, pltpu.roll(x, s, 0)), x)
        s *= 2
    return x


def _mlstm_chunk(rows, q_ref, k_ref, v_ref, gate_ref, bias_ref, num_ref, dst_ref, c_state, m_prev):
    L = CHUNK
    row = lax.broadcasted_iota(jnp.int32, (L, LANES), 0)
    lane = lax.broadcasted_iota(jnp.int32, (L, LANES), 1)
    gate_lanes = jnp.logical_and(lane >= IGATE_LANE, lane < FGATE_LANE + HEADS)
    tri = (lax.broadcasted_iota(jnp.int32, (L, L), 0) >= lax.broadcasted_iota(jnp.int32, (L, L), 1))
    lane_v = lax.broadcasted_iota(jnp.int32, (L, MLSTM_DV), 1)
    c_state = list(c_state)
    pre = jnp.where(gate_lanes, gate_ref[rows, :] + bias_ref[...], 0.0)
    capped = GATE_CAP * jnp.tanh(pre * (1.0 / GATE_CAP))
    log_i = capped * LOG2E
    log_f = (jnp.minimum(capped, 0.0) - jnp.log1p(jnp.exp(-jnp.abs(capped)))) * LOG2E
    log_f = pltpu.roll(log_f, LANES - HEADS, 1)
    b = _time_scan(log_f, jnp.add, row)
    b_last = b[L - 1:L, :]
    rowd = log_i - b
    big_m = jnp.maximum(_time_scan(rowd, jnp.maximum, row), m_prev)
    e_neg_m = jnp.exp2(-(b + big_m))
    a = b_last + rowd
    m_loc = jnp.max(a, axis=0, keepdims=True)
    m_new = jnp.maximum(b_last + m_prev, m_loc)
    s_old = jnp.exp2(b_last + m_prev - m_new)
    w = jnp.exp2(a - m_new)
    rowd_t = rowd.T
    w_t = w.T
    q_all = q_ref[rows, :] * (MLSTM_DK ** -0.5)
    k_t = k_ref[rows, :].astype(F32).T

    hk = [slice(h * MLSTM_DK, (h + 1) * MLSTM_DK) for h in range(HEADS)]
    hv = [slice(h * MLSTM_DV, (h + 1) * MLSTM_DV) for h in range(HEADS)]
    gl = [slice(IGATE_LANE + h, IGATE_LANE + h + 1) for h in range(HEADS)]
    v_aug = [jnp.concatenate([v_ref[rows, hv[h]], (lane_v == IGATE_LANE + h).astype(BF16)], axis=1)
             for h in range(HEADS)]
    scores = [_dot(q_all[:, hk[h]], k_t[hk[h], :].astype(BF16)) for h in range(HEADS)]
    c_loc = [_dot((k_t[hk[h], :] * w_t[gl[h], :]).astype(BF16), v_aug[h]) for h in range(HEADS)]
    big_m_b = [jnp.broadcast_to(big_m[:, gl[h]], (L, L)) for h in range(HEADS)]
    dmat = [jnp.where(tri, jnp.exp2(rowd_t[gl[h], :] - big_m_b[h]), 0.0) for h in range(HEADS)]
    q_state = [(q_all[:, hk[h]].astype(F32)
                * jnp.exp2(m_prev[:, gl[h]] - big_m_b[h][:, :MLSTM_DK])).astype(BF16) for h in range(HEADS)]
    num_aug = []
    for h in range(HEADS):
        lhs = jnp.concatenate([(scores[h] * dmat[h]).astype(BF16), q_state[h],
                               jnp.zeros((L, L - MLSTM_DK), BF16)], axis=1)
        rhs = jnp.concatenate([v_aug[h], c_state[h].astype(BF16),
                               jnp.zeros((L - MLSTM_DK, 2 * MLSTM_DV), BF16)], axis=0)
        num_aug.append(_dot(lhs, rhs))
    for h in range(HEADS):
        num_ref[rows, hv[h]] = num_aug[h][:, :MLSTM_DV].astype(num_ref.dtype)
        c_state[h] = s_old[:, gl[h]] * c_state[h] + c_loc[h]
    den = functools.reduce(jnp.add, [n[:, MLSTM_DV:] for n in num_aug])
    dst_ref[rows, :] = jnp.maximum(jnp.abs(den), e_neg_m)
    return c_state, m_new


def _mlstm_kernel(q_ref, k_ref, v_ref, gate_ref, bias_ref, num_ref, dst_ref, c_ref, m_ref, *, n_chunk):
    @pl.when(pl.program_id(1) == 0)
    def _():
        c_ref[...] = jnp.zeros_like(c_ref)
        m_ref[...] = jnp.zeros_like(m_ref)

    m_prev = m_ref[...]
    c_state = [c_ref[h] for h in range(HEADS)]
    for g in range(n_chunk):
        c_state, m_prev = _mlstm_chunk(slice(g * CHUNK, (g + 1) * CHUNK), q_ref, k_ref, v_ref, gate_ref, bias_ref,
                                       num_ref, dst_ref, c_state, m_prev)
    m_ref[...] = m_prev
    for h in range(HEADS):
        c_ref[h] = c_state[h]


def _mlstm(main, tail, gate_bias, bsz, seq, n_chunk):
    t = main.shape[0]
    rows = n_chunk * CHUNK
    nc = seq // rows
    row = lambda b, c: b * nc + c
    col_block = lambda col0, width: pl.BlockSpec((rows, width), lambda b, c: (row(b, c), col0 // width))
    return pl.pallas_call(
        functools.partial(_mlstm_kernel, n_chunk=n_chunk),
        grid=(bsz, nc),
        in_specs=[col_block(COL_MQ, HEADS * MLSTM_DK), col_block(COL_MK, HEADS * MLSTM_DK),
                  col_block(COL_MV, HEADS * MLSTM_DV),
                  col_block(TAIL_KROPE, LANES),
                  pl.BlockSpec((1, LANES), lambda b, c: (0, 0))],
        out_specs=[pl.BlockSpec((rows, D_MODEL), lambda b, c: (row(b, c), 0)),
                   pl.BlockSpec((rows, LANES), lambda b, c: (row(b, c), 0))],
        out_shape=[jax.ShapeDtypeStruct((t, D_MODEL), BF16), jax.ShapeDtypeStruct((t, LANES), F32)],
        scratch_shapes=[pltpu.VMEM((HEADS, MLSTM_DK, 2 * MLSTM_DV), F32),
                        pltpu.VMEM((1, LANES), F32)],
        compiler_params=pltpu.CompilerParams(dimension_semantics=("parallel", "arbitrary")),
        name="mlstm",
    )(main, main, main, tail, gate_bias)


def _rms0(x, g_col):
    return x * lax.rsqrt(jnp.mean(x * x, axis=0, keepdims=True) + NORM_EPS) * g_col


def _rope(x, cs):
    half = MLA_ROPE // 2
    rot = jnp.concatenate([x[:, half:], x[:, :half]], axis=-1)
    return x * cs[:, :MLA_ROPE] + rot * cs[:, MLA_ROPE:]


def _mla_prep_kernel(tail_ref, cs_ref, cst_ref, qa_ref, wqt_ref, kva_ref, wk_ref, wvt_ref, qn_ref, kn_ref,
                     qt_out, k_out, vt_out):
    tm = ATT_T
    half = MLA_ROPE // 2
    qn_g, qp_g = qn_ref[:MLA_NOPE, :], qn_ref[MLA_NOPE:, :]
    kn_g, kp_g = kn_ref[:, :MLA_NOPE], kn_ref[:, MLA_NOPE:]
    pe0 = HEADS * MLA_NOPE
    pad = QK_PAD - MLA_NOPE - MLA_ROPE
    q_pad = jnp.where(lax.broadcasted_iota(jnp.int32, (pad, tm), 0) == 0, -SCORE_SHIFT, 0.0).astype(BF16)
    k_pad = (lax.broadcasted_iota(jnp.int32, (tm, pad), 1) == 0).astype(BF16)
    for s in range(tail_ref.shape[0] // tm):
        rows = slice(s * tm, (s + 1) * tm)
        cq = _rms(tail_ref[rows, :Q_LORA], qa_ref[...]).astype(BF16)
        ckv = _rms(tail_ref[rows, Q_LORA:Q_LORA + KV_LORA], kva_ref[...]).astype(BF16)
        qft = _dot_nt(wqt_ref[...], cq)
        kf = _dot(ckv, wk_ref[...])
        vft = _dot_nt(wvt_ref[...], ckv)
        cos_t, sin_t = cst_ref[:half, rows], cst_ref[half:, rows]
        k_pe = _rope(_rms(tail_ref[rows, TAIL_KROPE:TAIL_KROPE + MLA_ROPE], kp_g), cs_ref[rows, :]).astype(BF16)
        for h in range(HEADS):
            q_nope = _rms0(qft[h * MLA_NOPE:(h + 1) * MLA_NOPE], qn_g)
            q_pe = _rms0(qft[pe0 + h * MLA_ROPE:pe0 + (h + 1) * MLA_ROPE], qp_g)
            x1, x2 = q_pe[:half], q_pe[half:]
            q_rot = jnp.concatenate([x1 * cos_t - x2 * sin_t, x1 * sin_t + x2 * cos_t], axis=0)
            qt_out[0, h, s] = jnp.concatenate([q_nope.astype(BF16), q_rot.astype(BF16), q_pad], axis=0)
            k_nope = _rms(kf[:, h * MLA_NOPE:(h + 1) * MLA_NOPE], kn_g)
            k_out[0, h, rows, :] = jnp.concatenate([k_nope.astype(BF16), k_pe, k_pad], axis=-1)
            vt_out[0, h, s] = vft[h * MLA_V:(h + 1) * MLA_V].astype(BF16)


def _mla_prep(tail, cs, cst, qa, wqt, kva, wk, wvt, qn_col, kn, bsz, seq):
    tiles = 2 if (seq // ATT_T) % 2 == 0 else 1
    tm = tiles * ATT_T
    nt = seq // tm
    full = lambda shape: pl.BlockSpec(shape, lambda i: (0,) * len(shape))
    blocked = lambda rows: pl.BlockSpec((1, HEADS, tiles, rows, ATT_T), lambda i: (i // nt, 0, i % nt, 0, 0))
    return pl.pallas_call(
        _mla_prep_kernel,
        grid=(bsz * nt,),
        in_specs=[pl.BlockSpec((tm, TAIL_COLS), lambda i: (i, 0)),
                  pl.BlockSpec((tm, 2 * MLA_ROPE), lambda i: (i, 0)),
                  pl.BlockSpec((MLA_ROPE, tm), lambda i: (0, i)),
                  full((1, Q_LORA)), full(wqt.shape), full((1, KV_LORA)), full(wk.shape), full(wvt.shape),
                  full((MLA_NOPE + MLA_ROPE, 1)), full((1, MLA_NOPE + MLA_ROPE))],
        out_specs=[blocked(QK_PAD),
                   pl.BlockSpec((1, HEADS, tm, QK_PAD), lambda i: (i // nt, 0, i % nt, 0)),
                   blocked(MLA_V)],
        out_shape=[jax.ShapeDtypeStruct((bsz, HEADS, seq // ATT_T, QK_PAD, ATT_T), BF16),
                   jax.ShapeDtypeStruct((bsz, HEADS, seq, QK_PAD), BF16),
                   jax.ShapeDtypeStruct((bsz, HEADS, seq // ATT_T, MLA_V, ATT_T), BF16)],
        compiler_params=pltpu.CompilerParams(dimension_semantics=("parallel",),
                                             vmem_limit_bytes=VMEM_LIMIT),
        name="mla_prep",
    )(tail, cs, cst, qa, wqt, kva, wk, wvt, qn_col, kn)


def _attn_kernel(qt_ref, k_ref, vt_ref, o_ref, s_scr, p_scr, l_all, acc_all, *online_scr, nq, online):
    t = ATT_T
    unroll = ATT_UNROLL if online else ATT_UNROLL_BOUNDED
    a_scr, m_all = online_scr if online else (None, None)

    def qk(qi, j):
        k0 = pl.multiple_of(j * t, t)
        return _dot(k_ref[0, 0, pl.ds(k0, t), :], qt_ref[0, 0, qi])

    def accumulate(tile, alpha, p):
        q, j = tile
        pv = _dot(vt_ref[0, 0, j], p)
        acc_all[q] = alpha * acc_all[q] + pv if online else acc_all[q] + pv

    def stage(prev, cur, nxt, s_cur, p_prev, alpha_prev, masked):
        accumulate(prev, alpha_prev, p_prev)
        qi = cur[0]
        if masked:
            keep = lax.broadcasted_iota(jnp.int32, (t, t), 0) <= lax.broadcasted_iota(jnp.int32, (t, t), 1)
        if online:
            if masked:
                s_cur = jnp.where(keep, s_cur, MASK_VALUE)
            m_old = m_all[qi]
            m_new = jnp.maximum(m_old, jnp.max(s_cur, axis=0, keepdims=True))
            m_all[qi] = m_new
            alpha = jnp.exp2(m_old - m_new)
            p = jnp.exp2((s_cur - m_new).astype(BF16))
            l_all[qi] = alpha * l_all[qi] + jnp.sum(p.astype(F32), axis=0, keepdims=True)
        else:
            alpha = None
            p = jnp.exp2(s_cur)
            if masked:
                p = jnp.where(keep, p, 0.0)
            l_all[qi] += jnp.sum(p, axis=0, keepdims=True)
            p = p.astype(BF16)
        s_next = qk(jnp.minimum(nxt[0], nq - 1), jnp.minimum(nxt[1], nq - 1))
        return s_next, p, alpha

    def advance_lower(qi, j):
        wrap = j + 1 == qi
        return jnp.where(wrap, qi + 1, qi), jnp.where(wrap, 0, j + 1)

    def advance_diag(qi, j):
        return qi + 1, j + 1

    def run_phase(n_steps, first, advance, masked):
        if n_steps == 0:
            return
        s_scr[...] = qk(*first)
        p_scr[...] = jnp.zeros_like(p_scr)
        if online:
            a_scr[...] = jnp.ones_like(a_scr)

        def steps(count, prev, cur):
            s_cur, p_prev = s_scr[...], p_scr[...]
            alpha_prev = a_scr[...] if online else None
            for _ in range(count):
                nxt = advance(*cur)
                s_cur, p_prev, alpha_prev = stage(prev, cur, nxt, s_cur, p_prev, alpha_prev, masked)
                prev, cur = cur, nxt
            s_scr[...] = s_cur
            p_scr[...] = p_prev
            if online:
                a_scr[...] = alpha_prev
            return prev, cur

        first = (jnp.int32(first[0]), jnp.int32(first[1]))
        prev, cur = lax.fori_loop(0, n_steps // unroll, lambda _, c: steps(unroll, *c), (first, first))
        if n_steps % unroll:
            prev, cur = steps(n_steps % unroll, prev, cur)
        accumulate(prev, a_scr[...] if online else None, p_scr[...])

    if online:
        m_all[...] = jnp.full_like(m_all, MASK_VALUE)
    l_all[...] = jnp.zeros_like(l_all)
    acc_all[...] = jnp.zeros_like(acc_all)
    run_phase(nq * (nq - 1) // 2, (1, 0), advance_lower, False)
    run_phase(nq, (0, 0), advance_diag, True)
    for qi in range(nq):
        o_ref[0, qi * t:(qi + 1) * t, :] = (acc_all[qi] / l_all[qi]).T.astype(o_ref.dtype)


def _attention(qt, k, vt, *, online):
    bsz, _, seq, _ = k.shape
    nq = seq // ATT_T
    online_scr = [pltpu.VMEM((1, ATT_T), F32), pltpu.VMEM((nq, 1, ATT_T), F32)] if online else []
    return pl.pallas_call(
        functools.partial(_attn_kernel, nq=nq, online=online),
        grid=(bsz, HEADS),
        in_specs=[pl.BlockSpec((1, 1, nq, QK_PAD, ATT_T), lambda b, h: (b, h, 0, 0, 0)),
                  pl.BlockSpec((1, 1, seq, QK_PAD), lambda b, h: (b, h, 0, 0)),
                  pl.BlockSpec((1, 1, nq, MLA_V, ATT_T), lambda b, h: (b, h, 0, 0, 0))],
        out_specs=pl.BlockSpec((1, seq, MLA_V), lambda b, h: (b, 0, h)),
        out_shape=jax.ShapeDtypeStruct((bsz, seq, HEADS * MLA_V), BF16),
        scratch_shapes=[pltpu.VMEM((ATT_T, ATT_T), F32), pltpu.VMEM((ATT_T, ATT_T), BF16),
                        pltpu.VMEM((nq, 1, ATT_T), F32), pltpu.VMEM((nq, MLA_V, ATT_T), F32)] + online_scr,
        compiler_params=pltpu.CompilerParams(dimension_semantics=("parallel", "parallel"),
                                             vmem_limit_bytes=VMEM_LIMIT),
        name="attn_online" if online else "attn_bounded",
    )(qt, k, vt)


def _mlstm_head_out(num_ref, dst_ref, o_ref, hn_ref):
    out_gain = _sigmoid(o_ref[...].astype(F32)) * hn_ref[...]
    heads = []
    for h in range(HEADS):
        hv = slice(h * MLSTM_DV, (h + 1) * MLSTM_DV)
        num = num_ref[:, hv].astype(F32)
        d = dst_ref[:, IGATE_LANE + h:IGATE_LANE + h + 1]
        scale = lax.rsqrt(jnp.mean(num * num, axis=-1, keepdims=True) + NORM_EPS * d * d)
        heads.append((num * scale * out_gain[:, hv]).astype(BF16))
    return jnp.concatenate(heads, axis=1)


def _merge_kernel(ya_ref, gate_ref, num_ref, dst_ref, o_ref, hn_ref, yc_ref, x_ref, wb_ref, wo_ref, out_ref):
    g = gate_ref[...]
    merged = g[:, :D_MODEL].astype(F32) * _dot(ya_ref[...], wb_ref[0])
    y_b = _mlstm_head_out(num_ref, dst_ref, o_ref, hn_ref)
    merged += g[:, D_MODEL:2 * D_MODEL].astype(F32) * _dot(y_b, wb_ref[1])
    merged += g[:, 2 * D_MODEL:].astype(F32) * _dot(yc_ref[...], wb_ref[2])
    out_ref[...] = x_ref[...] + _dot(merged.astype(BF16), wo_ref[...])


def _merge(y_a, main, num, dstat, head_norm, y_c, x2, w_branch, w_out, layer, tm):
    t = x2.shape[0]
    col = lambda c: pl.BlockSpec((tm, D_MODEL), lambda i: (i, c))
    return pl.pallas_call(
        _merge_kernel,
        grid=(t // tm,),
        in_specs=[col(0),
                  pl.BlockSpec((tm, N_BRANCH * D_MODEL), lambda i: (i, GATE_COL0 // (N_BRANCH * D_MODEL))),
                  col(0), pl.BlockSpec((tm, LANES), lambda i: (i, 0)), col(COL_MO // D_MODEL),
                  pl.BlockSpec((1, D_MODEL), lambda i: (0, 0)),
                  col(0), col(0),
                  pl.BlockSpec((None, N_BRANCH, D_MODEL, D_MODEL), lambda i: (layer, 0, 0, 0),
                               pipeline_mode=pl.Buffered(1)),
                  pl.BlockSpec((None, D_MODEL, D_MODEL), lambda i: (layer, 0, 0), pipeline_mode=pl.Buffered(1))],
        out_specs=col(0),
        out_shape=jax.ShapeDtypeStruct((t, D_MODEL), F32),
        compiler_params=pltpu.CompilerParams(dimension_semantics=("parallel",),
                                             vmem_limit_bytes=VMEM_LIMIT),
        name="merge",
    )(y_a, main, num, dstat, main, head_norm, y_c, x2, w_branch, w_out)


def _mlp_kernel(x_ref, g_ref, wu_ref, wd_ref, out_ref, *, n_chunk):
    x = x_ref[...]
    h = _rms(x, g_ref[...]).astype(BF16)
    ck = D_FF // n_chunk
    acc = x
    for c in range(n_chunk):
        u = jnp.maximum(_dot(h, wu_ref[:, c * ck:(c + 1) * ck]), 0.0)
        acc = acc + _dot((u * u).astype(BF16), wd_ref[c * ck:(c + 1) * ck, :])
    out_ref[...] = acc


def _mlp(x2, g, w_up, w_down, layer, tm):
    t = x2.shape[0]
    kern = functools.partial(_mlp_kernel, n_chunk=4)
    return pl.pallas_call(
        kern,
        grid=(t // tm,),
        in_specs=[pl.BlockSpec((tm, D_MODEL), lambda i: (i, 0)),
                  pl.BlockSpec((1, D_MODEL), lambda i: (0, 0)),
                  pl.BlockSpec((None, D_MODEL, D_FF), lambda i: (layer, 0, 0), pipeline_mode=pl.Buffered(1)),
                  pl.BlockSpec((None, D_FF, D_MODEL), lambda i: (layer, 0, 0), pipeline_mode=pl.Buffered(1))],
        out_specs=pl.BlockSpec((tm, D_MODEL), lambda i: (i, 0)),
        out_shape=jax.ShapeDtypeStruct((t, D_MODEL), F32),
        compiler_params=pltpu.CompilerParams(dimension_semantics=("parallel",),
                                             vmem_limit_bytes=VMEM_LIMIT),
        name="mlp",
    )(x2, g, w_up, w_down)


def _regroup_w_in(w_in):
    i0 = CONV_COLS + GATE_COL0
    cq0 = i0 + 2 * HEADS
    g0 = cq0 + Q_LORA + KV_LORA + MLA_ROPE
    pad = jnp.zeros(w_in.shape[:2] + (TAIL_COLS - (Q_LORA + KV_LORA + MLA_ROPE + 2 * HEADS),), w_in.dtype)
    w_main = jnp.concatenate([w_in[..., :i0], w_in[..., g0:]], axis=-1).astype(BF16)
    w_tail = jnp.concatenate([w_in[..., cq0:g0], w_in[..., i0:cq0], pad], axis=-1).astype(BF16)
    return w_main, w_tail


def _heads_last_split(w, first, second):
    k = w.shape[0]
    w3 = w.reshape(k, HEADS, first + second)
    return jnp.concatenate([w3[:, :, :first].reshape(k, HEADS * first),
                            w3[:, :, first:].reshape(k, HEADS * second)], axis=1).astype(BF16)


def _score_bound(q_gain, k_gain):
    def sq_norm(g):
        g = g.astype(F32)
        return MLA_NOPE * jnp.max(g[:MLA_NOPE] ** 2) + MLA_ROPE * jnp.max(g[MLA_NOPE:] ** 2)
    return jnp.sqrt(sq_norm(q_gain) * sq_norm(k_gain)) * (Q_SCALE * BOUND_SLACK)


def _pick(n, candidates):
    for c in candidates:
        if n % c == 0:
            return c
    raise ValueError(f"no tile size for {n}")


def kernel(x, positions, mix_norm, w_in, conv_w, mlstm_igate_bias, mlstm_fgate_bias, mlstm_head_norm,
           mla_q_a_norm, mla_w_uq, mla_kv_a_norm, mla_w_ukv, mla_q_norm, mla_k_norm, w_branch, w_out,
           mlp_norm, w_up, w_down):
    bsz, seq, _ = x.shape
    depth = w_in.shape[0]
    t = bsz * seq
    assert seq % ATT_T == 0
    n_chunk = _pick(seq // CHUNK, tuple(g for g in (8, 4, 2, 1) if g <= MLSTM_GROUP))
    tm_proj = _pick(seq, (1024, 512, 256, 128))
    tm = _pick(seq, (512, 256, 128))
    tm_mlp = _pick(t, (1024, 512, 256, 128))

    x2 = x.reshape(t, D_MODEL)
    cs, cst = _rope_table(positions, ATT_T)
    row = lambda a: a.reshape(1, -1).astype(F32)
    w_main, w_tail = _regroup_w_in(w_in)
    w_branch_b, w_out_b, w_up_b, w_down_b = (w.astype(BF16) for w in (w_branch, w_out, w_up, w_down))

    for l in range(depth):
        gate_bias = jnp.zeros((1, LANES), F32)
        gate_bias = gate_bias.at[0, IGATE_LANE:IGATE_LANE + HEADS].set(mlstm_igate_bias[l])
        gate_bias = gate_bias.at[0, FGATE_LANE:FGATE_LANE + HEADS].set(mlstm_fgate_bias[l])
        w_q = _heads_last_split(mla_w_uq[l], MLA_NOPE, MLA_ROPE)
        w_kv = _heads_last_split(mla_w_ukv[l], MLA_NOPE, MLA_V)

        y_a, main, tail = _in_proj(x2, row(mix_norm[l]), w_main, w_tail, conv_w[l].astype(F32), l, seq, tm_proj)
        qt, k, vt = _mla_prep(tail, cs, cst, row(mla_q_a_norm[l]), w_q.T, row(mla_kv_a_norm[l]),
                              w_kv[:, :HEADS * MLA_NOPE], w_kv[:, HEADS * MLA_NOPE:].T,
                              mla_q_norm[l].reshape(-1, 1).astype(F32) * Q_SCALE, row(mla_k_norm[l]), bsz, seq)
        num, dstat = _mlstm(main, tail, gate_bias, bsz, seq, n_chunk)
        y_c = lax.cond(_score_bound(mla_q_norm[l], mla_k_norm[l]) <= SCORE_BOUND,
                       functools.partial(_attention, online=False), functools.partial(_attention, online=True),
                       qt, k, vt).reshape(t, D_MODEL)
        x2 = _merge(y_a, main, num, dstat, row(mlstm_head_norm[l]), y_c, x2, w_branch_b, w_out_b, l, tm)
        x2 = _mlp(x2, row(mlp_norm[l]), w_up_b, w_down_b, l, tm_mlp)
    return x2.reshape(bsz, seq, D_MODEL)
```

```python
import functools

import jax
import jax.numpy as jnp
import numpy as np
from jax import lax
from jax.experimental import pallas as pl
from jax.experimental.pallas import tpu as pltpu

F32 = jnp.float32
BF16 = jnp.bfloat16

D_MODEL = 1024
N_BRANCH = 3
CONV_WIDTH = 3
HEADS = 8
MLSTM_DK = 64
MLSTM_DV = 128
CHUNK = 128
GATE_CAP = 15.0
MLA_NOPE = 128
MLA_ROPE = 64
MLA_V = 128
Q_LORA = 256
KV_LORA = 128
ROPE_BASE = 10000.0
D_FF = 4 * D_MODEL
NORM_EPS = 1e-6
QK_PAD = 256

CONV_COLS = 3 * D_MODEL
PROJ_COLS = 9 * D_MODEL
MAIN_COLS = PROJ_COLS - CONV_COLS
TAIL_COLS = 512
COL_MQ = 0
COL_MK = COL_MQ + HEADS * MLSTM_DK
COL_MV = COL_MK + HEADS * MLSTM_DK
COL_MO = COL_MV + HEADS * MLSTM_DV
GATE_COL0 = COL_MO + HEADS * MLSTM_DV
LANES = 128
SUBLANES = 8
TAIL_KROPE = Q_LORA + KV_LORA
IGATE_LANE = MLA_ROPE
FGATE_LANE = MLA_ROPE + HEADS

IN_PROJ_TN = 3 * D_MODEL
IN_PROJ_CHUNK = 1536
CONV_CHUNK = 256
MLSTM_GROUP = 8
ATT_T = 512
LOG2E = 1.4426950408889634
Q_SCALE = (MLA_NOPE + MLA_ROPE) ** -0.5 * LOG2E
MASK_VALUE = -1e30
SCORE_BOUND = 70.0
SCORE_SHIFT = 48.0
BOUND_SLACK = 1.02
ATT_UNROLL = 8
ATT_UNROLL_BOUNDED = 40

VMEM_LIMIT = 56 * 1024 * 1024


def _rms(x, g):
    return x * lax.rsqrt(jnp.mean(x * x, axis=-1, keepdims=True) + NORM_EPS) * g


def _sigmoid(x):
    return 0.5 * jnp.tanh(0.5 * x) + 0.5


def _dot(a, b):
    return jnp.dot(a, b, preferred_element_type=F32)


def _dot_nt(a, b):
    return lax.dot_general(a, b, (((1,), (1,)), ((), ())), preferred_element_type=F32)


def _rope_table_kernel(pos_ref, freq_ref, cs_ref, cst_ref):
    ang = freq_ref[...] * pos_ref[...].astype(F32)
    c, s = jnp.cos(ang), jnp.sin(ang)
    cst_ref[...] = jnp.concatenate([c, s], axis=0)
    cs_ref[...] = jnp.concatenate([c, c, -s, s], axis=0).T


def _rope_table(positions, tm):
    t = positions.size
    half = MLA_ROPE // 2
    inv_freq = (np.float32(ROPE_BASE) ** (-np.arange(0, MLA_ROPE, 2, dtype=np.float32) / np.float32(MLA_ROPE)))
    inv_freq = jnp.asarray(inv_freq.astype(np.float32))
    return pl.pallas_call(
        _rope_table_kernel,
        grid=(t // tm,),
        in_specs=[pl.BlockSpec((1, tm), lambda i: (0, i)),
                  pl.BlockSpec((half, 1), lambda i: (0, 0))],
        out_specs=[pl.BlockSpec((tm, 2 * MLA_ROPE), lambda i: (i, 0)),
                   pl.BlockSpec((MLA_ROPE, tm), lambda i: (0, i))],
        out_shape=[jax.ShapeDtypeStruct((t, 2 * MLA_ROPE), F32),
                   jax.ShapeDtypeStruct((MLA_ROPE, t), F32)],
        name="rope_table",
    )(positions.reshape(1, t), inv_freq.reshape(half, 1))


def _in_proj_kernel(x_ref, g_ref, w_ref, wt_ref, cw_ref, ya_ref, main_ref, tail_ref, xn_ref, zprev_ref, *,
                    tiles_per_seq):
    i, j = pl.program_id(0), pl.program_id(1)
    tm = x_ref.shape[0]

    @pl.when(jnp.logical_and(j == 0, i % tiles_per_seq == 0))
    def _():
        zprev_ref[...] = jnp.zeros_like(zprev_ref)

    @pl.when(j == 0)
    def _():
        xn = _rms(x_ref[...], g_ref[...]).astype(BF16)
        xn_ref[...] = xn
        tail_ref[...] = _dot(xn, wt_ref[...])
        row = lax.broadcasted_iota(jnp.int32, (tm, CONV_CHUNK), 0)
        for c0 in range(0, D_MODEL, CONV_CHUNK):
            c = slice(c0, c0 + CONV_CHUNK)
            gate_b, gate_c, u = (_dot(xn, w_ref[:, n * D_MODEL + c0:n * D_MODEL + c0 + CONV_CHUNK]) for n in range(3))
            z = gate_c * u
            p1, p2 = zprev_ref[SUBLANES - 1:SUBLANES, c], zprev_ref[SUBLANES - 2:SUBLANES - 1, c]
            z1 = jnp.where(row == 0, p1, pltpu.roll(z, 1, 0))
            z2 = jnp.where(row == 0, p2, jnp.where(row == 1, p1, pltpu.roll(z, 2, 0)))
            zprev_ref[:, c] = z[tm - SUBLANES:, :]
            ya_ref[:, c] = (gate_b * (cw_ref[0:1, c] * z2 + cw_ref[1:2, c] * z1 + cw_ref[2:3, c] * z)).astype(BF16)

    cols = [slice(c, c + IN_PROJ_CHUNK) for c in range(0, w_ref.shape[1], IN_PROJ_CHUNK)]

    @pl.when(j == 1)
    def _():
        for c in cols:
            main_ref[:, c] = _dot(xn_ref[...], w_ref[:, c]).astype(BF16)

    @pl.when(j == 2)
    def _():
        for c in cols:
            main_ref[:, c] = _sigmoid(_dot(xn_ref[...], w_ref[:, c])).astype(BF16)


def _in_proj(x2, g, w_main, w_tail, conv_w, layer, seq, tm):
    t = x2.shape[0]
    tn = IN_PROJ_TN
    assert CONV_COLS == tn and GATE_COL0 == tn and MAIN_COLS == 2 * tn
    kern = functools.partial(_in_proj_kernel, tiles_per_seq=seq // tm)
    return pl.pallas_call(
        kern,
        grid=(t // tm, PROJ_COLS // tn),
        in_specs=[pl.BlockSpec((tm, D_MODEL), lambda i, j: (i, 0)),
                  pl.BlockSpec((1, D_MODEL), lambda i, j: (0, 0)),
                  pl.BlockSpec((None, D_MODEL, tn), lambda i, j: (layer, 0, j)),
                  pl.BlockSpec((None, D_MODEL, TAIL_COLS), lambda i, j: (layer, 0, 0)),
                  pl.BlockSpec((CONV_WIDTH, D_MODEL), lambda i, j: (0, 0))],
        out_specs=[pl.BlockSpec((tm, D_MODEL), lambda i, j: (i, 0)),
                   pl.BlockSpec((tm, tn), lambda i, j: (i, jnp.maximum(j - 1, 0))),
                   pl.BlockSpec((tm, TAIL_COLS), lambda i, j: (i, 0))],
        out_shape=[jax.ShapeDtypeStruct((t, D_MODEL), BF16),
                   jax.ShapeDtypeStruct((t, MAIN_COLS), BF16),
                   jax.ShapeDtypeStruct((t, TAIL_COLS), F32)],
        scratch_shapes=[pltpu.VMEM((tm, D_MODEL), BF16), pltpu.VMEM((SUBLANES, D_MODEL), F32)],
        compiler_params=pltpu.CompilerParams(
            dimension_semantics=("arbitrary", "arbitrary"), vmem_limit_bytes=VMEM_LIMIT),
        name="in_proj",
    )(x2, g, w_main, w_tail, conv_w)


def _time_scan(x, op, row):
    s = 1
    while s < CHUNK:
        x = jnp.where(row >= s, op(x, pltpu.roll(x, s, 0)), x)
        s *= 2
    return x


def _mlstm_chunk(rows, q_ref, k_ref, v_ref, gate_ref, bias_ref, num_ref, dst_ref, c_state, m_prev):
    L = CHUNK
    row = lax.broadcasted_iota(jnp.int32, (L, LANES), 0)
    lane = lax.broadcasted_iota(jnp.int32, (L, LANES), 1)
    gate_lanes = jnp.logical_and(lane >= IGATE_LANE, lane < FGATE_LANE + HEADS)
    tri = (lax.broadcasted_iota(jnp.int32, (L, L), 0) >= lax.broadcasted_iota(jnp.int32, (L, L), 1))
    lane_v = lax.broadcasted_iota(jnp.int32, (L, MLSTM_DV), 1)
    c_state = list(c_state)
    pre = jnp.where(gate_lanes, gate_ref[rows, :] + bias_ref[...], 0.0)
    capped = GATE_CAP * jnp.tanh(pre * (1.0 / GATE_CAP))
    log_i = capped * LOG2E
    log_f = (jnp.minimum(capped, 0.0) - jnp.log1p(jnp.exp(-jnp.abs(capped)))) * LOG2E
    log_f = pltpu.roll(log_f, LANES - HEADS, 1)
    b = _time_scan(log_f, jnp.add, row)
    b_last = b[L - 1:L, :]
    rowd = log_i - b
    big_m = jnp.maximum(_time_scan(rowd, jnp.maximum, row), m_prev)
    e_neg_m = jnp.exp2(-(b + big_m))
    a = b_last + rowd
    m_loc = jnp.max(a, axis=0, keepdims=True)
    m_new = jnp.maximum(b_last + m_prev, m_loc)
    s_old = jnp.exp2(b_last + m_prev - m_new)
    w = jnp.exp2(a - m_new)
    rowd_t = rowd.T
    w_t = w.T
    q_all = q_ref[rows, :] * (MLSTM_DK ** -0.5)
    k_t = k_ref[rows, :].astype(F32).T

    hk = [slice(h * MLSTM_DK, (h + 1) * MLSTM_DK) for h in range(HEADS)]
    hv = [slice(h * MLSTM_DV, (h + 1) * MLSTM_DV) for h in range(HEADS)]
    gl = [slice(IGATE_LANE + h, IGATE_LANE + h + 1) for h in range(HEADS)]
    v_aug = [jnp.concatenate([v_ref[rows, hv[h]], (lane_v == IGATE_LANE + h).astype(BF16)], axis=1)
             for h in range(HEADS)]
    scores = [_dot(q_all[:, hk[h]], k_t[hk[h], :].astype(BF16)) for h in range(HEADS)]
    c_loc = [_dot((k_t[hk[h], :] * w_t[gl[h], :]).astype(BF16), v_aug[h]) for h in range(HEADS)]
    big_m_b = [jnp.broadcast_to(big_m[:, gl[h]], (L, L)) for h in range(HEADS)]
    dmat = [jnp.where(tri, jnp.exp2(rowd_t[gl[h], :] - big_m_b[h]), 0.0) for h in range(HEADS)]
    q_state = [(q_all[:, hk[h]].astype(F32)
                * jnp.exp2(m_prev[:, gl[h]] - big_m_b[h][:, :MLSTM_DK])).astype(BF16) for h in range(HEADS)]
    num_aug = []
    for h in range(HEADS):
        lhs = jnp.concatenate([(scores[h] * dmat[h]).astype(BF16), q_state[h],
                               jnp.zeros((L, L - MLSTM_DK), BF16)], axis=1)
        rhs = jnp.concatenate([v_aug[h], c_state[h].astype(BF16),
                               jnp.zeros((L - MLSTM_DK, 2 * MLSTM_DV), BF16)], axis=0)
        num_aug.append(_dot(lhs, rhs))
    for h in range(HEADS):
        num_ref[rows, hv[h]] = num_aug[h][:, :MLSTM_DV].astype(num_ref.dtype)
        c_state[h] = s_old[:, gl[h]] * c_state[h] + c_loc[h]
    den = functools.reduce(jnp.add, [n[:, MLSTM_DV:] for n in num_aug])
    dst_ref[rows, :] = jnp.maximum(jnp.abs(den), e_neg_m)
    return c_state, m_new


def _mlstm_kernel(q_ref, k_ref, v_ref, gate_ref, bias_ref, num_ref, dst_ref, c_ref, m_ref, *, n_chunk):
    @pl.when(pl.program_id(1) == 0)
    def _():
        c_ref[...] = jnp.zeros_like(c_ref)
        m_ref[...] = jnp.zeros_like(m_ref)

    m_prev = m_ref[...]
    c_state = [c_ref[h] for h in range(HEADS)]
    for g in range(n_chunk):
        c_state, m_prev = _mlstm_chunk(slice(g * CHUNK, (g + 1) * CHUNK), q_ref, k_ref, v_ref, gate_ref, bias_ref,
                                       num_ref, dst_ref, c_state, m_prev)
    m_ref[...] = m_prev
    for h in range(HEADS):
        c_ref[h] = c_state[h]


def _mlstm(main, tail, gate_bias, bsz, seq, n_chunk):
    t = main.shape[0]
    rows = n_chunk * CHUNK
    nc = seq // rows
    row = lambda b, c: b * nc + c
    col_block = lambda col0, width: pl.BlockSpec((rows, width), lambda b, c: (row(b, c), col0 // width))
    return pl.pallas_call(
        functools.partial(_mlstm_kernel, n_chunk=n_chunk),
        grid=(bsz, nc),
        in_specs=[col_block(COL_MQ, HEADS * MLSTM_DK), col_block(COL_MK, HEADS * MLSTM_DK),
                  col_block(COL_MV, HEADS * MLSTM_DV),
                  col_block(TAIL_KROPE, LANES),
                  pl.BlockSpec((1, LANES), lambda b, c: (0, 0))],
        out_specs=[pl.BlockSpec((rows, D_MODEL), lambda b, c: (row(b, c), 0)),
                   pl.BlockSpec((rows, LANES), lambda b, c: (row(b, c), 0))],
        out_shape=[jax.ShapeDtypeStruct((t, D_MODEL), BF16), jax.ShapeDtypeStruct((t, LANES), F32)],
        scratch_shapes=[pltpu.VMEM((HEADS, MLSTM_DK, 2 * MLSTM_DV), F32),
                        pltpu.VMEM((1, LANES), F32)],
        compiler_params=pltpu.CompilerParams(dimension_semantics=("parallel", "arbitrary")),
        name="mlstm",
    )(main, main, main, tail, gate_bias)


def _rms0(x, g_col):
    return x * lax.rsqrt(jnp.mean(x * x, axis=0, keepdims=True) + NORM_EPS) * g_col


def _rope(x, cs):
    half = MLA_ROPE // 2
    rot = jnp.concatenate([x[:, half:], x[:, :half]], axis=-1)
    return x * cs[:, :MLA_ROPE] + rot * cs[:, MLA_ROPE:]


def _mla_prep_kernel(tail_ref, cs_ref, cst_ref, qa_ref, wqt_ref, kva_ref, wk_ref, wvt_ref, qn_ref, kn_ref,
                     qt_out, k_out, vt_out):
    tm = ATT_T
    half = MLA_ROPE // 2
    qn_g, qp_g = qn_ref[:MLA_NOPE, :], qn_ref[MLA_NOPE:, :]
    kn_g, kp_g = kn_ref[:, :MLA_NOPE], kn_ref[:, MLA_NOPE:]
    pe0 = HEADS * MLA_NOPE
    pad = QK_PAD - MLA_NOPE - MLA_ROPE
    q_pad = jnp.where(lax.broadcasted_iota(jnp.int32, (pad, tm), 0) == 0, -SCORE_SHIFT, 0.0).astype(BF16)
    k_pad = (lax.broadcasted_iota(jnp.int32, (tm, pad), 1) == 0).astype(BF16)
    for s in range(tail_ref.shape[0] // tm):
        rows = slice(s * tm, (s + 1) * tm)
        cq = _rms(tail_ref[rows, :Q_LORA], qa_ref[...]).astype(BF16)
        ckv = _rms(tail_ref[rows, Q_LORA:Q_LORA + KV_LORA], kva_ref[...]).astype(BF16)
        qft = _dot_nt(wqt_ref[...], cq)
        kf = _dot(ckv, wk_ref[...])
        vft = _dot_nt(wvt_ref[...], ckv)
        cos_t, sin_t = cst_ref[:half, rows], cst_ref[half:, rows]
        k_pe = _rope(_rms(tail_ref[rows, TAIL_KROPE:TAIL_KROPE + MLA_ROPE], kp_g), cs_ref[rows, :]).astype(BF16)
        for h in range(HEADS):
            q_nope = _rms0(qft[h * MLA_NOPE:(h + 1) * MLA_NOPE], qn_g)
            q_pe = _rms0(qft[pe0 + h * MLA_ROPE:pe0 + (h + 1) * MLA_ROPE], qp_g)
            x1, x2 = q_pe[:half], q_pe[half:]
            q_rot = jnp.concatenate([x1 * cos_t - x2 * sin_t, x1 * sin_t + x2 * cos_t], axis=0)
            qt_out[0, h, s] = jnp.concatenate([q_nope.astype(BF16), q_rot.astype(BF16), q_pad], axis=0)
            k_nope = _rms(kf[:, h * MLA_NOPE:(h + 1) * MLA_NOPE], kn_g)
            k_out[0, h, rows, :] = jnp.concatenate([k_nope.astype(BF16), k_pe, k_pad], axis=-1)
            vt_out[0, h, s] = vft[h * MLA_V:(h + 1) * MLA_V].astype(BF16)


def _mla_prep(tail, cs, cst, qa, wqt, kva, wk, wvt, qn_col, kn, bsz, seq):
    tiles = 2 if (seq // ATT_T) % 2 == 0 else 1
    tm = tiles * ATT_T
    nt = seq // tm
    full = lambda shape: pl.BlockSpec(shape, lambda i: (0,) * len(shape))
    blocked = lambda rows: pl.BlockSpec((1, HEADS, tiles, rows, ATT_T), lambda i: (i // nt, 0, i % nt, 0, 0))
    return pl.pallas_call(
        _mla_prep_kernel,
        grid=(bsz * nt,),
        in_specs=[pl.BlockSpec((tm, TAIL_COLS), lambda i: (i, 0)),
                  pl.BlockSpec((tm, 2 * MLA_ROPE), lambda i: (i, 0)),
                  pl.BlockSpec((MLA_ROPE, tm), lambda i: (0, i)),
                  full((1, Q_LORA)), full(wqt.shape), full((1, KV_LORA)), full(wk.shape), full(wvt.shape),
                  full((MLA_NOPE + MLA_ROPE, 1)), full((1, MLA_NOPE + MLA_ROPE))],
        out_specs=[blocked(QK_PAD),
                   pl.BlockSpec((1, HEADS, tm, QK_PAD), lambda i: (i // nt, 0, i % nt, 0)),
                   blocked(MLA_V)],
        out_shape=[jax.ShapeDtypeStruct((bsz, HEADS, seq // ATT_T, QK_PAD, ATT_T), BF16),
                   jax.ShapeDtypeStruct((bsz, HEADS, seq, QK_PAD), BF16),
                   jax.ShapeDtypeStruct((bsz, HEADS, seq // ATT_T, MLA_V, ATT_T), BF16)],
        compiler_params=pltpu.CompilerParams(dimension_semantics=("parallel",),
                                             vmem_limit_bytes=VMEM_LIMIT),
        name="mla_prep",
    )(tail, cs, cst, qa, wqt, kva, wk, wvt, qn_col, kn)


def _attn_kernel(qt_ref, k_ref, vt_ref, o_ref, s_scr, p_scr, l_all, acc_all, *online_scr, nq, online):
    t = ATT_T
    unroll = ATT_UNROLL if online else ATT_UNROLL_BOUNDED
    a_scr, m_all = online_scr if online else (None, None)

    def qk(qi, j):
        k0 = pl.multiple_of(j * t, t)
        return _dot(k_ref[0, 0, pl.ds(k0, t), :], qt_ref[0, 0, qi])

    def accumulate(tile, alpha, p):
        q, j = tile
        pv = _dot(vt_ref[0, 0, j], p)
        acc_all[q] = alpha * acc_all[q] + pv if online else acc_all[q] + pv

    def stage(prev, cur, nxt, s_cur, p_prev, alpha_prev, masked):
        accumulate(prev, alpha_prev, p_prev)
        qi = cur[0]
        if masked:
            keep = lax.broadcasted_iota(jnp.int32, (t, t), 0) <= lax.broadcasted_iota(jnp.int32, (t, t), 1)
        if online:
            if masked:
                s_cur = jnp.where(keep, s_cur, MASK_VALUE)
            m_old = m_all[qi]
            m_new = jnp.maximum(m_old, jnp.max(s_cur, axis=0, keepdims=True))
            m_all[qi] = m_new
            alpha = jnp.exp2(m_old - m_new)
            p = jnp.exp2((s_cur - m_new).astype(BF16))
            l_all[qi] = alpha * l_all[qi] + jnp.sum(p.astype(F32), axis=0, keepdims=True)
        else:
            alpha = None
            p = jnp.exp2(s_cur)
            if masked:
                p = jnp.where(keep, p, 0.0)
            l_all[qi] += jnp.sum(p, axis=0, keepdims=True)
            p = p.astype(BF16)
        s_next = qk(jnp.minimum(nxt[0], nq - 1), jnp.minimum(nxt[1], nq - 1))
        return s_next, p, alpha

    def advance_lower(qi, j):
        wrap = j + 1 == qi
        return jnp.where(wrap, qi + 1, qi), jnp.where(wrap, 0, j + 1)

    def advance_diag(qi, j):
        return qi + 1, j + 1

    def run_phase(n_steps, first, advance, masked):
        if n_steps == 0:
            return
        s_scr[...] = qk(*first)
        p_scr[...] = jnp.zeros_like(p_scr)
        if online:
            a_scr[...] = jnp.ones_like(a_scr)

        def steps(count, prev, cur):
            s_cur, p_prev = s_scr[...], p_scr[...]
            alpha_prev = a_scr[...] if online else None
            for _ in range(count):
                nxt = advance(*cur)
                s_cur, p_prev, alpha_prev = stage(prev, cur, nxt, s_cur, p_prev, alpha_prev, masked)
                prev, cur = cur, nxt
            s_scr[...] = s_cur
            p_scr[...] = p_prev
            if online:
                a_scr[...] = alpha_prev
            return prev, cur

        first = (jnp.int32(first[0]), jnp.int32(first[1]))
        prev, cur = lax.fori_loop(0, n_steps // unroll, lambda _, c: steps(unroll, *c), (first, first))
        if n_steps % unroll:
            prev, cur = steps(n_steps % unroll, prev, cur)
        accumulate(prev, a_scr[...] if online else None, p_scr[...])

    if online:
        m_all[...] = jnp.full_like(m_all, MASK_VALUE)
    l_all[...] = jnp.zeros_like(l_all)
    acc_all[...] = jnp.zeros_like(acc_all)
    run_phase(nq * (nq - 1) // 2, (1, 0), advance_lower, False)
    run_phase(nq, (0, 0), advance_diag, True)
    for qi in range(nq):
        o_ref[0, qi * t:(qi + 1) * t, :] = (acc_all[qi] / l_all[qi]).T.astype(o_ref.dtype)


def _attention(qt, k, vt, *, online):
    bsz, _, seq, _ = k.shape
    nq = seq // ATT_T
    online_scr = [pltpu.VMEM((1, ATT_T), F32), pltpu.VMEM((nq, 1, ATT_T), F32)] if online else []
    return pl.pallas_call(
        functools.partial(_attn_kernel, nq=nq, online=online),
        grid=(bsz, HEADS),
        in_specs=[pl.BlockSpec((1, 1, nq, QK_PAD, ATT_T), lambda b, h: (b, h, 0, 0, 0)),
                  pl.BlockSpec((1, 1, seq, QK_PAD), lambda b, h: (b, h, 0, 0)),
                  pl.BlockSpec((1, 1, nq, MLA_V, ATT_T), lambda b, h: (b, h, 0, 0, 0))],
        out_specs=pl.BlockSpec((1, seq, MLA_V), lambda b, h: (b, 0, h)),
        out_shape=jax.ShapeDtypeStruct((bsz, seq, HEADS * MLA_V), BF16),
        scratch_shapes=[pltpu.VMEM((ATT_T, ATT_T), F32), pltpu.VMEM((ATT_T, ATT_T), BF16),
                        pltpu.VMEM((nq, 1, ATT_T), F32), pltpu.VMEM((nq, MLA_V, ATT_T), F32)] + online_scr,
        compiler_params=pltpu.CompilerParams(dimension_semantics=("parallel", "parallel"),
                                             vmem_limit_bytes=VMEM_LIMIT),
        name="attn_online" if online else "attn_bounded",
    )(qt, k, vt)


def _mlstm_head_out(num_ref, dst_ref, o_ref, hn_ref):
    out_gain = _sigmoid(o_ref[...].astype(F32)) * hn_ref[...]
    heads = []
    for h in range(HEADS):
        hv = slice(h * MLSTM_DV, (h + 1) * MLSTM_DV)
        num = num_ref[:, hv].astype(F32)
        d = dst_ref[:, IGATE_LANE + h:IGATE_LANE + h + 1]
        scale = lax.rsqrt(jnp.mean(num * num, axis=-1, keepdims=True) + NORM_EPS * d * d)
        heads.append((num * scale * out_gain[:, hv]).astype(BF16))
    return jnp.concatenate(heads, axis=1)


def _merge_kernel(ya_ref, gate_ref, num_ref, dst_ref, o_ref, hn_ref, yc_ref, x_ref, wb_ref, wo_ref, out_ref):
    g = gate_ref[...]
    merged = g[:, :D_MODEL].astype(F32) * _dot(ya_ref[...], wb_ref[0])
    y_b = _mlstm_head_out(num_ref, dst_ref, o_ref, hn_ref)
    merged += g[:, D_MODEL:2 * D_MODEL].astype(F32) * _dot(y_b, wb_ref[1])
    merged += g[:, 2 * D_MODEL:].astype(F32) * _dot(yc_ref[...], wb_ref[2])
    out_ref[...] = x_ref[...] + _dot(merged.astype(BF16), wo_ref[...])


def _merge(y_a, main, num, dstat, head_norm, y_c, x2, w_branch, w_out, layer, tm):
    t = x2.shape[0]
    col = lambda c: pl.BlockSpec((tm, D_MODEL), lambda i: (i, c))
    return pl.pallas_call(
        _merge_kernel,
        grid=(t // tm,),
        in_specs=[col(0),
                  pl.BlockSpec((tm, N_BRANCH * D_MODEL), lambda i: (i, GATE_COL0 // (N_BRANCH * D_MODEL))),
                  col(0), pl.BlockSpec((tm, LANES), lambda i: (i, 0)), col(COL_MO // D_MODEL),
                  pl.BlockSpec((1, D_MODEL), lambda i: (0, 0)),
                  col(0), col(0),
                  pl.BlockSpec((None, N_BRANCH, D_MODEL, D_MODEL), lambda i: (layer, 0, 0, 0),
                               pipeline_mode=pl.Buffered(1)),
                  pl.BlockSpec((None, D_MODEL, D_MODEL), lambda i: (layer, 0, 0), pipeline_mode=pl.Buffered(1))],
        out_specs=col(0),
        out_shape=jax.ShapeDtypeStruct((t, D_MODEL), F32),
        compiler_params=pltpu.CompilerParams(dimension_semantics=("parallel",),
                                             vmem_limit_bytes=VMEM_LIMIT),
        name="merge",
    )(y_a, main, num, dstat, main, head_norm, y_c, x2, w_branch, w_out)


def _mlp_kernel(x_ref, g_ref, wu_ref, wd_ref, out_ref, *, n_chunk):
    x = x_ref[...]
    h = _rms(x, g_ref[...]).astype(BF16)
    ck = D_FF // n_chunk
    acc = x
    for c in range(n_chunk):
        u = jnp.maximum(_dot(h, wu_ref[:, c * ck:(c + 1) * ck]), 0.0)
        acc = acc + _dot((u * u).astype(BF16), wd_ref[c * ck:(c + 1) * ck, :])
    out_ref[...] = acc


def _mlp(x2, g, w_up, w_down, layer, tm):
    t = x2.shape[0]
    kern = functools.partial(_mlp_kernel, n_chunk=4)
    return pl.pallas_call(
        kern,
        grid=(t // tm,),
        in_specs=[pl.BlockSpec((tm, D_MODEL), lambda i: (i, 0)),
                  pl.BlockSpec((1, D_MODEL), lambda i: (0, 0)),
                  pl.BlockSpec((None, D_MODEL, D_FF), lambda i: (layer, 0, 0), pipeline_mode=pl.Buffered(1)),
                  pl.BlockSpec((None, D_FF, D_MODEL), lambda i: (layer, 0, 0), pipeline_mode=pl.Buffered(1))],
        out_specs=pl.BlockSpec((tm, D_MODEL), lambda i: (i, 0)),
        out_shape=jax.ShapeDtypeStruct((t, D_MODEL), F32),
        compiler_params=pltpu.CompilerParams(dimension_semantics=("parallel",),
                                             vmem_limit_bytes=VMEM_LIMIT),
        name="mlp",
    )(x2, g, w_up, w_down)


def _regroup_w_in(w_in):
    i0 = CONV_COLS + GATE_COL0
    cq0 = i0 + 2 * HEADS
    g0 = cq0 + Q_LORA + KV_LORA + MLA_ROPE
    pad = jnp.zeros(w_in.shape[:2] + (TAIL_COLS - (Q_LORA + KV_LORA + MLA_ROPE + 2 * HEADS),), w_in.dtype)
    w_main = jnp.concatenate([w_in[..., :i0], w_in[..., g0:]], axis=-1).astype(BF16)
    w_tail = jnp.concatenate([w_in[..., cq0:g0], w_in[..., i0:cq0], pad], axis=-1).astype(BF16)
    return w_main, w_tail


def _heads_last_split(w, first, second):
    k = w.shape[0]
    w3 = w.reshape(k, HEADS, first + second)
    return jnp.concatenate([w3[:, :, :first].reshape(k, HEADS * first),
                            w3[:, :, first:].reshape(k, HEADS * second)], axis=1).astype(BF16)


def _score_bound(q_gain, k_gain):
    def sq_norm(g):
        g = g.astype(F32)
        return MLA_NOPE * jnp.max(g[:MLA_NOPE] ** 2) + MLA_ROPE * jnp.max(g[MLA_NOPE:] ** 2)
    return jnp.sqrt(sq_norm(q_gain) * sq_norm(k_gain)) * (Q_SCALE * BOUND_SLACK)


def _pick(n, candidates):
    for c in candidates:
        if n % c == 0:
            return c
    raise ValueError(f"no tile size for {n}")


def kernel(x, positions, mix_norm, w_in, conv_w, mlstm_igate_bias, mlstm_fgate_bias, mlstm_head_norm,
           mla_q_a_norm, mla_w_uq, mla_kv_a_norm, mla_w_ukv, mla_q_norm, mla_k_norm, w_branch, w_out,
           mlp_norm, w_up, w_down):
    bsz, seq, _ = x.shape
    depth = w_in.shape[0]
    t = bsz * seq
    assert seq % ATT_T == 0
    n_chunk = _pick(seq // CHUNK, tuple(g for g in (8, 4, 2, 1) if g <= MLSTM_GROUP))
    tm_proj = _pick(seq, (1024, 512, 256, 128))
    tm = _pick(seq, (512, 256, 128))
    tm_mlp = _pick(t, (1024, 512, 256, 128))

    x2 = x.reshape(t, D_MODEL)
    cs, cst = _rope_table(positions, ATT_T)
    row = lambda a: a.reshape(1, -1).astype(F32)
    w_main, w_tail = _regroup_w_in(w_in)
    w_branch_b, w_out_b, w_up_b, w_down_b = (w.astype(BF16) for w in (w_branch, w_out, w_up, w_down))

    for l in range(depth):
        gate_bias = jnp.zeros((1, LANES), F32)
        gate_bias = gate_bias.at[0, IGATE_LANE:IGATE_LANE + HEADS].set(mlstm_igate_bias[l])
        gate_bias = gate_bias.at[0, FGATE_LANE:FGATE_LANE + HEADS].set(mlstm_fgate_bias[l])
        w_q = _heads_last_split(mla_w_uq[l], MLA_NOPE, MLA_ROPE)
        w_kv = _heads_last_split(mla_w_ukv[l], MLA_NOPE, MLA_V)

        y_a, main, tail = _in_proj(x2, row(mix_norm[l]), w_main, w_tail, conv_w[l].astype(F32), l, seq, tm_proj)
        qt, k, vt = _mla_prep(tail, cs, cst, row(mla_q_a_norm[l]), w_q.T, row(mla_kv_a_norm[l]),
                              w_kv[:, :HEADS * MLA_NOPE], w_kv[:, HEADS * MLA_NOPE:].T,
                              mla_q_norm[l].reshape(-1, 1).astype(F32) * Q_SCALE, row(mla_k_norm[l]), bsz, seq)
        num, dstat = _mlstm(main, tail, gate_bias, bsz, seq, n_chunk)
        y_c = lax.cond(_score_bound(mla_q_norm[l], mla_k_norm[l]) <= SCORE_BOUND,
                       functools.partial(_attention, online=False), functools.partial(_attention, online=True),
                       qt, k, vt).reshape(t, D_MODEL)
        x2 = _merge(y_a, main, num, dstat, row(mlstm_head_norm[l]), y_c, x2, w_branch_b, w_out_b, l, tm)
        x2 = _mlp(x2, row(mlp_norm[l]), w_up_b, w_down_b, l, tm_mlp)
    return x2.reshape(bsz, seq, D_MODEL)
```

```python
import functools

import jax
import jax.numpy as jnp
import numpy as np
from jax import lax
from jax.experimental import pallas as pl
from jax.experimental.pallas import tpu as pltpu

F32 = jnp.float32
BF16 = jnp.bfloat16

D_MODEL = 1024
N_BRANCH = 3
CONV_WIDTH = 3
HEADS = 8
MLSTM_DK = 64
MLSTM_DV = 128
CHUNK = 128
GATE_CAP = 15.0
MLA_NOPE = 128
MLA_ROPE = 64
MLA_V = 128
Q_LORA = 256
KV_LORA = 128
ROPE_BASE = 10000.0
D_FF = 4 * D_MODEL
NORM_EPS = 1e-6
QK_PAD = 256

CONV_COLS = 3 * D_MODEL
PROJ_COLS = 9 * D_MODEL
MAIN_COLS = PROJ_COLS - CONV_COLS
TAIL_COLS = 512
COL_MQ = 0
COL_MK = COL_MQ + HEADS * MLSTM_DK
COL_MV = COL_MK + HEADS * MLSTM_DK
COL_MO = COL_MV + HEADS * MLSTM_DV
GATE_COL0 = COL_MO + HEADS * MLSTM_DV
LANES = 128
SUBLANES = 8
TAIL_KROPE = Q_LORA + KV_LORA
IGATE_LANE = MLA_ROPE
FGATE_LANE = MLA_ROPE + HEADS

IN_PROJ_TN = 3 * D_MODEL
IN_PROJ_CHUNK = 1536
CONV_CHUNK = 256
MLSTM_GROUP = 8
ATT_T = 512
LOG2E = 1.4426950408889634
Q_SCALE = (MLA_NOPE + MLA_ROPE) ** -0.5 * LOG2E
MASK_VALUE = -1e30
SCORE_BOUND = 70.0
SCORE_SHIFT = 48.0
BOUND_SLACK = 1.02
ATT_UNROLL = 8
ATT_UNROLL_BOUNDED = 60

VMEM_LIMIT = 56 * 1024 * 1024


def _rms(x, g):
    return x * lax.rsqrt(jnp.mean(x * x, axis=-1, keepdims=True) + NORM_EPS) * g


def _sigmoid(x):
    return 0.5 * jnp.tanh(0.5 * x) + 0.5


def _dot(a, b):
    return jnp.dot(a, b, preferred_element_type=F32)


def _dot_nt(a, b):
    return lax.dot_general(a, b, (((1,), (1,)), ((), ())), preferred_element_type=F32)


def _rope_table_kernel(pos_ref, freq_ref, cs_ref, cst_ref):
    ang = freq_ref[...] * pos_ref[...].astype(F32)
    c, s = jnp.cos(ang), jnp.sin(ang)
    cst_ref[...] = jnp.concatenate([c, s], axis=0)
    cs_ref[...] = jnp.concatenate([c, c, -s, s], axis=0).T


def _rope_table(positions, tm):
    t = positions.size
    half = MLA_ROPE // 2
    inv_freq = (np.float32(ROPE_BASE) ** (-np.arange(0, MLA_ROPE, 2, dtype=np.float32) / np.float32(MLA_ROPE)))
    inv_freq = jnp.asarray(inv_freq.astype(np.float32))
    return pl.pallas_call(
        _rope_table_kernel,
        grid=(t // tm,),
        in_specs=[pl.BlockSpec((1, tm), lambda i: (0, i)),
                  pl.BlockSpec((half, 1), lambda i: (0, 0))],
        out_specs=[pl.BlockSpec((tm, 2 * MLA_ROPE), lambda i: (i, 0)),
                   pl.BlockSpec((MLA_ROPE, tm), lambda i: (0, i))],
        out_shape=[jax.ShapeDtypeStruct((t, 2 * MLA_ROPE), F32),
                   jax.ShapeDtypeStruct((MLA_ROPE, t), F32)],
        name="rope_table",
    )(positions.reshape(1, t), inv_freq.reshape(half, 1))


def _in_proj_kernel(x_ref, g_ref, w_ref, wt_ref, cw_ref, ya_ref, main_ref, tail_ref, xn_ref, zprev_ref, *,
                    tiles_per_seq):
    i, j = pl.program_id(0), pl.program_id(1)
    tm = x_ref.shape[0]

    @pl.when(jnp.logical_and(j == 0, i % tiles_per_seq == 0))
    def _():
        zprev_ref[...] = jnp.zeros_like(zprev_ref)

    @pl.when(j == 0)
    def _():
        xn = _rms(x_ref[...], g_ref[...]).astype(BF16)
        xn_ref[...] = xn
        tail_ref[...] = _dot(xn, wt_ref[...])
        row = lax.broadcasted_iota(jnp.int32, (tm, CONV_CHUNK), 0)
        for c0 in range(0, D_MODEL, CONV_CHUNK):
            c = slice(c0, c0 + CONV_CHUNK)
            gate_b, gate_c, u = (_dot(xn, w_ref[:, n * D_MODEL + c0:n * D_MODEL + c0 + CONV_CHUNK]) for n in range(3))
            z = gate_c * u
            p1, p2 = zprev_ref[SUBLANES - 1:SUBLANES, c], zprev_ref[SUBLANES - 2:SUBLANES - 1, c]
            z1 = jnp.where(row == 0, p1, pltpu.roll(z, 1, 0))
            z2 = jnp.where(row == 0, p2, jnp.where(row == 1, p1, pltpu.roll(z, 2, 0)))
            zprev_ref[:, c] = z[tm - SUBLANES:, :]
            ya_ref[:, c] = (gate_b * (cw_ref[0:1, c] * z2 + cw_ref[1:2, c] * z1 + cw_ref[2:3, c] * z)).astype(BF16)

    cols = [slice(c, c + IN_PROJ_CHUNK) for c in range(0, w_ref.shape[1], IN_PROJ_CHUNK)]

    @pl.when(j == 1)
    def _():
        for c in cols:
            main_ref[:, c] = _dot(xn_ref[...], w_ref[:, c]).astype(BF16)

    @pl.when(j == 2)
    def _():
        for c in cols:
            main_ref[:, c] = _sigmoid(_dot(xn_ref[...], w_ref[:, c])).astype(BF16)


def _in_proj(x2, g, w_main, w_tail, conv_w, layer, seq, tm):
    t = x2.shape[0]
    tn = IN_PROJ_TN
    assert CONV_COLS == tn and GATE_COL0 == tn and MAIN_COLS == 2 * tn
    kern = functools.partial(_in_proj_kernel, tiles_per_seq=seq // tm)
    return pl.pallas_call(
        kern,
        grid=(t // tm, PROJ_COLS // tn),
        in_specs=[pl.BlockSpec((tm, D_MODEL), lambda i, j: (i, 0)),
                  pl.BlockSpec((1, D_MODEL), lambda i, j: (0, 0)),
                  pl.BlockSpec((None, D_MODEL, tn), lambda i, j: (layer, 0, j)),
                  pl.BlockSpec((None, D_MODEL, TAIL_COLS), lambda i, j: (layer, 0, 0)),
                  pl.BlockSpec((CONV_WIDTH, D_MODEL), lambda i, j: (0, 0))],
        out_specs=[pl.BlockSpec((tm, D_MODEL), lambda i, j: (i, 0)),
                   pl.BlockSpec((tm, tn), lambda i, j: (i, jnp.maximum(j - 1, 0))),
                   pl.BlockSpec((tm, TAIL_COLS), lambda i, j: (i, 0))],
        out_shape=[jax.ShapeDtypeStruct((t, D_MODEL), BF16),
                   jax.ShapeDtypeStruct((t, MAIN_COLS), BF16),
                   jax.ShapeDtypeStruct((t, TAIL_COLS), F32)],
        scratch_shapes=[pltpu.VMEM((tm, D_MODEL), BF16), pltpu.VMEM((SUBLANES, D_MODEL), F32)],
        compiler_params=pltpu.CompilerParams(
            dimension_semantics=("arbitrary", "arbitrary"), vmem_limit_bytes=VMEM_LIMIT),
        name="in_proj",
    )(x2, g, w_main, w_tail, conv_w)


def _time_scan(x, op, row):
    s = 1
    while s < CHUNK:
        x = jnp.where(row >= s, op(x, pltpu.roll(x, s, 0)), x)
        s *= 2
    return x


def _mlstm_chunk(rows, q_ref, k_ref, v_ref, gate_ref, bias_ref, num_ref, dst_ref, c_state, m_prev):
    L = CHUNK
    row = lax.broadcasted_iota(jnp.int32, (L, LANES), 0)
    lane = lax.broadcasted_iota(jnp.int32, (L, LANES), 1)
    gate_lanes = jnp.logical_and(lane >= IGATE_LANE, lane < FGATE_LANE + HEADS)
    tri = (lax.broadcasted_iota(jnp.int32, (L, L), 0) >= lax.broadcasted_iota(jnp.int32, (L, L), 1))
    lane_v = lax.broadcasted_iota(jnp.int32, (L, MLSTM_DV), 1)
    c_state = list(c_state)
    pre = jnp.where(gate_lanes, gate_ref[rows, :] + bias_ref[...], 0.0)
    capped = GATE_CAP * jnp.tanh(pre * (1.0 / GATE_CAP))
    log_i = capped * LOG2E
    log_f = (jnp.minimum(capped, 0.0) - jnp.log1p(jnp.exp(-jnp.abs(capped)))) * LOG2E
    log_f = pltpu.roll(log_f, LANES - HEADS, 1)
    b = _time_scan(log_f, jnp.add, row)
    b_last = b[L - 1:L, :]
    rowd = log_i - b
    big_m = jnp.maximum(_time_scan(rowd, jnp.maximum, row), m_prev)
    e_neg_m = jnp.exp2(-(b + big_m))
    a = b_last + rowd
    m_loc = jnp.max(a, axis=0, keepdims=True)
    m_new = jnp.maximum(b_last + m_prev, m_loc)
    s_old = jnp.exp2(b_last + m_prev - m_new)
    w = jnp.exp2(a - m_new)
    rowd_t = rowd.T
    w_t = w.T
    q_all = q_ref[rows, :] * (MLSTM_DK ** -0.5)
    k_t = k_ref[rows, :].astype(F32).T

    hk = [slice(h * MLSTM_DK, (h + 1) * MLSTM_DK) for h in range(HEADS)]
    hv = [slice(h * MLSTM_DV, (h + 1) * MLSTM_DV) for h in range(HEADS)]
    gl = [slice(IGATE_LANE + h, IGATE_LANE + h + 1) for h in range(HEADS)]
    v_aug = [jnp.concatenate([v_ref[rows, hv[h]], (lane_v == IGATE_LANE + h).astype(BF16)], axis=1)
             for h in range(HEADS)]
    scores = [_dot(q_all[:, hk[h]], k_t[hk[h], :].astype(BF16)) for h in range(HEADS)]
    c_loc = [_dot((k_t[hk[h], :] * w_t[gl[h], :]).astype(BF16), v_aug[h]) for h in range(HEADS)]
    big_m_b = [jnp.broadcast_to(big_m[:, gl[h]], (L, L)) for h in range(HEADS)]
    dmat = [jnp.where(tri, jnp.exp2(rowd_t[gl[h], :] - big_m_b[h]), 0.0) for h in range(HEADS)]
    q_state = [(q_all[:, hk[h]].astype(F32)
                * jnp.exp2(m_prev[:, gl[h]] - big_m_b[h][:, :MLSTM_DK])).astype(BF16) for h in range(HEADS)]
    num_aug = []
    for h in range(HEADS):
        lhs = jnp.concatenate([(scores[h] * dmat[h]).astype(BF16), q_state[h],
                               jnp.zeros((L, L - MLSTM_DK), BF16)], axis=1)
        rhs = jnp.concatenate([v_aug[h], c_state[h].astype(BF16),
                               jnp.zeros((L - MLSTM_DK, 2 * MLSTM_DV), BF16)], axis=0)
        num_aug.append(_dot(lhs, rhs))
    for h in range(HEADS):
        num_ref[rows, hv[h]] = num_aug[h][:, :MLSTM_DV].astype(num_ref.dtype)
        c_state[h] = s_old[:, gl[h]] * c_state[h] + c_loc[h]
    den = functools.reduce(jnp.add, [n[:, MLSTM_DV:] for n in num_aug])
    dst_ref[rows, :] = jnp.maximum(jnp.abs(den), e_neg_m)
    return c_state, m_new


def _mlstm_kernel(q_ref, k_ref, v_ref, gate_ref, bias_ref, num_ref, dst_ref, c_ref, m_ref, *, n_chunk):
    @pl.when(pl.program_id(1) == 0)
    def _():
        c_ref[...] = jnp.zeros_like(c_ref)
        m_ref[...] = jnp.zeros_like(m_ref)

    m_prev = m_ref[...]
    c_state = [c_ref[h] for h in range(HEADS)]
    for g in range(n_chunk):
        c_state, m_prev = _mlstm_chunk(slice(g * CHUNK, (g + 1) * CHUNK), q_ref, k_ref, v_ref, gate_ref, bias_ref,
                                       num_ref, dst_ref, c_state, m_prev)
    m_ref[...] = m_prev
    for h in range(HEADS):
        c_ref[h] = c_state[h]


def _mlstm(main, tail, gate_bias, bsz, seq, n_chunk):
    t = main.shape[0]
    rows = n_chunk * CHUNK
    nc = seq // rows
    row = lambda b, c: b * nc + c
    col_block = lambda col0, width: pl.BlockSpec((rows, width), lambda b, c: (row(b, c), col0 // width))
    return pl.pallas_call(
        functools.partial(_mlstm_kernel, n_chunk=n_chunk),
        grid=(bsz, nc),
        in_specs=[col_block(COL_MQ, HEADS * MLSTM_DK), col_block(COL_MK, HEADS * MLSTM_DK),
                  col_block(COL_MV, HEADS * MLSTM_DV),
                  col_block(TAIL_KROPE, LANES),
                  pl.BlockSpec((1, LANES), lambda b, c: (0, 0))],
        out_specs=[pl.BlockSpec((rows, D_MODEL), lambda b, c: (row(b, c), 0)),
                   pl.BlockSpec((rows, LANES), lambda b, c: (row(b, c), 0))],
        out_shape=[jax.ShapeDtypeStruct((t, D_MODEL), BF16), jax.ShapeDtypeStruct((t, LANES), F32)],
        scratch_shapes=[pltpu.VMEM((HEADS, MLSTM_DK, 2 * MLSTM_DV), F32),
                        pltpu.VMEM((1, LANES), F32)],
        compiler_params=pltpu.CompilerParams(dimension_semantics=("parallel", "arbitrary")),
        name="mlstm",
    )(main, main, main, tail, gate_bias)


def _rms0(x, g_col):
    return x * lax.rsqrt(jnp.mean(x * x, axis=0, keepdims=True) + NORM_EPS) * g_col


def _rope(x, cs):
    half = MLA_ROPE // 2
    rot = jnp.concatenate([x[:, half:], x[:, :half]], axis=-1)
    return x * cs[:, :MLA_ROPE] + rot * cs[:, MLA_ROPE:]


def _mla_prep_kernel(tail_ref, cs_ref, cst_ref, qa_ref, wqt_ref, kva_ref, wk_ref, wvt_ref, qn_ref, kn_ref,
                     qt_out, k_out, vt_out):
    tm = ATT_T
    half = MLA_ROPE // 2
    qn_g, qp_g = qn_ref[:MLA_NOPE, :], qn_ref[MLA_NOPE:, :]
    kn_g, kp_g = kn_ref[:, :MLA_NOPE], kn_ref[:, MLA_NOPE:]
    pe0 = HEADS * MLA_NOPE
    pad = QK_PAD - MLA_NOPE - MLA_ROPE
    q_pad = jnp.where(lax.broadcasted_iota(jnp.int32, (pad, tm), 0) == 0, -SCORE_SHIFT, 0.0).astype(BF16)
    k_pad = (lax.broadcasted_iota(jnp.int32, (tm, pad), 1) == 0).astype(BF16)
    for s in range(tail_ref.shape[0] // tm):
        rows = slice(s * tm, (s + 1) * tm)
        cq = _rms(tail_ref[rows, :Q_LORA], qa_ref[...]).astype(BF16)
        ckv = _rms(tail_ref[rows, Q_LORA:Q_LORA + KV_LORA], kva_ref[...]).astype(BF16)
        qft = _dot_nt(wqt_ref[...], cq)
        kf = _dot(ckv, wk_ref[...])
        vft = _dot_nt(wvt_ref[...], ckv)
        cos_t, sin_t = cst_ref[:half, rows], cst_ref[half:, rows]
        k_pe = _rope(_rms(tail_ref[rows, TAIL_KROPE:TAIL_KROPE + MLA_ROPE], kp_g), cs_ref[rows, :]).astype(BF16)
        for h in range(HEADS):
            q_nope = _rms0(qft[h * MLA_NOPE:(h + 1) * MLA_NOPE], qn_g)
            q_pe = _rms0(qft[pe0 + h * MLA_ROPE:pe0 + (h + 1) * MLA_ROPE], qp_g)
            x1, x2 = q_pe[:half], q_pe[half:]
            q_rot = jnp.concatenate([x1 * cos_t - x2 * sin_t, x1 * sin_t + x2 * cos_t], axis=0)
            qt_out[0, h, s] = jnp.concatenate([q_nope.astype(BF16), q_rot.astype(BF16), q_pad], axis=0)
            k_nope = _rms(kf[:, h * MLA_NOPE:(h + 1) * MLA_NOPE], kn_g)
            k_out[0, h, rows, :] = jnp.concatenate([k_nope.astype(BF16), k_pe, k_pad], axis=-1)
            vt_out[0, h, s] = vft[h * MLA_V:(h + 1) * MLA_V].astype(BF16)


def _mla_prep(tail, cs, cst, qa, wqt, kva, wk, wvt, qn_col, kn, bsz, seq):
    tiles = 2 if (seq // ATT_T) % 2 == 0 else 1
    tm = tiles * ATT_T
    nt = seq // tm
    full = lambda shape: pl.BlockSpec(shape, lambda i: (0,) * len(shape))
    blocked = lambda rows: pl.BlockSpec((1, HEADS, tiles, rows, ATT_T), lambda i: (i // nt, 0, i % nt, 0, 0))
    return pl.pallas_call(
        _mla_prep_kernel,
        grid=(bsz * nt,),
        in_specs=[pl.BlockSpec((tm, TAIL_COLS), lambda i: (i, 0)),
                  pl.BlockSpec((tm, 2 * MLA_ROPE), lambda i: (i, 0)),
                  pl.BlockSpec((MLA_ROPE, tm), lambda i: (0, i)),
                  full((1, Q_LORA)), full(wqt.shape), full((1, KV_LORA)), full(wk.shape), full(wvt.shape),
                  full((MLA_NOPE + MLA_ROPE, 1)), full((1, MLA_NOPE + MLA_ROPE))],
        out_specs=[blocked(QK_PAD),
                   pl.BlockSpec((1, HEADS, tm, QK_PAD), lambda i: (i // nt, 0, i % nt, 0)),
                   blocked(MLA_V)],
        out_shape=[jax.ShapeDtypeStruct((bsz, HEADS, seq // ATT_T, QK_PAD, ATT_T), BF16),
                   jax.ShapeDtypeStruct((bsz, HEADS, seq, QK_PAD), BF16),
                   jax.ShapeDtypeStruct((bsz, HEADS, seq // ATT_T, MLA_V, ATT_T), BF16)],
        compiler_params=pltpu.CompilerParams(dimension_semantics=("parallel",),
                                             vmem_limit_bytes=VMEM_LIMIT),
        name="mla_prep",
    )(tail, cs, cst, qa, wqt, kva, wk, wvt, qn_col, kn)


def _attn_kernel(qt_ref, k_ref, vt_ref, o_ref, s_scr, p_scr, l_all, acc_all, *online_scr, nq, online):
    t = ATT_T
    unroll = ATT_UNROLL if online else ATT_UNROLL_BOUNDED
    a_scr, m_all = online_scr if online else (None, None)

    def qk(qi, j):
        k0 = pl.multiple_of(j * t, t)
        return _dot(k_ref[0, 0, pl.ds(k0, t), :], qt_ref[0, 0, qi])

    def accumulate(tile, alpha, p):
        q, j = tile
        pv = _dot(vt_ref[0, 0, j], p)
        acc_all[q] = alpha * acc_all[q] + pv if online else acc_all[q] + pv

    def stage(prev, cur, nxt, s_cur, p_prev, alpha_prev, masked):
        accumulate(prev, alpha_prev, p_prev)
        qi = cur[0]
        if masked:
            keep = lax.broadcasted_iota(jnp.int32, (t, t), 0) <= lax.broadcasted_iota(jnp.int32, (t, t), 1)
        if online:
            if masked:
                s_cur = jnp.where(keep, s_cur, MASK_VALUE)
            m_old = m_all[qi]
            m_new = jnp.maximum(m_old, jnp.max(s_cur, axis=0, keepdims=True))
            m_all[qi] = m_new
            alpha = jnp.exp2(m_old - m_new)
            p = jnp.exp2((s_cur - m_new).astype(BF16))
            l_all[qi] = alpha * l_all[qi] + jnp.sum(p.astype(F32), axis=0, keepdims=True)
        else:
            alpha = None
            p = jnp.exp2(s_cur)
            if masked:
                p = jnp.where(keep, p, 0.0)
            l_all[qi] += jnp.sum(p, axis=0, keepdims=True)
            p = p.astype(BF16)
        s_next = qk(jnp.minimum(nxt[0], nq - 1), jnp.minimum(nxt[1], nq - 1))
        return s_next, p, alpha

    def advance_lower(qi, j):
        wrap = j + 1 == qi
        return jnp.where(wrap, qi + 1, qi), jnp.where(wrap, 0, j + 1)

    def advance_diag(qi, j):
        return qi + 1, j + 1

    def run_phase(n_steps, first, advance, masked):
        if n_steps == 0:
            return
        s_scr[...] = qk(*first)
        p_scr[...] = jnp.zeros_like(p_scr)
        if online:
            a_scr[...] = jnp.ones_like(a_scr)

        def steps(count, prev, cur):
            s_cur, p_prev = s_scr[...], p_scr[...]
            alpha_prev = a_scr[...] if online else None
            for _ in range(count):
                nxt = advance(*cur)
                s_cur, p_prev, alpha_prev = stage(prev, cur, nxt, s_cur, p_prev, alpha_prev, masked)
                prev, cur = cur, nxt
            s_scr[...] = s_cur
            p_scr[...] = p_prev
            if online:
                a_scr[...] = alpha_prev
            return prev, cur

        first = (jnp.int32(first[0]), jnp.int32(first[1]))
        prev, cur = lax.fori_loop(0, n_steps // unroll, lambda _, c: steps(unroll, *c), (first, first))
        if n_steps % unroll:
            prev, cur = steps(n_steps % unroll, prev, cur)
        accumulate(prev, a_scr[...] if online else None, p_scr[...])

    if online:
        m_all[...] = jnp.full_like(m_all, MASK_VALUE)
    l_all[...] = jnp.zeros_like(l_all)
    acc_all[...] = jnp.zeros_like(acc_all)
    run_phase(nq * (nq - 1) // 2, (1, 0), advance_lower, False)
    run_phase(nq, (0, 0), advance_diag, True)
    for qi in range(nq):
        o_ref[0, qi * t:(qi + 1) * t, :] = (acc_all[qi] / l_all[qi]).T.astype(o_ref.dtype)


def _attention(qt, k, vt, *, online):
    bsz, _, seq, _ = k.shape
    nq = seq // ATT_T
    online_scr = [pltpu.VMEM((1, ATT_T), F32), pltpu.VMEM((nq, 1, ATT_T), F32)] if online else []
    return pl.pallas_call(
        functools.partial(_attn_kernel, nq=nq, online=online),
        grid=(bsz, HEADS),
        in_specs=[pl.BlockSpec((1, 1, nq, QK_PAD, ATT_T), lambda b, h: (b, h, 0, 0, 0)),
                  pl.BlockSpec((1, 1, seq, QK_PAD), lambda b, h: (b, h, 0, 0)),
                  pl.BlockSpec((1, 1, nq, MLA_V, ATT_T), lambda b, h: (b, h, 0, 0, 0))],
        out_specs=pl.BlockSpec((1, seq, MLA_V), lambda b, h: (b, 0, h)),
        out_shape=jax.ShapeDtypeStruct((bsz, seq, HEADS * MLA_V), BF16),
        scratch_shapes=[pltpu.VMEM((ATT_T, ATT_T), F32), pltpu.VMEM((ATT_T, ATT_T), BF16),
                        pltpu.VMEM((nq, 1, ATT_T), F32), pltpu.VMEM((nq, MLA_V, ATT_T), F32)] + online_scr,
        compiler_params=pltpu.CompilerParams(dimension_semantics=("parallel", "parallel"),
                                             vmem_limit_bytes=VMEM_LIMIT),
        name="attn_online" if online else "attn_bounded",
    )(qt, k, vt)


def _mlstm_head_out(num_ref, dst_ref, o_ref, hn_ref):
    out_gain = _sigmoid(o_ref[...].astype(F32)) * hn_ref[...]
    heads = []
    for h in range(HEADS):
        hv = slice(h * MLSTM_DV, (h + 1) * MLSTM_DV)
        num = num_ref[:, hv].astype(F32)
        d = dst_ref[:, IGATE_LANE + h:IGATE_LANE + h + 1]
        scale = lax.rsqrt(jnp.mean(num * num, axis=-1, keepdims=True) + NORM_EPS * d * d)
        heads.append((num * scale * out_gain[:, hv]).astype(BF16))
    return jnp.concatenate(heads, axis=1)


def _merge_kernel(ya_ref, gate_ref, num_ref, dst_ref, o_ref, hn_ref, yc_ref, x_ref, wb_ref, wo_ref, out_ref):
    g = gate_ref[...]
    merged = g[:, :D_MODEL].astype(F32) * _dot(ya_ref[...], wb_ref[0])
    y_b = _mlstm_head_out(num_ref, dst_ref, o_ref, hn_ref)
    merged += g[:, D_MODEL:2 * D_MODEL].astype(F32) * _dot(y_b, wb_ref[1])
    merged += g[:, 2 * D_MODEL:].astype(F32) * _dot(yc_ref[...], wb_ref[2])
    out_ref[...] = x_ref[...] + _dot(merged.astype(BF16), wo_ref[...])


def _merge(y_a, main, num, dstat, head_norm, y_c, x2, w_branch, w_out, layer, tm):
    t = x2.shape[0]
    col = lambda c: pl.BlockSpec((tm, D_MODEL), lambda i: (i, c))
    return pl.pallas_call(
        _merge_kernel,
        grid=(t // tm,),
        in_specs=[col(0),
                  pl.BlockSpec((tm, N_BRANCH * D_MODEL), lambda i: (i, GATE_COL0 // (N_BRANCH * D_MODEL))),
                  col(0), pl.BlockSpec((tm, LANES), lambda i: (i, 0)), col(COL_MO // D_MODEL),
                  pl.BlockSpec((1, D_MODEL), lambda i: (0, 0)),
                  col(0), col(0),
                  pl.BlockSpec((None, N_BRANCH, D_MODEL, D_MODEL), lambda i: (layer, 0, 0, 0),
                               pipeline_mode=pl.Buffered(1)),
                  pl.BlockSpec((None, D_MODEL, D_MODEL), lambda i: (layer, 0, 0), pipeline_mode=pl.Buffered(1))],
        out_specs=col(0),
        out_shape=jax.ShapeDtypeStruct((t, D_MODEL), F32),
        compiler_params=pltpu.CompilerParams(dimension_semantics=("parallel",),
                                             vmem_limit_bytes=VMEM_LIMIT),
        name="merge",
    )(y_a, main, num, dstat, main, head_norm, y_c, x2, w_branch, w_out)


def _mlp_kernel(x_ref, g_ref, wu_ref, wd_ref, out_ref, *, n_chunk):
    x = x_ref[...]
    h = _rms(x, g_ref[...]).astype(BF16)
    ck = D_FF // n_chunk
    acc = x
    for c in range(n_chunk):
        u = jnp.maximum(_dot(h, wu_ref[:, c * ck:(c + 1) * ck]), 0.0)
        acc = acc + _dot((u * u).astype(BF16), wd_ref[c * ck:(c + 1) * ck, :])
    out_ref[...] = acc


def _mlp(x2, g, w_up, w_down, layer, tm):
    t = x2.shape[0]
    kern = functools.partial(_mlp_kernel, n_chunk=4)
    return pl.pallas_call(
        kern,
        grid=(t // tm,),
        in_specs=[pl.BlockSpec((tm, D_MODEL), lambda i: (i, 0)),
                  pl.BlockSpec((1, D_MODEL), lambda i: (0, 0)),
                  pl.BlockSpec((None, D_MODEL, D_FF), lambda i: (layer, 0, 0), pipeline_mode=pl.Buffered(1)),
                  pl.BlockSpec((None, D_FF, D_MODEL), lambda i: (layer, 0, 0), pipeline_mode=pl.Buffered(1))],
        out_specs=pl.BlockSpec((tm, D_MODEL), lambda i: (i, 0)),
        out_shape=jax.ShapeDtypeStruct((t, D_MODEL), F32),
        compiler_params=pltpu.CompilerParams(dimension_semantics=("parallel",),
                                             vmem_limit_bytes=VMEM_LIMIT),
        name="mlp",
    )(x2, g, w_up, w_down)


def _regroup_w_in(w_in):
    i0 = CONV_COLS + GATE_COL0
    cq0 = i0 + 2 * HEADS
    g0 = cq0 + Q_LORA + KV_LORA + MLA_ROPE
    pad = jnp.zeros(w_in.shape[:2] + (TAIL_COLS - (Q_LORA + KV_LORA + MLA_ROPE + 2 * HEADS),), w_in.dtype)
    w_main = jnp.concatenate([w_in[..., :i0], w_in[..., g0:]], axis=-1).astype(BF16)
    w_tail = jnp.concatenate([w_in[..., cq0:g0], w_in[..., i0:cq0], pad], axis=-1).astype(BF16)
    return w_main, w_tail


def _heads_last_split(w, first, second):
    k = w.shape[0]
    w3 = w.reshape(k, HEADS, first + second)
    return jnp.concatenate([w3[:, :, :first].reshape(k, HEADS * first),
                            w3[:, :, first:].reshape(k, HEADS * second)], axis=1).astype(BF16)


def _score_bound(q_gain, k_gain):
    def sq_norm(g):
        g = g.astype(F32)
        return MLA_NOPE * jnp.max(g[:MLA_NOPE] ** 2) + MLA_ROPE * jnp.max(g[MLA_NOPE:] ** 2)
    return jnp.sqrt(sq_norm(q_gain) * sq_norm(k_gain)) * (Q_SCALE * BOUND_SLACK)


def _pick(n, candidates):
    for c in candidates:
        if n % c == 0:
            return c
    raise ValueError(f"no tile size for {n}")


def kernel(x, positions, mix_norm, w_in, conv_w, mlstm_igate_bias, mlstm_fgate_bias, mlstm_head_norm,
           mla_q_a_norm, mla_w_uq, mla_kv_a_norm, mla_w_ukv, mla_q_norm, mla_k_norm, w_branch, w_out,
           mlp_norm, w_up, w_down):
    bsz, seq, _ = x.shape
    depth = w_in.shape[0]
    t = bsz * seq
    assert seq % ATT_T == 0
    n_chunk = _pick(seq // CHUNK, tuple(g for g in (8, 4, 2, 1) if g <= MLSTM_GROUP))
    tm_proj = _pick(seq, (1024, 512, 256, 128))
    tm = _pick(seq, (512, 256, 128))
    tm_mlp = _pick(t, (1024, 512, 256, 128))

    x2 = x.reshape(t, D_MODEL)
    cs, cst = _rope_table(positions, ATT_T)
    row = lambda a: a.reshape(1, -1).astype(F32)
    w_main, w_tail = _regroup_w_in(w_in)
    w_branch_b, w_out_b, w_up_b, w_down_b = (w.astype(BF16) for w in (w_branch, w_out, w_up, w_down))

    for l in range(depth):
        gate_bias = jnp.zeros((1, LANES), F32)
        gate_bias = gate_bias.at[0, IGATE_LANE:IGATE_LANE + HEADS].set(mlstm_igate_bias[l])
        gate_bias = gate_bias.at[0, FGATE_LANE:FGATE_LANE + HEADS].set(mlstm_fgate_bias[l])
        w_q = _heads_last_split(mla_w_uq[l], MLA_NOPE, MLA_ROPE)
        w_kv = _heads_last_split(mla_w_ukv[l], MLA_NOPE, MLA_V)

        y_a, main, tail = _in_proj(x2, row(mix_norm[l]), w_main, w_tail, conv_w[l].astype(F32), l, seq, tm_proj)
        qt, k, vt = _mla_prep(tail, cs, cst, row(mla_q_a_norm[l]), w_q.T, row(mla_kv_a_norm[l]),
                              w_kv[:, :HEADS * MLA_NOPE], w_kv[:, HEADS * MLA_NOPE:].T,
                              mla_q_norm[l].reshape(-1, 1).astype(F32) * Q_SCALE, row(mla_k_norm[l]), bsz, seq)
        num, dstat = _mlstm(main, tail, gate_bias, bsz, seq, n_chunk)
        y_c = lax.cond(_score_bound(mla_q_norm[l], mla_k_norm[l]) <= SCORE_BOUND,
                       functools.partial(_attention, online=False), functools.partial(_attention, online=True),
                       qt, k, vt).reshape(t, D_MODEL)
        x2 = _merge(y_a, main, num, dstat, row(mlstm_head_norm[l]), y_c, x2, w_branch_b, w_out_b, l, tm)
        x2 = _mlp(x2, row(mlp_norm[l]), w_up_b, w_down_b, l, tm_mlp)
    return x2.reshape(bsz, seq, D_MODEL)
```

```python
import functools

import jax
import jax.numpy as jnp
import numpy as np
from jax import lax
from jax.experimental import pallas as pl
from jax.experimental.pallas import tpu as pltpu

F32 = jnp.float32
BF16 = jnp.bfloat16

D_MODEL = 1024
N_BRANCH = 3
CONV_WIDTH = 3
HEADS = 8
MLSTM_DK = 64
MLSTM_DV = 128
CHUNK = 128
GATE_CAP = 15.0
MLA_NOPE = 128
MLA_ROPE = 64
MLA_V = 128
Q_LORA = 256
KV_LORA = 128
ROPE_BASE = 10000.0
D_FF = 4 * D_MODEL
NORM_EPS = 1e-6
QK_PAD = 256

CONV_COLS = 3 * D_MODEL
PROJ_COLS = 9 * D_MODEL
MAIN_COLS = PROJ_COLS - CONV_COLS
TAIL_COLS = 512
COL_MQ = 0
COL_MK = COL_MQ + HEADS * MLSTM_DK
COL_MV = COL_MK + HEADS * MLSTM_DK
COL_MO = COL_MV + HEADS * MLSTM_DV
GATE_COL0 = COL_MO + HEADS * MLSTM_DV
LANES = 128
SUBLANES = 8
TAIL_KROPE = Q_LORA + KV_LORA
IGATE_LANE = MLA_ROPE
FGATE_LANE = MLA_ROPE + HEADS

IN_PROJ_TN = 3 * D_MODEL
IN_PROJ_CHUNK = 1536
CONV_CHUNK = 256
MLSTM_GROUP = 8
ATT_T = 512
LOG2E = 1.4426950408889634
Q_SCALE = (MLA_NOPE + MLA_ROPE) ** -0.5 * LOG2E
MASK_VALUE = -1e30
SCORE_BOUND = 70.0
SCORE_SHIFT = 48.0
BOUND_SLACK = 1.02
ATT_UNROLL = 8
ATT_UNROLL_BOUNDED = 120

VMEM_LIMIT = 56 * 1024 * 1024


def _rms(x, g):
    return x * lax.rsqrt(jnp.mean(x * x, axis=-1, keepdims=True) + NORM_EPS) * g


def _sigmoid(x):
    return 0.5 * jnp.tanh(0.5 * x) + 0.5


def _dot(a, b):
    return jnp.dot(a, b, preferred_element_type=F32)


def _dot_nt(a, b):
    return lax.dot_general(a, b, (((1,), (1,)), ((), ())), preferred_element_type=F32)


def _rope_table_kernel(pos_ref, freq_ref, cs_ref, cst_ref):
    ang = freq_ref[...] * pos_ref[...].astype(F32)
    c, s = jnp.cos(ang), jnp.sin(ang)
    cst_ref[...] = jnp.concatenate([c, s], axis=0)
    cs_ref[...] = jnp.concatenate([c, c, -s, s], axis=0).T


def _rope_table(positions, tm):
    t = positions.size
    half = MLA_ROPE // 2
    inv_freq = (np.float32(ROPE_BASE) ** (-np.arange(0, MLA_ROPE, 2, dtype=np.float32) / np.float32(MLA_ROPE)))
    inv_freq = jnp.asarray(inv_freq.astype(np.float32))
    return pl.pallas_call(
        _rope_table_kernel,
        grid=(t // tm,),
        in_specs=[pl.BlockSpec((1, tm), lambda i: (0, i)),
                  pl.BlockSpec((half, 1), lambda i: (0, 0))],
        out_specs=[pl.BlockSpec((tm, 2 * MLA_ROPE), lambda i: (i, 0)),
                   pl.BlockSpec((MLA_ROPE, tm), lambda i: (0, i))],
        out_shape=[jax.ShapeDtypeStruct((t, 2 * MLA_ROPE), F32),
                   jax.ShapeDtypeStruct((MLA_ROPE, t), F32)],
        name="rope_table",
    )(positions.reshape(1, t), inv_freq.reshape(half, 1))


def _in_proj_kernel(x_ref, g_ref, w_ref, wt_ref, cw_ref, ya_ref, main_ref, tail_ref, xn_ref, zprev_ref, *,
                    tiles_per_seq):
    i, j = pl.program_id(0), pl.program_id(1)
    tm = x_ref.shape[0]

    @pl.when(jnp.logical_and(j == 0, i % tiles_per_seq == 0))
    def _():
        zprev_ref[...] = jnp.zeros_like(zprev_ref)

    @pl.when(j == 0)
    def _():
        xn = _rms(x_ref[...], g_ref[...]).astype(BF16)
        xn_ref[...] = xn
        tail_ref[...] = _dot(xn, wt_ref[...])
        row = lax.broadcasted_iota(jnp.int32, (tm, CONV_CHUNK), 0)
        for c0 in range(0, D_MODEL, CONV_CHUNK):
            c = slice(c0, c0 + CONV_CHUNK)
            gate_b, gate_c, u = (_dot(xn, w_ref[:, n * D_MODEL + c0:n * D_MODEL + c0 + CONV_CHUNK]) for n in range(3))
            z = gate_c * u
            p1, p2 = zprev_ref[SUBLANES - 1:SUBLANES, c], zprev_ref[SUBLANES - 2:SUBLANES - 1, c]
            z1 = jnp.where(row == 0, p1, pltpu.roll(z, 1, 0))
            z2 = jnp.where(row == 0, p2, jnp.where(row == 1, p1, pltpu.roll(z, 2, 0)))
            zprev_ref[:, c] = z[tm - SUBLANES:, :]
            ya_ref[:, c] = (gate_b * (cw_ref[0:1, c] * z2 + cw_ref[1:2, c] * z1 + cw_ref[2:3, c] * z)).astype(BF16)

    cols = [slice(c, c + IN_PROJ_CHUNK) for c in range(0, w_ref.shape[1], IN_PROJ_CHUNK)]

    @pl.when(j == 1)
    def _():
        for c in cols:
            main_ref[:, c] = _dot(xn_ref[...], w_ref[:, c]).astype(BF16)

    @pl.when(j == 2)
    def _():
        for c in cols:
            main_ref[:, c] = _sigmoid(_dot(xn_ref[...], w_ref[:, c])).astype(BF16)


def _in_proj(x2, g, w_main, w_tail, conv_w, layer, seq, tm):
    t = x2.shape[0]
    tn = IN_PROJ_TN
    assert CONV_COLS == tn and GATE_COL0 == tn and MAIN_COLS == 2 * tn
    kern = functools.partial(_in_proj_kernel, tiles_per_seq=seq // tm)
    return pl.pallas_call(
        kern,
        grid=(t // tm, PROJ_COLS // tn),
        in_specs=[pl.BlockSpec((tm, D_MODEL), lambda i, j: (i, 0)),
                  pl.BlockSpec((1, D_MODEL), lambda i, j: (0, 0)),
                  pl.BlockSpec((None, D_MODEL, tn), lambda i, j: (layer, 0, j)),
                  pl.BlockSpec((None, D_MODEL, TAIL_COLS), lambda i, j: (layer, 0, 0)),
                  pl.BlockSpec((CONV_WIDTH, D_MODEL), lambda i, j: (0, 0))],
        out_specs=[pl.BlockSpec((tm, D_MODEL), lambda i, j: (i, 0)),
                   pl.BlockSpec((tm, tn), lambda i, j: (i, jnp.maximum(j - 1, 0))),
                   pl.BlockSpec((tm, TAIL_COLS), lambda i, j: (i, 0))],
        out_shape=[jax.ShapeDtypeStruct((t, D_MODEL), BF16),
                   jax.ShapeDtypeStruct((t, MAIN_COLS), BF16),
                   jax.ShapeDtypeStruct((t, TAIL_COLS), F32)],
        scratch_shapes=[pltpu.VMEM((tm, D_MODEL), BF16), pltpu.VMEM((SUBLANES, D_MODEL), F32)],
        compiler_params=pltpu.CompilerParams(
            dimension_semantics=("arbitrary", "arbitrary"), vmem_limit_bytes=VMEM_LIMIT),
        name="in_proj",
    )(x2, g, w_main, w_tail, conv_w)


def _time_scan(x, op, row):
    s = 1
    while s < CHUNK:
        x = jnp.where(row >= s, op(x, pltpu.roll(x, s, 0)), x)
        s *= 2
    return x


def _mlstm_chunk(rows, q_ref, k_ref, v_ref, gate_ref, bias_ref, num_ref, dst_ref, c_state, m_prev):
    L = CHUNK
    row = lax.broadcasted_iota(jnp.int32, (L, LANES), 0)
    lane = lax.broadcasted_iota(jnp.int32, (L, LANES), 1)
    gate_lanes = jnp.logical_and(lane >= IGATE_LANE, lane < FGATE_LANE + HEADS)
    tri = (lax.broadcasted_iota(jnp.int32, (L, L), 0) >= lax.broadcasted_iota(jnp.int32, (L, L), 1))
    lane_v = lax.broadcasted_iota(jnp.int32, (L, MLSTM_DV), 1)
    c_state = list(c_state)
    pre = jnp.where(gate_lanes, gate_ref[rows, :] + bias_ref[...], 0.0)
    capped = GATE_CAP * jnp.tanh(pre * (1.0 / GATE_CAP))
    log_i = capped * LOG2E
    log_f = (jnp.minimum(capped, 0.0) - jnp.log1p(jnp.exp(-jnp.abs(capped)))) * LOG2E
    log_f = pltpu.roll(log_f, LANES - HEADS, 1)
    b = _time_scan(log_f, jnp.add, row)
    b_last = b[L - 1:L, :]
    rowd = log_i - b
    big_m = jnp.maximum(_time_scan(rowd, jnp.maximum, row), m_prev)
    e_neg_m = jnp.exp2(-(b + big_m))
    a = b_last + rowd
    m_loc = jnp.max(a, axis=0, keepdims=True)
    m_new = jnp.maximum(b_last + m_prev, m_loc)
    s_old = jnp.exp2(b_last + m_prev - m_new)
    w = jnp.exp2(a - m_new)
    rowd_t = rowd.T
    w_t = w.T
    q_all = q_ref[rows, :] * (MLSTM_DK ** -0.5)
    k_t = k_ref[rows, :].astype(F32).T

    hk = [slice(h * MLSTM_DK, (h + 1) * MLSTM_DK) for h in range(HEADS)]
    hv = [slice(h * MLSTM_DV, (h + 1) * MLSTM_DV) for h in range(HEADS)]
    gl = [slice(IGATE_LANE + h, IGATE_LANE + h + 1) for h in range(HEADS)]
    v_aug = [jnp.concatenate([v_ref[rows, hv[h]], (lane_v == IGATE_LANE + h).astype(BF16)], axis=1)
             for h in range(HEADS)]
    scores = [_dot(q_all[:, hk[h]], k_t[hk[h], :].astype(BF16)) for h in range(HEADS)]
    c_loc = [_dot((k_t[hk[h], :] * w_t[gl[h], :]).astype(BF16), v_aug[h]) for h in range(HEADS)]
    big_m_b = [jnp.broadcast_to(big_m[:, gl[h]], (L, L)) for h in range(HEADS)]
    dmat = [jnp.where(tri, jnp.exp2(rowd_t[gl[h], :] - big_m_b[h]), 0.0) for h in range(HEADS)]
    q_state = [(q_all[:, hk[h]].astype(F32)
                * jnp.exp2(m_prev[:, gl[h]] - big_m_b[h][:, :MLSTM_DK])).astype(BF16) for h in range(HEADS)]
    num_aug = []
    for h in range(HEADS):
        lhs = jnp.concatenate([(scores[h] * dmat[h]).astype(BF16), q_state[h],
                               jnp.zeros((L, L - MLSTM_DK), BF16)], axis=1)
        rhs = jnp.concatenate([v_aug[h], c_state[h].astype(BF16),
                               jnp.zeros((L - MLSTM_DK, 2 * MLSTM_DV), BF16)], axis=0)
        num_aug.append(_dot(lhs, rhs))
    for h in range(HEADS):
        num_ref[rows, hv[h]] = num_aug[h][:, :MLSTM_DV].astype(num_ref.dtype)
        c_state[h] = s_old[:, gl[h]] * c_state[h] + c_loc[h]
    den = functools.reduce(jnp.add, [n[:, MLSTM_DV:] for n in num_aug])
    dst_ref[rows, :] = jnp.maximum(jnp.abs(den), e_neg_m)
    return c_state, m_new


def _mlstm_kernel(q_ref, k_ref, v_ref, gate_ref, bias_ref, num_ref, dst_ref, c_ref, m_ref, *, n_chunk):
    @pl.when(pl.program_id(1) == 0)
    def _():
        c_ref[...] = jnp.zeros_like(c_ref)
        m_ref[...] = jnp.zeros_like(m_ref)

    m_prev = m_ref[...]
    c_state = [c_ref[h] for h in range(HEADS)]
    for g in range(n_chunk):
        c_state, m_prev = _mlstm_chunk(slice(g * CHUNK, (g + 1) * CHUNK), q_ref, k_ref, v_ref, gate_ref, bias_ref,
                                       num_ref, dst_ref, c_state, m_prev)
    m_ref[...] = m_prev
    for h in range(HEADS):
        c_ref[h] = c_state[h]


def _mlstm(main, tail, gate_bias, bsz, seq, n_chunk):
    t = main.shape[0]
    rows = n_chunk * CHUNK
    nc = seq // rows
    row = lambda b, c: b * nc + c
    col_block = lambda col0, width: pl.BlockSpec((rows, width), lambda b, c: (row(b, c), col0 // width))
    return pl.pallas_call(
        functools.partial(_mlstm_kernel, n_chunk=n_chunk),
        grid=(bsz, nc),
        in_specs=[col_block(COL_MQ, HEADS * MLSTM_DK), col_block(COL_MK, HEADS * MLSTM_DK),
                  col_block(COL_MV, HEADS * MLSTM_DV),
                  col_block(TAIL_KROPE, LANES),
                  pl.BlockSpec((1, LANES), lambda b, c: (0, 0))],
        out_specs=[pl.BlockSpec((rows, D_MODEL), lambda b, c: (row(b, c), 0)),
                   pl.BlockSpec((rows, LANES), lambda b, c: (row(b, c), 0))],
        out_shape=[jax.ShapeDtypeStruct((t, D_MODEL), BF16), jax.ShapeDtypeStruct((t, LANES), F32)],
        scratch_shapes=[pltpu.VMEM((HEADS, MLSTM_DK, 2 * MLSTM_DV), F32),
                        pltpu.VMEM((1, LANES), F32)],
        compiler_params=pltpu.CompilerParams(dimension_semantics=("parallel", "arbitrary")),
        name="mlstm",
    )(main, main, main, tail, gate_bias)


def _rms0(x, g_col):
    return x * lax.rsqrt(jnp.mean(x * x, axis=0, keepdims=True) + NORM_EPS) * g_col


def _rope(x, cs):
    half = MLA_ROPE // 2
    rot = jnp.concatenate([x[:, half:], x[:, :half]], axis=-1)
    return x * cs[:, :MLA_ROPE] + rot * cs[:, MLA_ROPE:]


def _mla_prep_kernel(tail_ref, cs_ref, cst_ref, qa_ref, wqt_ref, kva_ref, wk_ref, wvt_ref, qn_ref, kn_ref,
                     qt_out, k_out, vt_out):
    tm = ATT_T
    half = MLA_ROPE // 2
    qn_g, qp_g = qn_ref[:MLA_NOPE, :], qn_ref[MLA_NOPE:, :]
    kn_g, kp_g = kn_ref[:, :MLA_NOPE], kn_ref[:, MLA_NOPE:]
    pe0 = HEADS * MLA_NOPE
    pad = QK_PAD - MLA_NOPE - MLA_ROPE
    q_pad = jnp.where(lax.broadcasted_iota(jnp.int32, (pad, tm), 0) == 0, -SCORE_SHIFT, 0.0).astype(BF16)
    k_pad = (lax.broadcasted_iota(jnp.int32, (tm, pad), 1) == 0).astype(BF16)
    for s in range(tail_ref.shape[0] // tm):
        rows = slice(s * tm, (s + 1) * tm)
        cq = _rms(tail_ref[rows, :Q_LORA], qa_ref[...]).astype(BF16)
        ckv = _rms(tail_ref[rows, Q_LORA:Q_LORA + KV_LORA], kva_ref[...]).astype(BF16)
        qft = _dot_nt(wqt_ref[...], cq)
        kf = _dot(ckv, wk_ref[...])
        vft = _dot_nt(wvt_ref[...], ckv)
        cos_t, sin_t = cst_ref[:half, rows], cst_ref[half:, rows]
        k_pe = _rope(_rms(tail_ref[rows, TAIL_KROPE:TAIL_KROPE + MLA_ROPE], kp_g), cs_ref[rows, :]).astype(BF16)
        for h in range(HEADS):
            q_nope = _rms0(qft[h * MLA_NOPE:(h + 1) * MLA_NOPE], qn_g)
            q_pe = _rms0(qft[pe0 + h * MLA_ROPE:pe0 + (h + 1) * MLA_ROPE], qp_g)
            x1, x2 = q_pe[:half], q_pe[half:]
            q_rot = jnp.concatenate([x1 * cos_t - x2 * sin_t, x1 * sin_t + x2 * cos_t], axis=0)
            qt_out[0, h, s] = jnp.concatenate([q_nope.astype(BF16), q_rot.astype(BF16), q_pad], axis=0)
            k_nope = _rms(kf[:, h * MLA_NOPE:(h + 1) * MLA_NOPE], kn_g)
            k_out[0, h, rows, :] = jnp.concatenate([k_nope.astype(BF16), k_pe, k_pad], axis=-1)
            vt_out[0, h, s] = vft[h * MLA_V:(h + 1) * MLA_V].astype(BF16)


def _mla_prep(tail, cs, cst, qa, wqt, kva, wk, wvt, qn_col, kn, bsz, seq):
    tiles = 2 if (seq // ATT_T) % 2 == 0 else 1
    tm = tiles * ATT_T
    nt = seq // tm
    full = lambda shape: pl.BlockSpec(shape, lambda i: (0,) * len(shape))
    blocked = lambda rows: pl.BlockSpec((1, HEADS, tiles, rows, ATT_T), lambda i: (i // nt, 0, i % nt, 0, 0))
    return pl.pallas_call(
        _mla_prep_kernel,
        grid=(bsz * nt,),
        in_specs=[pl.BlockSpec((tm, TAIL_COLS), lambda i: (i, 0)),
                  pl.BlockSpec((tm, 2 * MLA_ROPE), lambda i: (i, 0)),
                  pl.BlockSpec((MLA_ROPE, tm), lambda i: (0, i)),
                  full((1, Q_LORA)), full(wqt.shape), full((1, KV_LORA)), full(wk.shape), full(wvt.shape),
                  full((MLA_NOPE + MLA_ROPE, 1)), full((1, MLA_NOPE + MLA_ROPE))],
        out_specs=[blocked(QK_PAD),
                   pl.BlockSpec((1, HEADS, tm, QK_PAD), lambda i: (i // nt, 0, i % nt, 0)),
                   blocked(MLA_V)],
        out_shape=[jax.ShapeDtypeStruct((bsz, HEADS, seq // ATT_T, QK_PAD, ATT_T), BF16),
                   jax.ShapeDtypeStruct((bsz, HEADS, seq, QK_PAD), BF16),
                   jax.ShapeDtypeStruct((bsz, HEADS, seq // ATT_T, MLA_V, ATT_T), BF16)],
        compiler_params=pltpu.CompilerParams(dimension_semantics=("parallel",),
                                             vmem_limit_bytes=VMEM_LIMIT),
        name="mla_prep",
    )(tail, cs, cst, qa, wqt, kva, wk, wvt, qn_col, kn)


def _attn_kernel(qt_ref, k_ref, vt_ref, o_ref, s_scr, p_scr, l_all, acc_all, *online_scr, nq, online):
    t = ATT_T
    unroll = ATT_UNROLL if online else ATT_UNROLL_BOUNDED
    a_scr, m_all = online_scr if online else (None, None)

    def qk(qi, j):
        k0 = pl.multiple_of(j * t, t)
        return _dot(k_ref[0, 0, pl.ds(k0, t), :], qt_ref[0, 0, qi])

    def accumulate(tile, alpha, p):
        q, j = tile
        pv = _dot(vt_ref[0, 0, j], p)
        acc_all[q] = alpha * acc_all[q] + pv if online else acc_all[q] + pv

    def stage(prev, cur, nxt, s_cur, p_prev, alpha_prev, masked):
        accumulate(prev, alpha_prev, p_prev)
        qi = cur[0]
        if masked:
            keep = lax.broadcasted_iota(jnp.int32, (t, t), 0) <= lax.broadcasted_iota(jnp.int32, (t, t), 1)
        if online:
            if masked:
                s_cur = jnp.where(keep, s_cur, MASK_VALUE)
            m_old = m_all[qi]
            m_new = jnp.maximum(m_old, jnp.max(s_cur, axis=0, keepdims=True))
            m_all[qi] = m_new
            alpha = jnp.exp2(m_old - m_new)
            p = jnp.exp2((s_cur - m_new).astype(BF16))
            l_all[qi] = alpha * l_all[qi] + jnp.sum(p.astype(F32), axis=0, keepdims=True)
        else:
            alpha = None
            p = jnp.exp2(s_cur)
            if masked:
                p = jnp.where(keep, p, 0.0)
            l_all[qi] += jnp.sum(p, axis=0, keepdims=True)
            p = p.astype(BF16)
        s_next = qk(jnp.minimum(nxt[0], nq - 1), jnp.minimum(nxt[1], nq - 1))
        return s_next, p, alpha

    def advance_lower(qi, j):
        wrap = j + 1 == qi
        return jnp.where(wrap, qi + 1, qi), jnp.where(wrap, 0, j + 1)

    def advance_diag(qi, j):
        return qi + 1, j + 1

    def run_phase(n_steps, first, advance, masked):
        if n_steps == 0:
            return
        s_scr[...] = qk(*first)
        p_scr[...] = jnp.zeros_like(p_scr)
        if online:
            a_scr[...] = jnp.ones_like(a_scr)

        def steps(count, prev, cur):
            s_cur, p_prev = s_scr[...], p_scr[...]
            alpha_prev = a_scr[...] if online else None
            for _ in range(count):
                nxt = advance(*cur)
                s_cur, p_prev, alpha_prev = stage(prev, cur, nxt, s_cur, p_prev, alpha_prev, masked)
                prev, cur = cur, nxt
            s_scr[...] = s_cur
            p_scr[...] = p_prev
            if online:
                a_scr[...] = alpha_prev
            return prev, cur

        first = (jnp.int32(first[0]), jnp.int32(first[1]))
        prev, cur = lax.fori_loop(0, n_steps // unroll, lambda _, c: steps(unroll, *c), (first, first))
        if n_steps % unroll:
            prev, cur = steps(n_steps % unroll, prev, cur)
        accumulate(prev, a_scr[...] if online else None, p_scr[...])

    if online:
        m_all[...] = jnp.full_like(m_all, MASK_VALUE)
    l_all[...] = jnp.zeros_like(l_all)
    acc_all[...] = jnp.zeros_like(acc_all)
    run_phase(nq * (nq - 1) // 2, (1, 0), advance_lower, False)
    run_phase(nq, (0, 0), advance_diag, True)
    for qi in range(nq):
        o_ref[0, qi * t:(qi + 1) * t, :] = (acc_all[qi] / l_all[qi]).T.astype(o_ref.dtype)


def _attention(qt, k, vt, *, online):
    bsz, _, seq, _ = k.shape
    nq = seq // ATT_T
    online_scr = [pltpu.VMEM((1, ATT_T), F32), pltpu.VMEM((nq, 1, ATT_T), F32)] if online else []
    return pl.pallas_call(
        functools.partial(_attn_kernel, nq=nq, online=online),
        grid=(bsz, HEADS),
        in_specs=[pl.BlockSpec((1, 1, nq, QK_PAD, ATT_T), lambda b, h: (b, h, 0, 0, 0)),
                  pl.BlockSpec((1, 1, seq, QK_PAD), lambda b, h: (b, h, 0, 0)),
                  pl.BlockSpec((1, 1, nq, MLA_V, ATT_T), lambda b, h: (b, h, 0, 0, 0))],
        out_specs=pl.BlockSpec((1, seq, MLA_V), lambda b, h: (b, 0, h)),
        out_shape=jax.ShapeDtypeStruct((bsz, seq, HEADS * MLA_V), BF16),
        scratch_shapes=[pltpu.VMEM((ATT_T, ATT_T), F32), pltpu.VMEM((ATT_T, ATT_T), BF16),
                        pltpu.VMEM((nq, 1, ATT_T), F32), pltpu.VMEM((nq, MLA_V, ATT_T), F32)] + online_scr,
        compiler_params=pltpu.CompilerParams(dimension_semantics=("parallel", "parallel"),
                                             vmem_limit_bytes=VMEM_LIMIT),
        name="attn_online" if online else "attn_bounded",
    )(qt, k, vt)


def _mlstm_head_out(num_ref, dst_ref, o_ref, hn_ref):
    out_gain = _sigmoid(o_ref[...].astype(F32)) * hn_ref[...]
    heads = []
    for h in range(HEADS):
        hv = slice(h * MLSTM_DV, (h + 1) * MLSTM_DV)
        num = num_ref[:, hv].astype(F32)
        d = dst_ref[:, IGATE_LANE + h:IGATE_LANE + h + 1]
        scale = lax.rsqrt(jnp.mean(num * num, axis=-1, keepdims=True) + NORM_EPS * d * d)
        heads.append((num * scale * out_gain[:, hv]).astype(BF16))
    return jnp.concatenate(heads, axis=1)


def _merge_kernel(ya_ref, gate_ref, num_ref, dst_ref, o_ref, hn_ref, yc_ref, x_ref, wb_ref, wo_ref, out_ref):
    g = gate_ref[...]
    merged = g[:, :D_MODEL].astype(F32) * _dot(ya_ref[...], wb_ref[0])
    y_b = _mlstm_head_out(num_ref, dst_ref, o_ref, hn_ref)
    merged += g[:, D_MODEL:2 * D_MODEL].astype(F32) * _dot(y_b, wb_ref[1])
    merged += g[:, 2 * D_MODEL:].astype(F32) * _dot(yc_ref[...], wb_ref[2])
    out_ref[...] = x_ref[...] + _dot(merged.astype(BF16), wo_ref[...])


def _merge(y_a, main, num, dstat, head_norm, y_c, x2, w_branch, w_out, layer, tm):
    t = x2.shape[0]
    col = lambda c: pl.BlockSpec((tm, D_MODEL), lambda i: (i, c))
    return pl.pallas_call(
        _merge_kernel,
        grid=(t // tm,),
        in_specs=[col(0),
                  pl.BlockSpec((tm, N_BRANCH * D_MODEL), lambda i: (i, GATE_COL0 // (N_BRANCH * D_MODEL))),
                  col(0), pl.BlockSpec((tm, LANES), lambda i: (i, 0)), col(COL_MO // D_MODEL),
                  pl.BlockSpec((1, D_MODEL), lambda i: (0, 0)),
                  col(0), col(0),
                  pl.BlockSpec((None, N_BRANCH, D_MODEL, D_MODEL), lambda i: (layer, 0, 0, 0),
                               pipeline_mode=pl.Buffered(1)),
                  pl.BlockSpec((None, D_MODEL, D_MODEL), lambda i: (layer, 0, 0), pipeline_mode=pl.Buffered(1))],
        out_specs=col(0),
        out_shape=jax.ShapeDtypeStruct((t, D_MODEL), F32),
        compiler_params=pltpu.CompilerParams(dimension_semantics=("parallel",),
                                             vmem_limit_bytes=VMEM_LIMIT),
        name="merge",
    )(y_a, main, num, dstat, main, head_norm, y_c, x2, w_branch, w_out)


def _mlp_kernel(x_ref, g_ref, wu_ref, wd_ref, out_ref, *, n_chunk):
    x = x_ref[...]
    h = _rms(x, g_ref[...]).astype(BF16)
    ck = D_FF // n_chunk
    acc = x
    for c in range(n_chunk):
        u = jnp.maximum(_dot(h, wu_ref[:, c * ck:(c + 1) * ck]), 0.0)
        acc = acc + _dot((u * u).astype(BF16), wd_ref[c * ck:(c + 1) * ck, :])
    out_ref[...] = acc


def _mlp(x2, g, w_up, w_down, layer, tm):
    t = x2.shape[0]
    kern = functools.partial(_mlp_kernel, n_chunk=4)
    return pl.pallas_call(
        kern,
        grid=(t // tm,),
        in_specs=[pl.BlockSpec((tm, D_MODEL), lambda i: (i, 0)),
                  pl.BlockSpec((1, D_MODEL), lambda i: (0, 0)),
                  pl.BlockSpec((None, D_MODEL, D_FF), lambda i: (layer, 0, 0), pipeline_mode=pl.Buffered(1)),
                  pl.BlockSpec((None, D_FF, D_MODEL), lambda i: (layer, 0, 0), pipeline_mode=pl.Buffered(1))],
        out_specs=pl.BlockSpec((tm, D_MODEL), lambda i: (i, 0)),
        out_shape=jax.ShapeDtypeStruct((t, D_MODEL), F32),
        compiler_params=pltpu.CompilerParams(dimension_semantics=("parallel",),
                                             vmem_limit_bytes=VMEM_LIMIT),
        name="mlp",
    )(x2, g, w_up, w_down)


def _regroup_w_in(w_in):
    i0 = CONV_COLS + GATE_COL0
    cq0 = i0 + 2 * HEADS
    g0 = cq0 + Q_LORA + KV_LORA + MLA_ROPE
    pad = jnp.zeros(w_in.shape[:2] + (TAIL_COLS - (Q_LORA + KV_LORA + MLA_ROPE + 2 * HEADS),), w_in.dtype)
    w_main = jnp.concatenate([w_in[..., :i0], w_in[..., g0:]], axis=-1).astype(BF16)
    w_tail = jnp.concatenate([w_in[..., cq0:g0], w_in[..., i0:cq0], pad], axis=-1).astype(BF16)
    return w_main, w_tail


def _heads_last_split(w, first, second):
    k = w.shape[0]
    w3 = w.reshape(k, HEADS, first + second)
    return jnp.concatenate([w3[:, :, :first].reshape(k, HEADS * first),
                            w3[:, :, first:].reshape(k, HEADS * second)], axis=1).astype(BF16)


def _score_bound(q_gain, k_gain):
    def sq_norm(g):
        g = g.astype(F32)
        return MLA_NOPE * jnp.max(g[:MLA_NOPE] ** 2) + MLA_ROPE * jnp.max(g[MLA_NOPE:] ** 2)
    return jnp.sqrt(sq_norm(q_gain) * sq_norm(k_gain)) * (Q_SCALE * BOUND_SLACK)


def _pick(n, candidates):
    for c in candidates:
        if n % c == 0:
            return c
    raise ValueError(f"no tile size for {n}")


def kernel(x, positions, mix_norm, w_in, conv_w, mlstm_igate_bias, mlstm_fgate_bias, mlstm_head_norm,
           mla_q_a_norm, mla_w_uq, mla_kv_a_norm, mla_w_ukv, mla_q_norm, mla_k_norm, w_branch, w_out,
           mlp_norm, w_up, w_down):
    bsz, seq, _ = x.shape
    depth = w_in.shape[0]
    t = bsz * seq
    assert seq % ATT_T == 0
    n_chunk = _pick(seq // CHUNK, tuple(g for g in (8, 4, 2, 1) if g <= MLSTM_GROUP))
    tm_proj = _pick(seq, (1024, 512, 256, 128))
    tm = _pick(seq, (512, 256, 128))
    tm_mlp = _pick(t, (1024, 512, 256, 128))

    x2 = x.reshape(t, D_MODEL)
    cs, cst = _rope_table(positions, ATT_T)
    row = lambda a: a.reshape(1, -1).astype(F32)
    w_main, w_tail = _regroup_w_in(w_in)
    w_branch_b, w_out_b, w_up_b, w_down_b = (w.astype(BF16) for w in (w_branch, w_out, w_up, w_down))

    for l in range(depth):
        gate_bias = jnp.zeros((1, LANES), F32)
        gate_bias = gate_bias.at[0, IGATE_LANE:IGATE_LANE + HEADS].set(mlstm_igate_bias[l])
        gate_bias = gate_bias.at[0, FGATE_LANE:FGATE_LANE + HEADS].set(mlstm_fgate_bias[l])
        w_q = _heads_last_split(mla_w_uq[l], MLA_NOPE, MLA_ROPE)
        w_kv = _heads_last_split(mla_w_ukv[l], MLA_NOPE, MLA_V)

        y_a, main, tail = _in_proj(x2, row(mix_norm[l]), w_main, w_tail, conv_w[l].astype(F32), l, seq, tm_proj)
        qt, k, vt = _mla_prep(tail, cs, cst, row(mla_q_a_norm[l]), w_q.T, row(mla_kv_a_norm[l]),
                              w_kv[:, :HEADS * MLA_NOPE], w_kv[:, HEADS * MLA_NOPE:].T,
                              mla_q_norm[l].reshape(-1, 1).astype(F32) * Q_SCALE, row(mla_k_norm[l]), bsz, seq)
        num, dstat = _mlstm(main, tail, gate_bias, bsz, seq, n_chunk)
        y_c = lax.cond(_score_bound(mla_q_norm[l], mla_k_norm[l]) <= SCORE_BOUND,
                       functools.partial(_attention, online=False), functools.partial(_attention, online=True),
                       qt, k, vt).reshape(t, D_MODEL)
        x2 = _merge(y_a, main, num, dstat, row(mlstm_head_norm[l]), y_c, x2, w_branch_b, w_out_b, l, tm)
        x2 = _mlp(x2, row(mlp_norm[l]), w_up_b, w_down_b, l, tm_mlp)
    return x2.reshape(bsz, seq, D_MODEL)
```
